```python
import jax, jax.numpy as jnp
from jax import lax
import numpy as np

D_MODEL = 2048
BATCH = 8
SEQ = 4096
DEPTH = 4

GRID_W = 64
MIX_WIDTH = D_MODEL
CONV_WIDTH = MIX_WIDTH // 2
NA_HEADS = 16
NA_HEAD_DIM = (MIX_WIDTH - CONV_WIDTH) // NA_HEADS
NA_WIDTH = NA_HEADS * NA_HEAD_DIM
CONV_KERNEL = 31
WIN_ROWS_MAX = 8
WIN_COLS = 16
D_FF = 4 * D_MODEL
IN_COLS = 2 * CONV_WIDTH + 3 * NA_WIDTH
RMS_EPS = 1e-6
LN_EPS = 1e-5
NEG_INF = -1e30

kernel_name = "hybrid_conv_natten_encoder"


def rms_norm(x, g):
    xf = x.astype(jnp.float32)
    y = xf * lax.rsqrt(jnp.mean(xf * xf, axis=-1, keepdims=True) + RMS_EPS)
    return (y * g.astype(jnp.float32)).astype(x.dtype)


def layer_norm(x, g, b):
    xf = x.astype(jnp.float32)
    mu = jnp.mean(xf, axis=-1, keepdims=True)
    xc = xf - mu
    var = jnp.mean(xc * xc, axis=-1, keepdims=True)
    y = xc * lax.rsqrt(var + LN_EPS) * g.astype(jnp.float32) + b.astype(jnp.float32)
    return y.astype(x.dtype)


def conformer_conv_group(a, gate, w_dw, b_dw, ln_g, ln_b):
    u = a * jax.nn.sigmoid(gate)
    u = lax.conv_general_dilated(
        u, w_dw[:, None, :].astype(u.dtype),
        window_strides=(1,),
        padding=[(CONV_KERNEL // 2, CONV_KERNEL // 2)],
        dimension_numbers=("NWC", "WIO", "NWC"),
        feature_group_count=CONV_WIDTH,
    ) + b_dw.astype(u.dtype)
    return jax.nn.silu(layer_norm(u, ln_g, ln_b))


def neighbourhood_attention_group(q, k, v, rpb):
    B, T, _ = q.shape
    rows = T // GRID_W
    kr = min(WIN_ROWS_MAX, rows)
    r = np.arange(rows)
    c = np.arange(GRID_W)
    row_start = np.clip(r - kr // 2, 0, rows - kr)
    row_idx = row_start[:, None] + np.arange(kr)[None, :]
    col_start = np.clip(c - WIN_COLS // 2, 0, GRID_W - WIN_COLS)
    col_mask = (c[None, :] >= col_start[:, None]) & (c[None, :] < col_start[:, None] + WIN_COLS)
    dr = row_idx - r[:, None] + (WIN_ROWS_MAX - 1)
    dc = np.clip(c[None, :] - c[:, None], -(WIN_COLS - 1), WIN_COLS - 1) + (WIN_COLS - 1)
    bias = rpb[:, dr[:, None, :, None], dc[None, :, None, :]].astype(jnp.float32)

    def to_grid(t):
        return t.reshape(B, rows, GRID_W, NA_HEADS, NA_HEAD_DIM)

    qg, kg, vg = to_grid(q), to_grid(k), to_grid(v)
    k_rows = kg[:, row_idx]
    v_rows = vg[:, row_idx]
    s = jnp.einsum("brwhd,brkvhd->bhrwkv", qg, k_rows,
                   preferred_element_type=jnp.float32) * (NA_HEAD_DIM ** -0.5)
    s = jnp.where(col_mask[:, None, :], s + bias, NEG_INF)
    p = jax.nn.softmax(s.reshape(B, NA_HEADS, rows, GRID_W, kr * GRID_W), axis=-1)
    p = p.reshape(s.shape).astype(v.dtype)
    o = jnp.einsum("bhrwkv,brkvhd->brwhd", p, v_rows)
    return o.reshape(B, T, NA_WIDTH)


def _fwd_setup_inputs(seed: int = 0) -> dict:
    key = jax.random.key(seed)
    ks = jax.random.split(key, 16)
    f32 = jnp.float32
    x = jax.random.normal(ks[0], (BATCH, SEQ, D_MODEL), f32)
    w_in = jax.random.normal(ks[1], (DEPTH, D_MODEL, IN_COLS), f32) * D_MODEL ** -0.5
    w_dw = jax.random.normal(ks[2], (DEPTH, CONV_KERNEL, CONV_WIDTH), f32) * CONV_KERNEL ** -0.5
    b_dw = jax.random.normal(ks[3], (DEPTH, CONV_WIDTH), f32) * 0.01
    conv_ln_g = 1.0 + 0.05 * jax.random.normal(ks[4], (DEPTH, CONV_WIDTH), f32)
    conv_ln_b = 0.01 * jax.random.normal(ks[5], (DEPTH, CONV_WIDTH), f32)
    rpb = 0.02 * jax.random.normal(ks[6], (DEPTH, NA_HEADS, 2 * WIN_ROWS_MAX - 1, 2 * WIN_COLS - 1), f32)
    w_out = jax.random.normal(ks[7], (DEPTH, MIX_WIDTH, D_MODEL), f32) * MIX_WIDTH ** -0.5
    w_up = jax.random.normal(ks[8], (DEPTH, D_MODEL, D_FF), f32) * D_MODEL ** -0.5
    w_down = jax.random.normal(ks[9], (DEPTH, D_FF, D_MODEL), f32) * D_FF ** -0.5
    pre_mix_g = 1.0 + 0.05 * jax.random.normal(ks[10], (DEPTH, D_MODEL), f32)
    post_mix_g = 1.0 + 0.05 * jax.random.normal(ks[11], (DEPTH, D_MODEL), f32)
    pre_mlp_g = 1.0 + 0.05 * jax.random.normal(ks[12], (DEPTH, D_MODEL), f32)
    post_mlp_g = 1.0 + 0.05 * jax.random.normal(ks[13], (DEPTH, D_MODEL), f32)
    return {"x": x, "w_in": w_in, "w_dw": w_dw, "b_dw": b_dw, "conv_ln_g": conv_ln_g,
            "conv_ln_b": conv_ln_b, "rpb": rpb, "w_out": w_out, "w_up": w_up, "w_down": w_down,
            "pre_mix_g": pre_mix_g, "post_mix_g": post_mix_g, "pre_mlp_g": pre_mlp_g,
            "post_mlp_g": post_mlp_g}


def _fwd_reference(x, w_in, w_dw, b_dw, conv_ln_g, conv_ln_b, rpb, w_out, w_up, w_down,
              pre_mix_g, post_mix_g, pre_mlp_g, post_mlp_g):
    splits = [CONV_WIDTH, 2 * CONV_WIDTH, 2 * CONV_WIDTH + NA_WIDTH, 2 * CONV_WIDTH + 2 * NA_WIDTH]
    for l in range(DEPTH):
        h = rms_norm(x, pre_mix_g[l])
        proj = h @ w_in[l]
        a, gate, q, k, v = jnp.split(proj, splits, axis=-1)
        yc = conformer_conv_group(a, gate, w_dw[l], b_dw[l], conv_ln_g[l], conv_ln_b[l])
        ya = neighbourhood_attention_group(q, k, v, rpb[l])
        mix = jnp.concatenate([yc, ya], axis=-1) @ w_out[l]
        x = x + rms_norm(mix, post_mix_g[l])
        h = rms_norm(x, pre_mlp_g[l])
        f = jnp.square(jax.nn.relu(h @ w_up[l])) @ w_down[l]
        x = x + rms_norm(f, post_mlp_g[l])
    return x


import jax as _jax
import jax.numpy as _jnp

TWIN_FORMAT = 'train_step'
FWD_PARAMS = ['x', 'w_in', 'w_dw', 'b_dw', 'conv_ln_g', 'conv_ln_b', 'rpb', 'w_out', 'w_up', 'w_down', 'pre_mix_g', 'post_mix_g', 'pre_mlp_g', 'post_mlp_g']
TWIN_WEIGHTS = ['w_in', 'w_dw', 'b_dw', 'conv_ln_g', 'conv_ln_b', 'rpb', 'w_out', 'w_up', 'w_down', 'pre_mix_g', 'post_mix_g', 'pre_mlp_g', 'post_mlp_g']
TWIN_DIFF_INPUT = 'x'
TWIN_INPUTS = ['x', 'w_in', 'w_dw', 'b_dw', 'conv_ln_g', 'conv_ln_b', 'rpb', 'w_out', 'w_up', 'w_down', 'pre_mix_g', 'post_mix_g', 'pre_mlp_g', 'post_mlp_g', 'loss_target', 'm_w_in', 'm_w_dw', 'm_b_dw', 'm_conv_ln_g', 'm_conv_ln_b', 'm_rpb', 'm_w_out', 'm_w_up', 'm_w_down', 'm_pre_mix_g', 'm_post_mix_g', 'm_pre_mlp_g', 'm_post_mlp_g', 'v_w_in', 'v_w_dw', 'v_b_dw', 'v_conv_ln_g', 'v_conv_ln_b', 'v_rpb', 'v_w_out', 'v_w_up', 'v_w_down', 'v_pre_mix_g', 'v_post_mix_g', 'v_pre_mlp_g', 'v_post_mlp_g']
TWIN_OUTPUTS = ['loss', 'grad_x', 'grad_w_in', 'grad_w_dw', 'grad_b_dw', 'grad_conv_ln_g', 'grad_conv_ln_b', 'grad_rpb', 'grad_w_out', 'grad_w_up', 'grad_w_down', 'grad_pre_mix_g', 'grad_post_mix_g', 'grad_pre_mlp_g', 'grad_post_mlp_g', 'delta_w_in', 'delta_w_dw', 'delta_b_dw', 'delta_conv_ln_g', 'delta_conv_ln_b', 'delta_rpb', 'delta_w_out', 'delta_w_up', 'delta_w_down', 'delta_pre_mix_g', 'delta_post_mix_g', 'delta_pre_mlp_g', 'delta_post_mlp_g', 'new_m_w_in', 'new_m_w_dw', 'new_m_b_dw', 'new_m_conv_ln_g', 'new_m_conv_ln_b', 'new_m_rpb', 'new_m_w_out', 'new_m_w_up', 'new_m_w_down', 'new_m_pre_mix_g', 'new_m_post_mix_g', 'new_m_pre_mlp_g', 'new_m_post_mlp_g', 'new_v_w_in', 'new_v_w_dw', 'new_v_b_dw', 'new_v_conv_ln_g', 'new_v_conv_ln_b', 'new_v_rpb', 'new_v_w_out', 'new_v_w_up', 'new_v_w_down', 'new_v_pre_mix_g', 'new_v_post_mix_g', 'new_v_pre_mlp_g', 'new_v_post_mlp_g']
TWIN_LEAF_KINDS = {'loss': 'loss', 'grad_x': 'grad_x', 'grad_w_in': 'grad_w', 'grad_w_dw': 'grad_w', 'grad_b_dw': 'grad_w', 'grad_conv_ln_g': 'grad_w', 'grad_conv_ln_b': 'grad_w', 'grad_rpb': 'grad_w', 'grad_w_out': 'grad_w', 'grad_w_up': 'grad_w', 'grad_w_down': 'grad_w', 'grad_pre_mix_g': 'grad_w', 'grad_post_mix_g': 'grad_w', 'grad_pre_mlp_g': 'grad_w', 'grad_post_mlp_g': 'grad_w', 'delta_w_in': 'delta_w', 'delta_w_dw': 'delta_w', 'delta_b_dw': 'delta_w', 'delta_conv_ln_g': 'delta_w', 'delta_conv_ln_b': 'delta_w', 'delta_rpb': 'delta_w', 'delta_w_out': 'delta_w', 'delta_w_up': 'delta_w', 'delta_w_down': 'delta_w', 'delta_pre_mix_g': 'delta_w', 'delta_post_mix_g': 'delta_w', 'delta_pre_mlp_g': 'delta_w', 'delta_post_mlp_g': 'delta_w', 'new_m_w_in': 'new_m', 'new_m_w_dw': 'new_m', 'new_m_b_dw': 'new_m', 'new_m_conv_ln_g': 'new_m', 'new_m_conv_ln_b': 'new_m', 'new_m_rpb': 'new_m', 'new_m_w_out': 'new_m', 'new_m_w_up': 'new_m', 'new_m_w_down': 'new_m', 'new_m_pre_mix_g': 'new_m', 'new_m_post_mix_g': 'new_m', 'new_m_pre_mlp_g': 'new_m', 'new_m_post_mlp_g': 'new_m', 'new_v_w_in': 'new_v', 'new_v_w_dw': 'new_v', 'new_v_b_dw': 'new_v', 'new_v_conv_ln_g': 'new_v', 'new_v_conv_ln_b': 'new_v', 'new_v_rpb': 'new_v', 'new_v_w_out': 'new_v', 'new_v_w_up': 'new_v', 'new_v_w_down': 'new_v', 'new_v_pre_mix_g': 'new_v', 'new_v_post_mix_g': 'new_v', 'new_v_pre_mlp_g': 'new_v', 'new_v_post_mlp_g': 'new_v'}


def _forward(args):
    return _fwd_reference(*[args[k] for k in FWD_PARAMS])


def _output_shape():
    def fwd():
        inp = _fwd_setup_inputs(0)
        return _fwd_reference(*[inp[k] for k in FWD_PARAMS])
    out = _jax.eval_shape(fwd)
    return out.shape, out.dtype

N_MICROBATCH = 1
ADAM_LR = 0.001
ADAM_B1 = 0.9
ADAM_B2 = 0.999
ADAM_EPS = 1e-08
ADAM_WD = 0.01
ADAM_STEP = 10
PER_EXAMPLE_BATCH_AXIS = {'x': 0, 'loss_target': 0}
SHARED_INPUTS = []
_WEIGHT_DTYPES = {'w_in': _jnp.float32, 'w_dw': _jnp.float32, 'b_dw': _jnp.float32, 'conv_ln_g': _jnp.float32, 'conv_ln_b': _jnp.float32, 'rpb': _jnp.float32, 'w_out': _jnp.float32, 'w_up': _jnp.float32, 'w_down': _jnp.float32, 'pre_mix_g': _jnp.float32, 'post_mix_g': _jnp.float32, 'pre_mlp_g': _jnp.float32, 'post_mlp_g': _jnp.float32}
MOMENT_SCALE = {'w_in': 9.417557e+00, 'w_dw': 1.041856e+01, 'b_dw': 7.383557e+01, 'conv_ln_g': 2.869072e+01, 'conv_ln_b': 3.922870e+01, 'rpb': 6.468372e-02, 'w_out': 1.851637e+01, 'w_up': 4.610630e+00, 'w_down': 1.650521e+01, 'pre_mix_g': 1.512185e+01, 'post_mix_g': 2.544416e+01, 'pre_mlp_g': 9.198317e+00, 'post_mlp_g': 2.369890e+01}


def _to_microbatches(a, axis):
    t = _jnp.moveaxis(a, axis, 0)
    t = t.reshape((N_MICROBATCH, t.shape[0] // N_MICROBATCH) + t.shape[1:])
    return _jnp.moveaxis(t, 1, axis + 1)


def setup_inputs(seed: int = 0) -> dict:
    inp = _fwd_setup_inputs(seed)
    key = _jax.random.fold_in(_jax.random.key(seed), 7919)
    shape, _ = _output_shape()
    out = dict(inp)
    out["loss_target"] = _jax.random.normal(_jax.random.fold_in(key, 0), shape, _jnp.float32)
    for i, name in enumerate(TWIN_WEIGHTS):
        w = inp[name].astype(_jnp.float32)
        if MOMENT_SCALE is None:
            s = _jnp.sqrt(_jnp.mean(_jnp.square(w)) + 1e-30)
        else:
            s = MOMENT_SCALE[name]
        km, kv = _jax.random.split(_jax.random.fold_in(key, i + 1))
        out[name] = w
        out["m_" + name] = s * _jax.random.normal(km, w.shape, _jnp.float32)
        out["v_" + name] = (s * s) * _jax.random.uniform(kv, w.shape, _jnp.float32, 0.5, 1.5)
    if N_MICROBATCH > 1:
        for name, axis in PER_EXAMPLE_BATCH_AXIS.items():
            out[name] = _to_microbatches(out[name], axis)
    return {'x': out['x'], 'w_in': out['w_in'], 'w_dw': out['w_dw'], 'b_dw': out['b_dw'], 'conv_ln_g': out['conv_ln_g'], 'conv_ln_b': out['conv_ln_b'], 'rpb': out['rpb'], 'w_out': out['w_out'], 'w_up': out['w_up'], 'w_down': out['w_down'], 'pre_mix_g': out['pre_mix_g'], 'post_mix_g': out['post_mix_g'], 'pre_mlp_g': out['pre_mlp_g'], 'post_mlp_g': out['post_mlp_g'], 'loss_target': out['loss_target'], 'm_w_in': out['m_w_in'], 'm_w_dw': out['m_w_dw'], 'm_b_dw': out['m_b_dw'], 'm_conv_ln_g': out['m_conv_ln_g'], 'm_conv_ln_b': out['m_conv_ln_b'], 'm_rpb': out['m_rpb'], 'm_w_out': out['m_w_out'], 'm_w_up': out['m_w_up'], 'm_w_down': out['m_w_down'], 'm_pre_mix_g': out['m_pre_mix_g'], 'm_post_mix_g': out['m_post_mix_g'], 'm_pre_mlp_g': out['m_pre_mlp_g'], 'm_post_mlp_g': out['m_post_mlp_g'], 'v_w_in': out['v_w_in'], 'v_w_dw': out['v_w_dw'], 'v_b_dw': out['v_b_dw'], 'v_conv_ln_g': out['v_conv_ln_g'], 'v_conv_ln_b': out['v_conv_ln_b'], 'v_rpb': out['v_rpb'], 'v_w_out': out['v_w_out'], 'v_w_up': out['v_w_up'], 'v_w_down': out['v_w_down'], 'v_pre_mix_g': out['v_pre_mix_g'], 'v_post_mix_g': out['v_post_mix_g'], 'v_pre_mlp_g': out['v_pre_mlp_g'], 'v_post_mlp_g': out['v_post_mlp_g']}


def _loss(weights, diff, rest, loss_target):
    with _jax.named_scope("forward"):
        args = {**rest, TWIN_DIFF_INPUT: diff, **{k: w.astype(_WEIGHT_DTYPES[k]) for k, w in weights.items()}}
        y = _forward(args)
    with _jax.named_scope("loss_head"):
        err = _jnp.square(y.astype(_jnp.float32) - loss_target)
        return 0.5 * _jnp.sum(_jnp.mean(err, axis=-1)) if err.ndim else 0.5 * err


def _adamw(w, g, m, v):
    m = ADAM_B1 * m + (1.0 - ADAM_B1) * g
    v = ADAM_B2 * v + (1.0 - ADAM_B2) * _jnp.square(g)
    m_hat = m / (1.0 - ADAM_B1 ** ADAM_STEP)
    v_hat = v / (1.0 - ADAM_B2 ** ADAM_STEP)
    delta = -ADAM_LR * (m_hat / (_jnp.sqrt(v_hat) + ADAM_EPS) + ADAM_WD * w)
    return delta, m, v


def reference(x, w_in, w_dw, b_dw, conv_ln_g, conv_ln_b, rpb, w_out, w_up, w_down, pre_mix_g, post_mix_g, pre_mlp_g, post_mlp_g, loss_target, m_w_in, m_w_dw, m_b_dw, m_conv_ln_g, m_conv_ln_b, m_rpb, m_w_out, m_w_up, m_w_down, m_pre_mix_g, m_post_mix_g, m_pre_mlp_g, m_post_mlp_g, v_w_in, v_w_dw, v_b_dw, v_conv_ln_g, v_conv_ln_b, v_rpb, v_w_out, v_w_up, v_w_down, v_pre_mix_g, v_post_mix_g, v_pre_mlp_g, v_post_mlp_g):
    given = dict(x=x, w_in=w_in, w_dw=w_dw, b_dw=b_dw, conv_ln_g=conv_ln_g, conv_ln_b=conv_ln_b, rpb=rpb, w_out=w_out, w_up=w_up, w_down=w_down, pre_mix_g=pre_mix_g, post_mix_g=post_mix_g, pre_mlp_g=pre_mlp_g, post_mlp_g=post_mlp_g, loss_target=loss_target, m_w_in=m_w_in, m_w_dw=m_w_dw, m_b_dw=m_b_dw, m_conv_ln_g=m_conv_ln_g, m_conv_ln_b=m_conv_ln_b, m_rpb=m_rpb, m_w_out=m_w_out, m_w_up=m_w_up, m_w_down=m_w_down, m_pre_mix_g=m_pre_mix_g, m_post_mix_g=m_post_mix_g, m_pre_mlp_g=m_pre_mlp_g, m_post_mlp_g=m_post_mlp_g, v_w_in=v_w_in, v_w_dw=v_w_dw, v_b_dw=v_b_dw, v_conv_ln_g=v_conv_ln_g, v_conv_ln_b=v_conv_ln_b, v_rpb=v_rpb, v_w_out=v_w_out, v_w_up=v_w_up, v_w_down=v_w_down, v_pre_mix_g=v_pre_mix_g, v_post_mix_g=v_post_mix_g, v_pre_mlp_g=v_pre_mlp_g, v_post_mlp_g=v_post_mlp_g)
    weights = {n: given[n] for n in TWIN_WEIGHTS}
    shared = {n: given[n] for n in SHARED_INPUTS}
    per_example = {n: given[n] for n in ['x']}
    grad_fn = _jax.value_and_grad(_loss, argnums=(0, 1))

    def one_microbatch(ex, loss_target):
        ex = dict(ex)
        diff = ex.pop(TWIN_DIFF_INPUT)
        return grad_fn(weights, diff, {**shared, **ex}, loss_target)

    if N_MICROBATCH == 1:
        loss, (grad_w, grad_x) = one_microbatch(per_example, given["loss_target"])
    else:
        def body(carry, xs):
            loss_sum, grad_sum = carry
            l_k, (gw_k, gx_k) = one_microbatch(xs[0], xs[1])
            with _jax.named_scope("update"):
                return (loss_sum + l_k, _jax.tree.map(_jnp.add, grad_sum, gw_k)), gx_k

        init = (_jnp.zeros((), _jnp.float32), _jax.tree.map(_jnp.zeros_like, weights))
        (loss, grad_w), grad_x = _jax.lax.scan(body, init, (per_example, given["loss_target"]))
    with _jax.named_scope("update"):
        delta_w, new_m, new_v = {}, {}, {}
        for n in TWIN_WEIGHTS:
            delta_w[n], new_m[n], new_v[n] = _adamw(weights[n], grad_w[n], given["m_" + n], given["v_" + n])
    return (loss, grad_x, *[grad_w[n] for n in TWIN_WEIGHTS], *[delta_w[n] for n in TWIN_WEIGHTS],
            *[new_m[n] for n in TWIN_WEIGHTS], *[new_v[n] for n in TWIN_WEIGHTS])
```

```python
import math

import jax
import jax.numpy as jnp
from jax import lax
from jax.experimental import pallas as pl
from jax.experimental.pallas import tpu as pltpu

_MXU = jnp.bfloat16
_WIRE = jnp.bfloat16
_F32 = jnp.float32

N_DEV = 8
GRID_W = 64
WIN_ROWS = 8
WIN_COLS = 16
HEAD_DIM = 64
LANES = 128
SUBLANES = 8
RMS_EPS = 1e-6
LN_EPS = 1e-5
NEG_INF = -1e30
ADAM_LR = 0.001
ADAM_B1 = 0.9
ADAM_B2 = 0.999
ADAM_EPS = 1e-08
ADAM_WD = 0.01
ADAM_STEP = 10
VMEM_BYTES_V7X = 64 << 20
VMEM_RESERVE = 12 << 20
MESH_AXES = ("x", "y", "c")


def _vmem_limit(block_bytes):
    return int(min(max(block_bytes + (8 << 20), 24 << 20), VMEM_BYTES_V7X - VMEM_RESERVE))


def _nbytes(shape, dtype):
    return math.prod(shape) * jnp.dtype(dtype).itemsize


def _blk(n, pref):
    if n <= pref:
        return n
    for t in range(pref, 7, -1):
        if n % t == 0 and t % SUBLANES == 0:
            return t
    return n


def _sigmoid(v):
    return 1.0 / (1.0 + jnp.exp(-v))


def _params(sem, nbytes):
    return pltpu.CompilerParams(dimension_semantics=sem, vmem_limit_bytes=_vmem_limit(nbytes))


def _cast_layer(w3, l, name):
    _, R, C = w3.shape
    tr = _blk(R, 512)

    def body(w_ref, o_ref):
        o_ref[...] = w_ref[...].astype(o_ref.dtype)

    return pl.pallas_call(
        body, name=name, out_shape=jax.ShapeDtypeStruct((R, C), _WIRE), grid=(R // tr,),
        in_specs=[pl.BlockSpec((None, tr, C), lambda i: (l, i, 0))],
        out_specs=pl.BlockSpec((tr, C), lambda i: (i, 0)),
        compiler_params=_params(("parallel",), 2 * tr * C * 6),
    )(w3)


def _norm_fwd(x, g, name):
    T, D = x.shape
    tm = _blk(T, 256)

    def body(x_ref, g_ref, h_ref):
        xv = x_ref[...]
        r = lax.rsqrt(jnp.mean(xv * xv, axis=-1, keepdims=True) + RMS_EPS)
        h_ref[...] = (xv * r * g_ref[...]).astype(h_ref.dtype)

    return pl.pallas_call(
        body, name=name, out_shape=jax.ShapeDtypeStruct((T, D), _MXU), grid=(T // tm,),
        in_specs=[pl.BlockSpec((tm, D), lambda i: (i, 0)), pl.BlockSpec((1, D), lambda i: (0, 0))],
        out_specs=pl.BlockSpec((tm, D), lambda i: (i, 0)),
        compiler_params=_params(("parallel",), 2 * tm * D * 6),
    )(x, g)


def _resid_norm_fwd(xres, y, g_post, g_next, name):
    T, D = xres.shape
    tm = _blk(T, 256)

    def body(x_ref, y_ref, gp_ref, gn_ref, xn_ref, h_ref):
        yv = y_ref[...]
        r = lax.rsqrt(jnp.mean(yv * yv, axis=-1, keepdims=True) + RMS_EPS)
        xn = x_ref[...] + yv * r * gp_ref[...]
        xn_ref[...] = xn
        r2 = lax.rsqrt(jnp.mean(xn * xn, axis=-1, keepdims=True) + RMS_EPS)
        h_ref[...] = (xn * r2 * gn_ref[...]).astype(h_ref.dtype)

    row = pl.BlockSpec((tm, D), lambda i: (i, 0))
    vec = pl.BlockSpec((1, D), lambda i: (0, 0))
    return pl.pallas_call(
        body, name=name,
        out_shape=(jax.ShapeDtypeStruct((T, D), _F32), jax.ShapeDtypeStruct((T, D), _MXU)),
        grid=(T // tm,), in_specs=[row, row, vec, vec], out_specs=(row, row),
        compiler_params=_params(("parallel",), 2 * tm * D * 14),
    )(xres, y, g_post, g_next)


def _resid_loss(xres, y, g_post, target, name):
    T, D = xres.shape
    tm = _blk(T, 256)

    def body(x_ref, y_ref, gp_ref, t_ref, dy_ref, loss_ref):
        yv = y_ref[...]
        r = lax.rsqrt(jnp.mean(yv * yv, axis=-1, keepdims=True) + RMS_EPS)
        err = x_ref[...] + yv * r * gp_ref[...] - t_ref[...]
        dy_ref[...] = err * (1.0 / D)

        @pl.when(pl.program_id(0) == 0)
        def _():
            loss_ref[...] = jnp.zeros_like(loss_ref)

        part = jnp.sum(jnp.sum(err * err, axis=-1, keepdims=True), axis=0, keepdims=True)
        loss_ref[...] += part

    row = pl.BlockSpec((tm, D), lambda i: (i, 0))
    vec = pl.BlockSpec((1, D), lambda i: (0, 0))
    return pl.pallas_call(
        body, name=name,
        out_shape=(jax.ShapeDtypeStruct((T, D), _F32), jax.ShapeDtypeStruct((1, 1), _F32)),
        grid=(T // tm,), in_specs=[row, row, vec, row],
        out_specs=(row, pl.BlockSpec((1, 1), lambda i: (0, 0))),
        compiler_params=_params(("arbitrary",), 2 * tm * D * 16),
    )(xres, y, g_post, target)


def _norm_bwd(y, g, dout, dres, out_dtype, name):
    T, D = y.shape
    tm = _blk(T, 256)
    nsteps = T // tm
    has_res = dres is not None

    def body(*refs):
        if has_res:
            y_ref, g_ref, do_ref, dr_ref, dy_ref, dg_ref, acc = refs
        else:
            y_ref, g_ref, do_ref, dy_ref, dg_ref, acc = refs
        i = pl.program_id(0)
        yv = y_ref[...]
        do = do_ref[...]
        r = lax.rsqrt(jnp.mean(yv * yv, axis=-1, keepdims=True) + RMS_EPS)
        gy = do * g_ref[...]
        dot = jnp.mean(yv * gy, axis=-1, keepdims=True)
        dy = r * gy - yv * (r * r * r * dot)
        if has_res:
            dy = dy + dr_ref[...]
        dy_ref[...] = dy.astype(dy_ref.dtype)

        @pl.when(i == 0)
        def _():
            acc[...] = jnp.zeros_like(acc)

        acc[...] += jnp.sum((do * yv * r).reshape(tm // SUBLANES, SUBLANES, D), axis=0)

        @pl.when(i == nsteps - 1)
        def _():
            dg_ref[...] = jnp.sum(acc[...], axis=0, keepdims=True)

    row = pl.BlockSpec((tm, D), lambda i: (i, 0))
    vec = pl.BlockSpec((1, D), lambda i: (0, 0))
    ins = [y, g, dout] + ([dres] if has_res else [])
    in_specs = [row, vec, row] + ([row] if has_res else [])
    return pl.pallas_call(
        body, name=name,
        out_shape=(jax.ShapeDtypeStruct((T, D), out_dtype), jax.ShapeDtypeStruct((1, D), _F32)),
        grid=(nsteps,), in_specs=in_specs, out_specs=(row, vec),
        scratch_shapes=[pltpu.VMEM((SUBLANES, D), _F32)],
        compiler_params=_params(("arbitrary",), 2 * tm * D * 16),
    )(*ins)


def _matmul(a, b, *, ta=False, tb=False, out_dtype=_F32, epilogue=None, extra=None, name):
    M, K = (a.shape[1], a.shape[0]) if ta else a.shape
    N = b.shape[0] if tb else b.shape[1]
    tm = _blk(M, 1024)
    tn = _blk(N, 1024)
    tk = _blk(K, 1024 if (ta or K % 2048) else 2048)
    nk = K // tk
    a_spec = (pl.BlockSpec((tk, tm), lambda i, j, k: (k, i)) if ta
              else pl.BlockSpec((tm, tk), lambda i, j, k: (i, k)))
    b_spec = (pl.BlockSpec((tn, tk), lambda i, j, k: (j, k)) if tb
              else pl.BlockSpec((tk, tn), lambda i, j, k: (k, j)))
    o_spec = pl.BlockSpec((tm, tn), lambda i, j, k: (i, j))
    dims = (((0 if ta else 1,), (1 if tb else 0,)), ((), ()))
    n_extra = 1 if epilogue == "mul2" else 0
    n_out = 2 if epilogue == "relu2" else 1

    def finish(acc, extra_refs, out_refs):
        if epilogue is None:
            out_refs[0][...] = acc.astype(out_refs[0].dtype)
        elif epilogue == "relu2":
            rl = jnp.maximum(acc, 0.0)
            out_refs[0][...] = (rl * rl).astype(out_refs[0].dtype)
            out_refs[1][...] = rl.astype(out_refs[1].dtype)
        else:
            out_refs[0][...] = (acc * (2.0 * extra_refs[0][...].astype(_F32))).astype(out_refs[0].dtype)

    def body(a_ref, b_ref, *rest):
        extra_refs = rest[:n_extra]
        out_refs = rest[n_extra:n_extra + n_out]
        part = lax.dot_general(a_ref[...], b_ref[...], dims, preferred_element_type=_F32)
        if nk == 1:
            finish(part, extra_refs, out_refs)
            return
        acc = rest[-1]
        k = pl.program_id(2)

        @pl.when(k == 0)
        def _():
            acc[...] = part

        @pl.when(k > 0)
        def _():
            acc[...] += part

        @pl.when(k == nk - 1)
        def _():
            finish(acc[...], extra_refs, out_refs)

    if epilogue == "relu2":
        out_shape = (jax.ShapeDtypeStruct((M, N), _MXU), jax.ShapeDtypeStruct((M, N), _MXU))
        out_specs = (o_spec, o_spec)
        out_bytes = 2 * tm * tn * 2
    else:
        odt = _MXU if epilogue == "mul2" else out_dtype
        out_shape = jax.ShapeDtypeStruct((M, N), odt)
        out_specs = o_spec
        out_bytes = tm * tn * jnp.dtype(odt).itemsize
    in_specs = [a_spec, b_spec] + ([o_spec] if n_extra else [])
    ins = [a, b] + ([extra] if n_extra else [])
    blocks = 2 * (tm * tk * 2 + tk * tn * 2 + out_bytes + n_extra * tm * tn * 2) + tm * tn * 4 * 2
    return pl.pallas_call(
        body, name=name, out_shape=out_shape, grid=(M // tm, N // tn, nk),
        in_specs=in_specs, out_specs=out_specs,
        scratch_shapes=[pltpu.VMEM((tm, tn), _F32)] if nk > 1 else [],
        compiler_params=_params(("parallel", "parallel", "arbitrary"), blocks),
    )(*ins)


CONV_HALO = 16
CONV_CHUNK = 256


def _tap_windows(win, n_taps_plus1, tc):
    n = win.shape[0]
    for s in range(SUBLANES):
        shifted = win if s == 0 else pltpu.roll(win, n - s, 0)
        for q in range((n_taps_plus1 + SUBLANES - 1) // SUBLANES):
            o = SUBLANES * q + s
            if 1 <= o < n_taps_plus1:
                yield o, shifted[SUBLANES * q:SUBLANES * q + tc, :]


def _conv_fwd(proj, wdw, bdw, cw, name):
    T = proj.shape[0]
    ks = wdw.shape[0]
    cb = LANES
    tc = _blk(T, CONV_CHUNK)
    nblk = cw // cb

    def body(a_ref, g_ref, w_ref, b_ref, c_ref, upad):
        zeros = jnp.zeros((CONV_HALO, cb), _F32)
        upad[0:CONV_HALO, :] = zeros
        upad[T + CONV_HALO:T + 2 * CONV_HALO, :] = zeros
        upad[CONV_HALO:T + CONV_HALO, :] = a_ref[...] * _sigmoid(g_ref[...])

        def chunk(i, carry):
            t0 = pl.multiple_of(i * tc, tc)
            win = upad[pl.ds(t0, tc + 2 * CONV_HALO), :]
            acc = jnp.broadcast_to(b_ref[...], (tc, cb))
            for o, rows in _tap_windows(win, ks + 1, tc):
                j = o + ks // 2 - CONV_HALO
                acc = acc + rows * w_ref[j:j + 1, :]
            c_ref[pl.ds(t0, tc), :] = acc
            return carry

        lax.fori_loop(0, T // tc, chunk, 0)

    col = lambda off: pl.BlockSpec((T, cb), lambda i, off=off: (0, off + i))
    return pl.pallas_call(
        body, name=name, out_shape=jax.ShapeDtypeStruct((T, cw), _F32), grid=(nblk,),
        in_specs=[col(0), col(nblk), pl.BlockSpec((ks, cb), lambda i: (0, i)),
                  pl.BlockSpec((1, cb), lambda i: (0, i))],
        out_specs=pl.BlockSpec((T, cb), lambda i: (0, i)),
        scratch_shapes=[pltpu.VMEM((T + 2 * CONV_HALO, cb), _F32)],
        compiler_params=_params(("parallel",), 2 * T * cb * 4 * 3 + T * cb * 4),
    )(proj, proj, wdw, bdw)


def _ln_silu_fwd(c, lng, lnb, name):
    T, cw = c.shape
    tm = _blk(T, 512)

    def body(c_ref, g_ref, b_ref, y_ref):
        cv = c_ref[...]
        mu = jnp.mean(cv, axis=-1, keepdims=True)
        xc = cv - mu
        var = jnp.mean(xc * xc, axis=-1, keepdims=True)
        z = xc * lax.rsqrt(var + LN_EPS) * g_ref[...] + b_ref[...]
        y_ref[...] = (z * _sigmoid(z)).astype(y_ref.dtype)

    row = pl.BlockSpec((tm, cw), lambda i: (i, 0))
    vec = pl.BlockSpec((1, cw), lambda i: (0, 0))
    return pl.pallas_call(
        body, name=name, out_shape=jax.ShapeDtypeStruct((T, cw), _MXU), grid=(T // tm,),
        in_specs=[row, vec, vec], out_specs=row,
        compiler_params=_params(("parallel",), 2 * tm * cw * 6),
    )(c, lng, lnb)


def _ln_silu_bwd(c, lng, lnb, dycat, name):
    T, cw = c.shape
    tm = _blk(T, 512)
    nsteps = T // tm

    def body(c_ref, g_ref, b_ref, dy_ref, dc_ref, dg_ref, db_ref, accg, accb):
        i = pl.program_id(0)
        cv = c_ref[...]
        mu = jnp.mean(cv, axis=-1, keepdims=True)
        xc = cv - mu
        var = jnp.mean(xc * xc, axis=-1, keepdims=True)
        rstd = lax.rsqrt(var + LN_EPS)
        xhat = xc * rstd
        z = xhat * g_ref[...] + b_ref[...]
        sg = _sigmoid(z)
        dz = dy_ref[...] * (sg * (1.0 + z * (1.0 - sg)))
        dxh = dz * g_ref[...]
        m1 = jnp.mean(dxh, axis=-1, keepdims=True)
        m2 = jnp.mean(dxh * xhat, axis=-1, keepdims=True)
        dc_ref[...] = rstd * (dxh - m1 - xhat * m2)

        @pl.when(i == 0)
        def _():
            accg[...] = jnp.zeros_like(accg)
            accb[...] = jnp.zeros_like(accb)

        accg[...] += jnp.sum((dz * xhat).reshape(tm // SUBLANES, SUBLANES, cw), axis=0)
        accb[...] += jnp.sum(dz.reshape(tm // SUBLANES, SUBLANES, cw), axis=0)

        @pl.when(i == nsteps - 1)
        def _():
            dg_ref[...] = jnp.sum(accg[...], axis=0, keepdims=True)
            db_ref[...] = jnp.sum(accb[...], axis=0, keepdims=True)

    row = pl.BlockSpec((tm, cw), lambda i: (i, 0))
    vec = pl.BlockSpec((1, cw), lambda i: (0, 0))
    return pl.pallas_call(
        body, name=name,
        out_shape=(jax.ShapeDtypeStruct((T, cw), _F32), jax.ShapeDtypeStruct((1, cw), _F32),
                   jax.ShapeDtypeStruct((1, cw), _F32)),
        grid=(nsteps,), in_specs=[row, vec, vec, row], out_specs=(row, vec, vec),
        scratch_shapes=[pltpu.VMEM((SUBLANES, cw), _F32), pltpu.VMEM((SUBLANES, cw), _F32)],
        compiler_params=_params(("arbitrary",), 2 * tm * cw * 12),
    )(c, lng, lnb, dycat)


def _conv_bwd(proj, dc, wdw, cw, name):
    T = proj.shape[0]
    ks = wdw.shape[0]
    cb = LANES
    tc = _blk(T, CONV_CHUNK)
    nblk = cw // cb
    half = ks // 2

    def body(a_ref, g_ref, dc_ref, w_ref, da_ref, dg_ref, dwb_ref, upad, dpad, du, wacc):
        zeros = jnp.zeros((CONV_HALO, cb), _F32)
        for pad in (upad, dpad):
            pad[0:CONV_HALO, :] = zeros
            pad[T + CONV_HALO:T + 2 * CONV_HALO, :] = zeros
        sg = _sigmoid(g_ref[...])
        upad[CONV_HALO:T + CONV_HALO, :] = a_ref[...] * sg
        dpad[CONV_HALO:T + CONV_HALO, :] = dc_ref[...]
        wacc[...] = jnp.zeros_like(wacc)

        def chunk(i, carry):
            t0 = pl.multiple_of(i * tc, tc)
            dwin = dpad[pl.ds(t0, tc + 2 * CONV_HALO), :]
            uwin = upad[pl.ds(t0, tc + 2 * CONV_HALO), :]
            dcc = dwin[CONV_HALO:CONV_HALO + tc, :]
            acc = jnp.zeros((tc, cb), _F32)
            for o, rows in _tap_windows(dwin, CONV_HALO + half + 1, tc):
                j = CONV_HALO + half - o
                if 0 <= j < ks:
                    acc = acc + rows * w_ref[j:j + 1, :]
            du[pl.ds(t0, tc), :] = acc
            for o, rows in _tap_windows(uwin, CONV_HALO + half + 1, tc):
                j = o + half - CONV_HALO
                if 0 <= j < ks:
                    wacc[j] += jnp.sum((rows * dcc).reshape(tc // SUBLANES, SUBLANES, cb), axis=0)
            wacc[ks] += jnp.sum(dcc.reshape(tc // SUBLANES, SUBLANES, cb), axis=0)
            return carry

        lax.fori_loop(0, T // tc, chunk, 0)
        duv = du[...]
        av = a_ref[...]
        da_ref[...] = (duv * sg).astype(da_ref.dtype)
        dg_ref[...] = (duv * av * sg * (1.0 - sg)).astype(dg_ref.dtype)
        dwb_ref[...] = jnp.sum(wacc[...], axis=1)

    col = lambda off: pl.BlockSpec((T, cb), lambda i, off=off: (0, off + i))
    blk = pl.BlockSpec((T, cb), lambda i: (0, i))
    return pl.pallas_call(
        body, name=name,
        out_shape=(jax.ShapeDtypeStruct((T, cw), _MXU), jax.ShapeDtypeStruct((T, cw), _MXU),
                   jax.ShapeDtypeStruct((ks + 1, cw), _F32)),
        grid=(nblk,),
        in_specs=[col(0), col(nblk), blk, pl.BlockSpec((ks, cb), lambda i: (0, i))],
        out_specs=(blk, blk, pl.BlockSpec((ks + 1, cb), lambda i: (0, i))),
        scratch_shapes=[pltpu.VMEM((T + 2 * CONV_HALO, cb), _F32), pltpu.VMEM((T + 2 * CONV_HALO, cb), _F32),
                        pltpu.VMEM((T, cb), _F32), pltpu.VMEM((ks + 1, SUBLANES, cb), _F32)],
        compiler_params=_params(("parallel",), 2 * T * cb * 4 * 4 + 3 * T * cb * 4),
    )(proj, proj, dc, wdw)


N_CLS = WIN_ROWS
N_DR = 2 * WIN_ROWS - 1
N_DC = 2 * WIN_COLS - 1
BAND = WIN_ROWS * GRID_W


def _pair_iotas():
    wq = lax.broadcasted_iota(jnp.int32, (GRID_W, LANES), 0)
    lane = lax.broadcasted_iota(jnp.int32, (GRID_W, LANES), 1)
    wk = jnp.bitwise_and(lane, GRID_W - 1)
    left = lane < GRID_W
    d = wk - wq + (WIN_COLS - 1)
    cs = jnp.clip(wq - WIN_COLS // 2, 0, GRID_W - WIN_COLS)
    window = (wk >= cs) & (wk < cs + WIN_COLS)
    return d, left, window


def _bias_table(rpb2, name):
    H = rpb2.shape[0]

    def body(rpb_ref, o_ref):
        h = pl.program_id(0)
        d, left, window = _pair_iotas()
        for e in range(N_DR - 1):
            val = jnp.zeros((GRID_W, LANES), _F32)
            for j in range(N_DC):
                sl = rpb_ref[h, e * N_DC + j]
                sr = rpb_ref[h, (e + 1) * N_DC + j]
                val = jnp.where(d == j, jnp.where(left, sl, sr), val)
            slab = jnp.where(window, val, NEG_INF)
            for cls in range(N_CLS):
                p2 = e - cls
                if p2 >= 0 and p2 % 2 == 0 and p2 // 2 < WIN_ROWS // 2:
                    o_ref[cls, :, (p2 // 2) * LANES:(p2 // 2 + 1) * LANES] = slab

    return pl.pallas_call(
        body, name=name, out_shape=jax.ShapeDtypeStruct((H, N_CLS, GRID_W, BAND), _F32), grid=(H,),
        in_specs=[pl.BlockSpec(memory_space=pltpu.SMEM)],
        out_specs=pl.BlockSpec((None, N_CLS, GRID_W, BAND), lambda h: (h, 0, 0, 0)),
        compiler_params=_params(("arbitrary",), 2 * N_CLS * GRID_W * BAND * 4),
    )(rpb2)


def _rpb_grad(gc, name):
    H = gc.shape[0]

    def body(g_ref, o_ref):
        d, left, _ = _pair_iotas()
        rowi = lax.broadcasted_iota(jnp.int32, (4 * SUBLANES, LANES), 0)
        lanei = lax.broadcasted_iota(jnp.int32, (4 * SUBLANES, LANES), 1)
        left1 = lax.broadcasted_iota(jnp.int32, (1, LANES), 1) < GRID_W
        tile = jnp.zeros((4 * SUBLANES, LANES), _F32)
        for e in range(N_DR - 1):
            ysum = jnp.zeros((GRID_W, LANES), _F32)
            for cls in range(N_CLS):
                p2 = e - cls
                if p2 >= 0 and p2 % 2 == 0 and p2 // 2 < WIN_ROWS // 2:
                    ysum = ysum + g_ref[cls, :, (p2 // 2) * LANES:(p2 // 2 + 1) * LANES]
            for j in range(N_DC):
                cs = jnp.sum(jnp.where(d == j, ysum, 0.0), axis=0, keepdims=True)
                sl = jnp.sum(jnp.where(left1, cs, 0.0), axis=1, keepdims=True)
                sr = jnp.sum(jnp.where(left1, 0.0, cs), axis=1, keepdims=True)
                tile = tile + jnp.where((rowi == j) & (lanei == e), sl, 0.0)
                tile = tile + jnp.where((rowi == j) & (lanei == e + 1), sr, 0.0)
        o_ref[...] = tile

    return pl.pallas_call(
        body, name=name, out_shape=jax.ShapeDtypeStruct((H, 4 * SUBLANES, LANES), _F32), grid=(H,),
        in_specs=[pl.BlockSpec((None, N_CLS, GRID_W, BAND), lambda h: (h, 0, 0, 0))],
        out_specs=pl.BlockSpec((None, 4 * SUBLANES, LANES), lambda h: (h, 0, 0)),
        compiler_params=_params(("parallel",), 2 * N_CLS * GRID_W * BAND * 4),
    )(gc)


def _attn_probs(q, kband, bias_ref, cls, m0):
    nt = (((1,), (1,)), ((), ()))
    out = []
    for hh in range(2):
        qm = jnp.where(m0 if hh == 0 else ~m0, q, 0.0).astype(_MXU)
        s = lax.dot_general(qm, kband, nt, preferred_element_type=_F32) * (HEAD_DIM ** -0.5)
        s = s + bias_ref[hh, cls]
        mx = jnp.max(s, axis=-1, keepdims=True)
        p = jnp.exp(s - mx)
        out.append((qm, p / jnp.sum(p, axis=-1, keepdims=True)))
    return out


def _attn_fwd(proj, bias, cw, naw, name):
    T = proj.shape[0]
    rows = T // GRID_W
    npair = naw // LANES
    qoff, koff, voff = 2 * cw // LANES, (2 * cw + naw) // LANES, (2 * cw + 2 * naw) // LANES

    def body(q_ref, k_ref, v_ref, b_ref, o_ref, kb, vb):
        kb[...] = k_ref[...].astype(_MXU)
        vb[...] = v_ref[...].astype(_MXU)
        m0 = lax.broadcasted_iota(jnp.int32, (GRID_W, LANES), 1) < HEAD_DIM

        def step(r, carry):
            rs = jnp.clip(r - WIN_ROWS // 2, 0, rows - WIN_ROWS)
            cls = rs - r + (WIN_ROWS - 1)
            t0 = pl.multiple_of(r * GRID_W, GRID_W)
            b0 = pl.multiple_of(rs * GRID_W, GRID_W)
            q = q_ref[pl.ds(t0, GRID_W), :]
            kband = kb[pl.ds(b0, BAND), :]
            vband = vb[pl.ds(b0, BAND), :]
            (_, p0), (_, p1) = _attn_probs(q, kband, b_ref, cls, m0)
            o0 = jnp.dot(p0.astype(_MXU), vband, preferred_element_type=_F32)
            o1 = jnp.dot(p1.astype(_MXU), vband, preferred_element_type=_F32)
            o_ref[pl.ds(t0, GRID_W), :] = jnp.where(m0, o0, o1).astype(o_ref.dtype)
            return carry

        lax.fori_loop(0, rows, step, 0)

    col = lambda off: pl.BlockSpec((T, LANES), lambda i, off=off: (0, off + i))
    return pl.pallas_call(
        body, name=name, out_shape=jax.ShapeDtypeStruct((T, naw), _MXU), grid=(npair,),
        in_specs=[col(qoff), col(koff), col(voff),
                  pl.BlockSpec((2, N_CLS, GRID_W, BAND), lambda i: (i, 0, 0, 0))],
        out_specs=pl.BlockSpec((T, LANES), lambda i: (0, i)),
        scratch_shapes=[pltpu.VMEM((T, LANES), _MXU), pltpu.VMEM((T, LANES), _MXU)],
        compiler_params=_params(("parallel",), 2 * (3 * T * LANES * 4 + 2 * N_CLS * GRID_W * BAND * 4 + T * LANES * 2)),
    )(proj, proj, proj, bias)


def _attn_bwd(proj, bias, dycat, cw, naw, name):
    T = proj.shape[0]
    rows = T // GRID_W
    npair = naw // LANES
    qoff, koff, voff = 2 * cw // LANES, (2 * cw + naw) // LANES, (2 * cw + 2 * naw) // LANES
    doff = cw // LANES
    nt = (((1,), (1,)), ((), ()))
    tn = (((0,), (0,)), ((), ()))
    scale = HEAD_DIM ** -0.5

    def body(q_ref, k_ref, v_ref, b_ref, do_ref, dq_ref, dk_ref, dv_ref, g_ref, kb, vb, dka, dva):
        kb[...] = k_ref[...].astype(_MXU)
        vb[...] = v_ref[...].astype(_MXU)
        dka[...] = jnp.zeros_like(dka)
        dva[...] = jnp.zeros_like(dva)
        g_ref[...] = jnp.zeros_like(g_ref)
        m0 = lax.broadcasted_iota(jnp.int32, (GRID_W, LANES), 1) < HEAD_DIM

        def step(r, carry):
            rs = jnp.clip(r - WIN_ROWS // 2, 0, rows - WIN_ROWS)
            cls = rs - r + (WIN_ROWS - 1)
            t0 = pl.multiple_of(r * GRID_W, GRID_W)
            b0 = pl.multiple_of(rs * GRID_W, GRID_W)
            q = q_ref[pl.ds(t0, GRID_W), :]
            do = do_ref[pl.ds(t0, GRID_W), :]
            kband = kb[pl.ds(b0, BAND), :]
            vband = vb[pl.ds(b0, BAND), :]
            probs = _attn_probs(q, kband, b_ref, cls, m0)
            pb = [p.astype(_MXU) for _, p in probs]
            o = jnp.where(m0, jnp.dot(pb[0], vband, preferred_element_type=_F32),
                          jnp.dot(pb[1], vband, preferred_element_type=_F32))
            prod = do * o
            dqs = []
            dk_band = jnp.zeros((BAND, LANES), _F32)
            dv_band = jnp.zeros((BAND, LANES), _F32)
            for hh in range(2):
                mh = m0 if hh == 0 else ~m0
                qm, p = probs[hh]
                dom = jnp.where(mh, do, 0.0).astype(_MXU)
                delta = jnp.sum(jnp.where(mh, prod, 0.0), axis=-1, keepdims=True)
                dp = lax.dot_general(dom, vband, nt, preferred_element_type=_F32)
                ds = p * (dp - delta)
                g_ref[hh, cls] += ds
                dsb = ds.astype(_MXU)
                dqs.append(jnp.dot(dsb, kband, preferred_element_type=_F32))
                dk_band = dk_band + lax.dot_general(dsb, qm, tn, preferred_element_type=_F32)
                dv_band = dv_band + lax.dot_general(pb[hh], dom, tn, preferred_element_type=_F32)
            dq_ref[pl.ds(t0, GRID_W), :] = (jnp.where(m0, dqs[0], dqs[1]) * scale).astype(dq_ref.dtype)
            dka[pl.ds(b0, BAND), :] += dk_band * scale
            dva[pl.ds(b0, BAND), :] += dv_band
            return carry

        lax.fori_loop(0, rows, step, 0)
        dk_ref[...] = dka[...].astype(dk_ref.dtype)
        dv_ref[...] = dva[...].astype(dv_ref.dtype)

    col = lambda off: pl.BlockSpec((T, LANES), lambda i, off=off: (0, off + i))
    blk = pl.BlockSpec((T, LANES), lambda i: (0, i))
    tbl = pl.BlockSpec((2, N_CLS, GRID_W, BAND), lambda i: (i, 0, 0, 0))
    H = 2 * npair
    o16 = jax.ShapeDtypeStruct((T, naw), _MXU)
    vm = 2 * (4 * T * LANES * 4 + 2 * 2 * N_CLS * GRID_W * BAND * 4 + 3 * T * LANES * 2) + 2 * T * LANES * 6
    return pl.pallas_call(
        body, name=name,
        out_shape=(o16, o16, o16, jax.ShapeDtypeStruct((H, N_CLS, GRID_W, BAND), _F32)),
        grid=(npair,),
        in_specs=[col(qoff), col(koff), col(voff), tbl, col(doff)],
        out_specs=(blk, blk, blk, tbl),
        scratch_shapes=[pltpu.VMEM((T, LANES), _MXU), pltpu.VMEM((T, LANES), _MXU),
                        pltpu.VMEM((T, LANES), _F32), pltpu.VMEM((T, LANES), _F32)],
        compiler_params=_params(("parallel",), vm),
    )(proj, proj, proj, bias, dycat)


_ANY = pl.BlockSpec(memory_space=pl.ANY)
_MESH_ID = pl.DeviceIdType.MESH


def _shard_ref(ref, axis, j, width):
    idx = [slice(None)] * len(ref.shape)
    idx[axis] = pl.ds(pl.multiple_of(j * width, math.gcd(width, LANES)), width)
    return ref.at[tuple(idx)]


def _all_gather(shards, axes, name):
    n = len(shards)
    widths = [s.shape[a] for s, a in zip(shards, axes)]
    out_shape = [jax.ShapeDtypeStruct(tuple(N_DEV * d if k == a else d for k, d in enumerate(s.shape)), s.dtype)
                 for s, a in zip(shards, axes)]

    def body(*refs):
        ins, outs = refs[:n], refs[n:2 * n]
        send_sems, recv_sems, local_sems = refs[2 * n:]
        x, y, c = (lax.axis_index(a) for a in MESH_AXES)
        me, sibling = (x, y, c), (x, y, 1 - c)
        chips = [(1 - x, y), (x, 1 - y), (1 - x, 1 - y)]

        def slot(i, px, py, pc):
            return _shard_ref(outs[i], axes[i], 4 * px + 2 * py + pc, widths[i])

        def copy(i, k, block, to, src=None):
            return pltpu.make_async_remote_copy(
                src_ref=slot(i, *block) if src is None else src, dst_ref=slot(i, *block),
                send_sem=send_sems.at[7 * i + k], recv_sem=recv_sems.at[7 * i + k],
                device_id=to, device_id_type=_MESH_ID)

        mine = [pltpu.make_async_copy(ins[i], slot(i, *me), local_sems.at[i]) for i in range(n)]
        for cp in mine:
            cp.start()
        first = []
        for i in range(n):
            first.append(copy(i, 0, me, sibling, src=ins[i]))
            first += [copy(i, 1 + j, me, (*chip, c), src=ins[i]) for j, chip in enumerate(chips)]
        for cp in first:
            cp.start()
        passed = []
        for j, chip in enumerate(chips):
            for i in range(n):
                copy(i, 1 + j, (*chip, c), me).wait_recv()
                fwd = copy(i, 4 + j, (*chip, c), sibling)
                fwd.start()
                passed.append(fwd)
        for i in range(n):
            copy(i, 0, sibling, me).wait_recv()
            for j, chip in enumerate(chips):
                copy(i, 4 + j, (*chip, 1 - c), me).wait_recv()
        for cp in first + passed:
            cp.wait_send()
        for cp in mine:
            cp.wait()

    return pl.pallas_call(
        body, name=name, out_shape=out_shape, in_specs=[_ANY] * n, out_specs=[_ANY] * n,
        scratch_shapes=[pltpu.SemaphoreType.DMA((7 * n,)), pltpu.SemaphoreType.DMA((7 * n,)),
                        pltpu.SemaphoreType.DMA((n,))],
    )(*shards)


def _rs_sibling(fulls, axes, name):
    n = len(fulls)
    widths = [f.shape[a] // N_DEV for f, a in zip(fulls, axes)]
    shard_shapes = [tuple(w if k == a else d for k, d in enumerate(f.shape)) for f, a, w in zip(fulls, axes, widths)]
    out_shape = [jax.ShapeDtypeStruct((4,) + s, f.dtype) for s, f in zip(shard_shapes, fulls)] * 2

    def body(*refs):
        ins, owns, gots = refs[:n], refs[n:2 * n], refs[2 * n:3 * n]
        send_sems, recv_sems, local_sems = refs[3 * n:]
        x, y, c = (lax.axis_index(a) for a in MESH_AXES)
        sibling = (x, y, 1 - c)
        local, remote = [], []
        for i in range(n):
            for q in range(4):
                local.append(pltpu.make_async_copy(
                    _shard_ref(ins[i], axes[i], 2 * q + c, widths[i]), owns[i].at[q], local_sems.at[4 * i + q]))
                remote.append(pltpu.make_async_remote_copy(
                    src_ref=_shard_ref(ins[i], axes[i], 2 * q + (1 - c), widths[i]), dst_ref=gots[i].at[q],
                    send_sem=send_sems.at[4 * i + q], recv_sem=recv_sems.at[4 * i + q],
                    device_id=sibling, device_id_type=_MESH_ID))
        for cp in remote + local:
            cp.start()
        for cp in remote:
            cp.wait_recv()
        for cp in remote:
            cp.wait_send()
        for cp in local:
            cp.wait()

    return pl.pallas_call(
        body, name=name, out_shape=out_shape, in_specs=[_ANY] * n, out_specs=[_ANY] * (2 * n),
        scratch_shapes=[pltpu.SemaphoreType.DMA((4 * n,)), pltpu.SemaphoreType.DMA((4 * n,)),
                        pltpu.SemaphoreType.DMA((4 * n,))],
    )(*fulls)


def _rs_chips(parts, name):
    n = len(parts)
    out_shape = [jax.ShapeDtypeStruct(p.shape, p.dtype) for p in parts]

    def body(*refs):
        ins, outs = refs[:n], refs[n:2 * n]
        send_sems, recv_sems, local_sems = refs[2 * n:]
        x, y, c = (lax.axis_index(a) for a in MESH_AXES)
        mine = 2 * x + y
        chips = [(1 - x, y), (x, 1 - y), (1 - x, 1 - y)]
        local, remote = [], []
        for i in range(n):
            local.append(pltpu.make_async_copy(ins[i].at[mine], outs[i].at[mine], local_sems.at[i]))
            for k, (tx, ty) in enumerate(chips):
                remote.append((pltpu.make_async_remote_copy(
                    src_ref=ins[i].at[2 * tx + ty], dst_ref=outs[i].at[mine],
                    send_sem=send_sems.at[3 * i + k], recv_sem=recv_sems.at[3 * i + k],
                    device_id=(tx, ty, c), device_id_type=_MESH_ID),
                    pltpu.make_async_remote_copy(
                    src_ref=ins[i].at[mine], dst_ref=outs[i].at[2 * tx + ty],
                    send_sem=send_sems.at[3 * i + k], recv_sem=recv_sems.at[3 * i + k],
                    device_id=(tx, ty, c), device_id_type=_MESH_ID)))
        for cp in local:
            cp.start()
        for send, _ in remote:
            send.start()
        for _, recv in remote:
            recv.wait_recv()
        for send, _ in remote:
            send.wait_send()
        for cp in local:
            cp.wait()

    return pl.pallas_call(
        body, name=name, out_shape=out_shape, in_specs=[_ANY] * n, out_specs=[_ANY] * n,
        scratch_shapes=[pltpu.SemaphoreType.DMA((3 * n,)), pltpu.SemaphoreType.DMA((3 * n,)),
                        pltpu.SemaphoreType.DMA((n,))],
    )(*parts)


def _add_pairs(own, got, name):
    _, R, C = own.shape
    tr = _blk(R, 512)

    def body(a_ref, b_ref, o_ref):
        o_ref[...] = (a_ref[...].astype(_F32) + b_ref[...].astype(_F32)).astype(o_ref.dtype)

    spec = pl.BlockSpec((None, tr, C), lambda q, i: (q, i, 0))
    return pl.pallas_call(
        body, name=name, out_shape=jax.ShapeDtypeStruct(own.shape, own.dtype), grid=(4, R // tr),
        in_specs=[spec, spec], out_specs=spec,
        compiler_params=_params(("parallel", "parallel"), 2 * tr * C * 6),
    )(own, got)


def _adamw(g, w, m, v):
    m = ADAM_B1 * m + (1.0 - ADAM_B1) * g
    v = ADAM_B2 * v + (1.0 - ADAM_B2) * (g * g)
    m_hat = m / (1.0 - ADAM_B1 ** ADAM_STEP)
    v_hat = v / (1.0 - ADAM_B2 ** ADAM_STEP)
    delta = -ADAM_LR * (m_hat / (jnp.sqrt(v_hat) + ADAM_EPS) + ADAM_WD * w)
    return delta, m, v


def _adam_layer(fin, w3, m3, v3, l, prev, name):
    L, R, C = w3.shape
    tr = _blk(R, max(SUBLANES, (1 << 18) // C))

    def body(f_ref, w_ref, m_ref, v_ref, *rest):
        g_ref, d_ref, nm_ref, nv_ref = rest[-4:]
        g = ((f_ref[0].astype(_F32) + f_ref[1].astype(_F32)) + f_ref[2].astype(_F32)) + f_ref[3].astype(_F32)
        d, nm, nv = _adamw(g, w_ref[...], m_ref[...], v_ref[...])
        g_ref[...] = g
        d_ref[...] = d
        nm_ref[...] = nm
        nv_ref[...] = nv

    lay = pl.BlockSpec((None, tr, C), lambda i: (l, i, 0))
    ins = [fin, w3, m3, v3]
    in_specs = [pl.BlockSpec((4, tr, C), lambda i: (0, i, 0)), lay, lay, lay]
    aliases = {}
    if prev is not None:
        ins += list(prev)
        in_specs += [_ANY] * 4
        aliases = {4 + k: k for k in range(4)}
    return pl.pallas_call(
        body, name=name, out_shape=[jax.ShapeDtypeStruct((L, R, C), _F32)] * 4, grid=(R // tr,),
        in_specs=in_specs, out_specs=[lay] * 4, input_output_aliases=aliases,
        compiler_params=_params(("parallel",), 2 * tr * C * (4 * 2 + 7 * 4)),
    )(*ins)


def _sum_parts(parts, name):
    _, R, C = parts.shape

    def body(p_ref, o_ref):
        acc = p_ref[0]
        for k in range(1, N_DEV):
            acc = acc + p_ref[k]
        o_ref[...] = acc

    tr = _blk(R, 512)
    return pl.pallas_call(
        body, name=name, out_shape=jax.ShapeDtypeStruct((R, C), _F32), grid=(R // tr,),
        in_specs=[pl.BlockSpec((N_DEV, tr, C), lambda i: (0, i, 0))],
        out_specs=pl.BlockSpec((tr, C), lambda i: (i, 0)),
        compiler_params=_params(("parallel",), 2 * tr * C * 4 * 9),
    )(parts)


def _adam_flat(g, w, m, v, name):
    R, C = g.shape
    tr = _blk(R, 512)

    def body(g_ref, w_ref, m_ref, v_ref, d_ref, nm_ref, nv_ref):
        d, nm, nv = _adamw(g_ref[...], w_ref[...], m_ref[...], v_ref[...])
        d_ref[...] = d
        nm_ref[...] = nm
        nv_ref[...] = nv

    spec = pl.BlockSpec((tr, C), lambda i: (i, 0))
    return pl.pallas_call(
        body, name=name, out_shape=[jax.ShapeDtypeStruct((R, C), _F32)] * 3, grid=(R // tr,),
        in_specs=[spec] * 4, out_specs=[spec] * 3,
        compiler_params=_params(("parallel",), 2 * tr * C * 4 * 7),
    )(g, w, m, v)


def _pack(arrays):
    flat = jnp.concatenate([a.reshape(-1) for a in arrays])
    tile = SUBLANES * LANES
    pad = (-flat.shape[0]) % tile
    return jnp.pad(flat, (0, pad)).reshape(-1, LANES)


def _unpack(packed, shapes):
    flat = packed.reshape(-1)
    out, off = [], 0
    for s in shapes:
        n = math.prod(s)
        out.append(flat[off:off + n].reshape(s))
        off += n
    return out


def kernel(x, w_in, w_dw, b_dw, conv_ln_g, conv_ln_b, rpb, w_out, w_up, w_down, pre_mix_g, post_mix_g, pre_mlp_g, post_mlp_g, loss_target, m_w_in, m_w_dw, m_b_dw, m_conv_ln_g, m_conv_ln_b, m_rpb, m_w_out, m_w_up, m_w_down, m_pre_mix_g, m_post_mix_g, m_pre_mlp_g, m_post_mlp_g, v_w_in, v_w_dw, v_b_dw, v_conv_ln_g, v_conv_ln_b, v_rpb, v_w_out, v_w_up, v_w_down, v_pre_mix_g, v_post_mix_g, v_pre_mlp_g, v_post_mlp_g):
    _, T, D = x.shape
    L = w_in.shape[0]
    cw = b_dw.shape[1]
    H = rpb.shape[1]
    naw = H * HEAD_DIM
    ks = w_dw.shape[1]
    assert T % GRID_W == 0 and T // GRID_W >= WIN_ROWS and H % 2 == 0 and cw % LANES == 0
    assert rpb.shape[2:] == (N_DR, N_DC) and w_dw.shape[2] * N_DEV == cw and ks // 2 < CONV_HALO

    xs = x.reshape(T, D)
    tgt = loss_target.reshape(T, D)
    row = lambda p, l: p[l:l + 1]

    ks_pad = ks + (-ks) % SUBLANES
    wdw_pad = jnp.pad(w_dw, ((0, 0), (0, ks_pad - ks), (0, 0))).reshape(L * ks_pad, w_dw.shape[2])
    wdw_full = _all_gather([wdw_pad], [1], "ag_wdw")[0].reshape(L, ks_pad, cw)[:, :ks]

    big = (w_in, w_out, w_up, w_down)
    big_axes = [1, 0, 1, 0]
    saved = []
    xin = xs
    h = _norm_fwd(xs, row(pre_mix_g, 0), "norm_first")
    dy = loss_sum = None
    for l in range(L):
        shards = [_cast_layer(w, l, f"cast_{nm}") for w, nm in zip(big, ("in", "out", "up", "down"))]
        Win, Wout, Wup, Wdown = _all_gather(shards, big_axes, "ag_layer")
        proj = _matmul(h, Win, name="mm_proj")
        c = _conv_fwd(proj, wdw_full[l], row(b_dw, l), cw, "conv_fwd")
        yc = _ln_silu_fwd(c, row(conv_ln_g, l), row(conv_ln_b, l), "ln_silu_fwd")
        bias = _bias_table(rpb[l].reshape(H, N_DR * N_DC), "bias_table")
        ya = _attn_fwd(proj, bias, cw, naw, "attn_fwd")
        ycat = jnp.concatenate([yc, ya], axis=1)
        mix = _matmul(ycat, Wout, name="mm_mix")
        x1, h2 = _resid_norm_fwd(xin, mix, row(post_mix_g, l), row(pre_mlp_g, l), "resid_mix")
        act, rl = _matmul(h2, Wup, epilogue="relu2", name="mm_up")
        f = _matmul(act, Wdown, name="mm_down")
        saved.append(dict(xin=xin, h=h, W=(Win, Wout, Wup, Wdown), proj=proj, c=c, bias=bias, ycat=ycat,
                          mix=mix, x1=x1, h2=h2, act=act, rl=rl, f=f))
        if l + 1 < L:
            xin, h = _resid_norm_fwd(x1, f, row(post_mlp_g, l), row(pre_mix_g, l + 1), "resid_mlp")
        else:
            dy, loss_sum = _resid_loss(x1, f, row(post_mlp_g, l), tgt, "resid_loss")

    loss = lax.psum(loss_sum[0, 0] * (0.5 / D), MESH_AXES)

    small_grads = [None] * L
    big_out = [None] * 4
    moments = ((m_w_in, v_w_in), (m_w_out, v_w_out), (m_w_up, v_w_up), (m_w_down, v_w_down))
    dxo = dy
    for l in reversed(range(L)):
        s = saved[l]
        Win, Wout, Wup, Wdown = s["W"]
        d_f, dg_post_mlp = _norm_bwd(s["f"], row(post_mlp_g, l), dxo, None, _MXU, "norm_bwd_mlp")
        d_up = _matmul(d_f, Wdown, tb=True, epilogue="mul2", extra=s["rl"], name="mm_d_up")
        dWdown = _matmul(s["act"], d_f, ta=True, out_dtype=_WIRE, name="mm_dw_down")
        d_h2 = _matmul(d_up, Wup, tb=True, name="mm_d_h2")
        dWup = _matmul(s["h2"], d_up, ta=True, out_dtype=_WIRE, name="mm_dw_up")
        dx1, dg_pre_mlp = _norm_bwd(s["x1"], row(pre_mlp_g, l), d_h2, dxo, _F32, "norm_bwd_premlp")
        d_mix, dg_post_mix = _norm_bwd(s["mix"], row(post_mix_g, l), dx1, None, _MXU, "norm_bwd_mix")
        d_ycat = _matmul(d_mix, Wout, tb=True, name="mm_d_ycat")
        dWout = _matmul(s["ycat"], d_mix, ta=True, out_dtype=_WIRE, name="mm_dw_out")
        dc, dlng, dlnb = _ln_silu_bwd(s["c"], row(conv_ln_g, l), row(conv_ln_b, l), d_ycat, "ln_silu_bwd")
        da, dgate, dwb = _conv_bwd(s["proj"], dc, wdw_full[l], cw, "conv_bwd")
        dq, dk, dv, gcls = _attn_bwd(s["proj"], s["bias"], d_ycat, cw, naw, "attn_bwd")
        drpb = _rpb_grad(gcls, "rpb_grad")[:, :N_DC, :N_DR].transpose(0, 2, 1)
        dproj = jnp.concatenate([da, dgate, dq, dk, dv], axis=1)
        dh = _matmul(dproj, Win, tb=True, name="mm_d_h")
        dWin = _matmul(s["h"], dproj, ta=True, out_dtype=_WIRE, name="mm_dw_in")
        dxo, dg_pre_mix = _norm_bwd(s["xin"], row(pre_mix_g, l), dh, dx1, _F32, "norm_bwd_premix")

        outs = _rs_sibling([dWin, dWout, dWup, dWdown], big_axes, "rs_sibling")
        partial = [_add_pairs(o, g, f"rs_add_{k}") for k, (o, g) in enumerate(zip(outs[:4], outs[4:]))]
        fins = _rs_chips(partial, "rs_chips")
        for k in range(4):
            big_out[k] = _adam_layer(fins[k], big[k], moments[k][0], moments[k][1], l, big_out[k], f"adam_{k}_{l}")
        small_grads[l] = [dwb[ks], dlng[0], dlnb[0], drpb, dg_pre_mix[0], dg_post_mix[0], dg_pre_mlp[0],
                          dg_post_mlp[0], dwb[:ks]]

    rep_shapes = [(L, cw), (L, cw), (L, cw), (L, H, N_DR, N_DC), (L, D), (L, D), (L, D), (L, D)]
    stacked = [jnp.stack([small_grads[l][k] for l in range(L)]) for k in range(9)]
    n_rep = sum(math.prod(s) for s in rep_shapes)
    packed = _pack(stacked)
    parts = _all_gather([packed], [0], "ag_small")[0].reshape(N_DEV, *packed.shape)
    gsum = _sum_parts(parts, "sum_small")
    g_small = _unpack(gsum, rep_shapes + [(L, ks, cw)])
    g_rep, g_wdw_full = g_small[:8], g_small[8]
    dev = 4 * lax.axis_index("x") + 2 * lax.axis_index("y") + lax.axis_index("c")
    wsh = w_dw.shape[2]
    g_wdw = lax.dynamic_slice_in_dim(g_wdw_full, dev * wsh, wsh, axis=2)

    rep_w = [b_dw, conv_ln_g, conv_ln_b, rpb, pre_mix_g, post_mix_g, pre_mlp_g, post_mlp_g]
    rep_m = [m_b_dw, m_conv_ln_g, m_conv_ln_b, m_rpb, m_pre_mix_g, m_post_mix_g, m_pre_mlp_g, m_post_mlp_g]
    rep_v = [v_b_dw, v_conv_ln_g, v_conv_ln_b, v_rpb, v_pre_mix_g, v_post_mix_g, v_pre_mlp_g, v_post_mlp_g]
    rep_pack = _pack(g_rep)
    rep_out = _adam_flat(rep_pack, _pack(rep_w), _pack(rep_m), _pack(rep_v), "adam_small")
    rep_delta, rep_nm, rep_nv = (_unpack(o, rep_shapes) for o in rep_out)
    del n_rep
    dw_out = _adam_flat(_pack([g_wdw]), _pack([w_dw]), _pack([m_w_dw]), _pack([v_w_dw]), "adam_wdw")
    wdw_delta, wdw_nm, wdw_nv = (_unpack(o, [w_dw.shape])[0] for o in dw_out)

    def assemble(kind_big, rep_list, wdw_val):
        return [big_out[0][kind_big], wdw_val, rep_list[0], rep_list[1], rep_list[2], rep_list[3],
                big_out[1][kind_big], big_out[2][kind_big], big_out[3][kind_big],
                rep_list[4], rep_list[5], rep_list[6], rep_list[7]]

    grads = assemble(0, g_rep, g_wdw)
    deltas = assemble(1, rep_delta, wdw_delta)
    new_m = assemble(2, rep_nm, wdw_nm)
    new_v = assemble(3, rep_nv, wdw_nv)
    return (loss, dxo.reshape(1, T, D), *grads, *deltas, *new_m, *new_v)
```

```python
import math

import jax
import jax.numpy as jnp
from jax import lax
from jax.experimental import pallas as pl
from jax.experimental.pallas import tpu as pltpu

_MXU = jnp.bfloat16
_WIRE = jnp.bfloat16
_F32 = jnp.float32

N_DEV = 8
GRID_W = 64
WIN_ROWS = 8
WIN_COLS = 16
HEAD_DIM = 64
LANES = 128
SUBLANES = 8
RMS_EPS = 1e-6
LN_EPS = 1e-5
NEG_INF = -1e30
ADAM_LR = 0.001
ADAM_B1 = 0.9
ADAM_B2 = 0.999
ADAM_EPS = 1e-08
ADAM_WD = 0.01
ADAM_STEP = 10
VMEM_BYTES_V7X = 64 << 20
VMEM_RESERVE = 12 << 20
MESH_AXES = ("x", "y", "c")


def _vmem_limit(block_bytes):
    return int(min(max(block_bytes + (8 << 20), 24 << 20), VMEM_BYTES_V7X - VMEM_RESERVE))


def _blk(n, pref):
    if n <= pref:
        return n
    for t in range(pref, 7, -1):
        if n % t == 0 and t % SUBLANES == 0:
            return t
    return n


def _sigmoid(v):
    return 1.0 / (1.0 + jnp.exp(-v))


def _params(sem, nbytes):
    return pltpu.CompilerParams(dimension_semantics=sem, vmem_limit_bytes=_vmem_limit(nbytes))


def _cast_slot(w3, l, dev, name):
    _, R, C = w3.shape
    tr = _blk(R, 512)

    def body(dev_ref, w_ref, o_ref):
        o_ref[...] = w_ref[...].astype(o_ref.dtype)

    grid_spec = pltpu.PrefetchScalarGridSpec(
        num_scalar_prefetch=1, grid=(R // tr,),
        in_specs=[pl.BlockSpec((None, tr, C), lambda i, d: (l, i, 0))],
        out_specs=pl.BlockSpec((None, tr, C), lambda i, d: (d[0], i, 0)))
    return pl.pallas_call(
        body, name=name, out_shape=jax.ShapeDtypeStruct((N_DEV, R, C), _WIRE), grid_spec=grid_spec,
        compiler_params=_params(("parallel",), 2 * tr * C * 6),
    )(dev, w3)


def _norm_fwd(x, g, name):
    T, D = x.shape
    tm = _blk(T, 256)

    def body(x_ref, g_ref, h_ref):
        xv = x_ref[...]
        r = lax.rsqrt(jnp.mean(xv * xv, axis=-1, keepdims=True) + RMS_EPS)
        h_ref[...] = (xv * r * g_ref[...]).astype(h_ref.dtype)

    return pl.pallas_call(
        body, name=name, out_shape=jax.ShapeDtypeStruct((T, D), _MXU), grid=(T // tm,),
        in_specs=[pl.BlockSpec((tm, D), lambda i: (i, 0)), pl.BlockSpec((1, D), lambda i: (0, 0))],
        out_specs=pl.BlockSpec((tm, D), lambda i: (i, 0)),
        compiler_params=_params(("parallel",), 2 * tm * D * 6),
    )(x, g)


def _resid_norm_fwd(xres, y, g_post, g_next, name):
    T, D = xres.shape
    tm = _blk(T, 256)

    def body(x_ref, y_ref, gp_ref, gn_ref, xn_ref, h_ref):
        yv = y_ref[...]
        r = lax.rsqrt(jnp.mean(yv * yv, axis=-1, keepdims=True) + RMS_EPS)
        xn = x_ref[...] + yv * r * gp_ref[...]
        xn_ref[...] = xn
        r2 = lax.rsqrt(jnp.mean(xn * xn, axis=-1, keepdims=True) + RMS_EPS)
        h_ref[...] = (xn * r2 * gn_ref[...]).astype(h_ref.dtype)

    row = pl.BlockSpec((tm, D), lambda i: (i, 0))
    vec = pl.BlockSpec((1, D), lambda i: (0, 0))
    return pl.pallas_call(
        body, name=name,
        out_shape=(jax.ShapeDtypeStruct((T, D), _F32), jax.ShapeDtypeStruct((T, D), _MXU)),
        grid=(T // tm,), in_specs=[row, row, vec, vec], out_specs=(row, row),
        compiler_params=_params(("parallel",), 2 * tm * D * 14),
    )(xres, y, g_post, g_next)


def _resid_loss(xres, y, g_post, target, name):
    T, D = xres.shape
    tm = _blk(T, 256)

    def body(x_ref, y_ref, gp_ref, t_ref, dy_ref, loss_ref):
        yv = y_ref[...]
        r = lax.rsqrt(jnp.mean(yv * yv, axis=-1, keepdims=True) + RMS_EPS)
        err = x_ref[...] + yv * r * gp_ref[...] - t_ref[...]
        dy_ref[...] = err * (1.0 / D)

        @pl.when(pl.program_id(0) == 0)
        def _():
            loss_ref[...] = jnp.zeros_like(loss_ref)

        part = jnp.sum(jnp.sum(err * err, axis=-1, keepdims=True), axis=0, keepdims=True)
        loss_ref[...] += part

    row = pl.BlockSpec((tm, D), lambda i: (i, 0))
    vec = pl.BlockSpec((1, D), lambda i: (0, 0))
    return pl.pallas_call(
        body, name=name,
        out_shape=(jax.ShapeDtypeStruct((T, D), _F32), jax.ShapeDtypeStruct((1, 1), _F32)),
        grid=(T // tm,), in_specs=[row, row, vec, row],
        out_specs=(row, pl.BlockSpec((1, 1), lambda i: (0, 0))),
        compiler_params=_params(("arbitrary",), 2 * tm * D * 16),
    )(xres, y, g_post, target)


def _norm_bwd(y, g, dout, dres, out_dtype, name):
    T, D = y.shape
    tm = _blk(T, 256)
    nsteps = T // tm
    has_res = dres is not None

    def body(*refs):
        if has_res:
            y_ref, g_ref, do_ref, dr_ref, dy_ref, dg_ref, acc = refs
        else:
            y_ref, g_ref, do_ref, dy_ref, dg_ref, acc = refs
        i = pl.program_id(0)
        yv = y_ref[...]
        do = do_ref[...]
        r = lax.rsqrt(jnp.mean(yv * yv, axis=-1, keepdims=True) + RMS_EPS)
        gy = do * g_ref[...]
        dot = jnp.mean(yv * gy, axis=-1, keepdims=True)
        dy = r * gy - yv * (r * r * r * dot)
        if has_res:
            dy = dy + dr_ref[...]
        dy_ref[...] = dy.astype(dy_ref.dtype)

        @pl.when(i == 0)
        def _():
            acc[...] = jnp.zeros_like(acc)

        acc[...] += jnp.sum((do * yv * r).reshape(tm // SUBLANES, SUBLANES, D), axis=0)

        @pl.when(i == nsteps - 1)
        def _():
            dg_ref[...] = jnp.sum(acc[...], axis=0, keepdims=True)

    row = pl.BlockSpec((tm, D), lambda i: (i, 0))
    vec = pl.BlockSpec((1, D), lambda i: (0, 0))
    ins = [y, g, dout] + ([dres] if has_res else [])
    in_specs = [row, vec, row] + ([row] if has_res else [])
    return pl.pallas_call(
        body, name=name,
        out_shape=(jax.ShapeDtypeStruct((T, D), out_dtype), jax.ShapeDtypeStruct((1, D), _F32)),
        grid=(nsteps,), in_specs=in_specs, out_specs=(row, vec),
        scratch_shapes=[pltpu.VMEM((SUBLANES, D), _F32)],
        compiler_params=_params(("arbitrary",), 2 * tm * D * 16),
    )(*ins)


def _matmul(a, b, *, ta=False, tb=False, out_dtype=_F32, epilogue=None, extra=None, out_cols=0, name):
    M, K = (a.shape[1], a.shape[0]) if ta else a.shape
    tm = _blk(M, 1024)
    if b.ndim == 3 and not tb:
        N, tn = b.shape[0] * b.shape[2], b.shape[2]
        tk = _blk(K, 1024 if (ta or K % 2048) else 2048)
        b_spec = pl.BlockSpec((None, tk, tn), lambda i, j, k: (j, k, 0))
    elif b.ndim == 3:
        N, tn, tk = b.shape[1], _blk(b.shape[1], 1024), b.shape[2]
        b_spec = pl.BlockSpec((None, tn, tk), lambda i, j, k: (k, j, 0))
    else:
        N = b.shape[0] if tb else b.shape[1]
        tn = N // out_cols if out_cols else _blk(N, 1024)
        tk = _blk(K, 1024 if (ta or K % 2048) else 2048)
        b_spec = (pl.BlockSpec((tn, tk), lambda i, j, k: (j, k)) if tb
                  else pl.BlockSpec((tk, tn), lambda i, j, k: (k, j)))
    nk = K // tk
    a_spec = (pl.BlockSpec((tk, tm), lambda i, j, k: (k, i)) if ta
              else pl.BlockSpec((tm, tk), lambda i, j, k: (i, k)))
    if out_cols:
        assert N // tn == out_cols and epilogue is None
        o_spec = pl.BlockSpec((None, tm, tn), lambda i, j, k: (j, i, 0))
        o_shape = (out_cols, M, tn)
    else:
        o_spec = pl.BlockSpec((tm, tn), lambda i, j, k: (i, j))
        o_shape = (M, N)
    dims = (((0 if ta else 1,), (1 if tb else 0,)), ((), ()))
    n_extra = 1 if epilogue == "mul2" else 0
    n_out = 2 if epilogue == "relu2" else 1

    def finish(acc, extra_refs, out_refs):
        if epilogue is None:
            out_refs[0][...] = acc.astype(out_refs[0].dtype)
        elif epilogue == "relu2":
            rl = jnp.maximum(acc, 0.0)
            out_refs[0][...] = (rl * rl).astype(out_refs[0].dtype)
            out_refs[1][...] = rl.astype(out_refs[1].dtype)
        else:
            out_refs[0][...] = (acc * (2.0 * extra_refs[0][...].astype(_F32))).astype(out_refs[0].dtype)

    def body(a_ref, b_ref, *rest):
        extra_refs = rest[:n_extra]
        out_refs = rest[n_extra:n_extra + n_out]
        part = lax.dot_general(a_ref[...], b_ref[...], dims, preferred_element_type=_F32)
        if nk == 1:
            finish(part, extra_refs, out_refs)
            return
        acc = rest[-1]
        k = pl.program_id(2)

        @pl.when(k == 0)
        def _():
            acc[...] = part

        @pl.when(k > 0)
        def _():
            acc[...] += part

        @pl.when(k == nk - 1)
        def _():
            finish(acc[...], extra_refs, out_refs)

    if epilogue == "relu2":
        out_shape = (jax.ShapeDtypeStruct((M, N), _MXU), jax.ShapeDtypeStruct((M, N), _MXU))
        out_specs = (o_spec, o_spec)
        out_bytes = 2 * tm * tn * 2
    else:
        odt = _MXU if epilogue == "mul2" else out_dtype
        out_shape = jax.ShapeDtypeStruct(o_shape, odt)
        out_specs = o_spec
        out_bytes = tm * tn * jnp.dtype(odt).itemsize
    in_specs = [a_spec, b_spec] + ([o_spec] if n_extra else [])
    ins = [a, b] + ([extra] if n_extra else [])
    blocks = 2 * (tm * tk * 2 + tk * tn * 2 + out_bytes + n_extra * tm * tn * 2) + tm * tn * 4 * 2
    return pl.pallas_call(
        body, name=name, out_shape=out_shape, grid=(M // tm, N // tn, nk),
        in_specs=in_specs, out_specs=out_specs,
        scratch_shapes=[pltpu.VMEM((tm, tn), _F32)] if nk > 1 else [],
        compiler_params=_params(("parallel", "parallel", "arbitrary"), blocks),
    )(*ins)


CONV_HALO = 16
CONV_CHUNK = 256


def _tap_windows(win, n_taps_plus1, tc):
    n = win.shape[0]
    for s in range(SUBLANES):
        shifted = win if s == 0 else pltpu.roll(win, n - s, 0)
        for q in range((n_taps_plus1 + SUBLANES - 1) // SUBLANES):
            o = SUBLANES * q + s
            if 1 <= o < n_taps_plus1:
                yield o, shifted[SUBLANES * q:SUBLANES * q + tc, :]


def _conv_fwd(proj, wdw, bdw, cw, name):
    T = proj.shape[0]
    ks = wdw.shape[0]
    cb = LANES
    tc = _blk(T, CONV_CHUNK)
    nblk = cw // cb

    def body(a_ref, g_ref, w_ref, b_ref, c_ref, upad):
        zeros = jnp.zeros((CONV_HALO, cb), _F32)
        upad[0:CONV_HALO, :] = zeros
        upad[T + CONV_HALO:T + 2 * CONV_HALO, :] = zeros
        upad[CONV_HALO:T + CONV_HALO, :] = a_ref[...] * _sigmoid(g_ref[...])

        def chunk(i, carry):
            t0 = pl.multiple_of(i * tc, tc)
            win = upad[pl.ds(t0, tc + 2 * CONV_HALO), :]
            acc = jnp.broadcast_to(b_ref[...], (tc, cb))
            for o, rows in _tap_windows(win, ks + 1, tc):
                j = o + ks // 2 - CONV_HALO
                acc = acc + rows * w_ref[j:j + 1, :]
            c_ref[pl.ds(t0, tc), :] = acc
            return carry

        lax.fori_loop(0, T // tc, chunk, 0)

    col = lambda off: pl.BlockSpec((T, cb), lambda i, off=off: (0, off + i))
    return pl.pallas_call(
        body, name=name, out_shape=jax.ShapeDtypeStruct((T, cw), _F32), grid=(nblk,),
        in_specs=[col(0), col(nblk), pl.BlockSpec((ks, cb), lambda i: (0, i)),
                  pl.BlockSpec((1, cb), lambda i: (0, i))],
        out_specs=pl.BlockSpec((T, cb), lambda i: (0, i)),
        scratch_shapes=[pltpu.VMEM((T + 2 * CONV_HALO, cb), _F32)],
        compiler_params=_params(("parallel",), 2 * T * cb * 4 * 3 + T * cb * 4),
    )(proj, proj, wdw, bdw)


def _ln_silu_fwd(c, lng, lnb, name):
    T, cw = c.shape
    tm = _blk(T, 512)

    def body(c_ref, g_ref, b_ref, y_ref):
        cv = c_ref[...]
        mu = jnp.mean(cv, axis=-1, keepdims=True)
        xc = cv - mu
        var = jnp.mean(xc * xc, axis=-1, keepdims=True)
        z = xc * lax.rsqrt(var + LN_EPS) * g_ref[...] + b_ref[...]
        y_ref[...] = (z * _sigmoid(z)).astype(y_ref.dtype)

    row = pl.BlockSpec((tm, cw), lambda i: (i, 0))
    vec = pl.BlockSpec((1, cw), lambda i: (0, 0))
    return pl.pallas_call(
        body, name=name, out_shape=jax.ShapeDtypeStruct((T, cw), _MXU), grid=(T // tm,),
        in_specs=[row, vec, vec], out_specs=row,
        compiler_params=_params(("parallel",), 2 * tm * cw * 6),
    )(c, lng, lnb)


def _ln_silu_bwd(c, lng, lnb, dycat, name):
    T, cw = c.shape
    tm = _blk(T, 512)
    nsteps = T // tm

    def body(c_ref, g_ref, b_ref, dy_ref, dc_ref, dg_ref, db_ref, accg, accb):
        i = pl.program_id(0)
        cv = c_ref[...]
        mu = jnp.mean(cv, axis=-1, keepdims=True)
        xc = cv - mu
        var = jnp.mean(xc * xc, axis=-1, keepdims=True)
        rstd = lax.rsqrt(var + LN_EPS)
        xhat = xc * rstd
        z = xhat * g_ref[...] + b_ref[...]
        sg = _sigmoid(z)
        dz = dy_ref[...] * (sg * (1.0 + z * (1.0 - sg)))
        dxh = dz * g_ref[...]
        m1 = jnp.mean(dxh, axis=-1, keepdims=True)
        m2 = jnp.mean(dxh * xhat, axis=-1, keepdims=True)
        dc_ref[...] = rstd * (dxh - m1 - xhat * m2)

        @pl.when(i == 0)
        def _():
            accg[...] = jnp.zeros_like(accg)
            accb[...] = jnp.zeros_like(accb)

        accg[...] += jnp.sum((dz * xhat).reshape(tm // SUBLANES, SUBLANES, cw), axis=0)
        accb[...] += jnp.sum(dz.reshape(tm // SUBLANES, SUBLANES, cw), axis=0)

        @pl.when(i == nsteps - 1)
        def _():
            dg_ref[...] = jnp.sum(accg[...], axis=0, keepdims=True)
            db_ref[...] = jnp.sum(accb[...], axis=0, keepdims=True)

    row = pl.BlockSpec((tm, cw), lambda i: (i, 0))
    vec = pl.BlockSpec((1, cw), lambda i: (0, 0))
    return pl.pallas_call(
        body, name=name,
        out_shape=(jax.ShapeDtypeStruct((T, cw), _F32), jax.ShapeDtypeStruct((1, cw), _F32),
                   jax.ShapeDtypeStruct((1, cw), _F32)),
        grid=(nsteps,), in_specs=[row, vec, vec, row], out_specs=(row, vec, vec),
        scratch_shapes=[pltpu.VMEM((SUBLANES, cw), _F32), pltpu.VMEM((SUBLANES, cw), _F32)],
        compiler_params=_params(("arbitrary",), 2 * tm * cw * 12),
    )(c, lng, lnb, dycat)


def _conv_bwd(proj, dc, wdw, cw, name):
    T = proj.shape[0]
    ks = wdw.shape[0]
    cb = LANES
    tc = _blk(T, CONV_CHUNK)
    nblk = cw // cb
    half = ks // 2

    def body(a_ref, g_ref, dc_ref, w_ref, da_ref, dg_ref, dwb_ref, upad, dpad, du, wacc):
        zeros = jnp.zeros((CONV_HALO, cb), _F32)
        for pad in (upad, dpad):
            pad[0:CONV_HALO, :] = zeros
            pad[T + CONV_HALO:T + 2 * CONV_HALO, :] = zeros
        sg = _sigmoid(g_ref[...])
        upad[CONV_HALO:T + CONV_HALO, :] = a_ref[...] * sg
        dpad[CONV_HALO:T + CONV_HALO, :] = dc_ref[...]
        wacc[...] = jnp.zeros_like(wacc)

        def chunk(i, carry):
            t0 = pl.multiple_of(i * tc, tc)
            dwin = dpad[pl.ds(t0, tc + 2 * CONV_HALO), :]
            uwin = upad[pl.ds(t0, tc + 2 * CONV_HALO), :]
            dcc = dwin[CONV_HALO:CONV_HALO + tc, :]
            acc = jnp.zeros((tc, cb), _F32)
            for o, rows in _tap_windows(dwin, CONV_HALO + half + 1, tc):
                j = CONV_HALO + half - o
                if 0 <= j < ks:
                    acc = acc + rows * w_ref[j:j + 1, :]
            du[pl.ds(t0, tc), :] = acc
            for o, rows in _tap_windows(uwin, CONV_HALO + half + 1, tc):
                j = o + half - CONV_HALO
                if 0 <= j < ks:
                    wacc[j] += jnp.sum((rows * dcc).reshape(tc // SUBLANES, SUBLANES, cb), axis=0)
            wacc[ks] += jnp.sum(dcc.reshape(tc // SUBLANES, SUBLANES, cb), axis=0)
            return carry

        lax.fori_loop(0, T // tc, chunk, 0)
        duv = du[...]
        av = a_ref[...]
        da_ref[...] = (duv * sg).astype(da_ref.dtype)
        dg_ref[...] = (duv * av * sg * (1.0 - sg)).astype(dg_ref.dtype)
        dwb_ref[...] = jnp.sum(wacc[...], axis=1)

    col = lambda off: pl.BlockSpec((T, cb), lambda i, off=off: (0, off + i))
    blk = pl.BlockSpec((T, cb), lambda i: (0, i))
    return pl.pallas_call(
        body, name=name,
        out_shape=(jax.ShapeDtypeStruct((T, cw), _MXU), jax.ShapeDtypeStruct((T, cw), _MXU),
                   jax.ShapeDtypeStruct((ks + 1, cw), _F32)),
        grid=(nblk,),
        in_specs=[col(0), col(nblk), blk, pl.BlockSpec((ks, cb), lambda i: (0, i))],
        out_specs=(blk, blk, pl.BlockSpec((ks + 1, cb), lambda i: (0, i))),
        scratch_shapes=[pltpu.VMEM((T + 2 * CONV_HALO, cb), _F32), pltpu.VMEM((T + 2 * CONV_HALO, cb), _F32),
                        pltpu.VMEM((T, cb), _F32), pltpu.VMEM((ks + 1, SUBLANES, cb), _F32)],
        compiler_params=_params(("parallel",), 2 * T * cb * 4 * 4 + 3 * T * cb * 4),
    )(proj, proj, dc, wdw)


N_CLS = WIN_ROWS
N_DR = 2 * WIN_ROWS - 1
N_DC = 2 * WIN_COLS - 1
BAND = WIN_ROWS * GRID_W
QK_SCALE = HEAD_DIM ** -0.5
_NT = (((1,), (1,)), ((), ()))
_TN = (((0,), (0,)), ((), ()))


def _slab_iotas():
    wk = lax.broadcasted_iota(jnp.int32, (GRID_W, LANES), 0)
    lane = lax.broadcasted_iota(jnp.int32, (GRID_W, LANES), 1)
    wq = jnp.bitwise_and(lane, GRID_W - 1)
    head1 = lane >= GRID_W
    d = wk - wq + (WIN_COLS - 1)
    cs = jnp.clip(wq - WIN_COLS // 2, 0, GRID_W - WIN_COLS)
    window = (wk >= cs) & (wk < cs + WIN_COLS)
    return d, head1, window


def _bias_table(rpb2, name):
    npair = rpb2.shape[0] // 2

    def body(rpb_ref, o_ref):
        p = pl.program_id(0)
        d, head1, window = _slab_iotas()
        for dr in range(N_DR):
            val = jnp.zeros((GRID_W, LANES), _F32)
            for j in range(N_DC):
                s0 = rpb_ref[2 * p, dr * N_DC + j]
                s1 = rpb_ref[2 * p + 1, dr * N_DC + j]
                val = jnp.where(d == j, jnp.where(head1, s1, s0), val)
            slab = jnp.where(window, val, NEG_INF)
            for cls in range(N_CLS):
                k = dr - cls
                if 0 <= k < WIN_ROWS:
                    o_ref[cls, k * GRID_W:(k + 1) * GRID_W, :] = slab

    return pl.pallas_call(
        body, name=name, out_shape=jax.ShapeDtypeStruct((npair, N_CLS, BAND, LANES), _F32), grid=(npair,),
        in_specs=[pl.BlockSpec(memory_space=pltpu.SMEM)],
        out_specs=pl.BlockSpec((None, N_CLS, BAND, LANES), lambda p: (p, 0, 0, 0)),
        compiler_params=_params(("arbitrary",), 2 * N_CLS * BAND * LANES * 4),
    )(rpb2)


def _rpb_grad(gc, name):
    npair = gc.shape[0]

    def body(g_ref, o_ref):
        d, _, _ = _slab_iotas()
        rowi = lax.broadcasted_iota(jnp.int32, (4 * SUBLANES, LANES), 0)
        lanei = lax.broadcasted_iota(jnp.int32, (4 * SUBLANES, LANES), 1)
        head1 = lax.broadcasted_iota(jnp.int32, (1, LANES), 1) >= GRID_W
        tiles = [jnp.zeros((4 * SUBLANES, LANES), _F32) for _ in range(2)]
        for dr in range(N_DR):
            ysum = jnp.zeros((GRID_W, LANES), _F32)
            for cls in range(N_CLS):
                k = dr - cls
                if 0 <= k < WIN_ROWS:
                    ysum = ysum + g_ref[cls, k * GRID_W:(k + 1) * GRID_W, :]
            for j in range(N_DC):
                cs = jnp.sum(jnp.where(d == j, ysum, 0.0), axis=0, keepdims=True)
                s0 = jnp.sum(jnp.where(head1, 0.0, cs), axis=1, keepdims=True)
                s1 = jnp.sum(jnp.where(head1, cs, 0.0), axis=1, keepdims=True)
                here = (rowi == j) & (lanei == dr)
                tiles[0] = tiles[0] + jnp.where(here, s0, 0.0)
                tiles[1] = tiles[1] + jnp.where(here, s1, 0.0)
        o_ref[0] = tiles[0]
        o_ref[1] = tiles[1]

    return pl.pallas_call(
        body, name=name, out_shape=jax.ShapeDtypeStruct((npair, 2, 4 * SUBLANES, LANES), _F32), grid=(npair,),
        in_specs=[pl.BlockSpec((None, N_CLS, BAND, LANES), lambda p: (p, 0, 0, 0))],
        out_specs=pl.BlockSpec((None, 2, 4 * SUBLANES, LANES), lambda p: (p, 0, 0, 0)),
        compiler_params=_params(("parallel",), 2 * N_CLS * BAND * LANES * 4),
    )(gc)


def _block_diag(v, diag):
    return jnp.where(diag, jnp.concatenate([v, v], axis=0), 0.0).astype(_MXU)


def _diag_mask():
    r = lax.broadcasted_iota(jnp.int32, (LANES, LANES), 0) < GRID_W
    c = lax.broadcasted_iota(jnp.int32, (LANES, LANES), 1) < HEAD_DIM
    return r == c


def _row_geometry(r, rows):
    rs = jnp.clip(r - WIN_ROWS // 2, 0, rows - WIN_ROWS)
    cls = rs - r + (WIN_ROWS - 1)
    return pl.multiple_of(r * GRID_W, GRID_W), pl.multiple_of(rs * GRID_W, GRID_W), cls


def _probs_t(qsel, kband, bias):
    s = lax.dot_general(kband, qsel, _NT, preferred_element_type=_F32) + bias
    mx = jnp.max(s, axis=0, keepdims=True)
    e = jnp.exp(s - mx)
    return e * (1.0 / jnp.sum(e, axis=0, keepdims=True))


def _attn_fwd(proj, bias, cw, naw, name):
    T = proj.shape[0]
    rows = T // GRID_W
    npair = naw // LANES
    qoff, koff, voff = 2 * cw // LANES, (2 * cw + naw) // LANES, (2 * cw + 2 * naw) // LANES

    def body(q_ref, k_ref, v_ref, b_ref, o_ref, kb, vb):
        kb[...] = k_ref[...].astype(_MXU)
        vb[...] = v_ref[...].astype(_MXU)
        diag = _diag_mask()
        m0 = lax.broadcasted_iota(jnp.int32, (GRID_W, LANES), 1) < HEAD_DIM

        def step(r, carry):
            t0, b0, cls = _row_geometry(r, rows)
            qsel = _block_diag(q_ref[pl.ds(t0, GRID_W), :] * QK_SCALE, diag)
            pt = _probs_t(qsel, kb[pl.ds(b0, BAND), :], b_ref[cls])
            of = lax.dot_general(pt.astype(_MXU), vb[pl.ds(b0, BAND), :], _TN, preferred_element_type=_F32)
            o_ref[pl.ds(t0, GRID_W), :] = jnp.where(m0, of[:GRID_W], of[GRID_W:]).astype(o_ref.dtype)
            return carry

        lax.fori_loop(0, rows, step, 0, unroll=2)

    col = lambda off: pl.BlockSpec((T, LANES), lambda i, off=off: (0, off + i))
    return pl.pallas_call(
        body, name=name, out_shape=jax.ShapeDtypeStruct((T, naw), _MXU), grid=(npair,),
        in_specs=[col(qoff), col(koff), col(voff),
                  pl.BlockSpec((None, N_CLS, BAND, LANES), lambda i: (i, 0, 0, 0))],
        out_specs=pl.BlockSpec((T, LANES), lambda i: (0, i)),
        scratch_shapes=[pltpu.VMEM((T, LANES), _MXU), pltpu.VMEM((T, LANES), _MXU)],
        compiler_params=_params(("parallel",), 2 * (3 * T * LANES * 4 + N_CLS * BAND * LANES * 4 + T * LANES * 2)),
    )(proj, proj, proj, bias)


def _attn_bwd(proj, bias, dycat, cw, naw, name):
    T = proj.shape[0]
    rows = T // GRID_W
    npair = naw // LANES
    qoff, koff, voff = 2 * cw // LANES, (2 * cw + naw) // LANES, (2 * cw + 2 * naw) // LANES
    doff = cw // LANES

    def body(q_ref, k_ref, v_ref, b_ref, do_ref, dq_ref, dk_ref, dv_ref, g_ref, kb, vb, dka, dva):
        kb[...] = k_ref[...].astype(_MXU)
        vb[...] = v_ref[...].astype(_MXU)
        dka[...] = jnp.zeros_like(dka)
        dva[...] = jnp.zeros_like(dva)
        g_ref[...] = jnp.zeros_like(g_ref)
        diag = _diag_mask()
        m0 = lax.broadcasted_iota(jnp.int32, (GRID_W, LANES), 1) < HEAD_DIM

        def step(r, carry):
            t0, b0, cls = _row_geometry(r, rows)
            kband = kb[pl.ds(b0, BAND), :]
            vband = vb[pl.ds(b0, BAND), :]
            qsel = _block_diag(q_ref[pl.ds(t0, GRID_W), :] * QK_SCALE, diag)
            dosel = _block_diag(do_ref[pl.ds(t0, GRID_W), :], diag)
            pt = _probs_t(qsel, kband, b_ref[cls])
            dpt = lax.dot_general(vband, dosel, _NT, preferred_element_type=_F32)
            delta = jnp.sum(pt * dpt, axis=0, keepdims=True)
            dst = pt * (dpt - delta)
            g_ref[cls] += dst
            dsb = dst.astype(_MXU)
            dqf = lax.dot_general(dsb, kband, _TN, preferred_element_type=_F32)
            dq = jnp.where(m0, dqf[:GRID_W], dqf[GRID_W:]) * QK_SCALE
            dq_ref[pl.ds(t0, GRID_W), :] = dq.astype(dq_ref.dtype)
            dka[pl.ds(b0, BAND), :] += jnp.dot(dsb, qsel, preferred_element_type=_F32)
            dva[pl.ds(b0, BAND), :] += jnp.dot(pt.astype(_MXU), dosel, preferred_element_type=_F32)
            return carry

        lax.fori_loop(0, rows, step, 0)
        dk_ref[...] = dka[...].astype(dk_ref.dtype)
        dv_ref[...] = dva[...].astype(dv_ref.dtype)

    col = lambda off: pl.BlockSpec((T, LANES), lambda i, off=off: (0, off + i))
    blk = pl.BlockSpec((T, LANES), lambda i: (0, i))
    tbl = pl.BlockSpec((None, N_CLS, BAND, LANES), lambda i: (i, 0, 0, 0))
    o16 = jax.ShapeDtypeStruct((T, naw), _MXU)
    vm = 2 * (4 * T * LANES * 4 + 2 * N_CLS * BAND * LANES * 4 + 3 * T * LANES * 2) + 2 * T * LANES * 6
    return pl.pallas_call(
        body, name=name,
        out_shape=(o16, o16, o16, jax.ShapeDtypeStruct((npair, N_CLS, BAND, LANES), _F32)),
        grid=(npair,),
        in_specs=[col(qoff), col(koff), col(voff), tbl, col(doff)],
        out_specs=(blk, blk, blk, tbl),
        scratch_shapes=[pltpu.VMEM((T, LANES), _MXU), pltpu.VMEM((T, LANES), _MXU),
                        pltpu.VMEM((T, LANES), _F32), pltpu.VMEM((T, LANES), _F32)],
        compiler_params=_params(("parallel",), vm),
    )(proj, proj, proj, bias, dycat)


_ANY = pl.BlockSpec(memory_space=pl.ANY)
_HBM = pl.BlockSpec(memory_space=pltpu.HBM)
_SEM = pl.BlockSpec(memory_space=pltpu.SEMAPHORE)
_VMEM = pl.BlockSpec(memory_space=pltpu.VMEM)
_MESH_ID = pl.DeviceIdType.MESH
_EFFECT = pltpu.SideEffectType.DATAFLOW_SIDE_EFFECTING
_TOKEN = jax.ShapeDtypeStruct((SUBLANES, LANES), _F32)


def _mesh_pos():
    return tuple(lax.axis_index(a) for a in MESH_AXES)


def _in_hbm(a):
    return pltpu.with_memory_space_constraint(a, pltpu.HBM)


def _hbm_like(arrays):
    return [pltpu.HBM(a.shape, a.dtype) for a in arrays]


def _after(arr, token):
    return arr + token[0:1, 0:1].astype(arr.dtype)


def _shard_ref(ref, axis, j, width):
    idx = [slice(None)] * len(ref.shape)
    idx[axis] = pl.ds(pl.multiple_of(j * width, math.gcd(width, LANES)), width)
    return ref.at[tuple(idx)]


def _all_gather(shards, axes, name):
    n = len(shards)
    widths = [s.shape[a] for s, a in zip(shards, axes)]
    out_shape = [jax.ShapeDtypeStruct(tuple(N_DEV * d if k == a else d for k, d in enumerate(s.shape)), s.dtype)
                 for s, a in zip(shards, axes)]

    def body(*refs):
        ins, outs = refs[:n], refs[n:2 * n]
        send_sems, recv_sems, local_sems = refs[2 * n:]
        x, y, c = _mesh_pos()
        me, sibling = (x, y, c), (x, y, 1 - c)
        chips = [(1 - x, y), (x, 1 - y), (1 - x, 1 - y)]

        def slot(i, px, py, pc):
            return _shard_ref(outs[i], axes[i], 4 * px + 2 * py + pc, widths[i])

        def copy(i, k, block, to, src=None):
            return pltpu.make_async_remote_copy(
                src_ref=slot(i, *block) if src is None else src, dst_ref=slot(i, *block),
                send_sem=send_sems.at[7 * i + k], recv_sem=recv_sems.at[7 * i + k],
                device_id=to, device_id_type=_MESH_ID)

        mine = [pltpu.make_async_copy(ins[i], slot(i, *me), local_sems.at[i]) for i in range(n)]
        for cp in mine:
            cp.start()
        first = []
        for i in range(n):
            first.append(copy(i, 0, me, sibling, src=ins[i]))
            first += [copy(i, 1 + j, me, (*chip, c), src=ins[i]) for j, chip in enumerate(chips)]
        for cp in first:
            cp.start()
        passed = []
        for j, chip in enumerate(chips):
            for i in range(n):
                copy(i, 1 + j, (*chip, c), me).wait_recv()
                fwd = copy(i, 4 + j, (*chip, c), sibling)
                fwd.start()
                passed.append(fwd)
        for i in range(n):
            copy(i, 0, sibling, me).wait_recv()
            for j, chip in enumerate(chips):
                copy(i, 4 + j, (*chip, 1 - c), me).wait_recv()
        for cp in first + passed:
            cp.wait_send()
        for cp in mine:
            cp.wait()

    return pl.pallas_call(
        body, name=name, out_shape=out_shape, in_specs=[_ANY] * n, out_specs=[_ANY] * n,
        scratch_shapes=[pltpu.SemaphoreType.DMA((7 * n,)), pltpu.SemaphoreType.DMA((7 * n,)),
                        pltpu.SemaphoreType.DMA((n,))],
    )(*shards)


def _gather_start(lands, name):
    n = len(lands)

    def body(*refs):
        land = refs[:n]
        send, recv_sib, recv_ici = refs[n:n + 3]
        token = refs[-1]
        x, y, c = _mesh_pos()
        me = 4 * x + 2 * y + c
        for i in range(n):
            pltpu.make_async_remote_copy(
                src_ref=land[i].at[me], dst_ref=land[i].at[me], send_sem=send.at[4 * i],
                recv_sem=recv_sib.at[i], device_id=(x, y, 1 - c), device_id_type=_MESH_ID).start()
            for j, chip in enumerate([(1 - x, y), (x, 1 - y), (1 - x, 1 - y)]):
                pltpu.make_async_remote_copy(
                    src_ref=land[i].at[me], dst_ref=land[i].at[me], send_sem=send.at[4 * i + 1 + j],
                    recv_sem=recv_ici.at[3 * i + j], device_id=(*chip, c), device_id_type=_MESH_ID).start()
        token[...] = jnp.zeros_like(token)

    dma = pltpu.SemaphoreType.DMA
    out = pl.pallas_call(
        body, name=name,
        out_shape=(dma((4 * n,)), dma((n,)), dma((3 * n,)), *_hbm_like(lands), _TOKEN),
        in_specs=[_HBM] * n, out_specs=(_SEM, _SEM, _SEM, *[_HBM] * n, _VMEM),
        input_output_aliases={i: 3 + i for i in range(n)},
        compiler_params=pltpu.CompilerParams(has_side_effects=_EFFECT),
    )(*[_in_hbm(a) for a in lands])
    return dict(send=out[0], recv_sib=out[1], recv_ici=out[2], lands=list(out[3:3 + n]), token=out[-1])


def _gather_forward(st, after, name):
    lands = st["lands"]
    n = len(lands)

    def body(*refs):
        land = refs[:n]
        recv_ici = refs[n]
        send2, recv2 = refs[n + 2], refs[n + 3]
        token = refs[-1]
        x, y, c = _mesh_pos()
        for j, (px, py) in enumerate([(1 - x, y), (x, 1 - y), (1 - x, 1 - y)]):
            blk = 4 * px + 2 * py + c
            for i in range(n):
                pltpu.make_async_remote_copy(
                    src_ref=land[i].at[blk], dst_ref=land[i].at[blk], send_sem=send2.at[3 * i + j],
                    recv_sem=recv_ici.at[3 * i + j], device_id=(px, py, c), device_id_type=_MESH_ID).wait_recv()
                pltpu.make_async_remote_copy(
                    src_ref=land[i].at[blk], dst_ref=land[i].at[blk], send_sem=send2.at[3 * i + j],
                    recv_sem=recv2.at[3 * i + j], device_id=(x, y, 1 - c), device_id_type=_MESH_ID).start()
        token[...] = jnp.zeros_like(token)

    dma = pltpu.SemaphoreType.DMA
    out = pl.pallas_call(
        body, name=name,
        out_shape=(dma((3 * n,)), dma((3 * n,)), *_hbm_like(lands), _TOKEN),
        in_specs=[_HBM] * n + [_SEM, _ANY], out_specs=(_SEM, _SEM, *[_HBM] * n, _VMEM),
        input_output_aliases={i: 2 + i for i in range(n)},
        compiler_params=pltpu.CompilerParams(has_side_effects=_EFFECT),
    )(*lands, st["recv_ici"], after)
    return dict(st, send2=out[0], recv2=out[1], lands=list(out[2:2 + n]), token=out[-1])


def _gather_finish(st, after, name):
    lands = st["lands"]
    n = len(lands)

    def body(*refs):
        land = refs[:n]
        send, recv_sib, send2, recv2 = refs[n:n + 4]
        x, y, c = _mesh_pos()
        me = 4 * x + 2 * y + c
        sib = 4 * x + 2 * y + (1 - c)

        def desc(i, blk, s_sem, r_sem):
            return pltpu.make_async_remote_copy(
                src_ref=land[i].at[blk], dst_ref=land[i].at[blk], send_sem=s_sem, recv_sem=r_sem,
                device_id=(x, y, 1 - c), device_id_type=_MESH_ID)

        for i in range(n):
            desc(i, sib, send.at[4 * i], recv_sib.at[i]).wait_recv()
            for j, (px, py) in enumerate([(1 - x, y), (x, 1 - y), (1 - x, 1 - y)]):
                desc(i, 4 * px + 2 * py + (1 - c), send2.at[3 * i + j], recv2.at[3 * i + j]).wait_recv()
            for k in range(4):
                desc(i, me, send.at[4 * i + k], recv_sib.at[i]).wait_send()
            for j, (px, py) in enumerate([(1 - x, y), (x, 1 - y), (1 - x, 1 - y)]):
                desc(i, 4 * px + 2 * py + c, send2.at[3 * i + j], recv2.at[3 * i + j]).wait_send()

    out = pl.pallas_call(
        body, name=name, out_shape=tuple(_hbm_like(lands)),
        in_specs=[_HBM] * n + [_SEM] * 4 + [_ANY], out_specs=tuple([_HBM] * n),
        input_output_aliases={i: i for i in range(n)},
        compiler_params=pltpu.CompilerParams(has_side_effects=_EFFECT),
    )(*lands, st["send"], st["recv_sib"], st["send2"], st["recv2"], after)
    return list(out)


def _scatter_sibling_start(grads, name):
    n = len(grads)
    gots = [lax.empty((4,) + g.shape[1:], g.dtype) for g in grads]

    def body(*refs):
        grad, got = refs[:n], refs[n:2 * n]
        send, recv = refs[2 * n], refs[2 * n + 1]
        token = refs[-1]
        x, y, c = _mesh_pos()
        for i in range(n):
            for q in range(4):
                pltpu.make_async_remote_copy(
                    src_ref=grad[i].at[2 * q + (1 - c)], dst_ref=got[i].at[q], send_sem=send.at[4 * i + q],
                    recv_sem=recv.at[4 * i + q], device_id=(x, y, 1 - c), device_id_type=_MESH_ID).start()
        token[...] = jnp.zeros_like(token)

    dma = pltpu.SemaphoreType.DMA
    out = pl.pallas_call(
        body, name=name,
        out_shape=(dma((4 * n,)), dma((4 * n,)), *_hbm_like(grads), *_hbm_like(gots), _TOKEN),
        in_specs=[_HBM] * (2 * n), out_specs=(_SEM, _SEM, *[_HBM] * (2 * n), _VMEM),
        input_output_aliases={i: 2 + i for i in range(2 * n)},
        compiler_params=pltpu.CompilerParams(has_side_effects=_EFFECT),
    )(*[_in_hbm(a) for a in grads], *[_in_hbm(a) for a in gots])
    return dict(send=out[0], recv=out[1], grads=list(out[2:2 + n]), gots=list(out[2 + n:2 + 2 * n]), token=out[-1])


def _scatter_sibling_finish(st, after, name):
    grads, gots = st["grads"], st["gots"]
    n = len(grads)

    def body(*refs):
        grad, got = refs[:n], refs[n:2 * n]
        send, recv = refs[2 * n], refs[2 * n + 1]
        x, y, c = _mesh_pos()
        for i in range(n):
            for q in range(4):
                cp = pltpu.make_async_remote_copy(
                    src_ref=grad[i].at[2 * q + (1 - c)], dst_ref=got[i].at[q], send_sem=send.at[4 * i + q],
                    recv_sem=recv.at[4 * i + q], device_id=(x, y, 1 - c), device_id_type=_MESH_ID)
                cp.wait_recv()
                cp.wait_send()

    out = pl.pallas_call(
        body, name=name, out_shape=tuple(_hbm_like(grads) + _hbm_like(gots)),
        in_specs=[_HBM] * (2 * n) + [_SEM, _SEM, _ANY], out_specs=tuple([_HBM] * (2 * n)),
        input_output_aliases={i: i for i in range(2 * n)},
        compiler_params=pltpu.CompilerParams(has_side_effects=_EFFECT),
    )(*grads, *gots, st["send"], st["recv"], after)
    return list(out[:n]), list(out[n:])


def _scatter_add(grad, got, pos, name):
    _, R, C = grad.shape
    tr = _blk(R, 512)

    def body(pos_ref, a_ref, b_ref, part_ref, fin_ref):
        s = (a_ref[...].astype(_F32) + b_ref[...].astype(_F32)).astype(part_ref.dtype)
        part_ref[...] = s

        @pl.when(pl.program_id(1) == pos_ref[1])
        def _():
            fin_ref[...] = s

    grid_spec = pltpu.PrefetchScalarGridSpec(
        num_scalar_prefetch=1, grid=(R // tr, 4),
        in_specs=[pl.BlockSpec((None, tr, C), lambda i, q, p: (2 * q + p[0], i, 0)),
                  pl.BlockSpec((None, tr, C), lambda i, q, p: (q, i, 0))],
        out_specs=[pl.BlockSpec((None, tr, C), lambda i, q, p: (q, i, 0)),
                   pl.BlockSpec((None, tr, C), lambda i, q, p: (p[1], i, 0))])
    shape = jax.ShapeDtypeStruct((4, R, C), grad.dtype)
    return pl.pallas_call(
        body, name=name, out_shape=(shape, shape), grid_spec=grid_spec,
        compiler_params=_params(("parallel", "arbitrary"), 2 * tr * C * 8),
    )(pos, grad, got)


def _scatter_chips_start(parts, fins, name):
    n = len(parts)

    def body(*refs):
        part, fin = refs[:n], refs[n:2 * n]
        send, recv = refs[2 * n], refs[2 * n + 1]
        token = refs[-1]
        x, y, c = _mesh_pos()
        mine = 2 * x + y
        for i in range(n):
            for k, (tx, ty) in enumerate([(1 - x, y), (x, 1 - y), (1 - x, 1 - y)]):
                pltpu.make_async_remote_copy(
                    src_ref=part[i].at[2 * tx + ty], dst_ref=fin[i].at[mine], send_sem=send.at[3 * i + k],
                    recv_sem=recv.at[3 * i + k], device_id=(tx, ty, c), device_id_type=_MESH_ID).start()
        token[...] = jnp.zeros_like(token)

    dma = pltpu.SemaphoreType.DMA
    out = pl.pallas_call(
        body, name=name,
        out_shape=(dma((3 * n,)), dma((3 * n,)), *_hbm_like(parts), *_hbm_like(fins), _TOKEN),
        in_specs=[_HBM] * (2 * n), out_specs=(_SEM, _SEM, *[_HBM] * (2 * n), _VMEM),
        input_output_aliases={i: 2 + i for i in range(2 * n)},
        compiler_params=pltpu.CompilerParams(has_side_effects=_EFFECT),
    )(*[_in_hbm(a) for a in parts], *[_in_hbm(a) for a in fins])
    return dict(send=out[0], recv=out[1], parts=list(out[2:2 + n]), fins=list(out[2 + n:2 + 2 * n]), token=out[-1])


def _scatter_chips_finish(st, after, name):
    parts, fins = st["parts"], st["fins"]
    n = len(parts)

    def body(*refs):
        part, fin = refs[:n], refs[n:2 * n]
        send, recv = refs[2 * n], refs[2 * n + 1]
        x, y, c = _mesh_pos()
        for i in range(n):
            for k, (tx, ty) in enumerate([(1 - x, y), (x, 1 - y), (1 - x, 1 - y)]):
                cp = pltpu.make_async_remote_copy(
                    src_ref=part[i].at[2 * tx + ty], dst_ref=fin[i].at[2 * tx + ty], send_sem=send.at[3 * i + k],
                    recv_sem=recv.at[3 * i + k], device_id=(tx, ty, c), device_id_type=_MESH_ID)
                cp.wait_recv()
                cp.wait_send()

    out = pl.pallas_call(
        body, name=name, out_shape=tuple(_hbm_like(parts) + _hbm_like(fins)),
        in_specs=[_HBM] * (2 * n) + [_SEM, _SEM, _ANY], out_specs=tuple([_HBM] * (2 * n)),
        input_output_aliases={i: i for i in range(2 * n)},
        compiler_params=pltpu.CompilerParams(has_side_effects=_EFFECT),
    )(*parts, *fins, st["send"], st["recv"], after)
    return list(out[n:])


def _adamw(g, w, m, v):
    m = ADAM_B1 * m + (1.0 - ADAM_B1) * g
    v = ADAM_B2 * v + (1.0 - ADAM_B2) * (g * g)
    m_hat = m / (1.0 - ADAM_B1 ** ADAM_STEP)
    v_hat = v / (1.0 - ADAM_B2 ** ADAM_STEP)
    delta = -ADAM_LR * (m_hat / (jnp.sqrt(v_hat) + ADAM_EPS) + ADAM_WD * w)
    return delta, m, v


def _adam_layer(fin, w3, m3, v3, l, prev, name):
    L, R, C = w3.shape
    tr = _blk(R, max(SUBLANES, (1 << 18) // C))

    def body(f_ref, w_ref, m_ref, v_ref, *rest):
        g_ref, d_ref, nm_ref, nv_ref = rest[-4:]
        g = ((f_ref[0].astype(_F32) + f_ref[1].astype(_F32)) + f_ref[2].astype(_F32)) + f_ref[3].astype(_F32)
        d, nm, nv = _adamw(g, w_ref[...], m_ref[...], v_ref[...])
        g_ref[...] = g
        d_ref[...] = d
        nm_ref[...] = nm
        nv_ref[...] = nv

    lay = pl.BlockSpec((None, tr, C), lambda i: (l, i, 0))
    ins = [fin, w3, m3, v3]
    in_specs = [pl.BlockSpec((4, tr, C), lambda i: (0, i, 0)), lay, lay, lay]
    aliases = {}
    if prev is not None:
        ins += list(prev)
        in_specs += [_ANY] * 4
        aliases = {4 + k: k for k in range(4)}
    return pl.pallas_call(
        body, name=name, out_shape=[jax.ShapeDtypeStruct((L, R, C), _F32)] * 4, grid=(R // tr,),
        in_specs=in_specs, out_specs=[lay] * 4, input_output_aliases=aliases,
        compiler_params=_params(("parallel",), 2 * tr * C * (4 * 2 + 7 * 4)),
    )(*ins)


def _sum_parts(parts, name):
    _, R, C = parts.shape

    def body(p_ref, o_ref):
        acc = p_ref[0]
        for k in range(1, N_DEV):
            acc = acc + p_ref[k]
        o_ref[...] = acc

    tr = _blk(R, 512)
    return pl.pallas_call(
        body, name=name, out_shape=jax.ShapeDtypeStruct((R, C), _F32), grid=(R // tr,),
        in_specs=[pl.BlockSpec((N_DEV, tr, C), lambda i: (0, i, 0))],
        out_specs=pl.BlockSpec((tr, C), lambda i: (i, 0)),
        compiler_params=_params(("parallel",), 2 * tr * C * 4 * 9),
    )(parts)


def _adam_flat(g, w, m, v, name):
    R, C = g.shape
    tr = _blk(R, 512)

    def body(g_ref, w_ref, m_ref, v_ref, d_ref, nm_ref, nv_ref):
        d, nm, nv = _adamw(g_ref[...], w_ref[...], m_ref[...], v_ref[...])
        d_ref[...] = d
        nm_ref[...] = nm
        nv_ref[...] = nv

    spec = pl.BlockSpec((tr, C), lambda i: (i, 0))
    return pl.pallas_call(
        body, name=name, out_shape=[jax.ShapeDtypeStruct((R, C), _F32)] * 3, grid=(R // tr,),
        in_specs=[spec] * 4, out_specs=[spec] * 3,
        compiler_params=_params(("parallel",), 2 * tr * C * 4 * 7),
    )(g, w, m, v)


def _pack(arrays):
    flat = jnp.concatenate([a.reshape(-1) for a in arrays])
    tile = SUBLANES * LANES
    pad = (-flat.shape[0]) % tile
    return jnp.pad(flat, (0, pad)).reshape(-1, LANES)


def _unpack(packed, shapes):
    flat = packed.reshape(-1)
    out, off = [], 0
    for s in shapes:
        n = math.prod(s)
        out.append(flat[off:off + n].reshape(s))
        off += n
    return out


def kernel(x, w_in, w_dw, b_dw, conv_ln_g, conv_ln_b, rpb, w_out, w_up, w_down, pre_mix_g, post_mix_g, pre_mlp_g, post_mlp_g, loss_target, m_w_in, m_w_dw, m_b_dw, m_conv_ln_g, m_conv_ln_b, m_rpb, m_w_out, m_w_up, m_w_down, m_pre_mix_g, m_post_mix_g, m_pre_mlp_g, m_post_mlp_g, v_w_in, v_w_dw, v_b_dw, v_conv_ln_g, v_conv_ln_b, v_rpb, v_w_out, v_w_up, v_w_down, v_pre_mix_g, v_post_mix_g, v_pre_mlp_g, v_post_mlp_g):
    _, T, D = x.shape
    L = w_in.shape[0]
    cw = b_dw.shape[1]
    H = rpb.shape[1]
    naw = H * HEAD_DIM
    ks = w_dw.shape[1]
    assert T % GRID_W == 0 and T // GRID_W >= WIN_ROWS and H % 2 == 0 and cw % LANES == 0
    assert rpb.shape[2:] == (N_DR, N_DC) and w_dw.shape[2] * N_DEV == cw and ks // 2 < CONV_HALO
    assert naw == cw and w_out.shape[1] * N_DEV == cw + naw

    xs = x.reshape(T, D)
    tgt = loss_target.reshape(T, D)
    row = lambda p, l: p[l:l + 1]
    mx, my, mc = (lax.axis_index(a) for a in MESH_AXES)
    dev = (4 * mx + 2 * my + mc).astype(jnp.int32).reshape(1)
    pos = jnp.stack([mc, 2 * mx + my]).astype(jnp.int32)

    ks_pad = ks + (-ks) % SUBLANES
    wdw_pad = jnp.pad(w_dw, ((0, 0), (0, ks_pad - ks), (0, 0))).reshape(L * ks_pad, w_dw.shape[2])
    wdw_full = _all_gather([wdw_pad], [1], "ag_wdw")[0].reshape(L, ks_pad, cw)[:, :ks]

    big = (w_in, w_out, w_up, w_down)
    names = ("in", "out", "up", "down")

    def gather_start(l):
        return _gather_start([_cast_slot(w, l, dev, f"cast_{nm}") for w, nm in zip(big, names)], "gather_start")

    def weights(lands):
        return lands[0], lands[1].reshape(-1, D), lands[2], lands[3].reshape(-1, D)

    saved = []
    xin = xs
    h = _norm_fwd(xs, row(pre_mix_g, 0), "norm_first")
    gat = gather_start(0)
    gat = _gather_forward(gat, gat["token"], "gather_forward")
    Ws = weights(_gather_finish(gat, gat["token"], "gather_finish"))
    dy = loss_sum = None
    for l in range(L):
        Win, Wout, Wup, Wdown = Ws
        nxt = gather_start(l + 1) if l + 1 < L else None
        proj = _matmul(h, Win, name="mm_proj")
        bdw = _after(row(b_dw, l), nxt["token"]) if nxt else row(b_dw, l)
        c = _conv_fwd(proj, wdw_full[l], bdw, cw, "conv_fwd")
        yc = _ln_silu_fwd(c, row(conv_ln_g, l), row(conv_ln_b, l), "ln_silu_fwd")
        bias = _bias_table(rpb[l].reshape(H, N_DR * N_DC), "bias_table")
        ya = _attn_fwd(proj, bias, cw, naw, "attn_fwd")
        ycat = jnp.concatenate([yc, ya], axis=1)
        mix = _matmul(ycat, Wout, name="mm_mix")
        g_post = row(post_mix_g, l)
        if nxt:
            nxt = _gather_forward(nxt, mix, "gather_forward")
            g_post = _after(g_post, nxt["token"])
        x1, h2 = _resid_norm_fwd(xin, mix, g_post, row(pre_mlp_g, l), "resid_mix")
        act, rl = _matmul(h2, Wup, epilogue="relu2", name="mm_up")
        f = _matmul(act, Wdown, name="mm_down")
        saved.append(dict(xin=xin, h=h, W=Ws, proj=proj, c=c, bias=bias, ycat=ycat,
                          mix=mix, x1=x1, h2=h2, act=act, rl=rl, f=f))
        if nxt:
            Ws = weights(_gather_finish(nxt, f, "gather_finish"))
            xin, h = _resid_norm_fwd(x1, f, row(post_mlp_g, l), row(pre_mix_g, l + 1), "resid_mlp")
        else:
            dy, loss_sum = _resid_loss(x1, f, row(post_mlp_g, l), tgt, "resid_loss")

    loss = lax.psum(loss_sum[0, 0] * (0.5 / D), MESH_AXES)

    small_grads = [None] * L
    big_out = [None] * 4
    moments = ((m_w_in, v_w_in), (m_w_out, v_w_out), (m_w_up, v_w_up), (m_w_down, v_w_down))

    def adam(fins, l):
        for k in range(4):
            big_out[k] = _adam_layer(fins[k], big[k], moments[k][0], moments[k][1], l, big_out[k], f"adam_{k}_{l}")

    def scatter_mid(st, after):
        grads, gots = _scatter_sibling_finish(st, after, "scatter_sibling_finish")
        pf = [_scatter_add(g, o, pos, f"scatter_add_{k}") for k, (g, o) in enumerate(zip(grads, gots))]
        return _scatter_chips_start([p for p, _ in pf], [q for _, q in pf], "scatter_chips_start")

    dxo = dy
    pending = None
    for l in reversed(range(L)):
        s = saved[l]
        Win, Wout, Wup, Wdown = s["W"]
        g_post_mlp = row(post_mlp_g, l)
        if pending is not None:
            g_post_mlp = _after(g_post_mlp, pending[0]["token"])
        d_f, dg_post_mlp = _norm_bwd(s["f"], g_post_mlp, dxo, None, _MXU, "norm_bwd_mlp")
        d_up = _matmul(d_f, Wdown, tb=True, epilogue="mul2", extra=s["rl"], name="mm_d_up")
        g_pre_mlp = row(pre_mlp_g, l)
        if pending is not None:
            st2 = scatter_mid(pending[0], d_up)
            pending = (st2, pending[1])
            g_pre_mlp = _after(g_pre_mlp, st2["token"])
        dWdown = _matmul(s["act"], d_f, ta=True, out_dtype=_WIRE, name="mm_dw_down")
        d_h2 = _matmul(d_up, Wup, tb=True, name="mm_d_h2")
        dWup = _matmul(s["h2"], d_up, ta=True, out_dtype=_WIRE, out_cols=N_DEV, name="mm_dw_up")
        dx1, dg_pre_mlp = _norm_bwd(s["x1"], g_pre_mlp, d_h2, dxo, _F32, "norm_bwd_premlp")
        d_mix, dg_post_mix = _norm_bwd(s["mix"], row(post_mix_g, l), dx1, None, _MXU, "norm_bwd_mix")
        d_ycat = _matmul(d_mix, Wout, tb=True, name="mm_d_ycat")
        dWout = _matmul(s["ycat"], d_mix, ta=True, out_dtype=_WIRE, name="mm_dw_out")
        dc, dlng, dlnb = _ln_silu_bwd(s["c"], row(conv_ln_g, l), row(conv_ln_b, l), d_ycat, "ln_silu_bwd")
        da, dgate, dwb = _conv_bwd(s["proj"], dc, wdw_full[l], cw, "conv_bwd")
        dq, dk, dv, gcls = _attn_bwd(s["proj"], s["bias"], d_ycat, cw, naw, "attn_bwd")
        drpb = _rpb_grad(gcls, "rpb_grad").reshape(H, 4 * SUBLANES, LANES)[:, :N_DC, :N_DR].transpose(0, 2, 1)
        dproj = jnp.concatenate([da, dgate, dq, dk, dv], axis=1)
        dh = _matmul(dproj, Win, tb=True, name="mm_d_h")
        dWin = _matmul(s["h"], dproj, ta=True, out_dtype=_WIRE, out_cols=N_DEV, name="mm_dw_in")
        dxo, dg_pre_mix = _norm_bwd(s["xin"], row(pre_mix_g, l), dh, dx1, _F32, "norm_bwd_premix")
        if pending is not None:
            adam(_scatter_chips_finish(pending[0], dxo, "scatter_chips_finish"), pending[1])
        grads = [dWin, dWout.reshape(N_DEV, -1, D), dWup, dWdown.reshape(N_DEV, -1, D)]
        pending = (_scatter_sibling_start(grads, "scatter_sibling_start"), l)
        small_grads[l] = [dwb[ks], dlng[0], dlnb[0], drpb, dg_pre_mix[0], dg_post_mix[0], dg_pre_mlp[0],
                          dg_post_mlp[0], dwb[:ks]]
    st2 = scatter_mid(pending[0], pending[0]["token"])
    adam(_scatter_chips_finish(st2, st2["token"], "scatter_chips_finish"), pending[1])

    rep_shapes = [(L, cw), (L, cw), (L, cw), (L, H, N_DR, N_DC), (L, D), (L, D), (L, D), (L, D)]
    stacked = [jnp.stack([small_grads[l][k] for l in range(L)]) for k in range(9)]
    packed = _pack(stacked)
    parts = _all_gather([packed], [0], "ag_small")[0].reshape(N_DEV, *packed.shape)
    gsum = _sum_parts(parts, "sum_small")
    g_small = _unpack(gsum, rep_shapes + [(L, ks, cw)])
    g_rep, g_wdw_full = g_small[:8], g_small[8]
    wsh = w_dw.shape[2]
    g_wdw = lax.dynamic_slice_in_dim(g_wdw_full, dev[0] * wsh, wsh, axis=2)

    rep_w = [b_dw, conv_ln_g, conv_ln_b, rpb, pre_mix_g, post_mix_g, pre_mlp_g, post_mlp_g]
    rep_m = [m_b_dw, m_conv_ln_g, m_conv_ln_b, m_rpb, m_pre_mix_g, m_post_mix_g, m_pre_mlp_g, m_post_mlp_g]
    rep_v = [v_b_dw, v_conv_ln_g, v_conv_ln_b, v_rpb, v_pre_mix_g, v_post_mix_g, v_pre_mlp_g, v_post_mlp_g]
    rep_out = _adam_flat(_pack(g_rep), _pack(rep_w), _pack(rep_m), _pack(rep_v), "adam_small")
    rep_delta, rep_nm, rep_nv = (_unpack(o, rep_shapes) for o in rep_out)
    dw_out = _adam_flat(_pack([g_wdw]), _pack([w_dw]), _pack([m_w_dw]), _pack([v_w_dw]), "adam_wdw")
    wdw_delta, wdw_nm, wdw_nv = (_unpack(o, [w_dw.shape])[0] for o in dw_out)

    def assemble(kind_big, rep_list, wdw_val):
        return [big_out[0][kind_big], wdw_val, rep_list[0], rep_list[1], rep_list[2], rep_list[3],
                big_out[1][kind_big], big_out[2][kind_big], big_out[3][kind_big],
                rep_list[4], rep_list[5], rep_list[6], rep_list[7]]

    grads_out = assemble(0, g_rep, g_wdw)
    deltas = assemble(1, rep_delta, wdw_delta)
    new_m = assemble(2, rep_nm, wdw_nm)
    new_v = assemble(3, rep_nv, wdw_nv)
    return (loss, dxo.reshape(1, T, D), *grads_out, *deltas, *new_m, *new_v)
```

```python
import math

import jax
import jax.numpy as jnp
from jax import lax
from jax.experimental import pallas as pl
from jax.experimental.pallas import tpu as pltpu

_MXU = jnp.bfloat16
_WIRE = jnp.bfloat16
_F32 = jnp.float32

N_DEV = 8
GRID_W = 64
WIN_ROWS = 8
WIN_COLS = 16
HEAD_DIM = 64
LANES = 128
SUBLANES = 8
RMS_EPS = 1e-6
LN_EPS = 1e-5
NEG_INF = -1e30
ADAM_LR = 0.001
ADAM_B1 = 0.9
ADAM_B2 = 0.999
ADAM_EPS = 1e-08
ADAM_WD = 0.01
ADAM_STEP = 10
VMEM_BYTES_V7X = 64 << 20
VMEM_RESERVE = 12 << 20
MESH_AXES = ("x", "y", "c")


def _vmem_limit(block_bytes):
    return int(min(max(block_bytes + (8 << 20), 24 << 20), VMEM_BYTES_V7X - VMEM_RESERVE))


def _blk(n, pref):
    if n <= pref:
        return n
    for t in range(pref, 7, -1):
        if n % t == 0 and t % SUBLANES == 0:
            return t
    return n


def _sigmoid(v):
    return 1.0 / (1.0 + jnp.exp(-v))


def _params(sem, nbytes):
    return pltpu.CompilerParams(dimension_semantics=sem, vmem_limit_bytes=_vmem_limit(nbytes))


def _cast_slot(w3, l, dev, by_cols, name):
    _, R, C = w3.shape
    tr = _blk(R, 512)

    def body(dev_ref, w_ref, o_ref):
        o_ref[...] = w_ref[...].astype(o_ref.dtype)

    if by_cols:
        shape, o_spec = (R, N_DEV * C), pl.BlockSpec((tr, C), lambda i, d: (i, d[0]))
    else:
        shape, o_spec = (N_DEV * R, C), pl.BlockSpec((tr, C), lambda i, d: (d[0] * (R // tr) + i, 0))
    grid_spec = pltpu.PrefetchScalarGridSpec(
        num_scalar_prefetch=1, grid=(R // tr,),
        in_specs=[pl.BlockSpec((None, tr, C), lambda i, d: (l, i, 0))], out_specs=o_spec)
    return pl.pallas_call(
        body, name=name, out_shape=jax.ShapeDtypeStruct(shape, _WIRE), grid_spec=grid_spec,
        compiler_params=_params(("parallel",), 2 * tr * C * 6),
    )(dev, w3)


def _norm_fwd(x, g, name):
    T, D = x.shape
    tm = _blk(T, 256)

    def body(x_ref, g_ref, h_ref):
        xv = x_ref[...]
        r = lax.rsqrt(jnp.mean(xv * xv, axis=-1, keepdims=True) + RMS_EPS)
        h_ref[...] = (xv * r * g_ref[...]).astype(h_ref.dtype)

    return pl.pallas_call(
        body, name=name, out_shape=jax.ShapeDtypeStruct((T, D), _MXU), grid=(T // tm,),
        in_specs=[pl.BlockSpec((tm, D), lambda i: (i, 0)), pl.BlockSpec((1, D), lambda i: (0, 0))],
        out_specs=pl.BlockSpec((tm, D), lambda i: (i, 0)),
        compiler_params=_params(("parallel",), 2 * tm * D * 6),
    )(x, g)


def _resid_norm_fwd(xres, y, g_post, g_next, name):
    T, D = xres.shape
    tm = _blk(T, 256)

    def body(x_ref, y_ref, gp_ref, gn_ref, xn_ref, h_ref):
        yv = y_ref[...]
        r = lax.rsqrt(jnp.mean(yv * yv, axis=-1, keepdims=True) + RMS_EPS)
        xn = x_ref[...] + yv * r * gp_ref[...]
        xn_ref[...] = xn
        r2 = lax.rsqrt(jnp.mean(xn * xn, axis=-1, keepdims=True) + RMS_EPS)
        h_ref[...] = (xn * r2 * gn_ref[...]).astype(h_ref.dtype)

    row = pl.BlockSpec((tm, D), lambda i: (i, 0))
    vec = pl.BlockSpec((1, D), lambda i: (0, 0))
    return pl.pallas_call(
        body, name=name,
        out_shape=(jax.ShapeDtypeStruct((T, D), _F32), jax.ShapeDtypeStruct((T, D), _MXU)),
        grid=(T // tm,), in_specs=[row, row, vec, vec], out_specs=(row, row),
        compiler_params=_params(("parallel",), 2 * tm * D * 14),
    )(xres, y, g_post, g_next)


def _resid_loss(xres, y, g_post, target, name):
    T, D = xres.shape
    tm = _blk(T, 256)

    def body(x_ref, y_ref, gp_ref, t_ref, dy_ref, loss_ref):
        yv = y_ref[...]
        r = lax.rsqrt(jnp.mean(yv * yv, axis=-1, keepdims=True) + RMS_EPS)
        err = x_ref[...] + yv * r * gp_ref[...] - t_ref[...]
        dy_ref[...] = err * (1.0 / D)

        @pl.when(pl.program_id(0) == 0)
        def _():
            loss_ref[...] = jnp.zeros_like(loss_ref)

        part = jnp.sum(jnp.sum(err * err, axis=-1, keepdims=True), axis=0, keepdims=True)
        loss_ref[...] += part

    row = pl.BlockSpec((tm, D), lambda i: (i, 0))
    vec = pl.BlockSpec((1, D), lambda i: (0, 0))
    return pl.pallas_call(
        body, name=name,
        out_shape=(jax.ShapeDtypeStruct((T, D), _F32), jax.ShapeDtypeStruct((1, 1), _F32)),
        grid=(T // tm,), in_specs=[row, row, vec, row],
        out_specs=(row, pl.BlockSpec((1, 1), lambda i: (0, 0))),
        compiler_params=_params(("arbitrary",), 2 * tm * D * 16),
    )(xres, y, g_post, target)


def _norm_bwd(y, g, dout, dres, out_dtype, name):
    T, D = y.shape
    tm = _blk(T, 256)
    nsteps = T // tm
    has_res = dres is not None

    def body(*refs):
        if has_res:
            y_ref, g_ref, do_ref, dr_ref, dy_ref, dg_ref, acc = refs
        else:
            y_ref, g_ref, do_ref, dy_ref, dg_ref, acc = refs
        i = pl.program_id(0)
        yv = y_ref[...]
        do = do_ref[...]
        r = lax.rsqrt(jnp.mean(yv * yv, axis=-1, keepdims=True) + RMS_EPS)
        gy = do * g_ref[...]
        dot = jnp.mean(yv * gy, axis=-1, keepdims=True)
        dy = r * gy - yv * (r * r * r * dot)
        if has_res:
            dy = dy + dr_ref[...]
        dy_ref[...] = dy.astype(dy_ref.dtype)

        @pl.when(i == 0)
        def _():
            acc[...] = jnp.zeros_like(acc)

        acc[...] += jnp.sum((do * yv * r).reshape(tm // SUBLANES, SUBLANES, D), axis=0)

        @pl.when(i == nsteps - 1)
        def _():
            dg_ref[...] = jnp.sum(acc[...], axis=0, keepdims=True)

    row = pl.BlockSpec((tm, D), lambda i: (i, 0))
    vec = pl.BlockSpec((1, D), lambda i: (0, 0))
    ins = [y, g, dout] + ([dres] if has_res else [])
    in_specs = [row, vec, row] + ([row] if has_res else [])
    return pl.pallas_call(
        body, name=name,
        out_shape=(jax.ShapeDtypeStruct((T, D), out_dtype), jax.ShapeDtypeStruct((1, D), _F32)),
        grid=(nsteps,), in_specs=in_specs, out_specs=(row, vec),
        scratch_shapes=[pltpu.VMEM((SUBLANES, D), _F32)],
        compiler_params=_params(("arbitrary",), 2 * tm * D * 16),
    )(*ins)


def _matmul(a, b, *, ta=False, tb=False, out_dtype=_F32, epilogue=None, extra=None, out_cols=0,
            tm=1024, tk=2048, name):
    M, K = (a.shape[1], a.shape[0]) if ta else a.shape
    N = b.shape[0] if tb else b.shape[1]
    tm = _blk(M, tm)
    tn = N // out_cols if out_cols else _blk(N, 1024)
    tk = _blk(K, tk)
    b_spec = (pl.BlockSpec((tn, tk), lambda i, j, k: (j, k)) if tb
              else pl.BlockSpec((tk, tn), lambda i, j, k: (k, j)))
    nk = K // tk
    a_spec = (pl.BlockSpec((tk, tm), lambda i, j, k: (k, i)) if ta
              else pl.BlockSpec((tm, tk), lambda i, j, k: (i, k)))
    if out_cols:
        assert N // tn == out_cols and epilogue is None
        o_spec = pl.BlockSpec((None, tm, tn), lambda i, j, k: (j, i, 0))
        o_shape = (out_cols, M, tn)
    else:
        o_spec = pl.BlockSpec((tm, tn), lambda i, j, k: (i, j))
        o_shape = (M, N)
    dims = (((0 if ta else 1,), (1 if tb else 0,)), ((), ()))
    n_extra = 1 if epilogue == "mul2" else 0
    n_out = 2 if epilogue == "relu2" else 1

    def finish(acc, extra_refs, out_refs):
        if epilogue is None:
            out_refs[0][...] = acc.astype(out_refs[0].dtype)
        elif epilogue == "relu2":
            rl = jnp.maximum(acc, 0.0)
            out_refs[0][...] = (rl * rl).astype(out_refs[0].dtype)
            out_refs[1][...] = rl.astype(out_refs[1].dtype)
        else:
            out_refs[0][...] = (acc * (2.0 * extra_refs[0][...].astype(_F32))).astype(out_refs[0].dtype)

    def body(a_ref, b_ref, *rest):
        extra_refs = rest[:n_extra]
        out_refs = rest[n_extra:n_extra + n_out]
        part = lax.dot_general(a_ref[...], b_ref[...], dims, preferred_element_type=_F32)
        if nk == 1:
            finish(part, extra_refs, out_refs)
            return
        acc = rest[-1]
        k = pl.program_id(2)

        @pl.when(k == 0)
        def _():
            acc[...] = part

        @pl.when(k > 0)
        def _():
            acc[...] += part

        @pl.when(k == nk - 1)
        def _():
            finish(acc[...], extra_refs, out_refs)

    if epilogue == "relu2":
        out_shape = (jax.ShapeDtypeStruct((M, N), _MXU), jax.ShapeDtypeStruct((M, N), _MXU))
        out_specs = (o_spec, o_spec)
        out_bytes = 2 * tm * tn * 2
    else:
        odt = _MXU if epilogue == "mul2" else out_dtype
        out_shape = jax.ShapeDtypeStruct(o_shape, odt)
        out_specs = o_spec
        out_bytes = tm * tn * jnp.dtype(odt).itemsize
    in_specs = [a_spec, b_spec] + ([o_spec] if n_extra else [])
    ins = [a, b] + ([extra] if n_extra else [])
    blocks = 2 * (tm * tk * 2 + tk * tn * 2 + out_bytes + n_extra * tm * tn * 2) + tm * tn * 4 * 2
    return pl.pallas_call(
        body, name=name, out_shape=out_shape, grid=(M // tm, N // tn, nk),
        in_specs=in_specs, out_specs=out_specs,
        scratch_shapes=[pltpu.VMEM((tm, tn), _F32)] if nk > 1 else [],
        compiler_params=_params(("parallel", "parallel", "arbitrary"), blocks),
    )(*ins)


CONV_HALO = 16
CONV_CHUNK = 256


def _tap_windows(win, n_taps_plus1, tc):
    n = win.shape[0]
    for s in range(SUBLANES):
        shifted = win if s == 0 else pltpu.roll(win, n - s, 0)
        for q in range((n_taps_plus1 + SUBLANES - 1) // SUBLANES):
            o = SUBLANES * q + s
            if 1 <= o < n_taps_plus1:
                yield o, shifted[SUBLANES * q:SUBLANES * q + tc, :]


def _conv_fwd(proj, wdw, bdw, cw, name):
    T = proj.shape[0]
    ks = wdw.shape[0]
    cb = LANES
    tc = _blk(T, CONV_CHUNK)
    nblk = cw // cb

    def body(a_ref, g_ref, w_ref, b_ref, c_ref, upad):
        zeros = jnp.zeros((CONV_HALO, cb), _F32)
        upad[0:CONV_HALO, :] = zeros
        upad[T + CONV_HALO:T + 2 * CONV_HALO, :] = zeros
        upad[CONV_HALO:T + CONV_HALO, :] = a_ref[...] * _sigmoid(g_ref[...])

        def chunk(i, carry):
            t0 = pl.multiple_of(i * tc, tc)
            win = upad[pl.ds(t0, tc + 2 * CONV_HALO), :]
            acc = jnp.broadcast_to(b_ref[...], (tc, cb))
            for o, rows in _tap_windows(win, ks + 1, tc):
                j = o + ks // 2 - CONV_HALO
                acc = acc + rows * w_ref[j:j + 1, :]
            c_ref[pl.ds(t0, tc), :] = acc
            return carry

        lax.fori_loop(0, T // tc, chunk, 0)

    col = lambda off: pl.BlockSpec((T, cb), lambda i, off=off: (0, off + i))
    return pl.pallas_call(
        body, name=name, out_shape=jax.ShapeDtypeStruct((T, cw), _F32), grid=(nblk,),
        in_specs=[col(0), col(nblk), pl.BlockSpec((ks, cb), lambda i: (0, i)),
                  pl.BlockSpec((1, cb), lambda i: (0, i))],
        out_specs=pl.BlockSpec((T, cb), lambda i: (0, i)),
        scratch_shapes=[pltpu.VMEM((T + 2 * CONV_HALO, cb), _F32)],
        compiler_params=_params(("parallel",), 2 * T * cb * 4 * 3 + T * cb * 4),
    )(proj, proj, wdw, bdw)


def _ln_silu_fwd(c, lng, lnb, name):
    T, cw = c.shape
    tm = _blk(T, 512)

    def body(c_ref, g_ref, b_ref, y_ref):
        cv = c_ref[...]
        mu = jnp.mean(cv, axis=-1, keepdims=True)
        xc = cv - mu
        var = jnp.mean(xc * xc, axis=-1, keepdims=True)
        z = xc * lax.rsqrt(var + LN_EPS) * g_ref[...] + b_ref[...]
        y_ref[...] = (z * _sigmoid(z)).astype(y_ref.dtype)

    row = pl.BlockSpec((tm, cw), lambda i: (i, 0))
    vec = pl.BlockSpec((1, cw), lambda i: (0, 0))
    return pl.pallas_call(
        body, name=name, out_shape=jax.ShapeDtypeStruct((T, cw), _MXU), grid=(T // tm,),
        in_specs=[row, vec, vec], out_specs=row,
        compiler_params=_params(("parallel",), 2 * tm * cw * 6),
    )(c, lng, lnb)


def _ln_silu_bwd(c, lng, lnb, dycat, name):
    T, cw = c.shape
    tm = _blk(T, 512)
    nsteps = T // tm

    def body(c_ref, g_ref, b_ref, dy_ref, dc_ref, dg_ref, db_ref, accg, accb):
        i = pl.program_id(0)
        cv = c_ref[...]
        mu = jnp.mean(cv, axis=-1, keepdims=True)
        xc = cv - mu
        var = jnp.mean(xc * xc, axis=-1, keepdims=True)
        rstd = lax.rsqrt(var + LN_EPS)
        xhat = xc * rstd
        z = xhat * g_ref[...] + b_ref[...]
        sg = _sigmoid(z)
        dz = dy_ref[...] * (sg * (1.0 + z * (1.0 - sg)))
        dxh = dz * g_ref[...]
        m1 = jnp.mean(dxh, axis=-1, keepdims=True)
        m2 = jnp.mean(dxh * xhat, axis=-1, keepdims=True)
        dc_ref[...] = rstd * (dxh - m1 - xhat * m2)

        @pl.when(i == 0)
        def _():
            accg[...] = jnp.zeros_like(accg)
            accb[...] = jnp.zeros_like(accb)

        accg[...] += jnp.sum((dz * xhat).reshape(tm // SUBLANES, SUBLANES, cw), axis=0)
        accb[...] += jnp.sum(dz.reshape(tm // SUBLANES, SUBLANES, cw), axis=0)

        @pl.when(i == nsteps - 1)
        def _():
            dg_ref[...] = jnp.sum(accg[...], axis=0, keepdims=True)
            db_ref[...] = jnp.sum(accb[...], axis=0, keepdims=True)

    row = pl.BlockSpec((tm, cw), lambda i: (i, 0))
    vec = pl.BlockSpec((1, cw), lambda i: (0, 0))
    return pl.pallas_call(
        body, name=name,
        out_shape=(jax.ShapeDtypeStruct((T, cw), _F32), jax.ShapeDtypeStruct((1, cw), _F32),
                   jax.ShapeDtypeStruct((1, cw), _F32)),
        grid=(nsteps,), in_specs=[row, vec, vec, row], out_specs=(row, vec, vec),
        scratch_shapes=[pltpu.VMEM((SUBLANES, cw), _F32), pltpu.VMEM((SUBLANES, cw), _F32)],
        compiler_params=_params(("arbitrary",), 2 * tm * cw * 12),
    )(c, lng, lnb, dycat)


def _conv_bwd(proj, dc, wdw, cw, name):
    T = proj.shape[0]
    ks = wdw.shape[0]
    cb = LANES
    tc = _blk(T, CONV_CHUNK)
    nblk = cw // cb
    half = ks // 2

    def body(a_ref, g_ref, dc_ref, w_ref, da_ref, dg_ref, dwb_ref, upad, dpad, du, wacc):
        zeros = jnp.zeros((CONV_HALO, cb), _F32)
        for pad in (upad, dpad):
            pad[0:CONV_HALO, :] = zeros
            pad[T + CONV_HALO:T + 2 * CONV_HALO, :] = zeros
        sg = _sigmoid(g_ref[...])
        upad[CONV_HALO:T + CONV_HALO, :] = a_ref[...] * sg
        dpad[CONV_HALO:T + CONV_HALO, :] = dc_ref[...]
        wacc[...] = jnp.zeros_like(wacc)

        def chunk(i, carry):
            t0 = pl.multiple_of(i * tc, tc)
            dwin = dpad[pl.ds(t0, tc + 2 * CONV_HALO), :]
            uwin = upad[pl.ds(t0, tc + 2 * CONV_HALO), :]
            dcc = dwin[CONV_HALO:CONV_HALO + tc, :]
            acc = jnp.zeros((tc, cb), _F32)
            for o, rows in _tap_windows(dwin, CONV_HALO + half + 1, tc):
                j = CONV_HALO + half - o
                if 0 <= j < ks:
                    acc = acc + rows * w_ref[j:j + 1, :]
            du[pl.ds(t0, tc), :] = acc
            for o, rows in _tap_windows(uwin, CONV_HALO + half + 1, tc):
                j = o + half - CONV_HALO
                if 0 <= j < ks:
                    wacc[j] += jnp.sum((rows * dcc).reshape(tc // SUBLANES, SUBLANES, cb), axis=0)
            wacc[ks] += jnp.sum(dcc.reshape(tc // SUBLANES, SUBLANES, cb), axis=0)
            return carry

        lax.fori_loop(0, T // tc, chunk, 0)
        duv = du[...]
        av = a_ref[...]
        da_ref[...] = (duv * sg).astype(da_ref.dtype)
        dg_ref[...] = (duv * av * sg * (1.0 - sg)).astype(dg_ref.dtype)
        dwb_ref[...] = jnp.sum(wacc[...], axis=1)

    col = lambda off: pl.BlockSpec((T, cb), lambda i, off=off: (0, off + i))
    blk = pl.BlockSpec((T, cb), lambda i: (0, i))
    return pl.pallas_call(
        body, name=name,
        out_shape=(jax.ShapeDtypeStruct((T, cw), _MXU), jax.ShapeDtypeStruct((T, cw), _MXU),
                   jax.ShapeDtypeStruct((ks + 1, cw), _F32)),
        grid=(nblk,),
        in_specs=[col(0), col(nblk), blk, pl.BlockSpec((ks, cb), lambda i: (0, i))],
        out_specs=(blk, blk, pl.BlockSpec((ks + 1, cb), lambda i: (0, i))),
        scratch_shapes=[pltpu.VMEM((T + 2 * CONV_HALO, cb), _F32), pltpu.VMEM((T + 2 * CONV_HALO, cb), _F32),
                        pltpu.VMEM((T, cb), _F32), pltpu.VMEM((ks + 1, SUBLANES, cb), _F32)],
        compiler_params=_params(("parallel",), 2 * T * cb * 4 * 4 + 3 * T * cb * 4),
    )(proj, proj, dc, wdw)


N_CLS = WIN_ROWS
N_DR = 2 * WIN_ROWS - 1
N_DC = 2 * WIN_COLS - 1
BAND = WIN_ROWS * GRID_W
QK_SCALE = HEAD_DIM ** -0.5
ROWS_PER_STEP_FWD = 4
ROWS_PER_STEP_BWD = 2
_NT = (((1,), (1,)), ((), ()))
_TN = (((0,), (0,)), ((), ()))


def _slab_iotas():
    wk = lax.broadcasted_iota(jnp.int32, (GRID_W, LANES), 0)
    lane = lax.broadcasted_iota(jnp.int32, (GRID_W, LANES), 1)
    wq = jnp.bitwise_and(lane, GRID_W - 1)
    head1 = lane >= GRID_W
    d = wk - wq + (WIN_COLS - 1)
    cs = jnp.clip(wq - WIN_COLS // 2, 0, GRID_W - WIN_COLS)
    window = (wk >= cs) & (wk < cs + WIN_COLS)
    return d, head1, window


def _bias_table(rpb2, name):
    npair = rpb2.shape[0] // 2

    def body(rpb_ref, o_ref):
        p = pl.program_id(0)
        d, head1, window = _slab_iotas()
        for dr in range(N_DR):
            val = jnp.zeros((GRID_W, LANES), _F32)
            for j in range(N_DC):
                s0 = rpb_ref[2 * p, dr * N_DC + j]
                s1 = rpb_ref[2 * p + 1, dr * N_DC + j]
                val = jnp.where(d == j, jnp.where(head1, s1, s0), val)
            slab = jnp.where(window, val, NEG_INF)
            for cls in range(N_CLS):
                k = dr - cls
                if 0 <= k < WIN_ROWS:
                    o_ref[cls, k * GRID_W:(k + 1) * GRID_W, :] = slab

    return pl.pallas_call(
        body, name=name, out_shape=jax.ShapeDtypeStruct((npair, N_CLS, BAND, LANES), _F32), grid=(npair,),
        in_specs=[pl.BlockSpec(memory_space=pltpu.SMEM)],
        out_specs=pl.BlockSpec((None, N_CLS, BAND, LANES), lambda p: (p, 0, 0, 0)),
        compiler_params=_params(("arbitrary",), 2 * N_CLS * BAND * LANES * 4),
    )(rpb2)


def _rpb_grad(gc, name):
    npair = gc.shape[0]

    def body(g_ref, o_ref):
        d, _, _ = _slab_iotas()
        rowi = lax.broadcasted_iota(jnp.int32, (4 * SUBLANES, LANES), 0)
        lanei = lax.broadcasted_iota(jnp.int32, (4 * SUBLANES, LANES), 1)
        head1 = lax.broadcasted_iota(jnp.int32, (1, LANES), 1) >= GRID_W
        tiles = [jnp.zeros((4 * SUBLANES, LANES), _F32) for _ in range(2)]
        for dr in range(N_DR):
            ysum = jnp.zeros((GRID_W, LANES), _F32)
            for cls in range(N_CLS):
                k = dr - cls
                if 0 <= k < WIN_ROWS:
                    ysum = ysum + g_ref[cls, k * GRID_W:(k + 1) * GRID_W, :]
            for j in range(N_DC):
                cs = jnp.sum(jnp.where(d == j, ysum, 0.0), axis=0, keepdims=True)
                s0 = jnp.sum(jnp.where(head1, 0.0, cs), axis=1, keepdims=True)
                s1 = jnp.sum(jnp.where(head1, cs, 0.0), axis=1, keepdims=True)
                here = (rowi == j) & (lanei == dr)
                tiles[0] = tiles[0] + jnp.where(here, s0, 0.0)
                tiles[1] = tiles[1] + jnp.where(here, s1, 0.0)
        o_ref[0] = tiles[0]
        o_ref[1] = tiles[1]

    return pl.pallas_call(
        body, name=name, out_shape=jax.ShapeDtypeStruct((npair, 2, 4 * SUBLANES, LANES), _F32), grid=(npair,),
        in_specs=[pl.BlockSpec((None, N_CLS, BAND, LANES), lambda p: (p, 0, 0, 0))],
        out_specs=pl.BlockSpec((None, 2, 4 * SUBLANES, LANES), lambda p: (p, 0, 0, 0)),
        compiler_params=_params(("parallel",), 2 * N_CLS * BAND * LANES * 4),
    )(gc)


def _block_diag(v, diag):
    return jnp.where(diag, jnp.concatenate([v, v], axis=0), 0.0).astype(_MXU)


def _diag_mask():
    r = lax.broadcasted_iota(jnp.int32, (LANES, LANES), 0) < GRID_W
    c = lax.broadcasted_iota(jnp.int32, (LANES, LANES), 1) < HEAD_DIM
    return r == c


def _row_geometry(r, rows):
    rs = jnp.clip(r - WIN_ROWS // 2, 0, rows - WIN_ROWS)
    cls = rs - r + (WIN_ROWS - 1)
    return pl.multiple_of(r * GRID_W, GRID_W), pl.multiple_of(rs * GRID_W, GRID_W), cls


def _probs_t(qsel, kband, bias):
    s = lax.dot_general(kband, qsel, _NT, preferred_element_type=_F32) + bias
    mx = jnp.max(s, axis=0, keepdims=True)
    e = jnp.exp(s - mx)
    return e * (1.0 / jnp.sum(e, axis=0, keepdims=True))


def _attn_fwd(proj, bias, cw, naw, name):
    T = proj.shape[0]
    rows = T // GRID_W
    npair = naw // LANES
    qoff, koff, voff = 2 * cw // LANES, (2 * cw + naw) // LANES, (2 * cw + 2 * naw) // LANES

    def body(q_ref, k_ref, v_ref, b_ref, o_ref, kb, vb):
        kb[...] = k_ref[...].astype(_MXU)
        vb[...] = v_ref[...].astype(_MXU)
        diag = _diag_mask()
        m0 = lax.broadcasted_iota(jnp.int32, (GRID_W, LANES), 1) < HEAD_DIM

        def step(i, carry):
            us = range(ROWS_PER_STEP_FWD)
            geo = [_row_geometry(ROWS_PER_STEP_FWD * i + u, rows) for u in us]
            qsel = [_block_diag(q_ref[pl.ds(t0, GRID_W), :] * QK_SCALE, diag) for t0, _, _ in geo]
            kbands = [kb[pl.ds(b0, BAND), :] for _, b0, _ in geo]
            vbands = [vb[pl.ds(b0, BAND), :] for _, b0, _ in geo]
            biases = [b_ref[cls] for _, _, cls in geo]
            pts = [_probs_t(qsel[u], kbands[u], biases[u]) for u in us]
            ofs = [lax.dot_general(pts[u].astype(_MXU), vbands[u], _TN, preferred_element_type=_F32) for u in us]
            for u in us:
                o_ref[pl.ds(geo[u][0], GRID_W), :] = jnp.where(m0, ofs[u][:GRID_W], ofs[u][GRID_W:]).astype(o_ref.dtype)
            return carry

        lax.fori_loop(0, rows // ROWS_PER_STEP_FWD, step, 0)

    col = lambda off: pl.BlockSpec((T, LANES), lambda i, off=off: (0, off + i))
    return pl.pallas_call(
        body, name=name, out_shape=jax.ShapeDtypeStruct((T, naw), _MXU), grid=(npair,),
        in_specs=[col(qoff), col(koff), col(voff),
                  pl.BlockSpec((None, N_CLS, BAND, LANES), lambda i: (i, 0, 0, 0))],
        out_specs=pl.BlockSpec((T, LANES), lambda i: (0, i)),
        scratch_shapes=[pltpu.VMEM((T, LANES), _MXU), pltpu.VMEM((T, LANES), _MXU)],
        compiler_params=_params(("parallel",), 2 * (3 * T * LANES * 4 + N_CLS * BAND * LANES * 4 + T * LANES * 2)),
    )(proj, proj, proj, bias)


def _attn_bwd(proj, bias, dycat, cw, naw, name):
    T = proj.shape[0]
    rows = T // GRID_W
    npair = naw // LANES
    qoff, koff, voff = 2 * cw // LANES, (2 * cw + naw) // LANES, (2 * cw + 2 * naw) // LANES
    doff = cw // LANES

    def body(q_ref, k_ref, v_ref, b_ref, do_ref, dq_ref, dk_ref, dv_ref, g_ref, kb, vb, dka, dva):
        kb[...] = k_ref[...].astype(_MXU)
        vb[...] = v_ref[...].astype(_MXU)
        dka[...] = jnp.zeros_like(dka)
        dva[...] = jnp.zeros_like(dva)
        g_ref[...] = jnp.zeros_like(g_ref)
        diag = _diag_mask()
        m0 = lax.broadcasted_iota(jnp.int32, (GRID_W, LANES), 1) < HEAD_DIM

        def step(i, carry):
            us = range(ROWS_PER_STEP_BWD)
            geo = [_row_geometry(ROWS_PER_STEP_BWD * i + u, rows) for u in us]
            qsel = [_block_diag(q_ref[pl.ds(t0, GRID_W), :] * QK_SCALE, diag) for t0, _, _ in geo]
            dosel = [_block_diag(do_ref[pl.ds(t0, GRID_W), :], diag) for t0, _, _ in geo]
            kbands = [kb[pl.ds(b0, BAND), :] for _, b0, _ in geo]
            vbands = [vb[pl.ds(b0, BAND), :] for _, b0, _ in geo]
            biases = [b_ref[cls] for _, _, cls in geo]
            dsts, dqs, dks, dvs = [], [], [], []
            for u in us:
                pt = _probs_t(qsel[u], kbands[u], biases[u])
                dpt = lax.dot_general(vbands[u], dosel[u], _NT, preferred_element_type=_F32)
                delta = jnp.sum(pt * dpt, axis=0, keepdims=True)
                dst = pt * (dpt - delta)
                dsb = dst.astype(_MXU)
                dqf = lax.dot_general(dsb, kbands[u], _TN, preferred_element_type=_F32)
                dsts.append(dst)
                dqs.append((jnp.where(m0, dqf[:GRID_W], dqf[GRID_W:]) * QK_SCALE).astype(dq_ref.dtype))
                dks.append(jnp.dot(dsb, qsel[u], preferred_element_type=_F32))
                dvs.append(jnp.dot(pt.astype(_MXU), dosel[u], preferred_element_type=_F32))
            for u in us:
                t0, b0, cls = geo[u]
                g_ref[cls] += dsts[u]
                dq_ref[pl.ds(t0, GRID_W), :] = dqs[u]
                dka[pl.ds(b0, BAND), :] += dks[u]
                dva[pl.ds(b0, BAND), :] += dvs[u]
            return carry

        lax.fori_loop(0, rows // ROWS_PER_STEP_BWD, step, 0)
        dk_ref[...] = dka[...].astype(dk_ref.dtype)
        dv_ref[...] = dva[...].astype(dv_ref.dtype)

    col = lambda off: pl.BlockSpec((T, LANES), lambda i, off=off: (0, off + i))
    blk = pl.BlockSpec((T, LANES), lambda i: (0, i))
    tbl = pl.BlockSpec((None, N_CLS, BAND, LANES), lambda i: (i, 0, 0, 0))
    o16 = jax.ShapeDtypeStruct((T, naw), _MXU)
    vm = 2 * (4 * T * LANES * 4 + 2 * N_CLS * BAND * LANES * 4 + 3 * T * LANES * 2) + 2 * T * LANES * 6
    return pl.pallas_call(
        body, name=name,
        out_shape=(o16, o16, o16, jax.ShapeDtypeStruct((npair, N_CLS, BAND, LANES), _F32)),
        grid=(npair,),
        in_specs=[col(qoff), col(koff), col(voff), tbl, col(doff)],
        out_specs=(blk, blk, blk, tbl),
        scratch_shapes=[pltpu.VMEM((T, LANES), _MXU), pltpu.VMEM((T, LANES), _MXU),
                        pltpu.VMEM((T, LANES), _F32), pltpu.VMEM((T, LANES), _F32)],
        compiler_params=_params(("parallel",), vm),
    )(proj, proj, proj, bias, dycat)


_ANY = pl.BlockSpec(memory_space=pl.ANY)
_HBM = pl.BlockSpec(memory_space=pltpu.HBM)
_SEM = pl.BlockSpec(memory_space=pltpu.SEMAPHORE)
_VMEM = pl.BlockSpec(memory_space=pltpu.VMEM)
_MESH_ID = pl.DeviceIdType.MESH
_EFFECT = pltpu.SideEffectType.DATAFLOW_SIDE_EFFECTING
_TOKEN = jax.ShapeDtypeStruct((SUBLANES, LANES), _F32)


def _mesh_pos():
    return tuple(lax.axis_index(a) for a in MESH_AXES)


def _in_hbm(a):
    return pltpu.with_memory_space_constraint(a, pltpu.HBM)


def _hbm_like(arrays):
    return [pltpu.HBM(a.shape, a.dtype) for a in arrays]


def _after(arr, token):
    return arr + token[0:1, 0:1].astype(arr.dtype)


def _shard_ref(ref, axis, j, width):
    idx = [slice(None)] * len(ref.shape)
    idx[axis] = pl.ds(pl.multiple_of(j * width, math.gcd(width, LANES)), width)
    return ref.at[tuple(idx)]


def _all_gather(shards, axes, name):
    n = len(shards)
    widths = [s.shape[a] for s, a in zip(shards, axes)]
    out_shape = [jax.ShapeDtypeStruct(tuple(N_DEV * d if k == a else d for k, d in enumerate(s.shape)), s.dtype)
                 for s, a in zip(shards, axes)]

    def body(*refs):
        ins, outs = refs[:n], refs[n:2 * n]
        send_sems, recv_sems, local_sems = refs[2 * n:]
        x, y, c = _mesh_pos()
        me, sibling = (x, y, c), (x, y, 1 - c)
        chips = [(1 - x, y), (x, 1 - y), (1 - x, 1 - y)]

        def slot(i, px, py, pc):
            return _shard_ref(outs[i], axes[i], 4 * px + 2 * py + pc, widths[i])

        def copy(i, k, block, to, src=None):
            return pltpu.make_async_remote_copy(
                src_ref=slot(i, *block) if src is None else src, dst_ref=slot(i, *block),
                send_sem=send_sems.at[7 * i + k], recv_sem=recv_sems.at[7 * i + k],
                device_id=to, device_id_type=_MESH_ID)

        mine = [pltpu.make_async_copy(ins[i], slot(i, *me), local_sems.at[i]) for i in range(n)]
        for cp in mine:
            cp.start()
        first = []
        for i in range(n):
            first.append(copy(i, 0, me, sibling, src=ins[i]))
            first += [copy(i, 1 + j, me, (*chip, c), src=ins[i]) for j, chip in enumerate(chips)]
        for cp in first:
            cp.start()
        passed = []
        for j, chip in enumerate(chips):
            for i in range(n):
                copy(i, 1 + j, (*chip, c), me).wait_recv()
                fwd = copy(i, 4 + j, (*chip, c), sibling)
                fwd.start()
                passed.append(fwd)
        for i in range(n):
            copy(i, 0, sibling, me).wait_recv()
            for j, chip in enumerate(chips):
                copy(i, 4 + j, (*chip, 1 - c), me).wait_recv()
        for cp in first + passed:
            cp.wait_send()
        for cp in mine:
            cp.wait()

    return pl.pallas_call(
        body, name=name, out_shape=out_shape, in_specs=[_ANY] * n, out_specs=[_ANY] * n,
        scratch_shapes=[pltpu.SemaphoreType.DMA((7 * n,)), pltpu.SemaphoreType.DMA((7 * n,)),
                        pltpu.SemaphoreType.DMA((n,))],
    )(*shards)


def _block_of(ref, axis, blk):
    return _shard_ref(ref, axis, blk, ref.shape[axis] // N_DEV)


def _gather_start(lands, axes, after, name):
    n = len(lands)

    def body(*refs):
        land = refs[:n]
        send, recv_sib, recv_ici = refs[n + 1:n + 4]
        token = refs[-1]
        x, y, c = _mesh_pos()
        me = 4 * x + 2 * y + c
        for i in range(n):
            mine = _block_of(land[i], axes[i], me)
            pltpu.make_async_remote_copy(
                src_ref=mine, dst_ref=mine, send_sem=send.at[4 * i],
                recv_sem=recv_sib.at[i], device_id=(x, y, 1 - c), device_id_type=_MESH_ID).start()
            for j, chip in enumerate([(1 - x, y), (x, 1 - y), (1 - x, 1 - y)]):
                pltpu.make_async_remote_copy(
                    src_ref=mine, dst_ref=mine, send_sem=send.at[4 * i + 1 + j],
                    recv_sem=recv_ici.at[3 * i + j], device_id=(*chip, c), device_id_type=_MESH_ID).start()
        token[...] = jnp.zeros_like(token)

    dma = pltpu.SemaphoreType.DMA
    out = pl.pallas_call(
        body, name=name,
        out_shape=(dma((4 * n,)), dma((n,)), dma((3 * n,)), *_hbm_like(lands), _TOKEN),
        in_specs=[_HBM] * n + [_ANY], out_specs=(_SEM, _SEM, _SEM, *[_HBM] * n, _VMEM),
        input_output_aliases={i: 3 + i for i in range(n)},
        compiler_params=pltpu.CompilerParams(has_side_effects=_EFFECT),
    )(*[_in_hbm(a) for a in lands], after)
    return dict(send=out[0], recv_sib=out[1], recv_ici=out[2], lands=list(out[3:3 + n]), axes=axes, token=out[-1])


def _gather_forward(st, after, name):
    lands, axes = st["lands"], st["axes"]
    n = len(lands)

    def body(*refs):
        land = refs[:n]
        recv_ici = refs[n]
        send2, recv2 = refs[n + 2], refs[n + 3]
        token = refs[-1]
        x, y, c = _mesh_pos()
        for j, (px, py) in enumerate([(1 - x, y), (x, 1 - y), (1 - x, 1 - y)]):
            for i in range(n):
                blk = _block_of(land[i], axes[i], 4 * px + 2 * py + c)
                pltpu.make_async_remote_copy(
                    src_ref=blk, dst_ref=blk, send_sem=send2.at[3 * i + j],
                    recv_sem=recv_ici.at[3 * i + j], device_id=(px, py, c), device_id_type=_MESH_ID).wait_recv()
                pltpu.make_async_remote_copy(
                    src_ref=blk, dst_ref=blk, send_sem=send2.at[3 * i + j],
                    recv_sem=recv2.at[3 * i + j], device_id=(x, y, 1 - c), device_id_type=_MESH_ID).start()
        token[...] = jnp.zeros_like(token)

    dma = pltpu.SemaphoreType.DMA
    out = pl.pallas_call(
        body, name=name,
        out_shape=(dma((3 * n,)), dma((3 * n,)), *_hbm_like(lands), _TOKEN),
        in_specs=[_HBM] * n + [_SEM, _ANY], out_specs=(_SEM, _SEM, *[_HBM] * n, _VMEM),
        input_output_aliases={i: 2 + i for i in range(n)},
        compiler_params=pltpu.CompilerParams(has_side_effects=_EFFECT),
    )(*lands, st["recv_ici"], after)
    return dict(st, send2=out[0], recv2=out[1], lands=list(out[2:2 + n]), token=out[-1])


def _gather_finish(st, after, name):
    lands, axes = st["lands"], st["axes"]
    n = len(lands)

    def body(*refs):
        land = refs[:n]
        send, recv_sib, send2, recv2 = refs[n:n + 4]
        x, y, c = _mesh_pos()
        me = 4 * x + 2 * y + c
        sib = 4 * x + 2 * y + (1 - c)

        def desc(i, blk, s_sem, r_sem):
            ref = _block_of(land[i], axes[i], blk)
            return pltpu.make_async_remote_copy(
                src_ref=ref, dst_ref=ref, send_sem=s_sem, recv_sem=r_sem,
                device_id=(x, y, 1 - c), device_id_type=_MESH_ID)

        for i in range(n):
            desc(i, sib, send.at[4 * i], recv_sib.at[i]).wait_recv()
            for j, (px, py) in enumerate([(1 - x, y), (x, 1 - y), (1 - x, 1 - y)]):
                desc(i, 4 * px + 2 * py + (1 - c), send2.at[3 * i + j], recv2.at[3 * i + j]).wait_recv()
            for k in range(4):
                desc(i, me, send.at[4 * i + k], recv_sib.at[i]).wait_send()
            for j, (px, py) in enumerate([(1 - x, y), (x, 1 - y), (1 - x, 1 - y)]):
                desc(i, 4 * px + 2 * py + c, send2.at[3 * i + j], recv2.at[3 * i + j]).wait_send()

    out = pl.pallas_call(
        body, name=name, out_shape=tuple(_hbm_like(lands)),
        in_specs=[_HBM] * n + [_SEM] * 4 + [_ANY], out_specs=tuple([_HBM] * n),
        input_output_aliases={i: i for i in range(n)},
        compiler_params=pltpu.CompilerParams(has_side_effects=_EFFECT),
    )(*lands, st["send"], st["recv_sib"], st["send2"], st["recv2"], after)
    return list(out)


def _scatter_sibling_start(grads, name):
    n = len(grads)
    gots = [lax.empty((4,) + g.shape[1:], g.dtype) for g in grads]

    def body(*refs):
        grad, got = refs[:n], refs[n:2 * n]
        send, recv = refs[2 * n], refs[2 * n + 1]
        token = refs[-1]
        x, y, c = _mesh_pos()
        for i in range(n):
            for q in range(4):
                pltpu.make_async_remote_copy(
                    src_ref=grad[i].at[2 * q + (1 - c)], dst_ref=got[i].at[q], send_sem=send.at[4 * i + q],
                    recv_sem=recv.at[4 * i + q], device_id=(x, y, 1 - c), device_id_type=_MESH_ID).start()
        token[...] = jnp.zeros_like(token)

    dma = pltpu.SemaphoreType.DMA
    out = pl.pallas_call(
        body, name=name,
        out_shape=(dma((4 * n,)), dma((4 * n,)), *_hbm_like(grads), *_hbm_like(gots), _TOKEN),
        in_specs=[_HBM] * (2 * n), out_specs=(_SEM, _SEM, *[_HBM] * (2 * n), _VMEM),
        input_output_aliases={i: 2 + i for i in range(2 * n)},
        compiler_params=pltpu.CompilerParams(has_side_effects=_EFFECT),
    )(*[_in_hbm(a) for a in grads], *[_in_hbm(a) for a in gots])
    return dict(send=out[0], recv=out[1], grads=list(out[2:2 + n]), gots=list(out[2 + n:2 + 2 * n]), token=out[-1])


def _scatter_sibling_finish(st, after, name):
    grads, gots = st["grads"], st["gots"]
    n = len(grads)

    def body(*refs):
        grad, got = refs[:n], refs[n:2 * n]
        send, recv = refs[2 * n], refs[2 * n + 1]
        x, y, c = _mesh_pos()
        for i in range(n):
            for q in range(4):
                cp = pltpu.make_async_remote_copy(
                    src_ref=grad[i].at[2 * q + (1 - c)], dst_ref=got[i].at[q], send_sem=send.at[4 * i + q],
                    recv_sem=recv.at[4 * i + q], device_id=(x, y, 1 - c), device_id_type=_MESH_ID)
                cp.wait_recv()
                cp.wait_send()

    out = pl.pallas_call(
        body, name=name, out_shape=tuple(_hbm_like(grads) + _hbm_like(gots)),
        in_specs=[_HBM] * (2 * n) + [_SEM, _SEM, _ANY], out_specs=tuple([_HBM] * (2 * n)),
        input_output_aliases={i: i for i in range(2 * n)},
        compiler_params=pltpu.CompilerParams(has_side_effects=_EFFECT),
    )(*grads, *gots, st["send"], st["recv"], after)
    return list(out[:n]), list(out[n:])


def _scatter_add(grad, got, pos, name):
    _, R, C = grad.shape
    tr = _blk(R, 512)

    def body(pos_ref, a_ref, b_ref, part_ref, fin_ref):
        s = (a_ref[...].astype(_F32) + b_ref[...].astype(_F32)).astype(part_ref.dtype)
        part_ref[...] = s

        @pl.when(pl.program_id(1) == pos_ref[1])
        def _():
            fin_ref[...] = s

    grid_spec = pltpu.PrefetchScalarGridSpec(
        num_scalar_prefetch=1, grid=(R // tr, 4),
        in_specs=[pl.BlockSpec((None, tr, C), lambda i, q, p: (2 * q + p[0], i, 0)),
                  pl.BlockSpec((None, tr, C), lambda i, q, p: (q, i, 0))],
        out_specs=[pl.BlockSpec((None, tr, C), lambda i, q, p: (q, i, 0)),
                   pl.BlockSpec((None, tr, C), lambda i, q, p: (p[1], i, 0))])
    shape = jax.ShapeDtypeStruct((4, R, C), grad.dtype)
    return pl.pallas_call(
        body, name=name, out_shape=(shape, shape), grid_spec=grid_spec,
        compiler_params=_params(("parallel", "arbitrary"), 2 * tr * C * 8),
    )(pos, grad, got)


def _scatter_chips_start(parts, fins, name):
    n = len(parts)

    def body(*refs):
        part, fin = refs[:n], refs[n:2 * n]
        send, recv = refs[2 * n], refs[2 * n + 1]
        token = refs[-1]
        x, y, c = _mesh_pos()
        mine = 2 * x + y
        for i in range(n):
            for k, (tx, ty) in enumerate([(1 - x, y), (x, 1 - y), (1 - x, 1 - y)]):
                pltpu.make_async_remote_copy(
                    src_ref=part[i].at[2 * tx + ty], dst_ref=fin[i].at[mine], send_sem=send.at[3 * i + k],
                    recv_sem=recv.at[3 * i + k], device_id=(tx, ty, c), device_id_type=_MESH_ID).start()
        token[...] = jnp.zeros_like(token)

    dma = pltpu.SemaphoreType.DMA
    out = pl.pallas_call(
        body, name=name,
        out_shape=(dma((3 * n,)), dma((3 * n,)), *_hbm_like(parts), *_hbm_like(fins), _TOKEN),
        in_specs=[_HBM] * (2 * n), out_specs=(_SEM, _SEM, *[_HBM] * (2 * n), _VMEM),
        input_output_aliases={i: 2 + i for i in range(2 * n)},
        compiler_params=pltpu.CompilerParams(has_side_effects=_EFFECT),
    )(*[_in_hbm(a) for a in parts], *[_in_hbm(a) for a in fins])
    return dict(send=out[0], recv=out[1], parts=list(out[2:2 + n]), fins=list(out[2 + n:2 + 2 * n]), token=out[-1])


def _scatter_chips_finish(st, after, name):
    parts, fins = st["parts"], st["fins"]
    n = len(parts)

    def body(*refs):
        part, fin = refs[:n], refs[n:2 * n]
        send, recv = refs[2 * n], refs[2 * n + 1]
        x, y, c = _mesh_pos()
        for i in range(n):
            for k, (tx, ty) in enumerate([(1 - x, y), (x, 1 - y), (1 - x, 1 - y)]):
                cp = pltpu.make_async_remote_copy(
                    src_ref=part[i].at[2 * tx + ty], dst_ref=fin[i].at[2 * tx + ty], send_sem=send.at[3 * i + k],
                    recv_sem=recv.at[3 * i + k], device_id=(tx, ty, c), device_id_type=_MESH_ID)
                cp.wait_recv()
                cp.wait_send()

    out = pl.pallas_call(
        body, name=name, out_shape=tuple(_hbm_like(parts) + _hbm_like(fins)),
        in_specs=[_HBM] * (2 * n) + [_SEM, _SEM, _ANY], out_specs=tuple([_HBM] * (2 * n)),
        input_output_aliases={i: i for i in range(2 * n)},
        compiler_params=pltpu.CompilerParams(has_side_effects=_EFFECT),
    )(*parts, *fins, st["send"], st["recv"], after)
    return list(out[n:])


def _adamw(g, w, m, v):
    m = ADAM_B1 * m + (1.0 - ADAM_B1) * g
    v = ADAM_B2 * v + (1.0 - ADAM_B2) * (g * g)
    m_hat = m / (1.0 - ADAM_B1 ** ADAM_STEP)
    v_hat = v / (1.0 - ADAM_B2 ** ADAM_STEP)
    delta = -ADAM_LR * (m_hat / (jnp.sqrt(v_hat) + ADAM_EPS) + ADAM_WD * w)
    return delta, m, v


def _adam_layer(fin, w3, m3, v3, l, prev, name):
    L, R, C = w3.shape
    tr = _blk(R, max(SUBLANES, (1 << 18) // C))

    def body(f_ref, w_ref, m_ref, v_ref, *rest):
        g_ref, d_ref, nm_ref, nv_ref = rest[-4:]
        g = ((f_ref[0].astype(_F32) + f_ref[1].astype(_F32)) + f_ref[2].astype(_F32)) + f_ref[3].astype(_F32)
        d, nm, nv = _adamw(g, w_ref[...], m_ref[...], v_ref[...])
        g_ref[...] = g
        d_ref[...] = d
        nm_ref[...] = nm
        nv_ref[...] = nv

    lay = pl.BlockSpec((None, tr, C), lambda i: (l, i, 0))
    ins = [fin, w3, m3, v3]
    in_specs = [pl.BlockSpec((4, tr, C), lambda i: (0, i, 0)), lay, lay, lay]
    aliases = {}
    if prev is not None:
        ins += list(prev)
        in_specs += [_ANY] * 4
        aliases = {4 + k: k for k in range(4)}
    return pl.pallas_call(
        body, name=name, out_shape=[jax.ShapeDtypeStruct((L, R, C), _F32)] * 4, grid=(R // tr,),
        in_specs=in_specs, out_specs=[lay] * 4, input_output_aliases=aliases,
        compiler_params=_params(("parallel",), 2 * tr * C * (4 * 2 + 7 * 4)),
    )(*ins)


def _sum_parts(parts, name):
    _, R, C = parts.shape

    def body(p_ref, o_ref):
        acc = p_ref[0]
        for k in range(1, N_DEV):
            acc = acc + p_ref[k]
        o_ref[...] = acc

    tr = _blk(R, 512)
    return pl.pallas_call(
        body, name=name, out_shape=jax.ShapeDtypeStruct((R, C), _F32), grid=(R // tr,),
        in_specs=[pl.BlockSpec((N_DEV, tr, C), lambda i: (0, i, 0))],
        out_specs=pl.BlockSpec((tr, C), lambda i: (i, 0)),
        compiler_params=_params(("parallel",), 2 * tr * C * 4 * 9),
    )(parts)


def _adam_flat(g, w, m, v, name):
    R, C = g.shape
    tr = _blk(R, 512)

    def body(g_ref, w_ref, m_ref, v_ref, d_ref, nm_ref, nv_ref):
        d, nm, nv = _adamw(g_ref[...], w_ref[...], m_ref[...], v_ref[...])
        d_ref[...] = d
        nm_ref[...] = nm
        nv_ref[...] = nv

    spec = pl.BlockSpec((tr, C), lambda i: (i, 0))
    return pl.pallas_call(
        body, name=name, out_shape=[jax.ShapeDtypeStruct((R, C), _F32)] * 3, grid=(R // tr,),
        in_specs=[spec] * 4, out_specs=[spec] * 3,
        compiler_params=_params(("parallel",), 2 * tr * C * 4 * 7),
    )(g, w, m, v)


def _pack(arrays):
    flat = jnp.concatenate([a.reshape(-1) for a in arrays])
    tile = SUBLANES * LANES
    pad = (-flat.shape[0]) % tile
    return jnp.pad(flat, (0, pad)).reshape(-1, LANES)


def _unpack(packed, shapes):
    flat = packed.reshape(-1)
    out, off = [], 0
    for s in shapes:
        n = math.prod(s)
        out.append(flat[off:off + n].reshape(s))
        off += n
    return out


def kernel(x, w_in, w_dw, b_dw, conv_ln_g, conv_ln_b, rpb, w_out, w_up, w_down, pre_mix_g, post_mix_g, pre_mlp_g, post_mlp_g, loss_target, m_w_in, m_w_dw, m_b_dw, m_conv_ln_g, m_conv_ln_b, m_rpb, m_w_out, m_w_up, m_w_down, m_pre_mix_g, m_post_mix_g, m_pre_mlp_g, m_post_mlp_g, v_w_in, v_w_dw, v_b_dw, v_conv_ln_g, v_conv_ln_b, v_rpb, v_w_out, v_w_up, v_w_down, v_pre_mix_g, v_post_mix_g, v_pre_mlp_g, v_post_mlp_g):
    _, T, D = x.shape
    L = w_in.shape[0]
    cw = b_dw.shape[1]
    H = rpb.shape[1]
    naw = H * HEAD_DIM
    ks = w_dw.shape[1]
    assert T % GRID_W == 0 and T // GRID_W >= WIN_ROWS and H % 2 == 0 and cw % LANES == 0
    assert rpb.shape[2:] == (N_DR, N_DC) and w_dw.shape[2] * N_DEV == cw and ks // 2 < CONV_HALO
    assert naw == cw and w_out.shape[1] * N_DEV == cw + naw and (T // GRID_W) % ROWS_PER_STEP_FWD == 0

    xs = x.reshape(T, D)
    tgt = loss_target.reshape(T, D)
    row = lambda p, l: p[l:l + 1]
    mx, my, mc = (lax.axis_index(a) for a in MESH_AXES)
    dev = (4 * mx + 2 * my + mc).astype(jnp.int32).reshape(1)
    pos = jnp.stack([mc, 2 * mx + my]).astype(jnp.int32)

    ks_pad = ks + (-ks) % SUBLANES
    wdw_pad = jnp.pad(w_dw, ((0, 0), (0, ks_pad - ks), (0, 0))).reshape(L * ks_pad, w_dw.shape[2])
    wdw_full = _all_gather([wdw_pad], [1], "ag_wdw")[0].reshape(L, ks_pad, cw)[:, :ks]

    big = (w_in, w_out, w_up, w_down)
    names = ("in", "out", "up", "down")

    big_axes = (1, 0, 1, 0)

    def gather_start(l, which, after):
        lands = [_cast_slot(big[k], l, dev, big_axes[k] == 1, f"cast_{names[k]}") for k in which]
        return _gather_start(lands, [big_axes[k] for k in which], after, "gather_start_%d" % len(which))

    saved = []
    xin = xs
    h = _norm_fwd(xs, row(pre_mix_g, 0), "norm_first")
    g_a = gather_start(0, [0], h)
    g_b = gather_start(0, [1, 2, 3], g_a["token"])
    g_a = _gather_forward(g_a, g_b["token"], "gather_forward_1")
    Win = _gather_finish(g_a, g_a["token"], "gather_finish_1")[0]
    Ws = None
    dy = loss_sum = None
    for l in range(L):
        if l > 0:
            Win, Wout, Wup, Wdown = Ws
        nxt = gather_start(l + 1, [0, 1, 2, 3], Win) if l + 1 < L else None
        proj = _matmul(h, Win, name="mm_proj")
        bdw = _after(row(b_dw, l), nxt["token"]) if nxt else row(b_dw, l)
        c = _conv_fwd(proj, wdw_full[l], bdw, cw, "conv_fwd")
        yc = _ln_silu_fwd(c, row(conv_ln_g, l), row(conv_ln_b, l), "ln_silu_fwd")
        bias = _bias_table(rpb[l].reshape(H, N_DR * N_DC), "bias_table")
        ya = _attn_fwd(proj, bias, cw, naw, "attn_fwd")
        ycat = jnp.concatenate([yc, ya], axis=1)
        if l == 0:
            g_b = _gather_forward(g_b, ycat, "gather_forward_3")
            Wout, Wup, Wdown = _gather_finish(g_b, g_b["token"], "gather_finish_3")
            Ws = (Win, Wout, Wup, Wdown)
        mix = _matmul(ycat, Wout, name="mm_mix")
        g_post = row(post_mix_g, l)
        if nxt:
            nxt = _gather_forward(nxt, mix, "gather_forward")
            g_post = _after(g_post, nxt["token"])
        x1, h2 = _resid_norm_fwd(xin, mix, g_post, row(pre_mlp_g, l), "resid_mix")
        act, rl = _matmul(h2, Wup, epilogue="relu2", name="mm_up")
        f = _matmul(act, Wdown, tm=512, tk=4096, name="mm_down")
        saved.append(dict(xin=xin, h=h, W=Ws, proj=proj, c=c, bias=bias, ycat=ycat,
                          mix=mix, x1=x1, h2=h2, act=act, rl=rl, f=f))
        if nxt:
            Ws = tuple(_gather_finish(nxt, f, "gather_finish"))
            xin, h = _resid_norm_fwd(x1, f, row(post_mlp_g, l), row(pre_mix_g, l + 1), "resid_mlp")
        else:
            dy, loss_sum = _resid_loss(x1, f, row(post_mlp_g, l), tgt, "resid_loss")

    loss = lax.psum(loss_sum[0, 0] * (0.5 / D), MESH_AXES)

    small_grads = [None] * L
    big_out = [None] * 4
    moments = ((m_w_in, v_w_in), (m_w_out, v_w_out), (m_w_up, v_w_up), (m_w_down, v_w_down))

    def scatter_begin(grads, which, l):
        tag = "_%d" % len(which)
        return dict(st=_scatter_sibling_start(grads, "scatter_sibling_start" + tag), which=which, l=l, tag=tag)

    def scatter_mid(sc, after):
        grads, gots = _scatter_sibling_finish(sc["st"], after, "scatter_sibling_finish" + sc["tag"])
        pf = [_scatter_add(g, o, pos, f"scatter_add_{k}") for k, g, o in zip(sc["which"], grads, gots)]
        st = _scatter_chips_start([p for p, _ in pf], [q for _, q in pf], "scatter_chips_start" + sc["tag"])
        return dict(sc, st=st)

    def scatter_end(sc, after):
        fins = _scatter_chips_finish(sc["st"], after, "scatter_chips_finish" + sc["tag"])
        for k, fin in zip(sc["which"], fins):
            big_out[k] = _adam_layer(fin, big[k], moments[k][0], moments[k][1], sc["l"], big_out[k],
                                     f"adam_{k}_{sc['l']}")

    dxo = dy
    pending = None
    for l in reversed(range(L)):
        s = saved[l]
        Win, Wout, Wup, Wdown = s["W"]
        last = l == 0
        g_post_mlp = row(post_mlp_g, l)
        if pending is not None:
            g_post_mlp = _after(g_post_mlp, pending["st"]["token"])
        d_f, dg_post_mlp = _norm_bwd(s["f"], g_post_mlp, dxo, None, _MXU, "norm_bwd_mlp")
        d_up = _matmul(d_f, Wdown, tb=True, epilogue="mul2", extra=s["rl"], name="mm_d_up")
        g_pre_mlp = row(pre_mlp_g, l)
        if pending is not None:
            pending = scatter_mid(pending, d_up)
            g_pre_mlp = _after(g_pre_mlp, pending["st"]["token"])
        dWdown = _matmul(s["act"], d_f, ta=True, out_dtype=_WIRE, tk=T, name="mm_dw_down").reshape(N_DEV, -1, D)
        d_h2 = _matmul(d_up, Wup, tb=True, tm=512, tk=4096, name="mm_d_h2")
        dWup = _matmul(s["h2"], d_up, ta=True, out_dtype=_WIRE, out_cols=N_DEV, tk=T, name="mm_dw_up")
        g_post_mix, ln_g, wdw_l, g_pre_mix = row(post_mix_g, l), row(conv_ln_g, l), wdw_full[l], row(pre_mix_g, l)
        if last:
            sc_mlp = scatter_begin([dWup, dWdown], [2, 3], l)
            g_post_mix = _after(g_post_mix, sc_mlp["st"]["token"])
        dx1, dg_pre_mlp = _norm_bwd(s["x1"], g_pre_mlp, d_h2, dxo, _F32, "norm_bwd_premlp")
        d_mix, dg_post_mix = _norm_bwd(s["mix"], g_post_mix, dx1, None, _MXU, "norm_bwd_mix")
        d_ycat = _matmul(d_mix, Wout, tb=True, name="mm_d_ycat")
        if last:
            sc_mlp = scatter_mid(sc_mlp, d_ycat)
            ln_g = _after(ln_g, sc_mlp["st"]["token"])
        dWout = _matmul(s["ycat"], d_mix, ta=True, out_dtype=_WIRE, tk=T, name="mm_dw_out").reshape(N_DEV, -1, D)
        if last:
            sc_out = scatter_begin([dWout], [1], l)
            wdw_l = _after(wdw_l, sc_out["st"]["token"])
        dc, dlng, dlnb = _ln_silu_bwd(s["c"], ln_g, row(conv_ln_b, l), d_ycat, "ln_silu_bwd")
        da, dgate, dwb = _conv_bwd(s["proj"], dc, wdw_l, cw, "conv_bwd")
        dq, dk, dv, gcls = _attn_bwd(s["proj"], s["bias"], d_ycat, cw, naw, "attn_bwd")
        if last:
            sc_out = scatter_mid(sc_out, dq)
            g_pre_mix = _after(g_pre_mix, sc_out["st"]["token"])
        drpb = _rpb_grad(gcls, "rpb_grad").reshape(H, 4 * SUBLANES, LANES)[:, :N_DC, :N_DR].transpose(0, 2, 1)
        dproj = jnp.concatenate([da, dgate, dq, dk, dv], axis=1)
        dh = _matmul(dproj, Win, tb=True, tm=512, tk=dproj.shape[1], name="mm_d_h")
        dWin = _matmul(s["h"], dproj, ta=True, out_dtype=_WIRE, out_cols=N_DEV, tk=T, name="mm_dw_in")
        dxo, dg_pre_mix = _norm_bwd(s["xin"], g_pre_mix, dh, dx1, _F32, "norm_bwd_premix")
        if pending is not None:
            scatter_end(pending, dxo)
        if last:
            sc_in = scatter_begin([dWin], [0], l)
            sc_in = scatter_mid(sc_in, sc_in["st"]["token"])
            scatter_end(sc_mlp, sc_in["st"]["token"])
            scatter_end(sc_out, sc_in["st"]["token"])
            scatter_end(sc_in, sc_in["st"]["token"])
        else:
            pending = scatter_begin([dWin, dWout, dWup, dWdown], [0, 1, 2, 3], l)
        small_grads[l] = [dwb[ks], dlng[0], dlnb[0], drpb, dg_pre_mix[0], dg_post_mix[0], dg_pre_mlp[0],
                          dg_post_mlp[0], dwb[:ks]]

    rep_shapes = [(L, cw), (L, cw), (L, cw), (L, H, N_DR, N_DC), (L, D), (L, D), (L, D), (L, D)]
    stacked = [jnp.stack([small_grads[l][k] for l in range(L)]) for k in range(9)]
    packed = _pack(stacked)
    parts = _all_gather([packed], [0], "ag_small")[0].reshape(N_DEV, *packed.shape)
    gsum = _sum_parts(parts, "sum_small")
    g_small = _unpack(gsum, rep_shapes + [(L, ks, cw)])
    g_rep, g_wdw_full = g_small[:8], g_small[8]
    wsh = w_dw.shape[2]
    g_wdw = lax.dynamic_slice_in_dim(g_wdw_full, dev[0] * wsh, wsh, axis=2)

    rep_w = [b_dw, conv_ln_g, conv_ln_b, rpb, pre_mix_g, post_mix_g, pre_mlp_g, post_mlp_g]
    rep_m = [m_b_dw, m_conv_ln_g, m_conv_ln_b, m_rpb, m_pre_mix_g, m_post_mix_g, m_pre_mlp_g, m_post_mlp_g]
    rep_v = [v_b_dw, v_conv_ln_g, v_conv_ln_b, v_rpb, v_pre_mix_g, v_post_mix_g, v_pre_mlp_g, v_post_mlp_g]
    rep_out = _adam_flat(_pack(g_rep), _pack(rep_w), _pack(rep_m), _pack(rep_v), "adam_small")
    rep_delta, rep_nm, rep_nv = (_unpack(o, rep_shapes) for o in rep_out)
    dw_out = _adam_flat(_pack([g_wdw]), _pack([w_dw]), _pack([m_w_dw]), _pack([v_w_dw]), "adam_wdw")
    wdw_delta, wdw_nm, wdw_nv = (_unpack(o, [w_dw.shape])[0] for o in dw_out)

    def assemble(kind_big, rep_list, wdw_val):
        return [big_out[0][kind_big], wdw_val, rep_list[0], rep_list[1], rep_list[2], rep_list[3],
                big_out[1][kind_big], big_out[2][kind_big], big_out[3][kind_big],
                rep_list[4], rep_list[5], rep_list[6], rep_list[7]]

    grads_out = assemble(0, g_rep, g_wdw)
    deltas = assemble(1, rep_delta, wdw_delta)
    new_m = assemble(2, rep_nm, wdw_nm)
    new_v = assemble(3, rep_nv, wdw_nv)
    return (loss, dxo.reshape(1, T, D), *grads_out, *deltas, *new_m, *new_v)
```

```python
import math

import jax
import jax.numpy as jnp
from jax import lax
from jax.experimental import pallas as pl
from jax.experimental.pallas import tpu as pltpu

_MXU = jnp.bfloat16
_WIRE = jnp.bfloat16
_F32 = jnp.float32

N_DEV = 8
GRID_W = 64
WIN_ROWS = 8
WIN_COLS = 16
HEAD_DIM = 64
LANES = 128
SUBLANES = 8
RMS_EPS = 1e-6
LN_EPS = 1e-5
NEG_INF = -1e30
ADAM_LR = 0.001
ADAM_B1 = 0.9
ADAM_B2 = 0.999
ADAM_EPS = 1e-08
ADAM_WD = 0.01
ADAM_STEP = 10
VMEM_BYTES_V7X = 64 << 20
VMEM_RESERVE = 12 << 20
MESH_AXES = ("x", "y", "c")


def _vmem_limit(block_bytes):
    return int(min(max(block_bytes + (8 << 20), 24 << 20), VMEM_BYTES_V7X - VMEM_RESERVE))


def _blk(n, pref):
    if n <= pref:
        return n
    for t in range(pref, 7, -1):
        if n % t == 0 and t % SUBLANES == 0:
            return t
    return n


def _sigmoid(v):
    return 1.0 / (1.0 + jnp.exp(-v))


def _params(sem, nbytes):
    return pltpu.CompilerParams(dimension_semantics=sem, vmem_limit_bytes=_vmem_limit(nbytes))


def _pallas(body, **kw):
    call = pl.pallas_call(body, **kw)

    def run(*operands):
        return call(*[pltpu.with_memory_space_constraint(o, pltpu.HBM)
                      if jnp.issubdtype(o.dtype, jnp.floating) else o for o in operands])

    return run


def _my_block():
    x, y, c = (lax.axis_index(a) for a in MESH_AXES)
    return 4 * x + 2 * y + c


def _cast_slot(w3, l, by_cols, name):
    _, R, C = w3.shape
    tr = _blk(R, 512)

    def body(w_ref, o_ref):
        o_ref[...] = w_ref[...].astype(o_ref.dtype)

    if by_cols:
        shape, o_spec = (R, N_DEV * C), pl.BlockSpec((tr, C), lambda i: (i, _my_block()))
    else:
        shape, o_spec = (N_DEV * R, C), pl.BlockSpec((tr, C), lambda i: (_my_block() * (R // tr) + i, 0))
    return _pallas(
        body, name=name, out_shape=jax.ShapeDtypeStruct(shape, _WIRE), grid=(R // tr,),
        in_specs=[pl.BlockSpec((None, tr, C), lambda i: (l, i, 0))], out_specs=o_spec,
        compiler_params=_params(("parallel",), 2 * tr * C * 6),
    )(w3)


def _norm_fwd(x, g, name):
    T, D = x.shape
    tm = _blk(T, 256)

    def body(x_ref, g_ref, h_ref):
        xv = x_ref[...]
        r = lax.rsqrt(jnp.mean(xv * xv, axis=-1, keepdims=True) + RMS_EPS)
        h_ref[...] = (xv * r * g_ref[...]).astype(h_ref.dtype)

    return _pallas(
        body, name=name, out_shape=jax.ShapeDtypeStruct((T, D), _MXU), grid=(T // tm,),
        in_specs=[pl.BlockSpec((tm, D), lambda i: (i, 0)), pl.BlockSpec((1, D), lambda i: (0, 0))],
        out_specs=pl.BlockSpec((tm, D), lambda i: (i, 0)),
        compiler_params=_params(("parallel",), 2 * tm * D * 6),
    )(x, g)


def _resid_norm_fwd(xres, y, g_post, g_next, name):
    T, D = xres.shape
    tm = _blk(T, 256)

    def body(x_ref, y_ref, gp_ref, gn_ref, xn_ref, h_ref):
        yv = y_ref[...]
        r = lax.rsqrt(jnp.mean(yv * yv, axis=-1, keepdims=True) + RMS_EPS)
        xn = x_ref[...] + yv * r * gp_ref[...]
        xn_ref[...] = xn
        r2 = lax.rsqrt(jnp.mean(xn * xn, axis=-1, keepdims=True) + RMS_EPS)
        h_ref[...] = (xn * r2 * gn_ref[...]).astype(h_ref.dtype)

    row = pl.BlockSpec((tm, D), lambda i: (i, 0))
    vec = pl.BlockSpec((1, D), lambda i: (0, 0))
    return _pallas(
        body, name=name,
        out_shape=(jax.ShapeDtypeStruct((T, D), _F32), jax.ShapeDtypeStruct((T, D), _MXU)),
        grid=(T // tm,), in_specs=[row, row, vec, vec], out_specs=(row, row),
        compiler_params=_params(("parallel",), 2 * tm * D * 14),
    )(xres, y, g_post, g_next)


def _resid_loss(xres, y, g_post, target, name):
    T, D = xres.shape
    tm = _blk(T, 256)

    def body(x_ref, y_ref, gp_ref, t_ref, dy_ref, loss_ref):
        yv = y_ref[...]
        r = lax.rsqrt(jnp.mean(yv * yv, axis=-1, keepdims=True) + RMS_EPS)
        err = x_ref[...] + yv * r * gp_ref[...] - t_ref[...]
        dy_ref[...] = err * (1.0 / D)

        @pl.when(pl.program_id(0) == 0)
        def _():
            loss_ref[...] = jnp.zeros_like(loss_ref)

        part = jnp.sum(jnp.sum(err * err, axis=-1, keepdims=True), axis=0, keepdims=True)
        loss_ref[...] += part

    row = pl.BlockSpec((tm, D), lambda i: (i, 0))
    vec = pl.BlockSpec((1, D), lambda i: (0, 0))
    return _pallas(
        body, name=name,
        out_shape=(jax.ShapeDtypeStruct((T, D), _F32), jax.ShapeDtypeStruct((1, 1), _F32)),
        grid=(T // tm,), in_specs=[row, row, vec, row],
        out_specs=(row, pl.BlockSpec((1, 1), lambda i: (0, 0))),
        compiler_params=_params(("arbitrary",), 2 * tm * D * 16),
    )(xres, y, g_post, target)


def _norm_bwd(y, g, dout, dres, out_dtype, name):
    T, D = y.shape
    tm = _blk(T, 256)
    nsteps = T // tm
    has_res = dres is not None

    def body(*refs):
        if has_res:
            y_ref, g_ref, do_ref, dr_ref, dy_ref, dg_ref, acc = refs
        else:
            y_ref, g_ref, do_ref, dy_ref, dg_ref, acc = refs
        i = pl.program_id(0)
        yv = y_ref[...]
        do = do_ref[...]
        r = lax.rsqrt(jnp.mean(yv * yv, axis=-1, keepdims=True) + RMS_EPS)
        gy = do * g_ref[...]
        dot = jnp.mean(yv * gy, axis=-1, keepdims=True)
        dy = r * gy - yv * (r * r * r * dot)
        if has_res:
            dy = dy + dr_ref[...]
        dy_ref[...] = dy.astype(dy_ref.dtype)

        @pl.when(i == 0)
        def _():
            acc[...] = jnp.zeros_like(acc)

        acc[...] += jnp.sum((do * yv * r).reshape(tm // SUBLANES, SUBLANES, D), axis=0)

        @pl.when(i == nsteps - 1)
        def _():
            dg_ref[...] = jnp.sum(acc[...], axis=0, keepdims=True)

    row = pl.BlockSpec((tm, D), lambda i: (i, 0))
    vec = pl.BlockSpec((1, D), lambda i: (0, 0))
    ins = [y, g, dout] + ([dres] if has_res else [])
    in_specs = [row, vec, row] + ([row] if has_res else [])
    return _pallas(
        body, name=name,
        out_shape=(jax.ShapeDtypeStruct((T, D), out_dtype), jax.ShapeDtypeStruct((1, D), _F32)),
        grid=(nsteps,), in_specs=in_specs, out_specs=(row, vec),
        scratch_shapes=[pltpu.VMEM((SUBLANES, D), _F32)],
        compiler_params=_params(("arbitrary",), 2 * tm * D * 16),
    )(*ins)


def _matmul(a, b, *, ta=False, tb=False, out_dtype=_F32, epilogue=None, extra=None, out_cols=0,
            tm=1024, tk=2048, name):
    M, K = (a.shape[1], a.shape[0]) if ta else a.shape
    N = b.shape[0] if tb else b.shape[1]
    tm = _blk(M, tm)
    tn = N // out_cols if out_cols else _blk(N, 1024)
    tk = _blk(K, tk)
    b_spec = (pl.BlockSpec((tn, tk), lambda i, j, k: (j, k)) if tb
              else pl.BlockSpec((tk, tn), lambda i, j, k: (k, j)))
    nk = K // tk
    a_spec = (pl.BlockSpec((tk, tm), lambda i, j, k: (k, i)) if ta
              else pl.BlockSpec((tm, tk), lambda i, j, k: (i, k)))
    if out_cols:
        assert N // tn == out_cols and epilogue is None
        o_spec = pl.BlockSpec((None, tm, tn), lambda i, j, k: (j, i, 0))
        o_shape = (out_cols, M, tn)
    else:
        o_spec = pl.BlockSpec((tm, tn), lambda i, j, k: (i, j))
        o_shape = (M, N)
    dims = (((0 if ta else 1,), (1 if tb else 0,)), ((), ()))
    n_extra = 1 if epilogue == "mul2" else 0
    n_out = 2 if epilogue == "relu2" else 1

    def finish(acc, extra_refs, out_refs):
        if epilogue is None:
            out_refs[0][...] = acc.astype(out_refs[0].dtype)
        elif epilogue == "relu2":
            rl = jnp.maximum(acc, 0.0)
            out_refs[0][...] = (rl * rl).astype(out_refs[0].dtype)
            out_refs[1][...] = rl.astype(out_refs[1].dtype)
        else:
            out_refs[0][...] = (acc * (2.0 * extra_refs[0][...].astype(_F32))).astype(out_refs[0].dtype)

    def body(a_ref, b_ref, *rest):
        extra_refs = rest[:n_extra]
        out_refs = rest[n_extra:n_extra + n_out]
        part = lax.dot_general(a_ref[...], b_ref[...], dims, preferred_element_type=_F32)
        if nk == 1:
            finish(part, extra_refs, out_refs)
            return
        acc = rest[-1]
        k = pl.program_id(2)

        @pl.when(k == 0)
        def _():
            acc[...] = part

        @pl.when(k > 0)
        def _():
            acc[...] += part

        @pl.when(k == nk - 1)
        def _():
            finish(acc[...], extra_refs, out_refs)

    if epilogue == "relu2":
        out_shape = (jax.ShapeDtypeStruct((M, N), _MXU), jax.ShapeDtypeStruct((M, N), _MXU))
        out_specs = (o_spec, o_spec)
        out_bytes = 2 * tm * tn * 2
    else:
        odt = _MXU if epilogue == "mul2" else out_dtype
        out_shape = jax.ShapeDtypeStruct(o_shape, odt)
        out_specs = o_spec
        out_bytes = tm * tn * jnp.dtype(odt).itemsize
    in_specs = [a_spec, b_spec] + ([o_spec] if n_extra else [])
    ins = [a, b] + ([extra] if n_extra else [])
    blocks = 2 * (tm * tk * 2 + tk * tn * 2 + out_bytes + n_extra * tm * tn * 2) + tm * tn * 4 * 2
    return _pallas(
        body, name=name, out_shape=out_shape, grid=(M // tm, N // tn, nk),
        in_specs=in_specs, out_specs=out_specs,
        scratch_shapes=[pltpu.VMEM((tm, tn), _F32)] if nk > 1 else [],
        compiler_params=_params(("parallel", "parallel", "arbitrary"), blocks),
    )(*ins)


CONV_HALO = 16
CONV_CHUNK = 256


def _tap_windows(win, n_taps_plus1, tc):
    n = win.shape[0]
    for s in range(SUBLANES):
        shifted = win if s == 0 else pltpu.roll(win, n - s, 0)
        for q in range((n_taps_plus1 + SUBLANES - 1) // SUBLANES):
            o = SUBLANES * q + s
            if 1 <= o < n_taps_plus1:
                yield o, shifted[SUBLANES * q:SUBLANES * q + tc, :]


def _conv_fwd(proj, wdw, bdw, cw, name):
    T = proj.shape[0]
    ks = wdw.shape[0]
    cb = LANES
    tc = _blk(T, CONV_CHUNK)
    nblk = cw // cb

    def body(a_ref, g_ref, w_ref, b_ref, c_ref, upad):
        zeros = jnp.zeros((CONV_HALO, cb), _F32)
        upad[0:CONV_HALO, :] = zeros
        upad[T + CONV_HALO:T + 2 * CONV_HALO, :] = zeros
        upad[CONV_HALO:T + CONV_HALO, :] = a_ref[...] * _sigmoid(g_ref[...])

        def chunk(i, carry):
            t0 = pl.multiple_of(i * tc, tc)
            win = upad[pl.ds(t0, tc + 2 * CONV_HALO), :]
            acc = jnp.broadcast_to(b_ref[...], (tc, cb))
            for o, rows in _tap_windows(win, ks + 1, tc):
                j = o + ks // 2 - CONV_HALO
                acc = acc + rows * w_ref[j:j + 1, :]
            c_ref[pl.ds(t0, tc), :] = acc
            return carry

        lax.fori_loop(0, T // tc, chunk, 0)

    col = lambda off: pl.BlockSpec((T, cb), lambda i, off=off: (0, off + i))
    return _pallas(
        body, name=name, out_shape=jax.ShapeDtypeStruct((T, cw), _F32), grid=(nblk,),
        in_specs=[col(0), col(nblk), pl.BlockSpec((ks, cb), lambda i: (0, i)),
                  pl.BlockSpec((1, cb), lambda i: (0, i))],
        out_specs=pl.BlockSpec((T, cb), lambda i: (0, i)),
        scratch_shapes=[pltpu.VMEM((T + 2 * CONV_HALO, cb), _F32)],
        compiler_params=_params(("parallel",), 2 * T * cb * 4 * 3 + T * cb * 4),
    )(proj, proj, wdw, bdw)


def _ln_silu_fwd(c, lng, lnb, name):
    T, cw = c.shape
    tm = _blk(T, 512)

    def body(c_ref, g_ref, b_ref, y_ref):
        cv = c_ref[...]
        mu = jnp.mean(cv, axis=-1, keepdims=True)
        xc = cv - mu
        var = jnp.mean(xc * xc, axis=-1, keepdims=True)
        z = xc * lax.rsqrt(var + LN_EPS) * g_ref[...] + b_ref[...]
        y_ref[...] = (z * _sigmoid(z)).astype(y_ref.dtype)

    row = pl.BlockSpec((tm, cw), lambda i: (i, 0))
    vec = pl.BlockSpec((1, cw), lambda i: (0, 0))
    return _pallas(
        body, name=name, out_shape=jax.ShapeDtypeStruct((T, cw), _MXU), grid=(T // tm,),
        in_specs=[row, vec, vec], out_specs=row,
        compiler_params=_params(("parallel",), 2 * tm * cw * 6),
    )(c, lng, lnb)


def _ln_silu_bwd(c, lng, lnb, dycat, name):
    T, cw = c.shape
    tm = _blk(T, 512)
    nsteps = T // tm

    def body(c_ref, g_ref, b_ref, dy_ref, dc_ref, dg_ref, db_ref, accg, accb):
        i = pl.program_id(0)
        cv = c_ref[...]
        mu = jnp.mean(cv, axis=-1, keepdims=True)
        xc = cv - mu
        var = jnp.mean(xc * xc, axis=-1, keepdims=True)
        rstd = lax.rsqrt(var + LN_EPS)
        xhat = xc * rstd
        z = xhat * g_ref[...] + b_ref[...]
        sg = _sigmoid(z)
        dz = dy_ref[...] * (sg * (1.0 + z * (1.0 - sg)))
        dxh = dz * g_ref[...]
        m1 = jnp.mean(dxh, axis=-1, keepdims=True)
        m2 = jnp.mean(dxh * xhat, axis=-1, keepdims=True)
        dc_ref[...] = rstd * (dxh - m1 - xhat * m2)

        @pl.when(i == 0)
        def _():
            accg[...] = jnp.zeros_like(accg)
            accb[...] = jnp.zeros_like(accb)

        accg[...] += jnp.sum((dz * xhat).reshape(tm // SUBLANES, SUBLANES, cw), axis=0)
        accb[...] += jnp.sum(dz.reshape(tm // SUBLANES, SUBLANES, cw), axis=0)

        @pl.when(i == nsteps - 1)
        def _():
            dg_ref[...] = jnp.sum(accg[...], axis=0, keepdims=True)
            db_ref[...] = jnp.sum(accb[...], axis=0, keepdims=True)

    row = pl.BlockSpec((tm, cw), lambda i: (i, 0))
    vec = pl.BlockSpec((1, cw), lambda i: (0, 0))
    return _pallas(
        body, name=name,
        out_shape=(jax.ShapeDtypeStruct((T, cw), _F32), jax.ShapeDtypeStruct((1, cw), _F32),
                   jax.ShapeDtypeStruct((1, cw), _F32)),
        grid=(nsteps,), in_specs=[row, vec, vec, row], out_specs=(row, vec, vec),
        scratch_shapes=[pltpu.VMEM((SUBLANES, cw), _F32), pltpu.VMEM((SUBLANES, cw), _F32)],
        compiler_params=_params(("arbitrary",), 2 * tm * cw * 12),
    )(c, lng, lnb, dycat)


def _conv_bwd(proj, dc, wdw, cw, name):
    T = proj.shape[0]
    ks = wdw.shape[0]
    cb = LANES
    tc = _blk(T, CONV_CHUNK)
    nblk = cw // cb
    half = ks // 2

    def body(a_ref, g_ref, dc_ref, w_ref, da_ref, dg_ref, dwb_ref, upad, dpad, du, wacc):
        zeros = jnp.zeros((CONV_HALO, cb), _F32)
        for pad in (upad, dpad):
            pad[0:CONV_HALO, :] = zeros
            pad[T + CONV_HALO:T + 2 * CONV_HALO, :] = zeros
        sg = _sigmoid(g_ref[...])
        upad[CONV_HALO:T + CONV_HALO, :] = a_ref[...] * sg
        dpad[CONV_HALO:T + CONV_HALO, :] = dc_ref[...]
        wacc[...] = jnp.zeros_like(wacc)

        def chunk(i, carry):
            t0 = pl.multiple_of(i * tc, tc)
            dwin = dpad[pl.ds(t0, tc + 2 * CONV_HALO), :]
            uwin = upad[pl.ds(t0, tc + 2 * CONV_HALO), :]
            dcc = dwin[CONV_HALO:CONV_HALO + tc, :]
            acc = jnp.zeros((tc, cb), _F32)
            for o, rows in _tap_windows(dwin, CONV_HALO + half + 1, tc):
                j = CONV_HALO + half - o
                if 0 <= j < ks:
                    acc = acc + rows * w_ref[j:j + 1, :]
            du[pl.ds(t0, tc), :] = acc
            for o, rows in _tap_windows(uwin, CONV_HALO + half + 1, tc):
                j = o + half - CONV_HALO
                if 0 <= j < ks:
                    wacc[j] += jnp.sum((rows * dcc).reshape(tc // SUBLANES, SUBLANES, cb), axis=0)
            wacc[ks] += jnp.sum(dcc.reshape(tc // SUBLANES, SUBLANES, cb), axis=0)
            return carry

        lax.fori_loop(0, T // tc, chunk, 0)
        duv = du[...]
        av = a_ref[...]
        da_ref[...] = (duv * sg).astype(da_ref.dtype)
        dg_ref[...] = (duv * av * sg * (1.0 - sg)).astype(dg_ref.dtype)
        dwb_ref[...] = jnp.sum(wacc[...], axis=1)

    col = lambda off: pl.BlockSpec((T, cb), lambda i, off=off: (0, off + i))
    blk = pl.BlockSpec((T, cb), lambda i: (0, i))
    return _pallas(
        body, name=name,
        out_shape=(jax.ShapeDtypeStruct((T, cw), _MXU), jax.ShapeDtypeStruct((T, cw), _MXU),
                   jax.ShapeDtypeStruct((ks + 1, cw), _F32)),
        grid=(nblk,),
        in_specs=[col(0), col(nblk), blk, pl.BlockSpec((ks, cb), lambda i: (0, i))],
        out_specs=(blk, blk, pl.BlockSpec((ks + 1, cb), lambda i: (0, i))),
        scratch_shapes=[pltpu.VMEM((T + 2 * CONV_HALO, cb), _F32), pltpu.VMEM((T + 2 * CONV_HALO, cb), _F32),
                        pltpu.VMEM((T, cb), _F32), pltpu.VMEM((ks + 1, SUBLANES, cb), _F32)],
        compiler_params=_params(("parallel",), 2 * T * cb * 4 * 4 + 3 * T * cb * 4),
    )(proj, proj, dc, wdw)


N_CLS = WIN_ROWS
N_DR = 2 * WIN_ROWS - 1
N_DC = 2 * WIN_COLS - 1
BAND = WIN_ROWS * GRID_W
QK_SCALE = HEAD_DIM ** -0.5
ROWS_PER_STEP_FWD = 4
ROWS_PER_STEP_BWD = 2
_NT = (((1,), (1,)), ((), ()))
_TN = (((0,), (0,)), ((), ()))


def _slab_iotas():
    wk = lax.broadcasted_iota(jnp.int32, (GRID_W, LANES), 0)
    lane = lax.broadcasted_iota(jnp.int32, (GRID_W, LANES), 1)
    wq = jnp.bitwise_and(lane, GRID_W - 1)
    head1 = lane >= GRID_W
    d = wk - wq + (WIN_COLS - 1)
    cs = jnp.clip(wq - WIN_COLS // 2, 0, GRID_W - WIN_COLS)
    window = (wk >= cs) & (wk < cs + WIN_COLS)
    return d, head1, window


def _bias_table(rpb2, name):
    npair = rpb2.shape[0] // 2

    def body(rpb_ref, o_ref):
        p = pl.program_id(0)
        d, head1, window = _slab_iotas()
        for dr in range(N_DR):
            val = jnp.zeros((GRID_W, LANES), _F32)
            for j in range(N_DC):
                s0 = rpb_ref[2 * p, dr * N_DC + j]
                s1 = rpb_ref[2 * p + 1, dr * N_DC + j]
                val = jnp.where(d == j, jnp.where(head1, s1, s0), val)
            slab = jnp.where(window, val, NEG_INF)
            for cls in range(N_CLS):
                k = dr - cls
                if 0 <= k < WIN_ROWS:
                    o_ref[cls, k * GRID_W:(k + 1) * GRID_W, :] = slab

    return pl.pallas_call(
        body, name=name, out_shape=jax.ShapeDtypeStruct((npair, N_CLS, BAND, LANES), _F32), grid=(npair,),
        in_specs=[pl.BlockSpec(memory_space=pltpu.SMEM)],
        out_specs=pl.BlockSpec((None, N_CLS, BAND, LANES), lambda p: (p, 0, 0, 0)),
        compiler_params=_params(("arbitrary",), 2 * N_CLS * BAND * LANES * 4),
    )(rpb2)


def _rpb_grad(gc, name):
    npair = gc.shape[0]

    def body(g_ref, o_ref):
        d, _, _ = _slab_iotas()
        rowi = lax.broadcasted_iota(jnp.int32, (4 * SUBLANES, LANES), 0)
        lanei = lax.broadcasted_iota(jnp.int32, (4 * SUBLANES, LANES), 1)
        head1 = lax.broadcasted_iota(jnp.int32, (1, LANES), 1) >= GRID_W
        tiles = [jnp.zeros((4 * SUBLANES, LANES), _F32) for _ in range(2)]
        for dr in range(N_DR):
            ysum = jnp.zeros((GRID_W, LANES), _F32)
            for cls in range(N_CLS):
                k = dr - cls
                if 0 <= k < WIN_ROWS:
                    ysum = ysum + g_ref[cls, k * GRID_W:(k + 1) * GRID_W, :]
            for j in range(N_DC):
                cs = jnp.sum(jnp.where(d == j, ysum, 0.0), axis=0, keepdims=True)
                s0 = jnp.sum(jnp.where(head1, 0.0, cs), axis=1, keepdims=True)
                s1 = jnp.sum(jnp.where(head1, cs, 0.0), axis=1, keepdims=True)
                here = (rowi == j) & (lanei == dr)
                tiles[0] = tiles[0] + jnp.where(here, s0, 0.0)
                tiles[1] = tiles[1] + jnp.where(here, s1, 0.0)
        o_ref[0] = tiles[0]
        o_ref[1] = tiles[1]

    return _pallas(
        body, name=name, out_shape=jax.ShapeDtypeStruct((npair, 2, 4 * SUBLANES, LANES), _F32), grid=(npair,),
        in_specs=[pl.BlockSpec((None, N_CLS, BAND, LANES), lambda p: (p, 0, 0, 0))],
        out_specs=pl.BlockSpec((None, 2, 4 * SUBLANES, LANES), lambda p: (p, 0, 0, 0)),
        compiler_params=_params(("parallel",), 2 * N_CLS * BAND * LANES * 4),
    )(gc)


def _block_diag(v, diag):
    return jnp.where(diag, jnp.concatenate([v, v], axis=0), 0.0).astype(_MXU)


def _diag_mask():
    r = lax.broadcasted_iota(jnp.int32, (LANES, LANES), 0) < GRID_W
    c = lax.broadcasted_iota(jnp.int32, (LANES, LANES), 1) < HEAD_DIM
    return r == c


def _row_geometry(r, rows):
    rs = jnp.clip(r - WIN_ROWS // 2, 0, rows - WIN_ROWS)
    cls = rs - r + (WIN_ROWS - 1)
    return pl.multiple_of(r * GRID_W, GRID_W), pl.multiple_of(rs * GRID_W, GRID_W), cls


def _probs_t(qsel, kband, bias):
    s = lax.dot_general(kband, qsel, _NT, preferred_element_type=_F32) + bias
    mx = jnp.max(s, axis=0, keepdims=True)
    e = jnp.exp(s - mx)
    return e * (1.0 / jnp.sum(e, axis=0, keepdims=True))


def _attn_fwd(proj, bias, cw, naw, name):
    T = proj.shape[0]
    rows = T // GRID_W
    npair = naw // LANES
    qoff, koff, voff = 2 * cw // LANES, (2 * cw + naw) // LANES, (2 * cw + 2 * naw) // LANES

    def body(q_ref, k_ref, v_ref, b_ref, o_ref, kb, vb):
        kb[...] = k_ref[...].astype(_MXU)
        vb[...] = v_ref[...].astype(_MXU)
        diag = _diag_mask()
        m0 = lax.broadcasted_iota(jnp.int32, (GRID_W, LANES), 1) < HEAD_DIM

        def step(i, carry):
            us = range(ROWS_PER_STEP_FWD)
            geo = [_row_geometry(ROWS_PER_STEP_FWD * i + u, rows) for u in us]
            qsel = [_block_diag(q_ref[pl.ds(t0, GRID_W), :] * QK_SCALE, diag) for t0, _, _ in geo]
            kbands = [kb[pl.ds(b0, BAND), :] for _, b0, _ in geo]
            vbands = [vb[pl.ds(b0, BAND), :] for _, b0, _ in geo]
            biases = [b_ref[cls] for _, _, cls in geo]
            pts = [_probs_t(qsel[u], kbands[u], biases[u]) for u in us]
            ofs = [lax.dot_general(pts[u].astype(_MXU), vbands[u], _TN, preferred_element_type=_F32) for u in us]
            for u in us:
                o_ref[pl.ds(geo[u][0], GRID_W), :] = jnp.where(m0, ofs[u][:GRID_W], ofs[u][GRID_W:]).astype(o_ref.dtype)
            return carry

        lax.fori_loop(0, rows // ROWS_PER_STEP_FWD, step, 0)

    col = lambda off: pl.BlockSpec((T, LANES), lambda i, off=off: (0, off + i))
    return _pallas(
        body, name=name, out_shape=jax.ShapeDtypeStruct((T, naw), _MXU), grid=(npair,),
        in_specs=[col(qoff), col(koff), col(voff),
                  pl.BlockSpec((None, N_CLS, BAND, LANES), lambda i: (i, 0, 0, 0))],
        out_specs=pl.BlockSpec((T, LANES), lambda i: (0, i)),
        scratch_shapes=[pltpu.VMEM((T, LANES), _MXU), pltpu.VMEM((T, LANES), _MXU)],
        compiler_params=_params(("parallel",), 2 * (3 * T * LANES * 4 + N_CLS * BAND * LANES * 4 + T * LANES * 2)),
    )(proj, proj, proj, bias)


def _attn_bwd(proj, bias, dycat, cw, naw, name):
    T = proj.shape[0]
    rows = T // GRID_W
    npair = naw // LANES
    qoff, koff, voff = 2 * cw // LANES, (2 * cw + naw) // LANES, (2 * cw + 2 * naw) // LANES
    doff = cw // LANES

    def body(q_ref, k_ref, v_ref, b_ref, do_ref, dq_ref, dk_ref, dv_ref, g_ref, kb, vb, dka, dva):
        kb[...] = k_ref[...].astype(_MXU)
        vb[...] = v_ref[...].astype(_MXU)
        dka[...] = jnp.zeros_like(dka)
        dva[...] = jnp.zeros_like(dva)
        g_ref[...] = jnp.zeros_like(g_ref)
        diag = _diag_mask()
        m0 = lax.broadcasted_iota(jnp.int32, (GRID_W, LANES), 1) < HEAD_DIM

        def step(i, carry):
            us = range(ROWS_PER_STEP_BWD)
            geo = [_row_geometry(ROWS_PER_STEP_BWD * i + u, rows) for u in us]
            qsel = [_block_diag(q_ref[pl.ds(t0, GRID_W), :] * QK_SCALE, diag) for t0, _, _ in geo]
            dosel = [_block_diag(do_ref[pl.ds(t0, GRID_W), :], diag) for t0, _, _ in geo]
            kbands = [kb[pl.ds(b0, BAND), :] for _, b0, _ in geo]
            vbands = [vb[pl.ds(b0, BAND), :] for _, b0, _ in geo]
            biases = [b_ref[cls] for _, _, cls in geo]
            dsts, dqs, dks, dvs = [], [], [], []
            for u in us:
                pt = _probs_t(qsel[u], kbands[u], biases[u])
                dpt = lax.dot_general(vbands[u], dosel[u], _NT, preferred_element_type=_F32)
                delta = jnp.sum(pt * dpt, axis=0, keepdims=True)
                dst = pt * (dpt - delta)
                dsb = dst.astype(_MXU)
                dqf = lax.dot_general(dsb, kbands[u], _TN, preferred_element_type=_F32)
                dsts.append(dst)
                dqs.append((jnp.where(m0, dqf[:GRID_W], dqf[GRID_W:]) * QK_SCALE).astype(dq_ref.dtype))
                dks.append(jnp.dot(dsb, qsel[u], preferred_element_type=_F32))
                dvs.append(jnp.dot(pt.astype(_MXU), dosel[u], preferred_element_type=_F32))
            for u in us:
                t0, b0, cls = geo[u]
                g_ref[cls] += dsts[u]
                dq_ref[pl.ds(t0, GRID_W), :] = dqs[u]
                dka[pl.ds(b0, BAND), :] += dks[u]
                dva[pl.ds(b0, BAND), :] += dvs[u]
            return carry

        lax.fori_loop(0, rows // ROWS_PER_STEP_BWD, step, 0)
        dk_ref[...] = dka[...].astype(dk_ref.dtype)
        dv_ref[...] = dva[...].astype(dv_ref.dtype)

    col = lambda off: pl.BlockSpec((T, LANES), lambda i, off=off: (0, off + i))
    blk = pl.BlockSpec((T, LANES), lambda i: (0, i))
    tbl = pl.BlockSpec((None, N_CLS, BAND, LANES), lambda i: (i, 0, 0, 0))
    o16 = jax.ShapeDtypeStruct((T, naw), _MXU)
    vm = 2 * (4 * T * LANES * 4 + 2 * N_CLS * BAND * LANES * 4 + 3 * T * LANES * 2) + 2 * T * LANES * 6
    return _pallas(
        body, name=name,
        out_shape=(o16, o16, o16, jax.ShapeDtypeStruct((npair, N_CLS, BAND, LANES), _F32)),
        grid=(npair,),
        in_specs=[col(qoff), col(koff), col(voff), tbl, col(doff)],
        out_specs=(blk, blk, blk, tbl),
        scratch_shapes=[pltpu.VMEM((T, LANES), _MXU), pltpu.VMEM((T, LANES), _MXU),
                        pltpu.VMEM((T, LANES), _F32), pltpu.VMEM((T, LANES), _F32)],
        compiler_params=_params(("parallel",), vm),
    )(proj, proj, proj, bias, dycat)


_ANY = pl.BlockSpec(memory_space=pl.ANY)
_HBM = pl.BlockSpec(memory_space=pltpu.HBM)
_SEM = pl.BlockSpec(memory_space=pltpu.SEMAPHORE)
_VMEM = pl.BlockSpec(memory_space=pltpu.VMEM)
_MESH_ID = pl.DeviceIdType.MESH
_EFFECT = pltpu.SideEffectType.DATAFLOW_SIDE_EFFECTING
_TOKEN = jax.ShapeDtypeStruct((SUBLANES, LANES), _F32)


def _mesh_pos():
    return tuple(lax.axis_index(a) for a in MESH_AXES)


def _in_hbm(a):
    return pltpu.with_memory_space_constraint(a, pltpu.HBM)


def _hbm_like(arrays):
    return [pltpu.HBM(a.shape, a.dtype) for a in arrays]


def _after(arr, token):
    return arr + token[0:1, 0:1].astype(arr.dtype)


def _shard_ref(ref, axis, j, width):
    idx = [slice(None)] * len(ref.shape)
    idx[axis] = pl.ds(pl.multiple_of(j * width, math.gcd(width, LANES)), width)
    return ref.at[tuple(idx)]


def _all_gather(shards, axes, name):
    n = len(shards)
    widths = [s.shape[a] for s, a in zip(shards, axes)]
    out_shape = [jax.ShapeDtypeStruct(tuple(N_DEV * d if k == a else d for k, d in enumerate(s.shape)), s.dtype)
                 for s, a in zip(shards, axes)]

    def body(*refs):
        ins, outs = refs[:n], refs[n:2 * n]
        send_sems, recv_sems, local_sems = refs[2 * n:]
        x, y, c = _mesh_pos()
        me, sibling = (x, y, c), (x, y, 1 - c)
        chips = [(1 - x, y), (x, 1 - y), (1 - x, 1 - y)]

        def slot(i, px, py, pc):
            return _shard_ref(outs[i], axes[i], 4 * px + 2 * py + pc, widths[i])

        def copy(i, k, block, to, src=None):
            return pltpu.make_async_remote_copy(
                src_ref=slot(i, *block) if src is None else src, dst_ref=slot(i, *block),
                send_sem=send_sems.at[7 * i + k], recv_sem=recv_sems.at[7 * i + k],
                device_id=to, device_id_type=_MESH_ID)

        mine = [pltpu.make_async_copy(ins[i], slot(i, *me), local_sems.at[i]) for i in range(n)]
        for cp in mine:
            cp.start()
        first = []
        for i in range(n):
            first.append(copy(i, 0, me, sibling, src=ins[i]))
            first += [copy(i, 1 + j, me, (*chip, c), src=ins[i]) for j, chip in enumerate(chips)]
        for cp in first:
            cp.start()
        passed = []
        for j, chip in enumerate(chips):
            for i in range(n):
                copy(i, 1 + j, (*chip, c), me).wait_recv()
                fwd = copy(i, 4 + j, (*chip, c), sibling)
                fwd.start()
                passed.append(fwd)
        for i in range(n):
            copy(i, 0, sibling, me).wait_recv()
            for j, chip in enumerate(chips):
                copy(i, 4 + j, (*chip, 1 - c), me).wait_recv()
        for cp in first + passed:
            cp.wait_send()
        for cp in mine:
            cp.wait()

    return _pallas(
        body, name=name, out_shape=out_shape, in_specs=[_ANY] * n, out_specs=[_ANY] * n,
        scratch_shapes=[pltpu.SemaphoreType.DMA((7 * n,)), pltpu.SemaphoreType.DMA((7 * n,)),
                        pltpu.SemaphoreType.DMA((n,))],
    )(*shards)


def _block_of(ref, axis, blk):
    return _shard_ref(ref, axis, blk, ref.shape[axis] // N_DEV)


def _gather_start(lands, axes, after, name):
    n = len(lands)

    def body(*refs):
        land = refs[:n]
        send, recv_sib, recv_ici = refs[n + 1:n + 4]
        token = refs[-1]
        x, y, c = _mesh_pos()
        me = 4 * x + 2 * y + c
        for i in range(n):
            mine = _block_of(land[i], axes[i], me)
            pltpu.make_async_remote_copy(
                src_ref=mine, dst_ref=mine, send_sem=send.at[4 * i],
                recv_sem=recv_sib.at[i], device_id=(x, y, 1 - c), device_id_type=_MESH_ID).start()
            for j, chip in enumerate([(1 - x, y), (x, 1 - y), (1 - x, 1 - y)]):
                pltpu.make_async_remote_copy(
                    src_ref=mine, dst_ref=mine, send_sem=send.at[4 * i + 1 + j],
                    recv_sem=recv_ici.at[3 * i + j], device_id=(*chip, c), device_id_type=_MESH_ID).start()
        token[...] = jnp.zeros_like(token)

    dma = pltpu.SemaphoreType.DMA
    out = pl.pallas_call(
        body, name=name,
        out_shape=(dma((4 * n,)), dma((n,)), dma((3 * n,)), *_hbm_like(lands), _TOKEN),
        in_specs=[_HBM] * n + [_ANY], out_specs=(_SEM, _SEM, _SEM, *[_HBM] * n, _VMEM),
        input_output_aliases={i: 3 + i for i in range(n)},
        compiler_params=pltpu.CompilerParams(has_side_effects=_EFFECT),
    )(*[_in_hbm(a) for a in lands], after)
    return dict(send=out[0], recv_sib=out[1], recv_ici=out[2], lands=list(out[3:3 + n]), axes=axes, token=out[-1])


def _gather_forward(st, after, name):
    lands, axes = st["lands"], st["axes"]
    n = len(lands)

    def body(*refs):
        land = refs[:n]
        recv_ici = refs[n]
        send2, recv2 = refs[n + 2], refs[n + 3]
        token = refs[-1]
        x, y, c = _mesh_pos()
        for j, (px, py) in enumerate([(1 - x, y), (x, 1 - y), (1 - x, 1 - y)]):
            for i in range(n):
                blk = _block_of(land[i], axes[i], 4 * px + 2 * py + c)
                pltpu.make_async_remote_copy(
                    src_ref=blk, dst_ref=blk, send_sem=send2.at[3 * i + j],
                    recv_sem=recv_ici.at[3 * i + j], device_id=(px, py, c), device_id_type=_MESH_ID).wait_recv()
                pltpu.make_async_remote_copy(
                    src_ref=blk, dst_ref=blk, send_sem=send2.at[3 * i + j],
                    recv_sem=recv2.at[3 * i + j], device_id=(x, y, 1 - c), device_id_type=_MESH_ID).start()
        token[...] = jnp.zeros_like(token)

    dma = pltpu.SemaphoreType.DMA
    out = pl.pallas_call(
        body, name=name,
        out_shape=(dma((3 * n,)), dma((3 * n,)), *_hbm_like(lands), _TOKEN),
        in_specs=[_HBM] * n + [_SEM, _ANY], out_specs=(_SEM, _SEM, *[_HBM] * n, _VMEM),
        input_output_aliases={i: 2 + i for i in range(n)},
        compiler_params=pltpu.CompilerParams(has_side_effects=_EFFECT),
    )(*lands, st["recv_ici"], after)
    return dict(st, send2=out[0], recv2=out[1], lands=list(out[2:2 + n]), token=out[-1])


def _gather_finish(st, after, name):
    lands, axes = st["lands"], st["axes"]
    n = len(lands)

    def body(*refs):
        land = refs[:n]
        send, recv_sib, send2, recv2 = refs[n:n + 4]
        x, y, c = _mesh_pos()
        me = 4 * x + 2 * y + c
        sib = 4 * x + 2 * y + (1 - c)

        def desc(i, blk, s_sem, r_sem):
            ref = _block_of(land[i], axes[i], blk)
            return pltpu.make_async_remote_copy(
                src_ref=ref, dst_ref=ref, send_sem=s_sem, recv_sem=r_sem,
                device_id=(x, y, 1 - c), device_id_type=_MESH_ID)

        for i in range(n):
            desc(i, sib, send.at[4 * i], recv_sib.at[i]).wait_recv()
            for j, (px, py) in enumerate([(1 - x, y), (x, 1 - y), (1 - x, 1 - y)]):
                desc(i, 4 * px + 2 * py + (1 - c), send2.at[3 * i + j], recv2.at[3 * i + j]).wait_recv()
            for k in range(4):
                desc(i, me, send.at[4 * i + k], recv_sib.at[i]).wait_send()
            for j, (px, py) in enumerate([(1 - x, y), (x, 1 - y), (1 - x, 1 - y)]):
                desc(i, 4 * px + 2 * py + c, send2.at[3 * i + j], recv2.at[3 * i + j]).wait_send()

    out = pl.pallas_call(
        body, name=name, out_shape=tuple(_hbm_like(lands)),
        in_specs=[_HBM] * n + [_SEM] * 4 + [_ANY], out_specs=tuple([_HBM] * n),
        input_output_aliases={i: i for i in range(n)},
        compiler_params=pltpu.CompilerParams(has_side_effects=_EFFECT),
    )(*lands, st["send"], st["recv_sib"], st["send2"], st["recv2"], after)
    return list(out)


def _scatter_sibling_start(grads, name):
    n = len(grads)
    gots = [lax.empty((4,) + g.shape[1:], g.dtype) for g in grads]

    def body(*refs):
        grad, got = refs[:n], refs[n:2 * n]
        send, recv = refs[2 * n], refs[2 * n + 1]
        token = refs[-1]
        x, y, c = _mesh_pos()
        for i in range(n):
            for q in range(4):
                pltpu.make_async_remote_copy(
                    src_ref=grad[i].at[2 * q + (1 - c)], dst_ref=got[i].at[q], send_sem=send.at[4 * i + q],
                    recv_sem=recv.at[4 * i + q], device_id=(x, y, 1 - c), device_id_type=_MESH_ID).start()
        token[...] = jnp.zeros_like(token)

    dma = pltpu.SemaphoreType.DMA
    out = pl.pallas_call(
        body, name=name,
        out_shape=(dma((4 * n,)), dma((4 * n,)), *_hbm_like(grads), *_hbm_like(gots), _TOKEN),
        in_specs=[_HBM] * (2 * n), out_specs=(_SEM, _SEM, *[_HBM] * (2 * n), _VMEM),
        input_output_aliases={i: 2 + i for i in range(2 * n)},
        compiler_params=pltpu.CompilerParams(has_side_effects=_EFFECT),
    )(*[_in_hbm(a) for a in grads], *[_in_hbm(a) for a in gots])
    return dict(send=out[0], recv=out[1], grads=list(out[2:2 + n]), gots=list(out[2 + n:2 + 2 * n]), token=out[-1])


def _scatter_sibling_finish(st, after, name):
    grads, gots = st["grads"], st["gots"]
    n = len(grads)

    def body(*refs):
        grad, got = refs[:n], refs[n:2 * n]
        send, recv = refs[2 * n], refs[2 * n + 1]
        x, y, c = _mesh_pos()
        for i in range(n):
            for q in range(4):
                cp = pltpu.make_async_remote_copy(
                    src_ref=grad[i].at[2 * q + (1 - c)], dst_ref=got[i].at[q], send_sem=send.at[4 * i + q],
                    recv_sem=recv.at[4 * i + q], device_id=(x, y, 1 - c), device_id_type=_MESH_ID)
                cp.wait_recv()
                cp.wait_send()

    out = pl.pallas_call(
        body, name=name, out_shape=tuple(_hbm_like(grads) + _hbm_like(gots)),
        in_specs=[_HBM] * (2 * n) + [_SEM, _SEM, _ANY], out_specs=tuple([_HBM] * (2 * n)),
        input_output_aliases={i: i for i in range(2 * n)},
        compiler_params=pltpu.CompilerParams(has_side_effects=_EFFECT),
    )(*grads, *gots, st["send"], st["recv"], after)
    return list(out[:n]), list(out[n:])


def _scatter_add(grad, got, name):
    _, R, C = grad.shape
    tr = _blk(R, 512)
    my_c = lambda: lax.axis_index("c")
    my_chip = lambda: 2 * lax.axis_index("x") + lax.axis_index("y")

    def body(a_ref, b_ref, part_ref, fin_ref):
        s = (a_ref[...].astype(_F32) + b_ref[...].astype(_F32)).astype(part_ref.dtype)
        part_ref[...] = s

        @pl.when(pl.program_id(1) == my_chip())
        def _():
            fin_ref[...] = s

    shape = jax.ShapeDtypeStruct((4, R, C), grad.dtype)
    return _pallas(
        body, name=name, out_shape=(shape, shape), grid=(R // tr, 4),
        in_specs=[pl.BlockSpec((None, tr, C), lambda i, q: (2 * q + my_c(), i, 0)),
                  pl.BlockSpec((None, tr, C), lambda i, q: (q, i, 0))],
        out_specs=[pl.BlockSpec((None, tr, C), lambda i, q: (q, i, 0)),
                   pl.BlockSpec((None, tr, C), lambda i, q: (my_chip(), i, 0))],
        compiler_params=_params(("parallel", "arbitrary"), 2 * tr * C * 8),
    )(grad, got)


def _scatter_chips_start(parts, fins, name):
    n = len(parts)

    def body(*refs):
        part, fin = refs[:n], refs[n:2 * n]
        send, recv = refs[2 * n], refs[2 * n + 1]
        token = refs[-1]
        x, y, c = _mesh_pos()
        mine = 2 * x + y
        for i in range(n):
            for k, (tx, ty) in enumerate([(1 - x, y), (x, 1 - y), (1 - x, 1 - y)]):
                pltpu.make_async_remote_copy(
                    src_ref=part[i].at[2 * tx + ty], dst_ref=fin[i].at[mine], send_sem=send.at[3 * i + k],
                    recv_sem=recv.at[3 * i + k], device_id=(tx, ty, c), device_id_type=_MESH_ID).start()
        token[...] = jnp.zeros_like(token)

    dma = pltpu.SemaphoreType.DMA
    out = pl.pallas_call(
        body, name=name,
        out_shape=(dma((3 * n,)), dma((3 * n,)), *_hbm_like(parts), *_hbm_like(fins), _TOKEN),
        in_specs=[_HBM] * (2 * n), out_specs=(_SEM, _SEM, *[_HBM] * (2 * n), _VMEM),
        input_output_aliases={i: 2 + i for i in range(2 * n)},
        compiler_params=pltpu.CompilerParams(has_side_effects=_EFFECT),
    )(*[_in_hbm(a) for a in parts], *[_in_hbm(a) for a in fins])
    return dict(send=out[0], recv=out[1], parts=list(out[2:2 + n]), fins=list(out[2 + n:2 + 2 * n]), token=out[-1])


def _scatter_chips_finish(st, after, name):
    parts, fins = st["parts"], st["fins"]
    n = len(parts)

    def body(*refs):
        part, fin = refs[:n], refs[n:2 * n]
        send, recv = refs[2 * n], refs[2 * n + 1]
        x, y, c = _mesh_pos()
        for i in range(n):
            for k, (tx, ty) in enumerate([(1 - x, y), (x, 1 - y), (1 - x, 1 - y)]):
                cp = pltpu.make_async_remote_copy(
                    src_ref=part[i].at[2 * tx + ty], dst_ref=fin[i].at[2 * tx + ty], send_sem=send.at[3 * i + k],
                    recv_sem=recv.at[3 * i + k], device_id=(tx, ty, c), device_id_type=_MESH_ID)
                cp.wait_recv()
                cp.wait_send()

    out = pl.pallas_call(
        body, name=name, out_shape=tuple(_hbm_like(parts) + _hbm_like(fins)),
        in_specs=[_HBM] * (2 * n) + [_SEM, _SEM, _ANY], out_specs=tuple([_HBM] * (2 * n)),
        input_output_aliases={i: i for i in range(2 * n)},
        compiler_params=pltpu.CompilerParams(has_side_effects=_EFFECT),
    )(*parts, *fins, st["send"], st["recv"], after)
    return list(out[n:])


def _adamw(g, w, m, v):
    m = ADAM_B1 * m + (1.0 - ADAM_B1) * g
    v = ADAM_B2 * v + (1.0 - ADAM_B2) * (g * g)
    m_hat = m / (1.0 - ADAM_B1 ** ADAM_STEP)
    v_hat = v / (1.0 - ADAM_B2 ** ADAM_STEP)
    delta = -ADAM_LR * (m_hat / (jnp.sqrt(v_hat) + ADAM_EPS) + ADAM_WD * w)
    return delta, m, v


def _adam_layer(fin, w3, m3, v3, l, prev, name):
    L, R, C = w3.shape
    tr = _blk(R, max(SUBLANES, (1 << 18) // C))

    def body(f_ref, w_ref, m_ref, v_ref, *rest):
        g_ref, d_ref, nm_ref, nv_ref = rest[-4:]
        g = ((f_ref[0].astype(_F32) + f_ref[1].astype(_F32)) + f_ref[2].astype(_F32)) + f_ref[3].astype(_F32)
        d, nm, nv = _adamw(g, w_ref[...], m_ref[...], v_ref[...])
        g_ref[...] = g
        d_ref[...] = d
        nm_ref[...] = nm
        nv_ref[...] = nv

    lay = pl.BlockSpec((None, tr, C), lambda i: (l, i, 0))
    ins = [fin, w3, m3, v3]
    in_specs = [pl.BlockSpec((4, tr, C), lambda i: (0, i, 0)), lay, lay, lay]
    aliases = {}
    if prev is not None:
        ins += list(prev)
        in_specs += [_ANY] * 4
        aliases = {4 + k: k for k in range(4)}
    return _pallas(
        body, name=name, out_shape=[jax.ShapeDtypeStruct((L, R, C), _F32)] * 4, grid=(R // tr,),
        in_specs=in_specs, out_specs=[lay] * 4, input_output_aliases=aliases,
        compiler_params=_params(("parallel",), 2 * tr * C * (4 * 2 + 7 * 4)),
    )(*ins)


def _sum_parts(parts, name):
    _, R, C = parts.shape

    def body(p_ref, o_ref):
        acc = p_ref[0]
        for k in range(1, N_DEV):
            acc = acc + p_ref[k]
        o_ref[...] = acc

    tr = _blk(R, 512)
    return _pallas(
        body, name=name, out_shape=jax.ShapeDtypeStruct((R, C), _F32), grid=(R // tr,),
        in_specs=[pl.BlockSpec((N_DEV, tr, C), lambda i: (0, i, 0))],
        out_specs=pl.BlockSpec((tr, C), lambda i: (i, 0)),
        compiler_params=_params(("parallel",), 2 * tr * C * 4 * 9),
    )(parts)


def _adam_flat(g, w, m, v, name):
    R, C = g.shape
    tr = _blk(R, 512)

    def body(g_ref, w_ref, m_ref, v_ref, d_ref, nm_ref, nv_ref):
        d, nm, nv = _adamw(g_ref[...], w_ref[...], m_ref[...], v_ref[...])
        d_ref[...] = d
        nm_ref[...] = nm
        nv_ref[...] = nv

    spec = pl.BlockSpec((tr, C), lambda i: (i, 0))
    return _pallas(
        body, name=name, out_shape=[jax.ShapeDtypeStruct((R, C), _F32)] * 3, grid=(R // tr,),
        in_specs=[spec] * 4, out_specs=[spec] * 3,
        compiler_params=_params(("parallel",), 2 * tr * C * 4 * 7),
    )(g, w, m, v)


def _pack(arrays):
    flat = jnp.concatenate([a.reshape(-1) for a in arrays])
    tile = SUBLANES * LANES
    pad = (-flat.shape[0]) % tile
    return jnp.pad(flat, (0, pad)).reshape(-1, LANES)


def _unpack(packed, shapes):
    flat = packed.reshape(-1)
    out, off = [], 0
    for s in shapes:
        n = math.prod(s)
        out.append(flat[off:off + n].reshape(s))
        off += n
    return out


def kernel(x, w_in, w_dw, b_dw, conv_ln_g, conv_ln_b, rpb, w_out, w_up, w_down, pre_mix_g, post_mix_g, pre_mlp_g, post_mlp_g, loss_target, m_w_in, m_w_dw, m_b_dw, m_conv_ln_g, m_conv_ln_b, m_rpb, m_w_out, m_w_up, m_w_down, m_pre_mix_g, m_post_mix_g, m_pre_mlp_g, m_post_mlp_g, v_w_in, v_w_dw, v_b_dw, v_conv_ln_g, v_conv_ln_b, v_rpb, v_w_out, v_w_up, v_w_down, v_pre_mix_g, v_post_mix_g, v_pre_mlp_g, v_post_mlp_g):
    _, T, D = x.shape
    L = w_in.shape[0]
    cw = b_dw.shape[1]
    H = rpb.shape[1]
    naw = H * HEAD_DIM
    ks = w_dw.shape[1]
    assert T % GRID_W == 0 and T // GRID_W >= WIN_ROWS and H % 2 == 0 and cw % LANES == 0
    assert rpb.shape[2:] == (N_DR, N_DC) and w_dw.shape[2] * N_DEV == cw and ks // 2 < CONV_HALO
    assert naw == cw and w_out.shape[1] * N_DEV == cw + naw and (T // GRID_W) % ROWS_PER_STEP_FWD == 0

    xs = x.reshape(T, D)
    tgt = loss_target.reshape(T, D)
    row = lambda p, l: p[l:l + 1]
    mx, my, mc = (lax.axis_index(a) for a in MESH_AXES)
    dev = 4 * mx + 2 * my + mc

    ks_pad = ks + (-ks) % SUBLANES
    wdw_pad = jnp.pad(w_dw, ((0, 0), (0, ks_pad - ks), (0, 0))).reshape(L * ks_pad, w_dw.shape[2])
    wdw_full = _all_gather([wdw_pad], [1], "ag_wdw")[0].reshape(L, ks_pad, cw)[:, :ks]

    big = (w_in, w_out, w_up, w_down)
    names = ("in", "out", "up", "down")

    big_axes = (1, 0, 1, 0)

    def gather_start(l, which, after):
        lands = [_cast_slot(big[k], l, big_axes[k] == 1, f"cast_{names[k]}") for k in which]
        return _gather_start(lands, [big_axes[k] for k in which], after, "gather_start_%d" % len(which))

    def gather_pair(l, after):
        g_in = gather_start(l, [0], after)
        return g_in, gather_start(l, [1, 2, 3], g_in["token"])

    saved = []
    xin = xs
    h = _norm_fwd(xs, row(pre_mix_g, 0), "norm_first")
    g_in, g_rest = gather_pair(0, wdw_full)
    g_in = _gather_forward(g_in, g_rest["token"], "gather_forward_1")
    Win = _gather_finish(g_in, g_in["token"], "gather_finish_1")[0]
    dy = loss_sum = None
    for l in range(L):
        nxt = gather_pair(l + 1, Win) if l + 1 < L else None
        proj = _matmul(h, Win, name="mm_proj")
        bdw = _after(row(b_dw, l), nxt[1]["token"]) if nxt else row(b_dw, l)
        if l > 0:
            g_rest = _gather_forward(g_rest, proj, "gather_forward_3")
            bdw = _after(bdw, g_rest["token"])
        c = _conv_fwd(proj, wdw_full[l], bdw, cw, "conv_fwd")
        yc = _ln_silu_fwd(c, row(conv_ln_g, l), row(conv_ln_b, l), "ln_silu_fwd")
        bias = _bias_table(rpb[l].reshape(H, N_DR * N_DC), "bias_table")
        ya = _attn_fwd(proj, bias, cw, naw, "attn_fwd")
        ycat = jnp.concatenate([yc, ya], axis=1)
        if l == 0:
            g_rest = _gather_forward(g_rest, ycat, "gather_forward_3")
        Wout, Wup, Wdown = _gather_finish(g_rest, ycat, "gather_finish_3")
        Ws = (Win, Wout, Wup, Wdown)
        mix = _matmul(ycat, Wout, name="mm_mix")
        x1, h2 = _resid_norm_fwd(xin, mix, row(post_mix_g, l), row(pre_mlp_g, l), "resid_mix")
        act, rl = _matmul(h2, Wup, epilogue="relu2", name="mm_up")
        if nxt:
            g_in = _gather_forward(nxt[0], act, "gather_forward_1")
        f = _matmul(act, Wdown, tm=512, tk=4096, name="mm_down")
        saved.append(dict(xin=xin, h=h, W=Ws, proj=proj, c=c, bias=bias, ycat=ycat,
                          mix=mix, x1=x1, h2=h2, act=act, rl=rl, f=f))
        if nxt:
            Win = _gather_finish(g_in, f, "gather_finish_1")[0]
            g_rest = nxt[1]
            xin, h = _resid_norm_fwd(x1, f, row(post_mlp_g, l), row(pre_mix_g, l + 1), "resid_mlp")
        else:
            dy, loss_sum = _resid_loss(x1, f, row(post_mlp_g, l), tgt, "resid_loss")

    loss = lax.psum(loss_sum[0, 0] * (0.5 / D), MESH_AXES)

    small_grads = [None] * L
    big_out = [None] * 4
    moments = ((m_w_in, v_w_in), (m_w_out, v_w_out), (m_w_up, v_w_up), (m_w_down, v_w_down))

    def scatter_begin(grads, which, l):
        tag = "_%d" % len(which)
        return dict(st=_scatter_sibling_start(grads, "scatter_sibling_start" + tag), which=which, l=l, tag=tag)

    def scatter_mid(sc, after):
        grads, gots = _scatter_sibling_finish(sc["st"], after, "scatter_sibling_finish" + sc["tag"])
        pf = [_scatter_add(g, o, f"scatter_add_{k}") for k, g, o in zip(sc["which"], grads, gots)]
        st = _scatter_chips_start([p for p, _ in pf], [q for _, q in pf], "scatter_chips_start" + sc["tag"])
        return dict(sc, st=st)

    def scatter_end(sc, after):
        fins = _scatter_chips_finish(sc["st"], after, "scatter_chips_finish" + sc["tag"])
        for k, fin in zip(sc["which"], fins):
            big_out[k] = _adam_layer(fin, big[k], moments[k][0], moments[k][1], sc["l"], big_out[k],
                                     f"adam_{k}_{sc['l']}")

    dxo = dy
    pending = None
    for l in reversed(range(L)):
        s = saved[l]
        Win, Wout, Wup, Wdown = s["W"]
        last = l == 0
        g_post_mlp = row(post_mlp_g, l)
        if pending is not None:
            g_post_mlp = _after(g_post_mlp, pending["st"]["token"])
        d_f, dg_post_mlp = _norm_bwd(s["f"], g_post_mlp, dxo, None, _MXU, "norm_bwd_mlp")
        d_up = _matmul(d_f, Wdown, tb=True, epilogue="mul2", extra=s["rl"], name="mm_d_up")
        g_pre_mlp = row(pre_mlp_g, l)
        if pending is not None:
            pending = scatter_mid(pending, d_up)
            g_pre_mlp = _after(g_pre_mlp, pending["st"]["token"])
        dWdown = _matmul(s["act"], d_f, ta=True, out_dtype=_WIRE, tk=T, name="mm_dw_down").reshape(N_DEV, -1, D)
        d_h2 = _matmul(d_up, Wup, tb=True, tm=512, tk=4096, name="mm_d_h2")
        dWup = _matmul(s["h2"], d_up, ta=True, out_dtype=_WIRE, out_cols=N_DEV, tk=T, name="mm_dw_up")
        g_post_mix, ln_g, wdw_l, g_pre_mix = row(post_mix_g, l), row(conv_ln_g, l), wdw_full[l], row(pre_mix_g, l)
        if last:
            sc_mlp = scatter_begin([dWup, dWdown], [2, 3], l)
            g_post_mix = _after(g_post_mix, sc_mlp["st"]["token"])
        dx1, dg_pre_mlp = _norm_bwd(s["x1"], g_pre_mlp, d_h2, dxo, _F32, "norm_bwd_premlp")
        d_mix, dg_post_mix = _norm_bwd(s["mix"], g_post_mix, dx1, None, _MXU, "norm_bwd_mix")
        d_ycat = _matmul(d_mix, Wout, tb=True, name="mm_d_ycat")
        if last:
            sc_mlp = scatter_mid(sc_mlp, d_ycat)
            ln_g = _after(ln_g, sc_mlp["st"]["token"])
        dWout = _matmul(s["ycat"], d_mix, ta=True, out_dtype=_WIRE, tk=T, name="mm_dw_out").reshape(N_DEV, -1, D)
        if last:
            sc_out = scatter_begin([dWout], [1], l)
            wdw_l = _after(wdw_l, sc_out["st"]["token"])
        dc, dlng, dlnb = _ln_silu_bwd(s["c"], ln_g, row(conv_ln_b, l), d_ycat, "ln_silu_bwd")
        da, dgate, dwb = _conv_bwd(s["proj"], dc, wdw_l, cw, "conv_bwd")
        dq, dk, dv, gcls = _attn_bwd(s["proj"], s["bias"], d_ycat, cw, naw, "attn_bwd")
        if last:
            sc_out = scatter_mid(sc_out, dq)
            g_pre_mix = _after(g_pre_mix, sc_out["st"]["token"])
        drpb = _rpb_grad(gcls, "rpb_grad").reshape(H, 4 * SUBLANES, LANES)[:, :N_DC, :N_DR].transpose(0, 2, 1)
        dproj = jnp.concatenate([da, dgate, dq, dk, dv], axis=1)
        dh = _matmul(dproj, Win, tb=True, tk=dproj.shape[1] // 2, name="mm_d_h")
        dWin = _matmul(s["h"], dproj, ta=True, out_dtype=_WIRE, out_cols=N_DEV, tk=T, name="mm_dw_in")
        dxo, dg_pre_mix = _norm_bwd(s["xin"], g_pre_mix, dh, dx1, _F32, "norm_bwd_premix")
        if pending is not None:
            scatter_end(pending, dxo)
        if last:
            sc_in = scatter_begin([dWin], [0], l)
            sc_in = scatter_mid(sc_in, sc_in["st"]["token"])
            scatter_end(sc_mlp, sc_in["st"]["token"])
            scatter_end(sc_out, sc_in["st"]["token"])
            scatter_end(sc_in, sc_in["st"]["token"])
        else:
            pending = scatter_begin([dWin, dWout, dWup, dWdown], [0, 1, 2, 3], l)
        small_grads[l] = [dwb[ks], dlng[0], dlnb[0], drpb, dg_pre_mix[0], dg_post_mix[0], dg_pre_mlp[0],
                          dg_post_mlp[0], dwb[:ks]]

    rep_shapes = [(L, cw), (L, cw), (L, cw), (L, H, N_DR, N_DC), (L, D), (L, D), (L, D), (L, D)]
    stacked = [jnp.stack([small_grads[l][k] for l in range(L)]) for k in range(9)]
    packed = _pack(stacked)
    parts = _all_gather([packed], [0], "ag_small")[0].reshape(N_DEV, *packed.shape)
    gsum = _sum_parts(parts, "sum_small")
    g_small = _unpack(gsum, rep_shapes + [(L, ks, cw)])
    g_rep, g_wdw_full = g_small[:8], g_small[8]
    wsh = w_dw.shape[2]
    g_wdw = lax.dynamic_slice_in_dim(g_wdw_full, dev * wsh, wsh, axis=2)

    rep_w = [b_dw, conv_ln_g, conv_ln_b, rpb, pre_mix_g, post_mix_g, pre_mlp_g, post_mlp_g]
    rep_m = [m_b_dw, m_conv_ln_g, m_conv_ln_b, m_rpb, m_pre_mix_g, m_post_mix_g, m_pre_mlp_g, m_post_mlp_g]
    rep_v = [v_b_dw, v_conv_ln_g, v_conv_ln_b, v_rpb, v_pre_mix_g, v_post_mix_g, v_pre_mlp_g, v_post_mlp_g]
    rep_out = _adam_flat(_pack(g_rep), _pack(rep_w), _pack(rep_m), _pack(rep_v), "adam_small")
    rep_delta, rep_nm, rep_nv = (_unpack(o, rep_shapes) for o in rep_out)
    dw_out = _adam_flat(_pack([g_wdw]), _pack([w_dw]), _pack([m_w_dw]), _pack([v_w_dw]), "adam_wdw")
    wdw_delta, wdw_nm, wdw_nv = (_unpack(o, [w_dw.shape])[0] for o in dw_out)

    def assemble(kind_big, rep_list, wdw_val):
        return [big_out[0][kind_big], wdw_val, rep_list[0], rep_list[1], rep_list[2], rep_list[3],
                big_out[1][kind_big], big_out[2][kind_big], big_out[3][kind_big],
                rep_list[4], rep_list[5], rep_list[6], rep_list[7]]

    grads_out = assemble(0, g_rep, g_wdw)
    deltas = assemble(1, rep_delta, wdw_delta)
    new_m = assemble(2, rep_nm, wdw_nm)
    new_v = assemble(3, rep_nv, wdw_nv)
    return (loss, dxo.reshape(1, T, D), *grads_out, *deltas, *new_m, *new_v)
```

```python
import math

import jax
import jax.numpy as jnp
from jax import lax
from jax.experimental import pallas as pl
from jax.experimental.pallas import tpu as pltpu

_MXU = jnp.bfloat16
_WIRE = jnp.bfloat16
_F32 = jnp.float32

N_DEV = 8
GRID_W = 64
WIN_ROWS = 8
WIN_COLS = 16
HEAD_DIM = 64
LANES = 128
SUBLANES = 8
RMS_EPS = 1e-6
LN_EPS = 1e-5
NEG_INF = -1e30
ADAM_LR = 0.001
ADAM_B1 = 0.9
ADAM_B2 = 0.999
ADAM_EPS = 1e-08
ADAM_WD = 0.01
ADAM_STEP = 10
VMEM_BYTES_V7X = 64 << 20
VMEM_RESERVE = 12 << 20
MESH_AXES = ("x", "y", "c")


def _vmem_limit(block_bytes):
    return int(min(max(block_bytes + (8 << 20), 24 << 20), VMEM_BYTES_V7X - VMEM_RESERVE))


def _blk(n, pref):
    if n <= pref:
        return n
    for t in range(pref, 7, -1):
        if n % t == 0 and t % SUBLANES == 0:
            return t
    return n


def _sigmoid(v):
    return 1.0 / (1.0 + jnp.exp(-v))


def _params(sem, nbytes):
    return pltpu.CompilerParams(dimension_semantics=sem, vmem_limit_bytes=_vmem_limit(nbytes))


def _pallas(body, **kw):
    call = pl.pallas_call(body, **kw)

    def run(*operands):
        return call(*[pltpu.with_memory_space_constraint(o, pltpu.HBM)
                      if jnp.issubdtype(o.dtype, jnp.floating) else o for o in operands])

    return run


def _my_block():
    x, y, c = (lax.axis_index(a) for a in MESH_AXES)
    return 4 * x + 2 * y + c


def _cast_slot(w3, l, by_cols, name):
    _, R, C = w3.shape
    tr = _blk(R, 512)

    def body(w_ref, o_ref):
        o_ref[...] = w_ref[...].astype(o_ref.dtype)

    if by_cols:
        shape, o_spec = (R, N_DEV * C), pl.BlockSpec((tr, C), lambda i: (i, _my_block()))
    else:
        shape, o_spec = (N_DEV * R, C), pl.BlockSpec((tr, C), lambda i: (_my_block() * (R // tr) + i, 0))
    return _pallas(
        body, name=name, out_shape=jax.ShapeDtypeStruct(shape, _WIRE), grid=(R // tr,),
        in_specs=[pl.BlockSpec((None, tr, C), lambda i: (l, i, 0))], out_specs=o_spec,
        compiler_params=_params(("parallel",), 2 * tr * C * 6),
    )(w3)


def _norm_fwd(x, g, name):
    T, D = x.shape
    tm = _blk(T, 256)

    def body(x_ref, g_ref, h_ref):
        xv = x_ref[...]
        r = lax.rsqrt(jnp.mean(xv * xv, axis=-1, keepdims=True) + RMS_EPS)
        h_ref[...] = (xv * r * g_ref[...]).astype(h_ref.dtype)

    return _pallas(
        body, name=name, out_shape=jax.ShapeDtypeStruct((T, D), _MXU), grid=(T // tm,),
        in_specs=[pl.BlockSpec((tm, D), lambda i: (i, 0)), pl.BlockSpec((1, D), lambda i: (0, 0))],
        out_specs=pl.BlockSpec((tm, D), lambda i: (i, 0)),
        compiler_params=_params(("parallel",), 2 * tm * D * 6),
    )(x, g)


def _resid_norm_fwd(xres, y, g_post, g_next, name):
    T, D = xres.shape
    tm = _blk(T, 256)

    def body(x_ref, y_ref, gp_ref, gn_ref, xn_ref, h_ref):
        yv = y_ref[...]
        r = lax.rsqrt(jnp.mean(yv * yv, axis=-1, keepdims=True) + RMS_EPS)
        xn = x_ref[...] + yv * r * gp_ref[...]
        xn_ref[...] = xn
        r2 = lax.rsqrt(jnp.mean(xn * xn, axis=-1, keepdims=True) + RMS_EPS)
        h_ref[...] = (xn * r2 * gn_ref[...]).astype(h_ref.dtype)

    row = pl.BlockSpec((tm, D), lambda i: (i, 0))
    vec = pl.BlockSpec((1, D), lambda i: (0, 0))
    return _pallas(
        body, name=name,
        out_shape=(jax.ShapeDtypeStruct((T, D), _F32), jax.ShapeDtypeStruct((T, D), _MXU)),
        grid=(T // tm,), in_specs=[row, row, vec, vec], out_specs=(row, row),
        compiler_params=_params(("parallel",), 2 * tm * D * 14),
    )(xres, y, g_post, g_next)


def _resid_loss(xres, y, g_post, target, name):
    T, D = xres.shape
    tm = _blk(T, 256)

    def body(x_ref, y_ref, gp_ref, t_ref, dy_ref, loss_ref):
        yv = y_ref[...]
        r = lax.rsqrt(jnp.mean(yv * yv, axis=-1, keepdims=True) + RMS_EPS)
        err = x_ref[...] + yv * r * gp_ref[...] - t_ref[...]
        dy_ref[...] = err * (1.0 / D)

        @pl.when(pl.program_id(0) == 0)
        def _():
            loss_ref[...] = jnp.zeros_like(loss_ref)

        part = jnp.sum(jnp.sum(err * err, axis=-1, keepdims=True), axis=0, keepdims=True)
        loss_ref[...] += part

    row = pl.BlockSpec((tm, D), lambda i: (i, 0))
    vec = pl.BlockSpec((1, D), lambda i: (0, 0))
    return _pallas(
        body, name=name,
        out_shape=(jax.ShapeDtypeStruct((T, D), _F32), jax.ShapeDtypeStruct((1, 1), _F32)),
        grid=(T // tm,), in_specs=[row, row, vec, row],
        out_specs=(row, pl.BlockSpec((1, 1), lambda i: (0, 0))),
        compiler_params=_params(("arbitrary",), 2 * tm * D * 16),
    )(xres, y, g_post, target)


def _norm_bwd(y, g, dout, dres, out_dtype, name):
    T, D = y.shape
    tm = _blk(T, 256)
    nsteps = T // tm
    has_res = dres is not None

    def body(*refs):
        if has_res:
            y_ref, g_ref, do_ref, dr_ref, dy_ref, dg_ref, acc = refs
        else:
            y_ref, g_ref, do_ref, dy_ref, dg_ref, acc = refs
        i = pl.program_id(0)
        yv = y_ref[...]
        do = do_ref[...]
        r = lax.rsqrt(jnp.mean(yv * yv, axis=-1, keepdims=True) + RMS_EPS)
        gy = do * g_ref[...]
        dot = jnp.mean(yv * gy, axis=-1, keepdims=True)
        dy = r * gy - yv * (r * r * r * dot)
        if has_res:
            dy = dy + dr_ref[...]
        dy_ref[...] = dy.astype(dy_ref.dtype)

        @pl.when(i == 0)
        def _():
            acc[...] = jnp.zeros_like(acc)

        acc[...] += jnp.sum((do * yv * r).reshape(tm // SUBLANES, SUBLANES, D), axis=0)

        @pl.when(i == nsteps - 1)
        def _():
            dg_ref[...] = jnp.sum(acc[...], axis=0, keepdims=True)

    row = pl.BlockSpec((tm, D), lambda i: (i, 0))
    vec = pl.BlockSpec((1, D), lambda i: (0, 0))
    ins = [y, g, dout] + ([dres] if has_res else [])
    in_specs = [row, vec, row] + ([row] if has_res else [])
    return _pallas(
        body, name=name,
        out_shape=(jax.ShapeDtypeStruct((T, D), out_dtype), jax.ShapeDtypeStruct((1, D), _F32)),
        grid=(nsteps,), in_specs=in_specs, out_specs=(row, vec),
        scratch_shapes=[pltpu.VMEM((SUBLANES, D), _F32)],
        compiler_params=_params(("arbitrary",), 2 * tm * D * 16),
    )(*ins)


def _matmul(a, b, *, ta=False, tb=False, out_dtype=_F32, epilogue=None, extra=None, out_cols=0,
            tm=1024, tk=2048, name):
    M, K = (a.shape[1], a.shape[0]) if ta else a.shape
    N = b.shape[0] if tb else b.shape[1]
    tm = _blk(M, tm)
    tn = N // out_cols if out_cols else _blk(N, 1024)
    tk = _blk(K, tk)
    b_spec = (pl.BlockSpec((tn, tk), lambda i, j, k: (j, k)) if tb
              else pl.BlockSpec((tk, tn), lambda i, j, k: (k, j)))
    nk = K // tk
    a_spec = (pl.BlockSpec((tk, tm), lambda i, j, k: (k, i)) if ta
              else pl.BlockSpec((tm, tk), lambda i, j, k: (i, k)))
    if out_cols:
        assert N // tn == out_cols and epilogue is None
        o_spec = pl.BlockSpec((None, tm, tn), lambda i, j, k: (j, i, 0))
        o_shape = (out_cols, M, tn)
    else:
        o_spec = pl.BlockSpec((tm, tn), lambda i, j, k: (i, j))
        o_shape = (M, N)
    dims = (((0 if ta else 1,), (1 if tb else 0,)), ((), ()))
    n_extra = 1 if epilogue == "mul2" else 0
    n_out = 2 if epilogue == "relu2" else 1

    def finish(acc, extra_refs, out_refs):
        if epilogue is None:
            out_refs[0][...] = acc.astype(out_refs[0].dtype)
        elif epilogue == "relu2":
            rl = jnp.maximum(acc, 0.0)
            out_refs[0][...] = (rl * rl).astype(out_refs[0].dtype)
            out_refs[1][...] = rl.astype(out_refs[1].dtype)
        else:
            out_refs[0][...] = (acc * (2.0 * extra_refs[0][...].astype(_F32))).astype(out_refs[0].dtype)

    def body(a_ref, b_ref, *rest):
        extra_refs = rest[:n_extra]
        out_refs = rest[n_extra:n_extra + n_out]
        part = lax.dot_general(a_ref[...], b_ref[...], dims, preferred_element_type=_F32)
        if nk == 1:
            finish(part, extra_refs, out_refs)
            return
        acc = rest[-1]
        k = pl.program_id(2)

        @pl.when(k == 0)
        def _():
            acc[...] = part

        @pl.when(k > 0)
        def _():
            acc[...] += part

        @pl.when(k == nk - 1)
        def _():
            finish(acc[...], extra_refs, out_refs)

    if epilogue == "relu2":
        out_shape = (jax.ShapeDtypeStruct((M, N), _MXU), jax.ShapeDtypeStruct((M, N), _MXU))
        out_specs = (o_spec, o_spec)
        out_bytes = 2 * tm * tn * 2
    else:
        odt = _MXU if epilogue == "mul2" else out_dtype
        out_shape = jax.ShapeDtypeStruct(o_shape, odt)
        out_specs = o_spec
        out_bytes = tm * tn * jnp.dtype(odt).itemsize
    in_specs = [a_spec, b_spec] + ([o_spec] if n_extra else [])
    ins = [a, b] + ([extra] if n_extra else [])
    blocks = 2 * (tm * tk * 2 + tk * tn * 2 + out_bytes + n_extra * tm * tn * 2) + tm * tn * 4 * 2
    return _pallas(
        body, name=name, out_shape=out_shape, grid=(M // tm, N // tn, nk),
        in_specs=in_specs, out_specs=out_specs,
        scratch_shapes=[pltpu.VMEM((tm, tn), _F32)] if nk > 1 else [],
        compiler_params=_params(("parallel", "parallel", "arbitrary"), blocks),
    )(*ins)


CONV_HALO = 16
CONV_CHUNK = 256


def _tap_windows(win, n_taps_plus1, tc):
    n = win.shape[0]
    for s in range(SUBLANES):
        shifted = win if s == 0 else pltpu.roll(win, n - s, 0)
        for q in range((n_taps_plus1 + SUBLANES - 1) // SUBLANES):
            o = SUBLANES * q + s
            if 1 <= o < n_taps_plus1:
                yield o, shifted[SUBLANES * q:SUBLANES * q + tc, :]


def _conv_fwd(proj, wdw, bdw, cw, name):
    T = proj.shape[0]
    ks = wdw.shape[0]
    cb = LANES
    tc = _blk(T, CONV_CHUNK)
    nblk = cw // cb

    def body(a_ref, g_ref, w_ref, b_ref, c_ref, upad):
        zeros = jnp.zeros((CONV_HALO, cb), _F32)
        upad[0:CONV_HALO, :] = zeros
        upad[T + CONV_HALO:T + 2 * CONV_HALO, :] = zeros
        upad[CONV_HALO:T + CONV_HALO, :] = a_ref[...] * _sigmoid(g_ref[...])

        def chunk(i, carry):
            t0 = pl.multiple_of(i * tc, tc)
            win = upad[pl.ds(t0, tc + 2 * CONV_HALO), :]
            acc = jnp.broadcast_to(b_ref[...], (tc, cb))
            for o, rows in _tap_windows(win, ks + 1, tc):
                j = o + ks // 2 - CONV_HALO
                acc = acc + rows * w_ref[j:j + 1, :]
            c_ref[pl.ds(t0, tc), :] = acc
            return carry

        lax.fori_loop(0, T // tc, chunk, 0)

    col = lambda off: pl.BlockSpec((T, cb), lambda i, off=off: (0, off + i))
    return _pallas(
        body, name=name, out_shape=jax.ShapeDtypeStruct((T, cw), _F32), grid=(nblk,),
        in_specs=[col(0), col(nblk), pl.BlockSpec((ks, cb), lambda i: (0, i)),
                  pl.BlockSpec((1, cb), lambda i: (0, i))],
        out_specs=pl.BlockSpec((T, cb), lambda i: (0, i)),
        scratch_shapes=[pltpu.VMEM((T + 2 * CONV_HALO, cb), _F32)],
        compiler_params=_params(("parallel",), 2 * T * cb * 4 * 3 + T * cb * 4),
    )(proj, proj, wdw, bdw)


def _ln_silu_fwd(c, lng, lnb, out_cols, name):
    T, cw = c.shape
    tm = _blk(T, 512)

    def body(c_ref, g_ref, b_ref, y_ref):
        cv = c_ref[...]
        mu = jnp.mean(cv, axis=-1, keepdims=True)
        xc = cv - mu
        var = jnp.mean(xc * xc, axis=-1, keepdims=True)
        z = xc * lax.rsqrt(var + LN_EPS) * g_ref[...] + b_ref[...]
        y_ref[...] = (z * _sigmoid(z)).astype(y_ref.dtype)

    row = pl.BlockSpec((tm, cw), lambda i: (i, 0))
    vec = pl.BlockSpec((1, cw), lambda i: (0, 0))
    return _pallas(
        body, name=name, out_shape=jax.ShapeDtypeStruct((T, out_cols), _MXU), grid=(T // tm,),
        in_specs=[row, vec, vec], out_specs=row,
        compiler_params=_params(("parallel",), 2 * tm * cw * 6),
    )(c, lng, lnb)


def _ln_silu_bwd(c, lng, lnb, dycat, name):
    T, cw = c.shape
    tm = _blk(T, 512)
    nsteps = T // tm

    def body(c_ref, g_ref, b_ref, dy_ref, dc_ref, dg_ref, db_ref, accg, accb):
        i = pl.program_id(0)
        cv = c_ref[...]
        mu = jnp.mean(cv, axis=-1, keepdims=True)
        xc = cv - mu
        var = jnp.mean(xc * xc, axis=-1, keepdims=True)
        rstd = lax.rsqrt(var + LN_EPS)
        xhat = xc * rstd
        z = xhat * g_ref[...] + b_ref[...]
        sg = _sigmoid(z)
        dz = dy_ref[...] * (sg * (1.0 + z * (1.0 - sg)))
        dxh = dz * g_ref[...]
        m1 = jnp.mean(dxh, axis=-1, keepdims=True)
        m2 = jnp.mean(dxh * xhat, axis=-1, keepdims=True)
        dc_ref[...] = rstd * (dxh - m1 - xhat * m2)

        @pl.when(i == 0)
        def _():
            accg[...] = jnp.zeros_like(accg)
            accb[...] = jnp.zeros_like(accb)

        accg[...] += jnp.sum((dz * xhat).reshape(tm // SUBLANES, SUBLANES, cw), axis=0)
        accb[...] += jnp.sum(dz.reshape(tm // SUBLANES, SUBLANES, cw), axis=0)

        @pl.when(i == nsteps - 1)
        def _():
            dg_ref[...] = jnp.sum(accg[...], axis=0, keepdims=True)
            db_ref[...] = jnp.sum(accb[...], axis=0, keepdims=True)

    row = pl.BlockSpec((tm, cw), lambda i: (i, 0))
    vec = pl.BlockSpec((1, cw), lambda i: (0, 0))
    return _pallas(
        body, name=name,
        out_shape=(jax.ShapeDtypeStruct((T, cw), _F32), jax.ShapeDtypeStruct((1, cw), _F32),
                   jax.ShapeDtypeStruct((1, cw), _F32)),
        grid=(nsteps,), in_specs=[row, vec, vec, row], out_specs=(row, vec, vec),
        scratch_shapes=[pltpu.VMEM((SUBLANES, cw), _F32), pltpu.VMEM((SUBLANES, cw), _F32)],
        compiler_params=_params(("arbitrary",), 2 * tm * cw * 12),
    )(c, lng, lnb, dycat)


def _conv_bwd(proj, dc, wdw, cw, name):
    T, n_in = proj.shape
    ks = wdw.shape[0]
    cb = LANES
    tc = _blk(T, CONV_CHUNK)
    nblk = cw // cb
    half = ks // 2

    def body(a_ref, g_ref, dc_ref, w_ref, dp_ref, dwb_ref, upad, dpad, du, wacc, dg_keep):
        @pl.when(pl.program_id(1) == 0)
        def _():
            compute(a_ref, g_ref, dc_ref, w_ref, dp_ref, dg_keep, dwb_ref, upad, dpad, du, wacc)

        @pl.when(pl.program_id(1) == 1)
        def _():
            dp_ref[...] = dg_keep[...]

    def compute(a_ref, g_ref, dc_ref, w_ref, da_ref, dg_ref, dwb_ref, upad, dpad, du, wacc):
        zeros = jnp.zeros((CONV_HALO, cb), _F32)
        for pad in (upad, dpad):
            pad[0:CONV_HALO, :] = zeros
            pad[T + CONV_HALO:T + 2 * CONV_HALO, :] = zeros
        sg = _sigmoid(g_ref[...])
        upad[CONV_HALO:T + CONV_HALO, :] = a_ref[...] * sg
        dpad[CONV_HALO:T + CONV_HALO, :] = dc_ref[...]
        wacc[...] = jnp.zeros_like(wacc)

        def chunk(i, carry):
            t0 = pl.multiple_of(i * tc, tc)
            dwin = dpad[pl.ds(t0, tc + 2 * CONV_HALO), :]
            uwin = upad[pl.ds(t0, tc + 2 * CONV_HALO), :]
            dcc = dwin[CONV_HALO:CONV_HALO + tc, :]
            acc = jnp.zeros((tc, cb), _F32)
            for o, rows in _tap_windows(dwin, CONV_HALO + half + 1, tc):
                j = CONV_HALO + half - o
                if 0 <= j < ks:
                    acc = acc + rows * w_ref[j:j + 1, :]
            du[pl.ds(t0, tc), :] = acc
            for o, rows in _tap_windows(uwin, CONV_HALO + half + 1, tc):
                j = o + half - CONV_HALO
                if 0 <= j < ks:
                    wacc[j] += jnp.sum((rows * dcc).reshape(tc // SUBLANES, SUBLANES, cb), axis=0)
            wacc[ks] += jnp.sum(dcc.reshape(tc // SUBLANES, SUBLANES, cb), axis=0)
            return carry

        lax.fori_loop(0, T // tc, chunk, 0)
        duv = du[...]
        av = a_ref[...]
        da_ref[...] = (duv * sg).astype(da_ref.dtype)
        dg_ref[...] = (duv * av * sg * (1.0 - sg)).astype(dg_ref.dtype)
        dwb_ref[...] = jnp.sum(wacc[...], axis=1)

    col = lambda off: pl.BlockSpec((T, cb), lambda i, s, off=off: (0, off + i))
    return _pallas(
        body, name=name,
        out_shape=(jax.ShapeDtypeStruct((T, n_in), _MXU), jax.ShapeDtypeStruct((ks + 1, cw), _F32)),
        grid=(nblk, 2),
        in_specs=[col(0), col(nblk), col(0), pl.BlockSpec((ks, cb), lambda i, s: (0, i))],
        out_specs=(pl.BlockSpec((T, cb), lambda i, s: (0, s * nblk + i)),
                   pl.BlockSpec((ks + 1, cb), lambda i, s: (0, i))),
        scratch_shapes=[pltpu.VMEM((T + 2 * CONV_HALO, cb), _F32), pltpu.VMEM((T + 2 * CONV_HALO, cb), _F32),
                        pltpu.VMEM((T, cb), _F32), pltpu.VMEM((ks + 1, SUBLANES, cb), _F32),
                        pltpu.VMEM((T, cb), _MXU)],
        compiler_params=_params(("parallel", "arbitrary"), 2 * T * cb * 4 * 4 + 4 * T * cb * 4),
    )(proj, proj, dc, wdw)


N_CLS = WIN_ROWS
N_DR = 2 * WIN_ROWS - 1
N_DC = 2 * WIN_COLS - 1
BAND = WIN_ROWS * GRID_W
QK_SCALE = HEAD_DIM ** -0.5
ROWS_PER_STEP_FWD = 4
ROWS_PER_STEP_BWD = 2
_NT = (((1,), (1,)), ((), ()))
_TN = (((0,), (0,)), ((), ()))


def _slab_iotas():
    wk = lax.broadcasted_iota(jnp.int32, (GRID_W, LANES), 0)
    lane = lax.broadcasted_iota(jnp.int32, (GRID_W, LANES), 1)
    wq = jnp.bitwise_and(lane, GRID_W - 1)
    head1 = lane >= GRID_W
    d = wk - wq + (WIN_COLS - 1)
    cs = jnp.clip(wq - WIN_COLS // 2, 0, GRID_W - WIN_COLS)
    window = (wk >= cs) & (wk < cs + WIN_COLS)
    return d, head1, window


def _bias_table(rpb2, name):
    npair = rpb2.shape[0] // 2

    def body(rpb_ref, o_ref):
        p = pl.program_id(0)
        d, head1, window = _slab_iotas()
        for dr in range(N_DR):
            val = jnp.zeros((GRID_W, LANES), _F32)
            for j in range(N_DC):
                s0 = rpb_ref[2 * p, dr * N_DC + j]
                s1 = rpb_ref[2 * p + 1, dr * N_DC + j]
                val = jnp.where(d == j, jnp.where(head1, s1, s0), val)
            slab = jnp.where(window, val, NEG_INF)
            for cls in range(N_CLS):
                k = dr - cls
                if 0 <= k < WIN_ROWS:
                    o_ref[cls, k * GRID_W:(k + 1) * GRID_W, :] = slab

    return pl.pallas_call(
        body, name=name, out_shape=jax.ShapeDtypeStruct((npair, N_CLS, BAND, LANES), _F32), grid=(npair,),
        in_specs=[pl.BlockSpec(memory_space=pltpu.SMEM)],
        out_specs=pl.BlockSpec((None, N_CLS, BAND, LANES), lambda p: (p, 0, 0, 0)),
        compiler_params=_params(("arbitrary",), 2 * N_CLS * BAND * LANES * 4),
    )(rpb2)


def _rpb_grad(gc, name):
    npair = gc.shape[0]

    def body(g_ref, o_ref):
        d, _, _ = _slab_iotas()
        rowi = lax.broadcasted_iota(jnp.int32, (4 * SUBLANES, LANES), 0)
        lanei = lax.broadcasted_iota(jnp.int32, (4 * SUBLANES, LANES), 1)
        head1 = lax.broadcasted_iota(jnp.int32, (1, LANES), 1) >= GRID_W
        tiles = [jnp.zeros((4 * SUBLANES, LANES), _F32) for _ in range(2)]
        for dr in range(N_DR):
            ysum = jnp.zeros((GRID_W, LANES), _F32)
            for cls in range(N_CLS):
                k = dr - cls
                if 0 <= k < WIN_ROWS:
                    ysum = ysum + g_ref[cls, k * GRID_W:(k + 1) * GRID_W, :]
            for j in range(N_DC):
                cs = jnp.sum(jnp.where(d == j, ysum, 0.0), axis=0, keepdims=True)
                s0 = jnp.sum(jnp.where(head1, 0.0, cs), axis=1, keepdims=True)
                s1 = jnp.sum(jnp.where(head1, cs, 0.0), axis=1, keepdims=True)
                here = (rowi == j) & (lanei == dr)
                tiles[0] = tiles[0] + jnp.where(here, s0, 0.0)
                tiles[1] = tiles[1] + jnp.where(here, s1, 0.0)
        o_ref[0] = tiles[0]
        o_ref[1] = tiles[1]

    return _pallas(
        body, name=name, out_shape=jax.ShapeDtypeStruct((npair, 2, 4 * SUBLANES, LANES), _F32), grid=(npair,),
        in_specs=[pl.BlockSpec((None, N_CLS, BAND, LANES), lambda p: (p, 0, 0, 0))],
        out_specs=pl.BlockSpec((None, 2, 4 * SUBLANES, LANES), lambda p: (p, 0, 0, 0)),
        compiler_params=_params(("parallel",), 2 * N_CLS * BAND * LANES * 4),
    )(gc)


def _block_diag(v, diag):
    return jnp.where(diag, jnp.concatenate([v, v], axis=0), 0.0).astype(_MXU)


def _diag_mask():
    r = lax.broadcasted_iota(jnp.int32, (LANES, LANES), 0) < GRID_W
    c = lax.broadcasted_iota(jnp.int32, (LANES, LANES), 1) < HEAD_DIM
    return r == c


def _row_geometry(r, rows):
    rs = jnp.clip(r - WIN_ROWS // 2, 0, rows - WIN_ROWS)
    cls = rs - r + (WIN_ROWS - 1)
    return pl.multiple_of(r * GRID_W, GRID_W), pl.multiple_of(rs * GRID_W, GRID_W), cls


def _probs_t(qsel, kband, bias):
    s = lax.dot_general(kband, qsel, _NT, preferred_element_type=_F32) + bias
    mx = jnp.max(s, axis=0, keepdims=True)
    e = jnp.exp(s - mx)
    return e * (1.0 / jnp.sum(e, axis=0, keepdims=True))


def _attn_fwd(proj, bias, ycat, cw, naw, name):
    T = proj.shape[0]
    rows = T // GRID_W
    npair = naw // LANES
    qoff, koff, voff = 2 * cw // LANES, (2 * cw + naw) // LANES, (2 * cw + 2 * naw) // LANES

    def body(q_ref, k_ref, v_ref, b_ref, ycat_ref, o_ref, kb, vb):
        kb[...] = k_ref[...].astype(_MXU)
        vb[...] = v_ref[...].astype(_MXU)
        diag = _diag_mask()
        m0 = lax.broadcasted_iota(jnp.int32, (GRID_W, LANES), 1) < HEAD_DIM

        def step(i, carry):
            us = range(ROWS_PER_STEP_FWD)
            geo = [_row_geometry(ROWS_PER_STEP_FWD * i + u, rows) for u in us]
            qsel = [_block_diag(q_ref[pl.ds(t0, GRID_W), :] * QK_SCALE, diag) for t0, _, _ in geo]
            kbands = [kb[pl.ds(b0, BAND), :] for _, b0, _ in geo]
            vbands = [vb[pl.ds(b0, BAND), :] for _, b0, _ in geo]
            biases = [b_ref[cls] for _, _, cls in geo]
            pts = [_probs_t(qsel[u], kbands[u], biases[u]) for u in us]
            ofs = [lax.dot_general(pts[u].astype(_MXU), vbands[u], _TN, preferred_element_type=_F32) for u in us]
            for u in us:
                o_ref[pl.ds(geo[u][0], GRID_W), :] = jnp.where(m0, ofs[u][:GRID_W], ofs[u][GRID_W:]).astype(o_ref.dtype)
            return carry

        lax.fori_loop(0, rows // ROWS_PER_STEP_FWD, step, 0)

    col = lambda off: pl.BlockSpec((T, LANES), lambda i, off=off: (0, off + i))
    return _pallas(
        body, name=name, out_shape=jax.ShapeDtypeStruct(ycat.shape, ycat.dtype), grid=(npair,),
        in_specs=[col(qoff), col(koff), col(voff),
                  pl.BlockSpec((None, N_CLS, BAND, LANES), lambda i: (i, 0, 0, 0)), _ANY],
        out_specs=pl.BlockSpec((T, LANES), lambda i: (0, cw // LANES + i)),
        input_output_aliases={4: 0},
        scratch_shapes=[pltpu.VMEM((T, LANES), _MXU), pltpu.VMEM((T, LANES), _MXU)],
        compiler_params=_params(("parallel",), 2 * (3 * T * LANES * 4 + N_CLS * BAND * LANES * 4 + T * LANES * 2)),
    )(proj, proj, proj, bias, ycat)


def _attn_bwd(proj, bias, dycat, dproj, cw, naw, name):
    T = proj.shape[0]
    rows = T // GRID_W
    npair = naw // LANES
    qoff, koff, voff = 2 * cw // LANES, (2 * cw + naw) // LANES, (2 * cw + 2 * naw) // LANES
    doff = cw // LANES

    def body(q_ref, k_ref, v_ref, b_ref, do_ref, dproj_ref, out_ref, g_ref, kb, vb, dka, dva):
        s = pl.program_id(1)

        @pl.when(s == 0)
        def _():
            compute(q_ref, k_ref, v_ref, b_ref, do_ref, out_ref, g_ref, kb, vb, dka, dva)

        @pl.when(s == 1)
        def _():
            out_ref[...] = dka[...].astype(out_ref.dtype)

        @pl.when(s == 2)
        def _():
            out_ref[...] = dva[...].astype(out_ref.dtype)

    def compute(q_ref, k_ref, v_ref, b_ref, do_ref, dq_ref, g_ref, kb, vb, dka, dva):
        kb[...] = k_ref[...].astype(_MXU)
        vb[...] = v_ref[...].astype(_MXU)
        dka[...] = jnp.zeros_like(dka)
        dva[...] = jnp.zeros_like(dva)
        g_ref[...] = jnp.zeros_like(g_ref)
        diag = _diag_mask()
        m0 = lax.broadcasted_iota(jnp.int32, (GRID_W, LANES), 1) < HEAD_DIM

        def step(i, carry):
            us = range(ROWS_PER_STEP_BWD)
            geo = [_row_geometry(ROWS_PER_STEP_BWD * i + u, rows) for u in us]
            qsel = [_block_diag(q_ref[pl.ds(t0, GRID_W), :] * QK_SCALE, diag) for t0, _, _ in geo]
            dosel = [_block_diag(do_ref[pl.ds(t0, GRID_W), :], diag) for t0, _, _ in geo]
            kbands = [kb[pl.ds(b0, BAND), :] for _, b0, _ in geo]
            vbands = [vb[pl.ds(b0, BAND), :] for _, b0, _ in geo]
            biases = [b_ref[cls] for _, _, cls in geo]
            dsts, dqs, dks, dvs = [], [], [], []
            for u in us:
                pt = _probs_t(qsel[u], kbands[u], biases[u])
                dpt = lax.dot_general(vbands[u], dosel[u], _NT, preferred_element_type=_F32)
                delta = jnp.sum(pt * dpt, axis=0, keepdims=True)
                dst = pt * (dpt - delta)
                dsb = dst.astype(_MXU)
                dqf = lax.dot_general(dsb, kbands[u], _TN, preferred_element_type=_F32)
                dsts.append(dst)
                dqs.append((jnp.where(m0, dqf[:GRID_W], dqf[GRID_W:]) * QK_SCALE).astype(dq_ref.dtype))
                dks.append(jnp.dot(dsb, qsel[u], preferred_element_type=_F32))
                dvs.append(jnp.dot(pt.astype(_MXU), dosel[u], preferred_element_type=_F32))
            for u in us:
                t0, b0, cls = geo[u]
                g_ref[cls] += dsts[u]
                dq_ref[pl.ds(t0, GRID_W), :] = dqs[u]
                dka[pl.ds(b0, BAND), :] += dks[u]
                dva[pl.ds(b0, BAND), :] += dvs[u]
            return carry

        lax.fori_loop(0, rows // ROWS_PER_STEP_BWD, step, 0)

    col = lambda off: pl.BlockSpec((T, LANES), lambda i, s, off=off: (0, off + i))
    tbl = pl.BlockSpec((None, N_CLS, BAND, LANES), lambda i, s: (i, 0, 0, 0))
    vm = 2 * (4 * T * LANES * 4 + 2 * N_CLS * BAND * LANES * 4 + T * LANES * 2) + 2 * T * LANES * 6
    return _pallas(
        body, name=name,
        out_shape=(jax.ShapeDtypeStruct(dproj.shape, dproj.dtype),
                   jax.ShapeDtypeStruct((npair, N_CLS, BAND, LANES), _F32)),
        grid=(npair, 3),
        in_specs=[col(qoff), col(koff), col(voff), tbl, col(doff), _ANY],
        out_specs=(pl.BlockSpec((T, LANES), lambda i, s: (0, qoff + s * npair + i)), tbl),
        input_output_aliases={5: 0},
        scratch_shapes=[pltpu.VMEM((T, LANES), _MXU), pltpu.VMEM((T, LANES), _MXU),
                        pltpu.VMEM((T, LANES), _F32), pltpu.VMEM((T, LANES), _F32)],
        compiler_params=_params(("parallel", "arbitrary"), vm),
    )(proj, proj, proj, bias, dycat, dproj)


_ANY = pl.BlockSpec(memory_space=pl.ANY)
_HBM = pl.BlockSpec(memory_space=pltpu.HBM)
_SEM = pl.BlockSpec(memory_space=pltpu.SEMAPHORE)
_VMEM = pl.BlockSpec(memory_space=pltpu.VMEM)
_MESH_ID = pl.DeviceIdType.MESH
_EFFECT = pltpu.SideEffectType.DATAFLOW_SIDE_EFFECTING
_TOKEN = jax.ShapeDtypeStruct((SUBLANES, LANES), _F32)


def _mesh_pos():
    return tuple(lax.axis_index(a) for a in MESH_AXES)


def _in_hbm(a):
    return pltpu.with_memory_space_constraint(a, pltpu.HBM)


def _hbm_like(arrays):
    return [pltpu.HBM(a.shape, a.dtype) for a in arrays]


def _after(arr, token):
    return arr + token[0:1, 0:1].astype(arr.dtype)


def _shard_ref(ref, axis, j, width):
    idx = [slice(None)] * len(ref.shape)
    idx[axis] = pl.ds(pl.multiple_of(j * width, math.gcd(width, LANES)), width)
    return ref.at[tuple(idx)]


def _all_gather(shards, axes, name):
    n = len(shards)
    widths = [s.shape[a] for s, a in zip(shards, axes)]
    out_shape = [jax.ShapeDtypeStruct(tuple(N_DEV * d if k == a else d for k, d in enumerate(s.shape)), s.dtype)
                 for s, a in zip(shards, axes)]

    def body(*refs):
        ins, outs = refs[:n], refs[n:2 * n]
        send_sems, recv_sems, local_sems = refs[2 * n:]
        x, y, c = _mesh_pos()
        me, sibling = (x, y, c), (x, y, 1 - c)
        chips = [(1 - x, y), (x, 1 - y), (1 - x, 1 - y)]

        def slot(i, px, py, pc):
            return _shard_ref(outs[i], axes[i], 4 * px + 2 * py + pc, widths[i])

        def copy(i, k, block, to, src=None):
            return pltpu.make_async_remote_copy(
                src_ref=slot(i, *block) if src is None else src, dst_ref=slot(i, *block),
                send_sem=send_sems.at[7 * i + k], recv_sem=recv_sems.at[7 * i + k],
                device_id=to, device_id_type=_MESH_ID)

        mine = [pltpu.make_async_copy(ins[i], slot(i, *me), local_sems.at[i]) for i in range(n)]
        for cp in mine:
            cp.start()
        first = []
        for i in range(n):
            first.append(copy(i, 0, me, sibling, src=ins[i]))
            first += [copy(i, 1 + j, me, (*chip, c), src=ins[i]) for j, chip in enumerate(chips)]
        for cp in first:
            cp.start()
        passed = []
        for j, chip in enumerate(chips):
            for i in range(n):
                copy(i, 1 + j, (*chip, c), me).wait_recv()
                fwd = copy(i, 4 + j, (*chip, c), sibling)
                fwd.start()
                passed.append(fwd)
        for i in range(n):
            copy(i, 0, sibling, me).wait_recv()
            for j, chip in enumerate(chips):
                copy(i, 4 + j, (*chip, 1 - c), me).wait_recv()
        for cp in first + passed:
            cp.wait_send()
        for cp in mine:
            cp.wait()

    return _pallas(
        body, name=name, out_shape=out_shape, in_specs=[_ANY] * n, out_specs=[_ANY] * n,
        scratch_shapes=[pltpu.SemaphoreType.DMA((7 * n,)), pltpu.SemaphoreType.DMA((7 * n,)),
                        pltpu.SemaphoreType.DMA((n,))],
    )(*shards)


def _block_of(ref, axis, blk):
    return _shard_ref(ref, axis, blk, ref.shape[axis] // N_DEV)


def _gather_start(lands, axes, after, name):
    n = len(lands)

    def body(*refs):
        land = refs[:n]
        send, recv_sib, recv_ici = refs[n + 1:n + 4]
        token = refs[-1]
        x, y, c = _mesh_pos()
        me = 4 * x + 2 * y + c
        for i in range(n):
            mine = _block_of(land[i], axes[i], me)
            pltpu.make_async_remote_copy(
                src_ref=mine, dst_ref=mine, send_sem=send.at[4 * i],
                recv_sem=recv_sib.at[i], device_id=(x, y, 1 - c), device_id_type=_MESH_ID).start()
            for j, chip in enumerate([(1 - x, y), (x, 1 - y), (1 - x, 1 - y)]):
                pltpu.make_async_remote_copy(
                    src_ref=mine, dst_ref=mine, send_sem=send.at[4 * i + 1 + j],
                    recv_sem=recv_ici.at[3 * i + j], device_id=(*chip, c), device_id_type=_MESH_ID).start()
        token[...] = jnp.zeros_like(token)

    dma = pltpu.SemaphoreType.DMA
    out = pl.pallas_call(
        body, name=name,
        out_shape=(dma((4 * n,)), dma((n,)), dma((3 * n,)), *_hbm_like(lands), _TOKEN),
        in_specs=[_HBM] * n + [_ANY], out_specs=(_SEM, _SEM, _SEM, *[_HBM] * n, _VMEM),
        input_output_aliases={i: 3 + i for i in range(n)},
        compiler_params=pltpu.CompilerParams(has_side_effects=_EFFECT),
    )(*[_in_hbm(a) for a in lands], after)
    return dict(send=out[0], recv_sib=out[1], recv_ici=out[2], lands=list(out[3:3 + n]), axes=axes, token=out[-1])


def _gather_forward(st, after, name):
    lands, axes = st["lands"], st["axes"]
    n = len(lands)

    def body(*refs):
        land = refs[:n]
        recv_ici = refs[n]
        send2, recv2 = refs[n + 2], refs[n + 3]
        token = refs[-1]
        x, y, c = _mesh_pos()
        for j, (px, py) in enumerate([(1 - x, y), (x, 1 - y), (1 - x, 1 - y)]):
            for i in range(n):
                blk = _block_of(land[i], axes[i], 4 * px + 2 * py + c)
                pltpu.make_async_remote_copy(
                    src_ref=blk, dst_ref=blk, send_sem=send2.at[3 * i + j],
                    recv_sem=recv_ici.at[3 * i + j], device_id=(px, py, c), device_id_type=_MESH_ID).wait_recv()
                pltpu.make_async_remote_copy(
                    src_ref=blk, dst_ref=blk, send_sem=send2.at[3 * i + j],
                    recv_sem=recv2.at[3 * i + j], device_id=(x, y, 1 - c), device_id_type=_MESH_ID).start()
        token[...] = jnp.zeros_like(token)

    dma = pltpu.SemaphoreType.DMA
    out = pl.pallas_call(
        body, name=name,
        out_shape=(dma((3 * n,)), dma((3 * n,)), *_hbm_like(lands), _TOKEN),
        in_specs=[_HBM] * n + [_SEM, _ANY], out_specs=(_SEM, _SEM, *[_HBM] * n, _VMEM),
        input_output_aliases={i: 2 + i for i in range(n)},
        compiler_params=pltpu.CompilerParams(has_side_effects=_EFFECT),
    )(*lands, st["recv_ici"], after)
    return dict(st, send2=out[0], recv2=out[1], lands=list(out[2:2 + n]), token=out[-1])


def _gather_finish(st, after, name):
    lands, axes = st["lands"], st["axes"]
    n = len(lands)

    def body(*refs):
        land = refs[:n]
        send, recv_sib, send2, recv2 = refs[n:n + 4]
        x, y, c = _mesh_pos()
        me = 4 * x + 2 * y + c
        sib = 4 * x + 2 * y + (1 - c)

        def desc(i, blk, s_sem, r_sem):
            ref = _block_of(land[i], axes[i], blk)
            return pltpu.make_async_remote_copy(
                src_ref=ref, dst_ref=ref, send_sem=s_sem, recv_sem=r_sem,
                device_id=(x, y, 1 - c), device_id_type=_MESH_ID)

        for i in range(n):
            desc(i, sib, send.at[4 * i], recv_sib.at[i]).wait_recv()
            for j, (px, py) in enumerate([(1 - x, y), (x, 1 - y), (1 - x, 1 - y)]):
                desc(i, 4 * px + 2 * py + (1 - c), send2.at[3 * i + j], recv2.at[3 * i + j]).wait_recv()
            for k in range(4):
                desc(i, me, send.at[4 * i + k], recv_sib.at[i]).wait_send()
            for j, (px, py) in enumerate([(1 - x, y), (x, 1 - y), (1 - x, 1 - y)]):
                desc(i, 4 * px + 2 * py + c, send2.at[3 * i + j], recv2.at[3 * i + j]).wait_send()

    out = pl.pallas_call(
        body, name=name, out_shape=tuple(_hbm_like(lands)),
        in_specs=[_HBM] * n + [_SEM] * 4 + [_ANY], out_specs=tuple([_HBM] * n),
        input_output_aliases={i: i for i in range(n)},
        compiler_params=pltpu.CompilerParams(has_side_effects=_EFFECT),
    )(*lands, st["send"], st["recv_sib"], st["send2"], st["recv2"], after)
    return list(out)


def _scatter_sibling_start(grads, name):
    n = len(grads)
    gots = [lax.empty((4,) + g.shape[1:], g.dtype) for g in grads]

    def body(*refs):
        grad, got = refs[:n], refs[n:2 * n]
        send, recv = refs[2 * n], refs[2 * n + 1]
        token = refs[-1]
        x, y, c = _mesh_pos()
        for i in range(n):
            for q in range(4):
                pltpu.make_async_remote_copy(
                    src_ref=grad[i].at[2 * q + (1 - c)], dst_ref=got[i].at[q], send_sem=send.at[4 * i + q],
                    recv_sem=recv.at[4 * i + q], device_id=(x, y, 1 - c), device_id_type=_MESH_ID).start()
        token[...] = jnp.zeros_like(token)

    dma = pltpu.SemaphoreType.DMA
    out = pl.pallas_call(
        body, name=name,
        out_shape=(dma((4 * n,)), dma((4 * n,)), *_hbm_like(grads), *_hbm_like(gots), _TOKEN),
        in_specs=[_HBM] * (2 * n), out_specs=(_SEM, _SEM, *[_HBM] * (2 * n), _VMEM),
        input_output_aliases={i: 2 + i for i in range(2 * n)},
        compiler_params=pltpu.CompilerParams(has_side_effects=_EFFECT),
    )(*[_in_hbm(a) for a in grads], *[_in_hbm(a) for a in gots])
    return dict(send=out[0], recv=out[1], grads=list(out[2:2 + n]), gots=list(out[2 + n:2 + 2 * n]), token=out[-1])


def _scatter_sibling_finish(st, after, name):
    grads, gots = st["grads"], st["gots"]
    n = len(grads)

    def body(*refs):
        grad, got = refs[:n], refs[n:2 * n]
        send, recv = refs[2 * n], refs[2 * n + 1]
        x, y, c = _mesh_pos()
        for i in range(n):
            for q in range(4):
                cp = pltpu.make_async_remote_copy(
                    src_ref=grad[i].at[2 * q + (1 - c)], dst_ref=got[i].at[q], send_sem=send.at[4 * i + q],
                    recv_sem=recv.at[4 * i + q], device_id=(x, y, 1 - c), device_id_type=_MESH_ID)
                cp.wait_recv()
                cp.wait_send()

    out = pl.pallas_call(
        body, name=name, out_shape=tuple(_hbm_like(grads) + _hbm_like(gots)),
        in_specs=[_HBM] * (2 * n) + [_SEM, _SEM, _ANY], out_specs=tuple([_HBM] * (2 * n)),
        input_output_aliases={i: i for i in range(2 * n)},
        compiler_params=pltpu.CompilerParams(has_side_effects=_EFFECT),
    )(*grads, *gots, st["send"], st["recv"], after)
    return list(out[:n]), list(out[n:])


def _scatter_add(grad, got, name):
    _, R, C = grad.shape
    tr = _blk(R, 512)
    my_c = lambda: lax.axis_index("c")
    my_chip = lambda: 2 * lax.axis_index("x") + lax.axis_index("y")

    def body(a_ref, b_ref, part_ref, fin_ref):
        s = (a_ref[...].astype(_F32) + b_ref[...].astype(_F32)).astype(part_ref.dtype)
        part_ref[...] = s

        @pl.when(pl.program_id(1) == my_chip())
        def _():
            fin_ref[...] = s

    shape = jax.ShapeDtypeStruct((4, R, C), grad.dtype)
    return _pallas(
        body, name=name, out_shape=(shape, shape), grid=(R // tr, 4),
        in_specs=[pl.BlockSpec((None, tr, C), lambda i, q: (2 * q + my_c(), i, 0)),
                  pl.BlockSpec((None, tr, C), lambda i, q: (q, i, 0))],
        out_specs=[pl.BlockSpec((None, tr, C), lambda i, q: (q, i, 0)),
                   pl.BlockSpec((None, tr, C), lambda i, q: (my_chip(), i, 0))],
        compiler_params=_params(("parallel", "arbitrary"), 2 * tr * C * 8),
    )(grad, got)


def _scatter_chips_start(parts, fins, name):
    n = len(parts)

    def body(*refs):
        part, fin = refs[:n], refs[n:2 * n]
        send, recv = refs[2 * n], refs[2 * n + 1]
        token = refs[-1]
        x, y, c = _mesh_pos()
        mine = 2 * x + y
        for i in range(n):
            for k, (tx, ty) in enumerate([(1 - x, y), (x, 1 - y), (1 - x, 1 - y)]):
                pltpu.make_async_remote_copy(
                    src_ref=part[i].at[2 * tx + ty], dst_ref=fin[i].at[mine], send_sem=send.at[3 * i + k],
                    recv_sem=recv.at[3 * i + k], device_id=(tx, ty, c), device_id_type=_MESH_ID).start()
        token[...] = jnp.zeros_like(token)

    dma = pltpu.SemaphoreType.DMA
    out = pl.pallas_call(
        body, name=name,
        out_shape=(dma((3 * n,)), dma((3 * n,)), *_hbm_like(parts), *_hbm_like(fins), _TOKEN),
        in_specs=[_HBM] * (2 * n), out_specs=(_SEM, _SEM, *[_HBM] * (2 * n), _VMEM),
        input_output_aliases={i: 2 + i for i in range(2 * n)},
        compiler_params=pltpu.CompilerParams(has_side_effects=_EFFECT),
    )(*[_in_hbm(a) for a in parts], *[_in_hbm(a) for a in fins])
    return dict(send=out[0], recv=out[1], parts=list(out[2:2 + n]), fins=list(out[2 + n:2 + 2 * n]), token=out[-1])


def _scatter_chips_finish(st, after, name):
    parts, fins = st["parts"], st["fins"]
    n = len(parts)

    def body(*refs):
        part, fin = refs[:n], refs[n:2 * n]
        send, recv = refs[2 * n], refs[2 * n + 1]
        x, y, c = _mesh_pos()
        for i in range(n):
            for k, (tx, ty) in enumerate([(1 - x, y), (x, 1 - y), (1 - x, 1 - y)]):
                cp = pltpu.make_async_remote_copy(
                    src_ref=part[i].at[2 * tx + ty], dst_ref=fin[i].at[2 * tx + ty], send_sem=send.at[3 * i + k],
                    recv_sem=recv.at[3 * i + k], device_id=(tx, ty, c), device_id_type=_MESH_ID)
                cp.wait_recv()
                cp.wait_send()

    out = pl.pallas_call(
        body, name=name, out_shape=tuple(_hbm_like(parts) + _hbm_like(fins)),
        in_specs=[_HBM] * (2 * n) + [_SEM, _SEM, _ANY], out_specs=tuple([_HBM] * (2 * n)),
        input_output_aliases={i: i for i in range(2 * n)},
        compiler_params=pltpu.CompilerParams(has_side_effects=_EFFECT),
    )(*parts, *fins, st["send"], st["recv"], after)
    return list(out[n:])


def _adamw(g, w, m, v):
    m = ADAM_B1 * m + (1.0 - ADAM_B1) * g
    v = ADAM_B2 * v + (1.0 - ADAM_B2) * (g * g)
    m_hat = m / (1.0 - ADAM_B1 ** ADAM_STEP)
    v_hat = v / (1.0 - ADAM_B2 ** ADAM_STEP)
    delta = -ADAM_LR * (m_hat / (jnp.sqrt(v_hat) + ADAM_EPS) + ADAM_WD * w)
    return delta, m, v


def _adam_layer(fin, w3, m3, v3, l, prev, name):
    L, R, C = w3.shape
    tr = _blk(R, max(SUBLANES, (1 << 18) // C))

    def body(f_ref, w_ref, m_ref, v_ref, *rest):
        g_ref, d_ref, nm_ref, nv_ref = rest[-4:]
        g = ((f_ref[0].astype(_F32) + f_ref[1].astype(_F32)) + f_ref[2].astype(_F32)) + f_ref[3].astype(_F32)
        d, nm, nv = _adamw(g, w_ref[...], m_ref[...], v_ref[...])
        g_ref[...] = g
        d_ref[...] = d
        nm_ref[...] = nm
        nv_ref[...] = nv

    lay = pl.BlockSpec((None, tr, C), lambda i: (l, i, 0))
    ins = [fin, w3, m3, v3]
    in_specs = [pl.BlockSpec((4, tr, C), lambda i: (0, i, 0)), lay, lay, lay]
    aliases = {}
    if prev is not None:
        ins += list(prev)
        in_specs += [_ANY] * 4
        aliases = {4 + k: k for k in range(4)}
    return _pallas(
        body, name=name, out_shape=[jax.ShapeDtypeStruct((L, R, C), _F32)] * 4, grid=(R // tr,),
        in_specs=in_specs, out_specs=[lay] * 4, input_output_aliases=aliases,
        compiler_params=_params(("parallel",), 2 * tr * C * (4 * 2 + 7 * 4)),
    )(*ins)


def _sum_parts(parts, name):
    _, R, C = parts.shape

    def body(p_ref, o_ref):
        acc = p_ref[0]
        for k in range(1, N_DEV):
            acc = acc + p_ref[k]
        o_ref[...] = acc

    tr = _blk(R, 512)
    return _pallas(
        body, name=name, out_shape=jax.ShapeDtypeStruct((R, C), _F32), grid=(R // tr,),
        in_specs=[pl.BlockSpec((N_DEV, tr, C), lambda i: (0, i, 0))],
        out_specs=pl.BlockSpec((tr, C), lambda i: (i, 0)),
        compiler_params=_params(("parallel",), 2 * tr * C * 4 * 9),
    )(parts)


def _adam_flat(g, w, m, v, name):
    R, C = g.shape
    tr = _blk(R, 512)

    def body(g_ref, w_ref, m_ref, v_ref, d_ref, nm_ref, nv_ref):
        d, nm, nv = _adamw(g_ref[...], w_ref[...], m_ref[...], v_ref[...])
        d_ref[...] = d
        nm_ref[...] = nm
        nv_ref[...] = nv

    spec = pl.BlockSpec((tr, C), lambda i: (i, 0))
    return _pallas(
        body, name=name, out_shape=[jax.ShapeDtypeStruct((R, C), _F32)] * 3, grid=(R // tr,),
        in_specs=[spec] * 4, out_specs=[spec] * 3,
        compiler_params=_params(("parallel",), 2 * tr * C * 4 * 7),
    )(g, w, m, v)


def _pack(arrays):
    flat = jnp.concatenate([a.reshape(-1) for a in arrays])
    tile = SUBLANES * LANES
    pad = (-flat.shape[0]) % tile
    return jnp.pad(flat, (0, pad)).reshape(-1, LANES)


def _unpack(packed, shapes):
    flat = packed.reshape(-1)
    out, off = [], 0
    for s in shapes:
        n = math.prod(s)
        out.append(flat[off:off + n].reshape(s))
        off += n
    return out


def kernel(x, w_in, w_dw, b_dw, conv_ln_g, conv_ln_b, rpb, w_out, w_up, w_down, pre_mix_g, post_mix_g, pre_mlp_g, post_mlp_g, loss_target, m_w_in, m_w_dw, m_b_dw, m_conv_ln_g, m_conv_ln_b, m_rpb, m_w_out, m_w_up, m_w_down, m_pre_mix_g, m_post_mix_g, m_pre_mlp_g, m_post_mlp_g, v_w_in, v_w_dw, v_b_dw, v_conv_ln_g, v_conv_ln_b, v_rpb, v_w_out, v_w_up, v_w_down, v_pre_mix_g, v_post_mix_g, v_pre_mlp_g, v_post_mlp_g):
    _, T, D = x.shape
    L = w_in.shape[0]
    cw = b_dw.shape[1]
    H = rpb.shape[1]
    naw = H * HEAD_DIM
    ks = w_dw.shape[1]
    assert T % GRID_W == 0 and T // GRID_W >= WIN_ROWS and H % 2 == 0 and cw % LANES == 0
    assert rpb.shape[2:] == (N_DR, N_DC) and w_dw.shape[2] * N_DEV == cw and ks // 2 < CONV_HALO
    assert naw == cw and w_out.shape[1] * N_DEV == cw + naw and (T // GRID_W) % ROWS_PER_STEP_FWD == 0

    xs = x.reshape(T, D)
    tgt = loss_target.reshape(T, D)
    row = lambda p, l: p[l:l + 1]
    mx, my, mc = (lax.axis_index(a) for a in MESH_AXES)
    dev = 4 * mx + 2 * my + mc

    ks_pad = ks + (-ks) % SUBLANES
    wdw_pad = jnp.pad(w_dw, ((0, 0), (0, ks_pad - ks), (0, 0))).reshape(L * ks_pad, w_dw.shape[2])
    wdw_full = _all_gather([wdw_pad], [1], "ag_wdw")[0].reshape(L, ks_pad, cw)[:, :ks]

    big = (w_in, w_out, w_up, w_down)
    names = ("in", "out", "up", "down")

    big_axes = (1, 0, 1, 0)

    def gather_start(l, which, after):
        lands = [_cast_slot(big[k], l, big_axes[k] == 1, f"cast_{names[k]}") for k in which]
        return _gather_start(lands, [big_axes[k] for k in which], after, "gather_start_%d" % len(which))

    def gather_pair(l, after):
        g_in = gather_start(l, [0], after)
        return g_in, gather_start(l, [1, 2, 3], g_in["token"])

    saved = []
    xin = xs
    h = _norm_fwd(xs, row(pre_mix_g, 0), "norm_first")
    g_in, g_rest = gather_pair(0, wdw_full)
    g_in = _gather_forward(g_in, g_rest["token"], "gather_forward_1")
    Win = _gather_finish(g_in, g_in["token"], "gather_finish_1")[0]
    dy = loss_sum = None
    for l in range(L):
        nxt = gather_pair(l + 1, Win) if l + 1 < L else None
        proj = _matmul(h, Win, name="mm_proj")
        bdw = _after(row(b_dw, l), nxt[1]["token"]) if nxt else row(b_dw, l)
        if l > 0:
            g_rest = _gather_forward(g_rest, proj, "gather_forward_3")
            bdw = _after(bdw, g_rest["token"])
        c = _conv_fwd(proj, wdw_full[l], bdw, cw, "conv_fwd")
        yc = _ln_silu_fwd(c, row(conv_ln_g, l), row(conv_ln_b, l), cw + naw, "ln_silu_fwd")
        bias = _bias_table(rpb[l].reshape(H, N_DR * N_DC), "bias_table")
        ycat = _attn_fwd(proj, bias, yc, cw, naw, "attn_fwd")
        if l == 0:
            g_rest = _gather_forward(g_rest, ycat, "gather_forward_3")
        Wout, Wup, Wdown = _gather_finish(g_rest, ycat, "gather_finish_3")
        Ws = (Win, Wout, Wup, Wdown)
        mix = _matmul(ycat, Wout, name="mm_mix")
        x1, h2 = _resid_norm_fwd(xin, mix, row(post_mix_g, l), row(pre_mlp_g, l), "resid_mix")
        act, rl = _matmul(h2, Wup, epilogue="relu2", name="mm_up")
        if nxt:
            g_in = _gather_forward(nxt[0], act, "gather_forward_1")
        f = _matmul(act, Wdown, tm=512, tk=4096, name="mm_down")
        saved.append(dict(xin=xin, h=h, W=Ws, proj=proj, c=c, bias=bias, ycat=ycat,
                          mix=mix, x1=x1, h2=h2, act=act, rl=rl, f=f))
        if nxt:
            Win = _gather_finish(g_in, f, "gather_finish_1")[0]
            g_rest = nxt[1]
            xin, h = _resid_norm_fwd(x1, f, row(post_mlp_g, l), row(pre_mix_g, l + 1), "resid_mlp")
        else:
            dy, loss_sum = _resid_loss(x1, f, row(post_mlp_g, l), tgt, "resid_loss")

    loss = lax.psum(loss_sum[0, 0] * (0.5 / D), MESH_AXES)

    small_grads = [None] * L
    big_out = [None] * 4
    moments = ((m_w_in, v_w_in), (m_w_out, v_w_out), (m_w_up, v_w_up), (m_w_down, v_w_down))

    def scatter_begin(grads, which, l):
        tag = "_%d" % len(which)
        return dict(st=_scatter_sibling_start(grads, "scatter_sibling_start" + tag), which=which, l=l, tag=tag)

    def scatter_mid(sc, after):
        grads, gots = _scatter_sibling_finish(sc["st"], after, "scatter_sibling_finish" + sc["tag"])
        pf = [_scatter_add(g, o, f"scatter_add_{k}") for k, g, o in zip(sc["which"], grads, gots)]
        st = _scatter_chips_start([p for p, _ in pf], [q for _, q in pf], "scatter_chips_start" + sc["tag"])
        return dict(sc, st=st)

    def scatter_end(sc, after):
        fins = _scatter_chips_finish(sc["st"], after, "scatter_chips_finish" + sc["tag"])
        for k, fin in zip(sc["which"], fins):
            big_out[k] = _adam_layer(fin, big[k], moments[k][0], moments[k][1], sc["l"], big_out[k],
                                     f"adam_{k}_{sc['l']}")

    dxo = dy
    pending = None
    for l in reversed(range(L)):
        s = saved[l]
        Win, Wout, Wup, Wdown = s["W"]
        last = l == 0
        g_post_mlp = row(post_mlp_g, l)
        if pending is not None:
            g_post_mlp = _after(g_post_mlp, pending["st"]["token"])
        d_f, dg_post_mlp = _norm_bwd(s["f"], g_post_mlp, dxo, None, _MXU, "norm_bwd_mlp")
        d_up = _matmul(d_f, Wdown, tb=True, epilogue="mul2", extra=s["rl"], name="mm_d_up")
        g_pre_mlp = row(pre_mlp_g, l)
        if pending is not None:
            pending = scatter_mid(pending, d_up)
            g_pre_mlp = _after(g_pre_mlp, pending["st"]["token"])
        dWdown = _matmul(s["act"], d_f, ta=True, out_dtype=_WIRE, tk=T, name="mm_dw_down").reshape(N_DEV, -1, D)
        d_h2 = _matmul(d_up, Wup, tb=True, tm=512, tk=4096, name="mm_d_h2")
        dWup = _matmul(s["h2"], d_up, ta=True, out_dtype=_WIRE, out_cols=N_DEV, tk=T, name="mm_dw_up")
        g_post_mix, ln_g, wdw_l, g_pre_mix = row(post_mix_g, l), row(conv_ln_g, l), wdw_full[l], row(pre_mix_g, l)
        if last:
            sc_mlp = scatter_begin([dWup, dWdown], [2, 3], l)
            g_post_mix = _after(g_post_mix, sc_mlp["st"]["token"])
        dx1, dg_pre_mlp = _norm_bwd(s["x1"], g_pre_mlp, d_h2, dxo, _F32, "norm_bwd_premlp")
        d_mix, dg_post_mix = _norm_bwd(s["mix"], g_post_mix, dx1, None, _MXU, "norm_bwd_mix")
        d_ycat = _matmul(d_mix, Wout, tb=True, name="mm_d_ycat")
        if last:
            sc_mlp = scatter_mid(sc_mlp, d_ycat)
            ln_g = _after(ln_g, sc_mlp["st"]["token"])
        dWout = _matmul(s["ycat"], d_mix, ta=True, out_dtype=_WIRE, tk=T, name="mm_dw_out").reshape(N_DEV, -1, D)
        if last:
            sc_out = scatter_begin([dWout], [1], l)
            wdw_l = _after(wdw_l, sc_out["st"]["token"])
        dc, dlng, dlnb = _ln_silu_bwd(s["c"], ln_g, row(conv_ln_b, l), d_ycat, "ln_silu_bwd")
        dproj, dwb = _conv_bwd(s["proj"], dc, wdw_l, cw, "conv_bwd")
        dproj, gcls = _attn_bwd(s["proj"], s["bias"], d_ycat, dproj, cw, naw, "attn_bwd")
        if last:
            sc_out = scatter_mid(sc_out, gcls)
            g_pre_mix = _after(g_pre_mix, sc_out["st"]["token"])
        drpb = _rpb_grad(gcls, "rpb_grad").reshape(H, 4 * SUBLANES, LANES)[:, :N_DC, :N_DR].transpose(0, 2, 1)
        dh = _matmul(dproj, Win, tb=True, tk=dproj.shape[1] // 2, name="mm_d_h")
        dWin = _matmul(s["h"], dproj, ta=True, out_dtype=_WIRE, out_cols=N_DEV, tk=T, name="mm_dw_in")
        dxo, dg_pre_mix = _norm_bwd(s["xin"], g_pre_mix, dh, dx1, _F32, "norm_bwd_premix")
        if pending is not None:
            scatter_end(pending, dxo)
        if last:
            sc_in = scatter_begin([dWin], [0], l)
            sc_in = scatter_mid(sc_in, sc_in["st"]["token"])
            scatter_end(sc_mlp, sc_in["st"]["token"])
            scatter_end(sc_out, sc_in["st"]["token"])
            scatter_end(sc_in, sc_in["st"]["token"])
        else:
            pending = scatter_begin([dWin, dWout, dWup, dWdown], [0, 1, 2, 3], l)
        small_grads[l] = [dwb[ks], dlng[0], dlnb[0], drpb, dg_pre_mix[0], dg_post_mix[0], dg_pre_mlp[0],
                          dg_post_mlp[0], dwb[:ks]]

    rep_shapes = [(L, cw), (L, cw), (L, cw), (L, H, N_DR, N_DC), (L, D), (L, D), (L, D), (L, D)]
    stacked = [jnp.stack([small_grads[l][k] for l in range(L)]) for k in range(9)]
    packed = _pack(stacked)
    parts = _all_gather([packed], [0], "ag_small")[0].reshape(N_DEV, *packed.shape)
    gsum = _sum_parts(parts, "sum_small")
    g_small = _unpack(gsum, rep_shapes + [(L, ks, cw)])
    g_rep, g_wdw_full = g_small[:8], g_small[8]
    wsh = w_dw.shape[2]
    g_wdw = lax.dynamic_slice_in_dim(g_wdw_full, dev * wsh, wsh, axis=2)

    rep_w = [b_dw, conv_ln_g, conv_ln_b, rpb, pre_mix_g, post_mix_g, pre_mlp_g, post_mlp_g]
    rep_m = [m_b_dw, m_conv_ln_g, m_conv_ln_b, m_rpb, m_pre_mix_g, m_post_mix_g, m_pre_mlp_g, m_post_mlp_g]
    rep_v = [v_b_dw, v_conv_ln_g, v_conv_ln_b, v_rpb, v_pre_mix_g, v_post_mix_g, v_pre_mlp_g, v_post_mlp_g]
    rep_out = _adam_flat(_pack(g_rep), _pack(rep_w), _pack(rep_m), _pack(rep_v), "adam_small")
    rep_delta, rep_nm, rep_nv = (_unpack(o, rep_shapes) for o in rep_out)
    dw_out = _adam_flat(_pack([g_wdw]), _pack([w_dw]), _pack([m_w_dw]), _pack([v_w_dw]), "adam_wdw")
    wdw_delta, wdw_nm, wdw_nv = (_unpack(o, [w_dw.shape])[0] for o in dw_out)

    def assemble(kind_big, rep_list, wdw_val):
        return [big_out[0][kind_big], wdw_val, rep_list[0], rep_list[1], rep_list[2], rep_list[3],
                big_out[1][kind_big], big_out[2][kind_big], big_out[3][kind_big],
                rep_list[4], rep_list[5], rep_list[6], rep_list[7]]

    grads_out = assemble(0, g_rep, g_wdw)
    deltas = assemble(1, rep_delta, wdw_delta)
    new_m = assemble(2, rep_nm, wdw_nm)
    new_v = assemble(3, rep_nv, wdw_nv)
    return (loss, dxo.reshape(1, T, D), *grads_out, *deltas, *new_m, *new_v)
```

```python
import math

import jax
import jax.numpy as jnp
from jax import lax
from jax.experimental import pallas as pl
from jax.experimental.pallas import tpu as pltpu

_MXU = jnp.bfloat16
_WIRE = jnp.bfloat16
_F32 = jnp.float32

N_DEV = 8
GRID_W = 64
WIN_ROWS = 8
WIN_COLS = 16
HEAD_DIM = 64
LANES = 128
SUBLANES = 8
RMS_EPS = 1e-6
LN_EPS = 1e-5
NEG_INF = -1e30
ADAM_LR = 0.001
ADAM_B1 = 0.9
ADAM_B2 = 0.999
ADAM_EPS = 1e-08
ADAM_WD = 0.01
ADAM_STEP = 10
VMEM_BYTES_V7X = 64 << 20
VMEM_RESERVE = 12 << 20
MESH_AXES = ("x", "y", "c")


def _vmem_limit(block_bytes):
    return int(min(max(block_bytes + (8 << 20), 24 << 20), VMEM_BYTES_V7X - VMEM_RESERVE))


def _blk(n, pref):
    if n <= pref:
        return n
    for t in range(pref, 7, -1):
        if n % t == 0 and t % SUBLANES == 0:
            return t
    return n


def _sigmoid(v):
    return 1.0 / (1.0 + jnp.exp(-v))


def _params(sem, nbytes):
    return pltpu.CompilerParams(dimension_semantics=sem, vmem_limit_bytes=_vmem_limit(nbytes))


def _pallas(body, **kw):
    call = pl.pallas_call(body, **kw)

    def run(*operands):
        return call(*[pltpu.with_memory_space_constraint(o, pltpu.HBM)
                      if jnp.issubdtype(o.dtype, jnp.floating) else o for o in operands])

    return run


def _my_block():
    x, y, c = (lax.axis_index(a) for a in MESH_AXES)
    return 4 * x + 2 * y + c


def _cast_slot(w3, l, by_cols, name):
    _, R, C = w3.shape
    tr = _blk(R, 512)

    def body(w_ref, o_ref):
        o_ref[...] = w_ref[...].astype(o_ref.dtype)

    if by_cols:
        shape, o_spec = (R, N_DEV * C), pl.BlockSpec((tr, C), lambda i: (i, _my_block()))
    else:
        shape, o_spec = (N_DEV * R, C), pl.BlockSpec((tr, C), lambda i: (_my_block() * (R // tr) + i, 0))
    return _pallas(
        body, name=name, out_shape=jax.ShapeDtypeStruct(shape, _WIRE), grid=(R // tr,),
        in_specs=[pl.BlockSpec((None, tr, C), lambda i: (l, i, 0))], out_specs=o_spec,
        compiler_params=_params(("parallel",), 2 * tr * C * 6),
    )(w3)


def _norm_fwd(x, g, name):
    T, D = x.shape
    tm = _blk(T, 256)

    def body(x_ref, g_ref, h_ref):
        xv = x_ref[...]
        r = lax.rsqrt(jnp.mean(xv * xv, axis=-1, keepdims=True) + RMS_EPS)
        h_ref[...] = (xv * r * g_ref[...]).astype(h_ref.dtype)

    return _pallas(
        body, name=name, out_shape=jax.ShapeDtypeStruct((T, D), _MXU), grid=(T // tm,),
        in_specs=[pl.BlockSpec((tm, D), lambda i: (i, 0)), pl.BlockSpec((1, D), lambda i: (0, 0))],
        out_specs=pl.BlockSpec((tm, D), lambda i: (i, 0)),
        compiler_params=_params(("parallel",), 2 * tm * D * 6),
    )(x, g)


def _resid_norm_fwd(xres, y, g_post, g_next, name):
    T, D = xres.shape
    tm = _blk(T, 256)

    def body(x_ref, y_ref, gp_ref, gn_ref, xn_ref, h_ref):
        yv = y_ref[...]
        r = lax.rsqrt(jnp.mean(yv * yv, axis=-1, keepdims=True) + RMS_EPS)
        xn = x_ref[...] + yv * r * gp_ref[...]
        xn_ref[...] = xn
        r2 = lax.rsqrt(jnp.mean(xn * xn, axis=-1, keepdims=True) + RMS_EPS)
        h_ref[...] = (xn * r2 * gn_ref[...]).astype(h_ref.dtype)

    row = pl.BlockSpec((tm, D), lambda i: (i, 0))
    vec = pl.BlockSpec((1, D), lambda i: (0, 0))
    return _pallas(
        body, name=name,
        out_shape=(jax.ShapeDtypeStruct((T, D), _F32), jax.ShapeDtypeStruct((T, D), _MXU)),
        grid=(T // tm,), in_specs=[row, row, vec, vec], out_specs=(row, row),
        compiler_params=_params(("parallel",), 2 * tm * D * 14),
    )(xres, y, g_post, g_next)


def _resid_loss(xres, y, g_post, target, name):
    T, D = xres.shape
    tm = _blk(T, 256)

    def body(x_ref, y_ref, gp_ref, t_ref, dy_ref, loss_ref):
        yv = y_ref[...]
        r = lax.rsqrt(jnp.mean(yv * yv, axis=-1, keepdims=True) + RMS_EPS)
        err = x_ref[...] + yv * r * gp_ref[...] - t_ref[...]
        dy_ref[...] = err * (1.0 / D)

        @pl.when(pl.program_id(0) == 0)
        def _():
            loss_ref[...] = jnp.zeros_like(loss_ref)

        part = jnp.sum(jnp.sum(err * err, axis=-1, keepdims=True), axis=0, keepdims=True)
        loss_ref[...] += part

    row = pl.BlockSpec((tm, D), lambda i: (i, 0))
    vec = pl.BlockSpec((1, D), lambda i: (0, 0))
    return _pallas(
        body, name=name,
        out_shape=(jax.ShapeDtypeStruct((T, D), _F32), jax.ShapeDtypeStruct((1, 1), _F32)),
        grid=(T // tm,), in_specs=[row, row, vec, row],
        out_specs=(row, pl.BlockSpec((1, 1), lambda i: (0, 0))),
        compiler_params=_params(("arbitrary",), 2 * tm * D * 16),
    )(xres, y, g_post, target)


def _norm_bwd(y, g, dout, dres, out_dtype, name):
    T, D = y.shape
    tm = _blk(T, 256)
    nsteps = T // tm
    has_res = dres is not None

    def body(*refs):
        if has_res:
            y_ref, g_ref, do_ref, dr_ref, dy_ref, dg_ref, acc = refs
        else:
            y_ref, g_ref, do_ref, dy_ref, dg_ref, acc = refs
        i = pl.program_id(0)
        yv = y_ref[...]
        do = do_ref[...]
        r = lax.rsqrt(jnp.mean(yv * yv, axis=-1, keepdims=True) + RMS_EPS)
        gy = do * g_ref[...]
        dot = jnp.mean(yv * gy, axis=-1, keepdims=True)
        dy = r * gy - yv * (r * r * r * dot)
        if has_res:
            dy = dy + dr_ref[...]
        dy_ref[...] = dy.astype(dy_ref.dtype)

        @pl.when(i == 0)
        def _():
            acc[...] = jnp.zeros_like(acc)

        acc[...] += jnp.sum((do * yv * r).reshape(tm // SUBLANES, SUBLANES, D), axis=0)

        @pl.when(i == nsteps - 1)
        def _():
            dg_ref[...] = jnp.sum(acc[...], axis=0, keepdims=True)

    row = pl.BlockSpec((tm, D), lambda i: (i, 0))
    vec = pl.BlockSpec((1, D), lambda i: (0, 0))
    ins = [y, g, dout] + ([dres] if has_res else [])
    in_specs = [row, vec, row] + ([row] if has_res else [])
    return _pallas(
        body, name=name,
        out_shape=(jax.ShapeDtypeStruct((T, D), out_dtype), jax.ShapeDtypeStruct((1, D), _F32)),
        grid=(nsteps,), in_specs=in_specs, out_specs=(row, vec),
        scratch_shapes=[pltpu.VMEM((SUBLANES, D), _F32)],
        compiler_params=_params(("arbitrary",), 2 * tm * D * 16),
    )(*ins)


def _matmul(a, b, *, ta=False, tb=False, out_dtype=_F32, epilogue=None, extra=None, out_cols=0,
            tm=1024, tk=2048, n_outer=False, name):
    M, K = (a.shape[1], a.shape[0]) if ta else a.shape
    N = b.shape[0] if tb else b.shape[1]
    tm = _blk(M, tm)
    tn = N // out_cols if out_cols else _blk(N, 1024)
    tk = _blk(K, tk)
    ij = (lambda g0, g1: (g1, g0)) if n_outer else (lambda g0, g1: (g0, g1))
    b_spec = (pl.BlockSpec((tn, tk), lambda g0, g1, k: (ij(g0, g1)[1], k)) if tb
              else pl.BlockSpec((tk, tn), lambda g0, g1, k: (k, ij(g0, g1)[1])))
    nk = K // tk
    a_spec = (pl.BlockSpec((tk, tm), lambda g0, g1, k: (k, ij(g0, g1)[0])) if ta
              else pl.BlockSpec((tm, tk), lambda g0, g1, k: (ij(g0, g1)[0], k)))
    if out_cols:
        assert N // tn == out_cols and epilogue is None
        o_spec = pl.BlockSpec((None, tm, tn), lambda g0, g1, k: (ij(g0, g1)[1], ij(g0, g1)[0], 0))
        o_shape = (out_cols, M, tn)
    else:
        o_spec = pl.BlockSpec((tm, tn), lambda g0, g1, k: ij(g0, g1))
        o_shape = (M, N)
    dims = (((0 if ta else 1,), (1 if tb else 0,)), ((), ()))
    n_extra = 1 if epilogue == "mul2" else 0
    n_out = 2 if epilogue == "relu2" else 1

    def finish(acc, extra_refs, out_refs):
        if epilogue is None:
            out_refs[0][...] = acc.astype(out_refs[0].dtype)
        elif epilogue == "relu2":
            rl = jnp.maximum(acc, 0.0)
            out_refs[0][...] = (rl * rl).astype(out_refs[0].dtype)
            out_refs[1][...] = rl.astype(out_refs[1].dtype)
        else:
            out_refs[0][...] = (acc * (2.0 * extra_refs[0][...].astype(_F32))).astype(out_refs[0].dtype)

    def body(a_ref, b_ref, *rest):
        extra_refs = rest[:n_extra]
        out_refs = rest[n_extra:n_extra + n_out]
        part = lax.dot_general(a_ref[...], b_ref[...], dims, preferred_element_type=_F32)
        if nk == 1:
            finish(part, extra_refs, out_refs)
            return
        acc = rest[-1]
        k = pl.program_id(2)

        @pl.when(k == 0)
        def _():
            acc[...] = part

        @pl.when(k > 0)
        def _():
            acc[...] += part

        @pl.when(k == nk - 1)
        def _():
            finish(acc[...], extra_refs, out_refs)

    if epilogue == "relu2":
        out_shape = (jax.ShapeDtypeStruct((M, N), _MXU), jax.ShapeDtypeStruct((M, N), _MXU))
        out_specs = (o_spec, o_spec)
        out_bytes = 2 * tm * tn * 2
    else:
        odt = _MXU if epilogue == "mul2" else out_dtype
        out_shape = jax.ShapeDtypeStruct(o_shape, odt)
        out_specs = o_spec
        out_bytes = tm * tn * jnp.dtype(odt).itemsize
    in_specs = [a_spec, b_spec] + ([o_spec] if n_extra else [])
    ins = [a, b] + ([extra] if n_extra else [])
    blocks = 2 * (tm * tk * 2 + tk * tn * 2 + out_bytes + n_extra * tm * tn * 2) + tm * tn * 4 * 2
    return _pallas(
        body, name=name, out_shape=out_shape,
        grid=(N // tn, M // tm, nk) if n_outer else (M // tm, N // tn, nk),
        in_specs=in_specs, out_specs=out_specs,
        scratch_shapes=[pltpu.VMEM((tm, tn), _F32)] if nk > 1 else [],
        compiler_params=_params(("parallel", "parallel", "arbitrary"), blocks),
    )(*ins)


CONV_HALO = 16
CONV_CHUNK = 256


def _tap_windows(win, n_taps_plus1, tc):
    n = win.shape[0]
    for s in range(SUBLANES):
        shifted = win if s == 0 else pltpu.roll(win, n - s, 0)
        for q in range((n_taps_plus1 + SUBLANES - 1) // SUBLANES):
            o = SUBLANES * q + s
            if 1 <= o < n_taps_plus1:
                yield o, shifted[SUBLANES * q:SUBLANES * q + tc, :]


def _conv_fwd(proj, wdw, bdw, cw, name):
    T = proj.shape[0]
    ks = wdw.shape[0]
    cb = LANES
    tc = _blk(T, CONV_CHUNK)
    nblk = cw // cb

    def body(a_ref, g_ref, w_ref, b_ref, c_ref, upad):
        zeros = jnp.zeros((CONV_HALO, cb), _F32)
        upad[0:CONV_HALO, :] = zeros
        upad[T + CONV_HALO:T + 2 * CONV_HALO, :] = zeros
        upad[CONV_HALO:T + CONV_HALO, :] = a_ref[...] * _sigmoid(g_ref[...])

        def chunk(i, carry):
            t0 = pl.multiple_of(i * tc, tc)
            win = upad[pl.ds(t0, tc + 2 * CONV_HALO), :]
            acc = jnp.broadcast_to(b_ref[...], (tc, cb))
            for o, rows in _tap_windows(win, ks + 1, tc):
                j = o + ks // 2 - CONV_HALO
                acc = acc + rows * w_ref[j:j + 1, :]
            c_ref[pl.ds(t0, tc), :] = acc
            return carry

        lax.fori_loop(0, T // tc, chunk, 0)

    col = lambda off: pl.BlockSpec((T, cb), lambda i, off=off: (0, off + i))
    return _pallas(
        body, name=name, out_shape=jax.ShapeDtypeStruct((T, cw), _F32), grid=(nblk,),
        in_specs=[col(0), col(nblk), pl.BlockSpec((ks, cb), lambda i: (0, i)),
                  pl.BlockSpec((1, cb), lambda i: (0, i))],
        out_specs=pl.BlockSpec((T, cb), lambda i: (0, i)),
        scratch_shapes=[pltpu.VMEM((T + 2 * CONV_HALO, cb), _F32)],
        compiler_params=_params(("parallel",), 2 * T * cb * 4 * 3 + T * cb * 4),
    )(proj, proj, wdw, bdw)


def _ln_silu_fwd(c, lng, lnb, out_cols, name):
    T, cw = c.shape
    tm = _blk(T, 512)

    def body(c_ref, g_ref, b_ref, y_ref):
        cv = c_ref[...]
        mu = jnp.mean(cv, axis=-1, keepdims=True)
        xc = cv - mu
        var = jnp.mean(xc * xc, axis=-1, keepdims=True)
        z = xc * lax.rsqrt(var + LN_EPS) * g_ref[...] + b_ref[...]
        y_ref[...] = (z * _sigmoid(z)).astype(y_ref.dtype)

    row = pl.BlockSpec((tm, cw), lambda i: (i, 0))
    vec = pl.BlockSpec((1, cw), lambda i: (0, 0))
    return _pallas(
        body, name=name, out_shape=jax.ShapeDtypeStruct((T, out_cols), _MXU), grid=(T // tm,),
        in_specs=[row, vec, vec], out_specs=row,
        compiler_params=_params(("parallel",), 2 * tm * cw * 6),
    )(c, lng, lnb)


def _ln_silu_bwd(c, lng, lnb, dycat, name):
    T, cw = c.shape
    tm = _blk(T, 512)
    nsteps = T // tm

    def body(c_ref, g_ref, b_ref, dy_ref, dc_ref, dg_ref, db_ref, accg, accb):
        i = pl.program_id(0)
        cv = c_ref[...]
        mu = jnp.mean(cv, axis=-1, keepdims=True)
        xc = cv - mu
        var = jnp.mean(xc * xc, axis=-1, keepdims=True)
        rstd = lax.rsqrt(var + LN_EPS)
        xhat = xc * rstd
        z = xhat * g_ref[...] + b_ref[...]
        sg = _sigmoid(z)
        dz = dy_ref[...] * (sg * (1.0 + z * (1.0 - sg)))
        dxh = dz * g_ref[...]
        m1 = jnp.mean(dxh, axis=-1, keepdims=True)
        m2 = jnp.mean(dxh * xhat, axis=-1, keepdims=True)
        dc_ref[...] = rstd * (dxh - m1 - xhat * m2)

        @pl.when(i == 0)
        def _():
            accg[...] = jnp.zeros_like(accg)
            accb[...] = jnp.zeros_like(accb)

        accg[...] += jnp.sum((dz * xhat).reshape(tm // SUBLANES, SUBLANES, cw), axis=0)
        accb[...] += jnp.sum(dz.reshape(tm // SUBLANES, SUBLANES, cw), axis=0)

        @pl.when(i == nsteps - 1)
        def _():
            dg_ref[...] = jnp.sum(accg[...], axis=0, keepdims=True)
            db_ref[...] = jnp.sum(accb[...], axis=0, keepdims=True)

    row = pl.BlockSpec((tm, cw), lambda i: (i, 0))
    vec = pl.BlockSpec((1, cw), lambda i: (0, 0))
    return _pallas(
        body, name=name,
        out_shape=(jax.ShapeDtypeStruct((T, cw), _F32), jax.ShapeDtypeStruct((1, cw), _F32),
                   jax.ShapeDtypeStruct((1, cw), _F32)),
        grid=(nsteps,), in_specs=[row, vec, vec, row], out_specs=(row, vec, vec),
        scratch_shapes=[pltpu.VMEM((SUBLANES, cw), _F32), pltpu.VMEM((SUBLANES, cw), _F32)],
        compiler_params=_params(("arbitrary",), 2 * tm * cw * 12),
    )(c, lng, lnb, dycat)


def _conv_bwd(proj, dc, wdw, cw, name):
    T = proj.shape[0]
    ks = wdw.shape[0]
    cb = LANES
    tc = _blk(T, CONV_CHUNK)
    nblk = cw // cb
    half = ks // 2

    def body(a_ref, g_ref, dc_ref, w_ref, da_ref, dg_ref, dwb_ref, upad, dpad, du, wacc):
        zeros = jnp.zeros((CONV_HALO, cb), _F32)
        for pad in (upad, dpad):
            pad[0:CONV_HALO, :] = zeros
            pad[T + CONV_HALO:T + 2 * CONV_HALO, :] = zeros
        sg = _sigmoid(g_ref[...])
        upad[CONV_HALO:T + CONV_HALO, :] = a_ref[...] * sg
        dpad[CONV_HALO:T + CONV_HALO, :] = dc_ref[...]
        wacc[...] = jnp.zeros_like(wacc)

        def chunk(i, carry):
            t0 = pl.multiple_of(i * tc, tc)
            dwin = dpad[pl.ds(t0, tc + 2 * CONV_HALO), :]
            uwin = upad[pl.ds(t0, tc + 2 * CONV_HALO), :]
            dcc = dwin[CONV_HALO:CONV_HALO + tc, :]
            acc = jnp.zeros((tc, cb), _F32)
            for o, rows in _tap_windows(dwin, CONV_HALO + half + 1, tc):
                j = CONV_HALO + half - o
                if 0 <= j < ks:
                    acc = acc + rows * w_ref[j:j + 1, :]
            du[pl.ds(t0, tc), :] = acc
            for o, rows in _tap_windows(uwin, CONV_HALO + half + 1, tc):
                j = o + half - CONV_HALO
                if 0 <= j < ks:
                    wacc[j] += jnp.sum((rows * dcc).reshape(tc // SUBLANES, SUBLANES, cb), axis=0)
            wacc[ks] += jnp.sum(dcc.reshape(tc // SUBLANES, SUBLANES, cb), axis=0)
            return carry

        lax.fori_loop(0, T // tc, chunk, 0)
        duv = du[...]
        av = a_ref[...]
        da_ref[...] = (duv * sg).astype(da_ref.dtype)
        dg_ref[...] = (duv * av * sg * (1.0 - sg)).astype(dg_ref.dtype)
        dwb_ref[...] = jnp.sum(wacc[...], axis=1)

    col = lambda off: pl.BlockSpec((T, cb), lambda i, off=off: (0, off + i))
    blk = pl.BlockSpec((T, cb), lambda i: (0, i))
    return _pallas(
        body, name=name,
        out_shape=(jax.ShapeDtypeStruct((T, cw), _MXU), jax.ShapeDtypeStruct((T, cw), _MXU),
                   jax.ShapeDtypeStruct((ks + 1, cw), _F32)),
        grid=(nblk,),
        in_specs=[col(0), col(nblk), blk, pl.BlockSpec((ks, cb), lambda i: (0, i))],
        out_specs=(blk, blk, pl.BlockSpec((ks + 1, cb), lambda i: (0, i))),
        scratch_shapes=[pltpu.VMEM((T + 2 * CONV_HALO, cb), _F32), pltpu.VMEM((T + 2 * CONV_HALO, cb), _F32),
                        pltpu.VMEM((T, cb), _F32), pltpu.VMEM((ks + 1, SUBLANES, cb), _F32)],
        compiler_params=_params(("parallel",), 2 * T * cb * 4 * 4 + 3 * T * cb * 4),
    )(proj, proj, dc, wdw)


N_CLS = WIN_ROWS
N_DR = 2 * WIN_ROWS - 1
N_DC = 2 * WIN_COLS - 1
BAND = WIN_ROWS * GRID_W
QK_SCALE = HEAD_DIM ** -0.5
ROWS_PER_STEP_FWD = 4
ROWS_PER_STEP_BWD = 4
_NT = (((1,), (1,)), ((), ()))
_TN = (((0,), (0,)), ((), ()))


def _slab_iotas():
    wk = lax.broadcasted_iota(jnp.int32, (GRID_W, LANES), 0)
    lane = lax.broadcasted_iota(jnp.int32, (GRID_W, LANES), 1)
    wq = jnp.bitwise_and(lane, GRID_W - 1)
    head1 = lane >= GRID_W
    d = wk - wq + (WIN_COLS - 1)
    cs = jnp.clip(wq - WIN_COLS // 2, 0, GRID_W - WIN_COLS)
    window = (wk >= cs) & (wk < cs + WIN_COLS)
    return d, head1, window


def _pair_tiles(cls):
    return [(cls + 2 * p, p) for p in range(WIN_ROWS // 2)]


def _bias_table(rpb2, name):
    npair = rpb2.shape[0] // 2

    def body(rpb_ref, ot_ref, on_ref):
        p = pl.program_id(0)
        d, head1, window = _slab_iotas()
        slabs = []
        for dr in range(N_DR):
            val = jnp.zeros((GRID_W, LANES), _F32)
            for j in range(N_DC):
                s0 = rpb_ref[2 * p, dr * N_DC + j]
                s1 = rpb_ref[2 * p + 1, dr * N_DC + j]
                val = jnp.where(d == j, jnp.where(head1, s1, s0), val)
            slab = jnp.where(window, val, NEG_INF)
            slabs.append(slab)
            for cls in range(N_CLS):
                k = dr - cls
                if 0 <= k < WIN_ROWS:
                    ot_ref[cls, k * GRID_W:(k + 1) * GRID_W, :] = slab
        for e in range(N_DR - 1):
            tile = jnp.concatenate([slabs[e], slabs[e + 1]], axis=0).T
            for cls in range(N_CLS):
                for ee, pp in _pair_tiles(cls):
                    if ee == e:
                        on_ref[cls, :, pp * LANES:(pp + 1) * LANES] = tile

    return pl.pallas_call(
        body, name=name,
        out_shape=(jax.ShapeDtypeStruct((npair, N_CLS, BAND, LANES), _F32),
                   jax.ShapeDtypeStruct((npair, N_CLS, LANES, BAND), _F32)),
        grid=(npair,),
        in_specs=[pl.BlockSpec(memory_space=pltpu.SMEM)],
        out_specs=(pl.BlockSpec((None, N_CLS, BAND, LANES), lambda p: (p, 0, 0, 0)),
                   pl.BlockSpec((None, N_CLS, LANES, BAND), lambda p: (p, 0, 0, 0))),
        compiler_params=_params(("arbitrary",), 4 * N_CLS * BAND * LANES * 4),
    )(rpb2)


def _rpb_grad(gc, name):
    npair = gc.shape[0]

    def body(g_ref, o_ref):
        d, _, _ = _slab_iotas()
        rowi = lax.broadcasted_iota(jnp.int32, (4 * SUBLANES, LANES), 0)
        lanei = lax.broadcasted_iota(jnp.int32, (4 * SUBLANES, LANES), 1)
        head1 = lax.broadcasted_iota(jnp.int32, (1, LANES), 1) >= GRID_W
        tiles = [jnp.zeros((4 * SUBLANES, LANES), _F32) for _ in range(2)]
        yts = []
        for e in range(N_DR - 1):
            y = jnp.zeros((LANES, LANES), _F32)
            for cls in range(N_CLS):
                for ee, pp in _pair_tiles(cls):
                    if ee == e:
                        y = y + g_ref[cls, :, pp * LANES:(pp + 1) * LANES]
            yts.append(y.T)
        for dr in range(N_DR):
            ysum = jnp.zeros((GRID_W, LANES), _F32)
            if dr < N_DR - 1:
                ysum = ysum + yts[dr][:GRID_W]
            if dr >= 1:
                ysum = ysum + yts[dr - 1][GRID_W:]
            for j in range(N_DC):
                cs = jnp.sum(jnp.where(d == j, ysum, 0.0), axis=0, keepdims=True)
                s0 = jnp.sum(jnp.where(head1, 0.0, cs), axis=1, keepdims=True)
                s1 = jnp.sum(jnp.where(head1, cs, 0.0), axis=1, keepdims=True)
                here = (rowi == j) & (lanei == dr)
                tiles[0] = tiles[0] + jnp.where(here, s0, 0.0)
                tiles[1] = tiles[1] + jnp.where(here, s1, 0.0)
        o_ref[0] = tiles[0]
        o_ref[1] = tiles[1]

    return _pallas(
        body, name=name, out_shape=jax.ShapeDtypeStruct((npair, 2, 4 * SUBLANES, LANES), _F32), grid=(npair,),
        in_specs=[pl.BlockSpec((None, N_CLS, LANES, BAND), lambda p: (p, 0, 0, 0))],
        out_specs=pl.BlockSpec((None, 2, 4 * SUBLANES, LANES), lambda p: (p, 0, 0, 0)),
        compiler_params=_params(("parallel",), 2 * N_CLS * BAND * LANES * 4),
    )(gc)


def _block_diag(v, diag):
    return jnp.where(diag, jnp.concatenate([v, v], axis=0), 0.0).astype(_MXU)


def _diag_mask():
    r = lax.broadcasted_iota(jnp.int32, (LANES, LANES), 0) < GRID_W
    c = lax.broadcasted_iota(jnp.int32, (LANES, LANES), 1) < HEAD_DIM
    return r == c


def _row_geometry(r, rows):
    rs = jnp.clip(r - WIN_ROWS // 2, 0, rows - WIN_ROWS)
    cls = rs - r + (WIN_ROWS - 1)
    return pl.multiple_of(r * GRID_W, GRID_W), pl.multiple_of(rs * GRID_W, GRID_W), cls


def _probs_t(qsel, kband, bias):
    s = lax.dot_general(kband, qsel, _NT, preferred_element_type=_F32) + bias
    mx = jnp.max(s, axis=0, keepdims=True)
    e = jnp.exp(s - mx)
    return e * (1.0 / jnp.sum(e, axis=0, keepdims=True))


def _attn_fwd(proj, bias, ycat, cw, naw, name):
    T = proj.shape[0]
    rows = T // GRID_W
    npair = naw // LANES
    qoff, koff, voff = 2 * cw // LANES, (2 * cw + naw) // LANES, (2 * cw + 2 * naw) // LANES

    def body(q_ref, k_ref, v_ref, b_ref, ycat_ref, o_ref, kb, vb):
        kb[...] = k_ref[...].astype(_MXU)
        vb[...] = v_ref[...].astype(_MXU)
        diag = _diag_mask()
        m0 = lax.broadcasted_iota(jnp.int32, (GRID_W, LANES), 1) < HEAD_DIM

        def step(i, carry):
            us = range(ROWS_PER_STEP_FWD)
            geo = [_row_geometry(ROWS_PER_STEP_FWD * i + u, rows) for u in us]
            qsel = [_block_diag(q_ref[pl.ds(t0, GRID_W), :] * QK_SCALE, diag) for t0, _, _ in geo]
            kbands = [kb[pl.ds(b0, BAND), :] for _, b0, _ in geo]
            vbands = [vb[pl.ds(b0, BAND), :] for _, b0, _ in geo]
            biases = [b_ref[cls] for _, _, cls in geo]
            pts = [_probs_t(qsel[u], kbands[u], biases[u]) for u in us]
            ofs = [lax.dot_general(pts[u].astype(_MXU), vbands[u], _TN, preferred_element_type=_F32) for u in us]
            for u in us:
                o_ref[pl.ds(geo[u][0], GRID_W), :] = jnp.where(m0, ofs[u][:GRID_W], ofs[u][GRID_W:]).astype(o_ref.dtype)
            return carry

        lax.fori_loop(0, rows // ROWS_PER_STEP_FWD, step, 0)

    col = lambda off: pl.BlockSpec((T, LANES), lambda i, off=off: (0, off + i))
    return _pallas(
        body, name=name, out_shape=jax.ShapeDtypeStruct(ycat.shape, ycat.dtype), grid=(npair,),
        in_specs=[col(qoff), col(koff), col(voff),
                  pl.BlockSpec((None, N_CLS, BAND, LANES), lambda i: (i, 0, 0, 0)), _ANY],
        out_specs=pl.BlockSpec((T, LANES), lambda i: (0, cw // LANES + i)),
        input_output_aliases={4: 0},
        scratch_shapes=[pltpu.VMEM((T, LANES), _MXU), pltpu.VMEM((T, LANES), _MXU)],
        compiler_params=_params(("parallel",), 2 * (3 * T * LANES * 4 + N_CLS * BAND * LANES * 4 + T * LANES * 2)),
    )(proj, proj, proj, bias, ycat)


def _attn_bwd(proj, bias, dycat, cw, naw, name):
    T = proj.shape[0]
    rows = T // GRID_W
    npair = naw // LANES
    qoff, koff, voff = 2 * cw // LANES, (2 * cw + naw) // LANES, (2 * cw + 2 * naw) // LANES
    doff = cw // LANES

    def body(q_ref, k_ref, v_ref, b_ref, do_ref, dq_ref, dk_ref, dv_ref, g_ref, kb, vb, dka, dva):
        kb[...] = k_ref[...].astype(_MXU)
        vb[...] = v_ref[...].astype(_MXU)
        dka[...] = jnp.zeros_like(dka)
        dva[...] = jnp.zeros_like(dva)
        g_ref[...] = jnp.zeros_like(g_ref)
        diag = _diag_mask()
        m0 = lax.broadcasted_iota(jnp.int32, (GRID_W, LANES), 1) < HEAD_DIM

        def step(i, carry):
            us = range(ROWS_PER_STEP_BWD)
            geo = [_row_geometry(ROWS_PER_STEP_BWD * i + u, rows) for u in us]
            qsel = [_block_diag(q_ref[pl.ds(t0, GRID_W), :] * QK_SCALE, diag) for t0, _, _ in geo]
            dosel = [_block_diag(do_ref[pl.ds(t0, GRID_W), :], diag) for t0, _, _ in geo]
            kbands = [kb[pl.ds(b0, BAND), :] for _, b0, _ in geo]
            vbands = [vb[pl.ds(b0, BAND), :] for _, b0, _ in geo]
            biases = [b_ref[cls] for _, _, cls in geo]
            dsts, dqs, dks, dvs = [], [], [], []
            for u in us:
                sc = lax.dot_general(qsel[u], kbands[u], _NT, preferred_element_type=_F32) + biases[u]
                ex = jnp.exp(sc - jnp.max(sc, axis=1, keepdims=True))
                p = ex * (1.0 / jnp.sum(ex, axis=1, keepdims=True))
                dp = lax.dot_general(dosel[u], vbands[u], _NT, preferred_element_type=_F32)
                delta = jnp.sum(p * dp, axis=1, keepdims=True)
                dst = p * (dp - delta)
                dsb = dst.astype(_MXU)
                dqf = jnp.dot(dsb, kbands[u], preferred_element_type=_F32)
                dsts.append(dst)
                dqs.append((jnp.where(m0, dqf[:GRID_W], dqf[GRID_W:]) * QK_SCALE).astype(dq_ref.dtype))
                dks.append(lax.dot_general(qsel[u], dsb, _TN, preferred_element_type=_F32).T)
                dvs.append(lax.dot_general(dosel[u], p.astype(_MXU), _TN, preferred_element_type=_F32).T)
            for u in us:
                t0, b0, cls = geo[u]
                g_ref[cls] += dsts[u]
                dq_ref[pl.ds(t0, GRID_W), :] = dqs[u]
                dka[pl.ds(b0, BAND), :] += dks[u]
                dva[pl.ds(b0, BAND), :] += dvs[u]
            return carry

        lax.fori_loop(0, rows // ROWS_PER_STEP_BWD, step, 0)
        dk_ref[...] = dka[...].astype(dk_ref.dtype)
        dv_ref[...] = dva[...].astype(dv_ref.dtype)

    col = lambda off: pl.BlockSpec((T, LANES), lambda i, off=off: (0, off + i))
    blk = pl.BlockSpec((T, LANES), lambda i: (0, i))
    tbl = pl.BlockSpec((None, N_CLS, LANES, BAND), lambda i: (i, 0, 0, 0))
    o16 = jax.ShapeDtypeStruct((T, naw), _MXU)
    vm = 2 * (4 * T * LANES * 4 + 2 * N_CLS * BAND * LANES * 4 + 3 * T * LANES * 2) + 2 * T * LANES * 6
    return _pallas(
        body, name=name,
        out_shape=(o16, o16, o16, jax.ShapeDtypeStruct((npair, N_CLS, LANES, BAND), _F32)),
        grid=(npair,),
        in_specs=[col(qoff), col(koff), col(voff), tbl, col(doff)],
        out_specs=(blk, blk, blk, tbl),
        scratch_shapes=[pltpu.VMEM((T, LANES), _MXU), pltpu.VMEM((T, LANES), _MXU),
                        pltpu.VMEM((T, LANES), _F32), pltpu.VMEM((T, LANES), _F32)],
        compiler_params=_params(("parallel",), vm),
    )(proj, proj, proj, bias, dycat)


_ANY = pl.BlockSpec(memory_space=pl.ANY)
_HBM = pl.BlockSpec(memory_space=pltpu.HBM)
_SEM = pl.BlockSpec(memory_space=pltpu.SEMAPHORE)
_VMEM = pl.BlockSpec(memory_space=pltpu.VMEM)
_MESH_ID = pl.DeviceIdType.MESH
_EFFECT = pltpu.SideEffectType.DATAFLOW_SIDE_EFFECTING
_TOKEN = jax.ShapeDtypeStruct((SUBLANES, LANES), _F32)


def _mesh_pos():
    return tuple(lax.axis_index(a) for a in MESH_AXES)


def _in_hbm(a):
    return pltpu.with_memory_space_constraint(a, pltpu.HBM)


def _hbm_like(arrays):
    return [pltpu.HBM(a.shape, a.dtype) for a in arrays]


def _after(arr, token):
    return arr + token[0:1, 0:1].astype(arr.dtype)


def _shard_ref(ref, axis, j, width):
    idx = [slice(None)] * len(ref.shape)
    idx[axis] = pl.ds(pl.multiple_of(j * width, math.gcd(width, LANES)), width)
    return ref.at[tuple(idx)]


def _all_gather(shards, axes, name):
    n = len(shards)
    widths = [s.shape[a] for s, a in zip(shards, axes)]
    out_shape = [jax.ShapeDtypeStruct(tuple(N_DEV * d if k == a else d for k, d in enumerate(s.shape)), s.dtype)
                 for s, a in zip(shards, axes)]

    def body(*refs):
        ins, outs = refs[:n], refs[n:2 * n]
        send_sems, recv_sems, local_sems = refs[2 * n:]
        x, y, c = _mesh_pos()
        me, sibling = (x, y, c), (x, y, 1 - c)
        chips = [(1 - x, y), (x, 1 - y), (1 - x, 1 - y)]

        def slot(i, px, py, pc):
            return _shard_ref(outs[i], axes[i], 4 * px + 2 * py + pc, widths[i])

        def copy(i, k, block, to, src=None):
            return pltpu.make_async_remote_copy(
                src_ref=slot(i, *block) if src is None else src, dst_ref=slot(i, *block),
                send_sem=send_sems.at[7 * i + k], recv_sem=recv_sems.at[7 * i + k],
                device_id=to, device_id_type=_MESH_ID)

        mine = [pltpu.make_async_copy(ins[i], slot(i, *me), local_sems.at[i]) for i in range(n)]
        for cp in mine:
            cp.start()
        first = []
        for i in range(n):
            first.append(copy(i, 0, me, sibling, src=ins[i]))
            first += [copy(i, 1 + j, me, (*chip, c), src=ins[i]) for j, chip in enumerate(chips)]
        for cp in first:
            cp.start()
        passed = []
        for j, chip in enumerate(chips):
            for i in range(n):
                copy(i, 1 + j, (*chip, c), me).wait_recv()
                fwd = copy(i, 4 + j, (*chip, c), sibling)
                fwd.start()
                passed.append(fwd)
        for i in range(n):
            copy(i, 0, sibling, me).wait_recv()
            for j, chip in enumerate(chips):
                copy(i, 4 + j, (*chip, 1 - c), me).wait_recv()
        for cp in first + passed:
            cp.wait_send()
        for cp in mine:
            cp.wait()

    return _pallas(
        body, name=name, out_shape=out_shape, in_specs=[_ANY] * n, out_specs=[_ANY] * n,
        scratch_shapes=[pltpu.SemaphoreType.DMA((7 * n,)), pltpu.SemaphoreType.DMA((7 * n,)),
                        pltpu.SemaphoreType.DMA((n,))],
    )(*shards)


def _block_of(ref, axis, blk):
    return _shard_ref(ref, axis, blk, ref.shape[axis] // N_DEV)


def _gather_start(lands, axes, after, name):
    n = len(lands)

    def body(*refs):
        land = refs[:n]
        send, recv_sib, recv_ici = refs[n + 1:n + 4]
        token = refs[-1]
        x, y, c = _mesh_pos()
        me = 4 * x + 2 * y + c
        for i in range(n):
            mine = _block_of(land[i], axes[i], me)
            pltpu.make_async_remote_copy(
                src_ref=mine, dst_ref=mine, send_sem=send.at[4 * i],
                recv_sem=recv_sib.at[i], device_id=(x, y, 1 - c), device_id_type=_MESH_ID).start()
            for j, chip in enumerate([(1 - x, y), (x, 1 - y), (1 - x, 1 - y)]):
                pltpu.make_async_remote_copy(
                    src_ref=mine, dst_ref=mine, send_sem=send.at[4 * i + 1 + j],
                    recv_sem=recv_ici.at[3 * i + j], device_id=(*chip, c), device_id_type=_MESH_ID).start()
        token[...] = jnp.zeros_like(token)

    dma = pltpu.SemaphoreType.DMA
    out = pl.pallas_call(
        body, name=name,
        out_shape=(dma((4 * n,)), dma((n,)), dma((3 * n,)), *_hbm_like(lands), _TOKEN),
        in_specs=[_HBM] * n + [_ANY], out_specs=(_SEM, _SEM, _SEM, *[_HBM] * n, _VMEM),
        input_output_aliases={i: 3 + i for i in range(n)},
        compiler_params=pltpu.CompilerParams(has_side_effects=_EFFECT),
    )(*[_in_hbm(a) for a in lands], after)
    return dict(send=out[0], recv_sib=out[1], recv_ici=out[2], lands=list(out[3:3 + n]), axes=axes, token=out[-1])


def _gather_forward(st, after, name):
    lands, axes = st["lands"], st["axes"]
    n = len(lands)

    def body(*refs):
        land = refs[:n]
        recv_ici = refs[n]
        send2, recv2 = refs[n + 2], refs[n + 3]
        token = refs[-1]
        x, y, c = _mesh_pos()
        for j, (px, py) in enumerate([(1 - x, y), (x, 1 - y), (1 - x, 1 - y)]):
            for i in range(n):
                blk = _block_of(land[i], axes[i], 4 * px + 2 * py + c)
                pltpu.make_async_remote_copy(
                    src_ref=blk, dst_ref=blk, send_sem=send2.at[3 * i + j],
                    recv_sem=recv_ici.at[3 * i + j], device_id=(px, py, c), device_id_type=_MESH_ID).wait_recv()
                pltpu.make_async_remote_copy(
                    src_ref=blk, dst_ref=blk, send_sem=send2.at[3 * i + j],
                    recv_sem=recv2.at[3 * i + j], device_id=(x, y, 1 - c), device_id_type=_MESH_ID).start()
        token[...] = jnp.zeros_like(token)

    dma = pltpu.SemaphoreType.DMA
    out = pl.pallas_call(
        body, name=name,
        out_shape=(dma((3 * n,)), dma((3 * n,)), *_hbm_like(lands), _TOKEN),
        in_specs=[_HBM] * n + [_SEM, _ANY], out_specs=(_SEM, _SEM, *[_HBM] * n, _VMEM),
        input_output_aliases={i: 2 + i for i in range(n)},
        compiler_params=pltpu.CompilerParams(has_side_effects=_EFFECT),
    )(*lands, st["recv_ici"], after)
    return dict(st, send2=out[0], recv2=out[1], lands=list(out[2:2 + n]), token=out[-1])


def _gather_finish(st, after, name):
    lands, axes = st["lands"], st["axes"]
    n = len(lands)

    def body(*refs):
        land = refs[:n]
        send, recv_sib, send2, recv2 = refs[n:n + 4]
        x, y, c = _mesh_pos()
        me = 4 * x + 2 * y + c
        sib = 4 * x + 2 * y + (1 - c)

        def desc(i, blk, s_sem, r_sem):
            ref = _block_of(land[i], axes[i], blk)
            return pltpu.make_async_remote_copy(
                src_ref=ref, dst_ref=ref, send_sem=s_sem, recv_sem=r_sem,
                device_id=(x, y, 1 - c), device_id_type=_MESH_ID)

        for i in range(n):
            desc(i, sib, send.at[4 * i], recv_sib.at[i]).wait_recv()
            for j, (px, py) in enumerate([(1 - x, y), (x, 1 - y), (1 - x, 1 - y)]):
                desc(i, 4 * px + 2 * py + (1 - c), send2.at[3 * i + j], recv2.at[3 * i + j]).wait_recv()
            for k in range(4):
                desc(i, me, send.at[4 * i + k], recv_sib.at[i]).wait_send()
            for j, (px, py) in enumerate([(1 - x, y), (x, 1 - y), (1 - x, 1 - y)]):
                desc(i, 4 * px + 2 * py + c, send2.at[3 * i + j], recv2.at[3 * i + j]).wait_send()

    out = pl.pallas_call(
        body, name=name, out_shape=tuple(_hbm_like(lands)),
        in_specs=[_HBM] * n + [_SEM] * 4 + [_ANY], out_specs=tuple([_HBM] * n),
        input_output_aliases={i: i for i in range(n)},
        compiler_params=pltpu.CompilerParams(has_side_effects=_EFFECT),
    )(*lands, st["send"], st["recv_sib"], st["send2"], st["recv2"], after)
    return list(out)


def _scatter_sibling_start(grads, name):
    n = len(grads)
    gots = [lax.empty((4,) + g.shape[1:], g.dtype) for g in grads]

    def body(*refs):
        grad, got = refs[:n], refs[n:2 * n]
        send, recv = refs[2 * n], refs[2 * n + 1]
        token = refs[-1]
        x, y, c = _mesh_pos()
        for i in range(n):
            for q in range(4):
                pltpu.make_async_remote_copy(
                    src_ref=grad[i].at[2 * q + (1 - c)], dst_ref=got[i].at[q], send_sem=send.at[4 * i + q],
                    recv_sem=recv.at[4 * i + q], device_id=(x, y, 1 - c), device_id_type=_MESH_ID).start()
        token[...] = jnp.zeros_like(token)

    dma = pltpu.SemaphoreType.DMA
    out = pl.pallas_call(
        body, name=name,
        out_shape=(dma((4 * n,)), dma((4 * n,)), *_hbm_like(grads), *_hbm_like(gots), _TOKEN),
        in_specs=[_HBM] * (2 * n), out_specs=(_SEM, _SEM, *[_HBM] * (2 * n), _VMEM),
        input_output_aliases={i: 2 + i for i in range(2 * n)},
        compiler_params=pltpu.CompilerParams(has_side_effects=_EFFECT),
    )(*[_in_hbm(a) for a in grads], *[_in_hbm(a) for a in gots])
    return dict(send=out[0], recv=out[1], grads=list(out[2:2 + n]), gots=list(out[2 + n:2 + 2 * n]), token=out[-1])


def _scatter_sibling_finish(st, after, name):
    grads, gots = st["grads"], st["gots"]
    n = len(grads)

    def body(*refs):
        grad, got = refs[:n], refs[n:2 * n]
        send, recv = refs[2 * n], refs[2 * n + 1]
        x, y, c = _mesh_pos()
        for i in range(n):
            for q in range(4):
                cp = pltpu.make_async_remote_copy(
                    src_ref=grad[i].at[2 * q + (1 - c)], dst_ref=got[i].at[q], send_sem=send.at[4 * i + q],
                    recv_sem=recv.at[4 * i + q], device_id=(x, y, 1 - c), device_id_type=_MESH_ID)
                cp.wait_recv()
                cp.wait_send()

    out = pl.pallas_call(
        body, name=name, out_shape=tuple(_hbm_like(grads) + _hbm_like(gots)),
        in_specs=[_HBM] * (2 * n) + [_SEM, _SEM, _ANY], out_specs=tuple([_HBM] * (2 * n)),
        input_output_aliases={i: i for i in range(2 * n)},
        compiler_params=pltpu.CompilerParams(has_side_effects=_EFFECT),
    )(*grads, *gots, st["send"], st["recv"], after)
    return list(out[:n]), list(out[n:])


def _scatter_add(grad, got, name):
    _, R, C = grad.shape
    tr = _blk(R, 512)
    my_c = lambda: lax.axis_index("c")
    my_chip = lambda: 2 * lax.axis_index("x") + lax.axis_index("y")

    def body(a_ref, b_ref, part_ref, fin_ref):
        s = (a_ref[...].astype(_F32) + b_ref[...].astype(_F32)).astype(part_ref.dtype)
        part_ref[...] = s

        @pl.when(pl.program_id(1) == my_chip())
        def _():
            fin_ref[...] = s

    shape = jax.ShapeDtypeStruct((4, R, C), grad.dtype)
    return _pallas(
        body, name=name, out_shape=(shape, shape), grid=(R // tr, 4),
        in_specs=[pl.BlockSpec((None, tr, C), lambda i, q: (2 * q + my_c(), i, 0)),
                  pl.BlockSpec((None, tr, C), lambda i, q: (q, i, 0))],
        out_specs=[pl.BlockSpec((None, tr, C), lambda i, q: (q, i, 0)),
                   pl.BlockSpec((None, tr, C), lambda i, q: (my_chip(), i, 0))],
        compiler_params=_params(("parallel", "arbitrary"), 2 * tr * C * 8),
    )(grad, got)


def _scatter_chips_start(parts, fins, name):
    n = len(parts)

    def body(*refs):
        part, fin = refs[:n], refs[n:2 * n]
        send, recv = refs[2 * n], refs[2 * n + 1]
        token = refs[-1]
        x, y, c = _mesh_pos()
        mine = 2 * x + y
        for i in range(n):
            for k, (tx, ty) in enumerate([(1 - x, y), (x, 1 - y), (1 - x, 1 - y)]):
                pltpu.make_async_remote_copy(
                    src_ref=part[i].at[2 * tx + ty], dst_ref=fin[i].at[mine], send_sem=send.at[3 * i + k],
                    recv_sem=recv.at[3 * i + k], device_id=(tx, ty, c), device_id_type=_MESH_ID).start()
        token[...] = jnp.zeros_like(token)

    dma = pltpu.SemaphoreType.DMA
    out = pl.pallas_call(
        body, name=name,
        out_shape=(dma((3 * n,)), dma((3 * n,)), *_hbm_like(parts), *_hbm_like(fins), _TOKEN),
        in_specs=[_HBM] * (2 * n), out_specs=(_SEM, _SEM, *[_HBM] * (2 * n), _VMEM),
        input_output_aliases={i: 2 + i for i in range(2 * n)},
        compiler_params=pltpu.CompilerParams(has_side_effects=_EFFECT),
    )(*[_in_hbm(a) for a in parts], *[_in_hbm(a) for a in fins])
    return dict(send=out[0], recv=out[1], parts=list(out[2:2 + n]), fins=list(out[2 + n:2 + 2 * n]), token=out[-1])


def _scatter_chips_finish(st, after, name):
    parts, fins = st["parts"], st["fins"]
    n = len(parts)

    def body(*refs):
        part, fin = refs[:n], refs[n:2 * n]
        send, recv = refs[2 * n], refs[2 * n + 1]
        x, y, c = _mesh_pos()
        for i in range(n):
            for k, (tx, ty) in enumerate([(1 - x, y), (x, 1 - y), (1 - x, 1 - y)]):
                cp = pltpu.make_async_remote_copy(
                    src_ref=part[i].at[2 * tx + ty], dst_ref=fin[i].at[2 * tx + ty], send_sem=send.at[3 * i + k],
                    recv_sem=recv.at[3 * i + k], device_id=(tx, ty, c), device_id_type=_MESH_ID)
                cp.wait_recv()
                cp.wait_send()

    out = pl.pallas_call(
        body, name=name, out_shape=tuple(_hbm_like(parts) + _hbm_like(fins)),
        in_specs=[_HBM] * (2 * n) + [_SEM, _SEM, _ANY], out_specs=tuple([_HBM] * (2 * n)),
        input_output_aliases={i: i for i in range(2 * n)},
        compiler_params=pltpu.CompilerParams(has_side_effects=_EFFECT),
    )(*parts, *fins, st["send"], st["recv"], after)
    return list(out[n:])


def _adamw(g, w, m, v):
    m = ADAM_B1 * m + (1.0 - ADAM_B1) * g
    v = ADAM_B2 * v + (1.0 - ADAM_B2) * (g * g)
    m_hat = m / (1.0 - ADAM_B1 ** ADAM_STEP)
    v_hat = v / (1.0 - ADAM_B2 ** ADAM_STEP)
    delta = -ADAM_LR * (m_hat / (jnp.sqrt(v_hat) + ADAM_EPS) + ADAM_WD * w)
    return delta, m, v


def _adam_layer(fin, w3, m3, v3, l, prev, name):
    L, R, C = w3.shape
    tr = _blk(R, max(SUBLANES, (1 << 18) // C))

    def body(f_ref, w_ref, m_ref, v_ref, *rest):
        g_ref, d_ref, nm_ref, nv_ref = rest[-4:]
        g = ((f_ref[0].astype(_F32) + f_ref[1].astype(_F32)) + f_ref[2].astype(_F32)) + f_ref[3].astype(_F32)
        d, nm, nv = _adamw(g, w_ref[...], m_ref[...], v_ref[...])
        g_ref[...] = g
        d_ref[...] = d
        nm_ref[...] = nm
        nv_ref[...] = nv

    lay = pl.BlockSpec((None, tr, C), lambda i: (l, i, 0))
    ins = [fin, w3, m3, v3]
    in_specs = [pl.BlockSpec((4, tr, C), lambda i: (0, i, 0)), lay, lay, lay]
    aliases = {}
    if prev is not None:
        ins += list(prev)
        in_specs += [_ANY] * 4
        aliases = {4 + k: k for k in range(4)}
    return _pallas(
        body, name=name, out_shape=[jax.ShapeDtypeStruct((L, R, C), _F32)] * 4, grid=(R // tr,),
        in_specs=in_specs, out_specs=[lay] * 4, input_output_aliases=aliases,
        compiler_params=_params(("parallel",), 2 * tr * C * (4 * 2 + 7 * 4)),
    )(*ins)


def _sum_parts(parts, name):
    _, R, C = parts.shape

    def body(p_ref, o_ref):
        acc = p_ref[0]
        for k in range(1, N_DEV):
            acc = acc + p_ref[k]
        o_ref[...] = acc

    tr = _blk(R, 512)
    return _pallas(
        body, name=name, out_shape=jax.ShapeDtypeStruct((R, C), _F32), grid=(R // tr,),
        in_specs=[pl.BlockSpec((N_DEV, tr, C), lambda i: (0, i, 0))],
        out_specs=pl.BlockSpec((tr, C), lambda i: (i, 0)),
        compiler_params=_params(("parallel",), 2 * tr * C * 4 * 9),
    )(parts)


def _adam_flat(g, w, m, v, name):
    R, C = g.shape
    tr = _blk(R, 512)

    def body(g_ref, w_ref, m_ref, v_ref, d_ref, nm_ref, nv_ref):
        d, nm, nv = _adamw(g_ref[...], w_ref[...], m_ref[...], v_ref[...])
        d_ref[...] = d
        nm_ref[...] = nm
        nv_ref[...] = nv

    spec = pl.BlockSpec((tr, C), lambda i: (i, 0))
    return _pallas(
        body, name=name, out_shape=[jax.ShapeDtypeStruct((R, C), _F32)] * 3, grid=(R // tr,),
        in_specs=[spec] * 4, out_specs=[spec] * 3,
        compiler_params=_params(("parallel",), 2 * tr * C * 4 * 7),
    )(g, w, m, v)


def _pack(arrays):
    flat = jnp.concatenate([a.reshape(-1) for a in arrays])
    tile = SUBLANES * LANES
    pad = (-flat.shape[0]) % tile
    return jnp.pad(flat, (0, pad)).reshape(-1, LANES)


def _unpack(packed, shapes):
    flat = packed.reshape(-1)
    out, off = [], 0
    for s in shapes:
        n = math.prod(s)
        out.append(flat[off:off + n].reshape(s))
        off += n
    return out


def kernel(x, w_in, w_dw, b_dw, conv_ln_g, conv_ln_b, rpb, w_out, w_up, w_down, pre_mix_g, post_mix_g, pre_mlp_g, post_mlp_g, loss_target, m_w_in, m_w_dw, m_b_dw, m_conv_ln_g, m_conv_ln_b, m_rpb, m_w_out, m_w_up, m_w_down, m_pre_mix_g, m_post_mix_g, m_pre_mlp_g, m_post_mlp_g, v_w_in, v_w_dw, v_b_dw, v_conv_ln_g, v_conv_ln_b, v_rpb, v_w_out, v_w_up, v_w_down, v_pre_mix_g, v_post_mix_g, v_pre_mlp_g, v_post_mlp_g):
    _, T, D = x.shape
    L = w_in.shape[0]
    cw = b_dw.shape[1]
    H = rpb.shape[1]
    naw = H * HEAD_DIM
    ks = w_dw.shape[1]
    assert T % GRID_W == 0 and T // GRID_W >= WIN_ROWS and H % 2 == 0 and cw % LANES == 0
    assert rpb.shape[2:] == (N_DR, N_DC) and w_dw.shape[2] * N_DEV == cw and ks // 2 < CONV_HALO
    assert naw == cw and w_out.shape[1] * N_DEV == cw + naw and (T // GRID_W) % ROWS_PER_STEP_FWD == 0

    xs = x.reshape(T, D)
    tgt = loss_target.reshape(T, D)
    row = lambda p, l: p[l:l + 1]
    mx, my, mc = (lax.axis_index(a) for a in MESH_AXES)
    dev = 4 * mx + 2 * my + mc

    ks_pad = ks + (-ks) % SUBLANES
    wdw_pad = jnp.pad(w_dw, ((0, 0), (0, ks_pad - ks), (0, 0))).reshape(L * ks_pad, w_dw.shape[2])
    wdw_full = _all_gather([wdw_pad], [1], "ag_wdw")[0].reshape(L, ks_pad, cw)[:, :ks]

    big = (w_in, w_out, w_up, w_down)
    names = ("in", "out", "up", "down")

    big_axes = (1, 0, 1, 0)

    def gather_start(l, which, after):
        lands = [_cast_slot(big[k], l, big_axes[k] == 1, f"cast_{names[k]}") for k in which]
        return _gather_start(lands, [big_axes[k] for k in which], after, "gather_start_%d" % len(which))

    def gather_pair(l, after):
        g_in = gather_start(l, [0], after)
        return g_in, gather_start(l, [1, 2, 3], g_in["token"])

    saved = []
    xin = xs
    h = _norm_fwd(xs, row(pre_mix_g, 0), "norm_first")
    g_in, g_rest = gather_pair(0, wdw_full)
    g_in = _gather_forward(g_in, g_rest["token"], "gather_forward_1")
    Win = _gather_finish(g_in, g_in["token"], "gather_finish_1")[0]
    dy = loss_sum = None
    for l in range(L):
        nxt = gather_pair(l + 1, Win) if l + 1 < L else None
        proj = _matmul(h, Win, name="mm_proj")
        bdw = _after(row(b_dw, l), nxt[1]["token"]) if nxt else row(b_dw, l)
        if l > 0:
            g_rest = _gather_forward(g_rest, proj, "gather_forward_3")
            bdw = _after(bdw, g_rest["token"])
        c = _conv_fwd(proj, wdw_full[l], bdw, cw, "conv_fwd")
        yc = _ln_silu_fwd(c, row(conv_ln_g, l), row(conv_ln_b, l), cw + naw, "ln_silu_fwd")
        bias_t, bias = _bias_table(rpb[l].reshape(H, N_DR * N_DC), "bias_table")
        ycat = _attn_fwd(proj, bias_t, yc, cw, naw, "attn_fwd")
        if l == 0:
            g_rest = _gather_forward(g_rest, ycat, "gather_forward_3")
        Wout, Wup, Wdown = _gather_finish(g_rest, ycat, "gather_finish_3")
        Ws = (Win, Wout, Wup, Wdown)
        mix = _matmul(ycat, Wout, name="mm_mix")
        x1, h2 = _resid_norm_fwd(xin, mix, row(post_mix_g, l), row(pre_mlp_g, l), "resid_mix")
        act, rl = _matmul(h2, Wup, epilogue="relu2", name="mm_up")
        if nxt:
            g_in = _gather_forward(nxt[0], act, "gather_forward_1")
        f = _matmul(act, Wdown, tm=256, tk=act.shape[1], n_outer=True, name="mm_down")
        saved.append(dict(xin=xin, h=h, W=Ws, proj=proj, c=c, bias=bias, ycat=ycat,
                          mix=mix, x1=x1, h2=h2, act=act, rl=rl, f=f))
        if nxt:
            Win = _gather_finish(g_in, f, "gather_finish_1")[0]
            g_rest = nxt[1]
            xin, h = _resid_norm_fwd(x1, f, row(post_mlp_g, l), row(pre_mix_g, l + 1), "resid_mlp")
        else:
            dy, loss_sum = _resid_loss(x1, f, row(post_mlp_g, l), tgt, "resid_loss")

    loss = lax.psum(loss_sum[0, 0] * (0.5 / D), MESH_AXES)

    small_grads = [None] * L
    big_out = [None] * 4
    moments = ((m_w_in, v_w_in), (m_w_out, v_w_out), (m_w_up, v_w_up), (m_w_down, v_w_down))

    def scatter_begin(grads, which, l):
        tag = "_%d" % len(which)
        return dict(st=_scatter_sibling_start(grads, "scatter_sibling_start" + tag), which=which, l=l, tag=tag)

    def scatter_mid(sc, after):
        grads, gots = _scatter_sibling_finish(sc["st"], after, "scatter_sibling_finish" + sc["tag"])
        pf = [_scatter_add(g, o, f"scatter_add_{k}") for k, g, o in zip(sc["which"], grads, gots)]
        st = _scatter_chips_start([p for p, _ in pf], [q for _, q in pf], "scatter_chips_start" + sc["tag"])
        return dict(sc, st=st)

    def scatter_end(sc, after):
        fins = _scatter_chips_finish(sc["st"], after, "scatter_chips_finish" + sc["tag"])
        for k, fin in zip(sc["which"], fins):
            big_out[k] = _adam_layer(fin, big[k], moments[k][0], moments[k][1], sc["l"], big_out[k],
                                     f"adam_{k}_{sc['l']}")

    dxo = dy
    pending = None
    for l in reversed(range(L)):
        s = saved[l]
        Win, Wout, Wup, Wdown = s["W"]
        last = l == 0
        g_post_mlp = row(post_mlp_g, l)
        if pending is not None:
            g_post_mlp = _after(g_post_mlp, pending["st"]["token"])
        d_f, dg_post_mlp = _norm_bwd(s["f"], g_post_mlp, dxo, None, _MXU, "norm_bwd_mlp")
        d_up = _matmul(d_f, Wdown, tb=True, epilogue="mul2", extra=s["rl"], name="mm_d_up")
        g_pre_mlp = row(pre_mlp_g, l)
        if pending is not None:
            pending = scatter_mid(pending, d_up)
            g_pre_mlp = _after(g_pre_mlp, pending["st"]["token"])
        dWdown = _matmul(s["act"], d_f, ta=True, out_dtype=_WIRE, tk=T, name="mm_dw_down").reshape(N_DEV, -1, D)
        d_h2 = _matmul(d_up, Wup, tb=True, tm=256, tk=d_up.shape[1], n_outer=True, name="mm_d_h2")
        dWup = _matmul(s["h2"], d_up, ta=True, out_dtype=_WIRE, out_cols=N_DEV, tk=T, name="mm_dw_up")
        g_post_mix, ln_g, wdw_l, g_pre_mix = row(post_mix_g, l), row(conv_ln_g, l), wdw_full[l], row(pre_mix_g, l)
        if last:
            sc_mlp = scatter_begin([dWup, dWdown], [2, 3], l)
            g_post_mix = _after(g_post_mix, sc_mlp["st"]["token"])
        dx1, dg_pre_mlp = _norm_bwd(s["x1"], g_pre_mlp, d_h2, dxo, _F32, "norm_bwd_premlp")
        d_mix, dg_post_mix = _norm_bwd(s["mix"], g_post_mix, dx1, None, _MXU, "norm_bwd_mix")
        d_ycat = _matmul(d_mix, Wout, tb=True, name="mm_d_ycat")
        if last:
            sc_mlp = scatter_mid(sc_mlp, d_ycat)
            ln_g = _after(ln_g, sc_mlp["st"]["token"])
        dWout = _matmul(s["ycat"], d_mix, ta=True, out_dtype=_WIRE, tk=T, name="mm_dw_out").reshape(N_DEV, -1, D)
        if last:
            sc_out = scatter_begin([dWout], [1], l)
            wdw_l = _after(wdw_l, sc_out["st"]["token"])
        dc, dlng, dlnb = _ln_silu_bwd(s["c"], ln_g, row(conv_ln_b, l), d_ycat, "ln_silu_bwd")
        da, dgate, dwb = _conv_bwd(s["proj"], dc, wdw_l, cw, "conv_bwd")
        dq, dk, dv, gcls = _attn_bwd(s["proj"], s["bias"], d_ycat, cw, naw, "attn_bwd")
        if last:
            sc_out = scatter_mid(sc_out, gcls)
            g_pre_mix = _after(g_pre_mix, sc_out["st"]["token"])
        drpb = _rpb_grad(gcls, "rpb_grad").reshape(H, 4 * SUBLANES, LANES)[:, :N_DC, :N_DR].transpose(0, 2, 1)
        dproj = jnp.concatenate([da, dgate, dq, dk, dv], axis=1)
        dh = _matmul(dproj, Win, tb=True, tm=512, tk=dproj.shape[1], n_outer=True, name="mm_d_h")
        dWin = _matmul(s["h"], dproj, ta=True, out_dtype=_WIRE, out_cols=N_DEV, tk=T, name="mm_dw_in")
        dxo, dg_pre_mix = _norm_bwd(s["xin"], g_pre_mix, dh, dx1, _F32, "norm_bwd_premix")
        if pending is not None:
            scatter_end(pending, dxo)
        if last:
            sc_in = scatter_begin([dWin], [0], l)
            sc_in = scatter_mid(sc_in, sc_in["st"]["token"])
            scatter_end(sc_mlp, sc_in["st"]["token"])
            scatter_end(sc_out, sc_in["st"]["token"])
            scatter_end(sc_in, sc_in["st"]["token"])
        else:
            pending = scatter_begin([dWin, dWout, dWup, dWdown], [0, 1, 2, 3], l)
        small_grads[l] = [dwb[ks], dlng[0], dlnb[0], drpb, dg_pre_mix[0], dg_post_mix[0], dg_pre_mlp[0],
                          dg_post_mlp[0], dwb[:ks]]

    rep_shapes = [(L, cw), (L, cw), (L, cw), (L, H, N_DR, N_DC), (L, D), (L, D), (L, D), (L, D)]
    stacked = [jnp.stack([small_grads[l][k] for l in range(L)]) for k in range(9)]
    packed = _pack(stacked)
    parts = _all_gather([packed], [0], "ag_small")[0].reshape(N_DEV, *packed.shape)
    gsum = _sum_parts(parts, "sum_small")
    g_small = _unpack(gsum, rep_shapes + [(L, ks, cw)])
    g_rep, g_wdw_full = g_small[:8], g_small[8]
    wsh = w_dw.shape[2]
    g_wdw = lax.dynamic_slice_in_dim(g_wdw_full, dev * wsh, wsh, axis=2)

    rep_w = [b_dw, conv_ln_g, conv_ln_b, rpb, pre_mix_g, post_mix_g, pre_mlp_g, post_mlp_g]
    rep_m = [m_b_dw, m_conv_ln_g, m_conv_ln_b, m_rpb, m_pre_mix_g, m_post_mix_g, m_pre_mlp_g, m_post_mlp_g]
    rep_v = [v_b_dw, v_conv_ln_g, v_conv_ln_b, v_rpb, v_pre_mix_g, v_post_mix_g, v_pre_mlp_g, v_post_mlp_g]
    rep_out = _adam_flat(_pack(g_rep), _pack(rep_w), _pack(rep_m), _pack(rep_v), "adam_small")
    rep_delta, rep_nm, rep_nv = (_unpack(o, rep_shapes) for o in rep_out)
    dw_out = _adam_flat(_pack([g_wdw]), _pack([w_dw]), _pack([m_w_dw]), _pack([v_w_dw]), "adam_wdw")
    wdw_delta, wdw_nm, wdw_nv = (_unpack(o, [w_dw.shape])[0] for o in dw_out)

    def assemble(kind_big, rep_list, wdw_val):
        return [big_out[0][kind_big], wdw_val, rep_list[0], rep_list[1], rep_list[2], rep_list[3],
                big_out[1][kind_big], big_out[2][kind_big], big_out[3][kind_big],
                rep_list[4], rep_list[5], rep_list[6], rep_list[7]]

    grads_out = assemble(0, g_rep, g_wdw)
    deltas = assemble(1, rep_delta, wdw_delta)
    new_m = assemble(2, rep_nm, wdw_nm)
    new_v = assemble(3, rep_nv, wdw_nv)
    return (loss, dxo.reshape(1, T, D), *grads_out, *deltas, *new_m, *new_v)
```

```python
import math

import jax
import jax.numpy as jnp
from jax import lax
from jax.experimental import pallas as pl
from jax.experimental.pallas import tpu as pltpu

_MXU = jnp.bfloat16
_WIRE = jnp.bfloat16
_F32 = jnp.float32

N_DEV = 8
GRID_W = 64
WIN_ROWS = 8
WIN_COLS = 16
HEAD_DIM = 64
LANES = 128
MXU_COLS_V7X = 256
SUBLANES = 8
RMS_EPS = 1e-6
LN_EPS = 1e-5
NEG_INF = -1e30
ADAM_LR = 0.001
ADAM_B1 = 0.9
ADAM_B2 = 0.999
ADAM_EPS = 1e-08
ADAM_WD = 0.01
ADAM_STEP = 10
VMEM_BYTES_V7X = 64 << 20
VMEM_RESERVE = 12 << 20
MESH_AXES = ("x", "y", "c")


def _vmem_limit(block_bytes):
    return int(min(max(block_bytes + (8 << 20), 24 << 20), VMEM_BYTES_V7X - VMEM_RESERVE))


def _blk(n, pref):
    if n <= pref:
        return n
    for t in range(pref, 7, -1):
        if n % t == 0 and t % SUBLANES == 0:
            return t
    return n


def _sigmoid(v):
    return 1.0 / (1.0 + jnp.exp(-v))


def _params(sem, nbytes):
    return pltpu.CompilerParams(dimension_semantics=sem, vmem_limit_bytes=_vmem_limit(nbytes))


def _pallas(body, deps=(), **kw):
    n_in = len(kw["in_specs"])
    if deps:
        kw["in_specs"] = list(kw["in_specs"]) + [pl.BlockSpec(memory_space=pl.ANY)] * len(deps)
        inner = body

        def body(*refs):
            return inner(*refs[:n_in], *refs[n_in + len(deps):])

    call = pl.pallas_call(body, **kw)

    def run(*operands):
        return call(*[pltpu.with_memory_space_constraint(o, pltpu.HBM)
                      if jnp.issubdtype(o.dtype, jnp.floating) else o for o in (*operands, *deps)])

    return run


def _my_block():
    x, y, c = (lax.axis_index(a) for a in MESH_AXES)
    return 4 * x + 2 * y + c


def _cast_slot(w3, l, by_cols, name):
    _, R, C = w3.shape
    tr = _blk(R, 512)

    def body(w_ref, o_ref):
        o_ref[...] = w_ref[...].astype(o_ref.dtype)

    if by_cols:
        shape, o_spec = (R, N_DEV * C), pl.BlockSpec((tr, C), lambda i: (i, _my_block()))
    else:
        shape, o_spec = (N_DEV * R, C), pl.BlockSpec((tr, C), lambda i: (_my_block() * (R // tr) + i, 0))
    return _pallas(
        body, name=name, out_shape=jax.ShapeDtypeStruct(shape, _WIRE), grid=(R // tr,),
        in_specs=[pl.BlockSpec((None, tr, C), lambda i: (l, i, 0))], out_specs=o_spec,
        compiler_params=_params(("parallel",), 2 * tr * C * 6),
    )(w3)


def _layer_row(param):
    arr, l = param
    L, W = arr.shape
    return arr.reshape(L, 1, W), pl.BlockSpec((None, 1, W), lambda *_: (l, 0, 0))


def _norm_fwd(x, g, name):
    T, D = x.shape
    tm = _blk(T, 256)
    g, g_spec = _layer_row(g)

    def body(x_ref, g_ref, h_ref):
        xv = x_ref[...]
        r = lax.rsqrt(jnp.mean(xv * xv, axis=-1, keepdims=True) + RMS_EPS)
        h_ref[...] = (xv * r * g_ref[...]).astype(h_ref.dtype)

    return _pallas(
        body, name=name, out_shape=jax.ShapeDtypeStruct((T, D), _MXU), grid=(T // tm,),
        in_specs=[pl.BlockSpec((tm, D), lambda i: (i, 0)), g_spec],
        out_specs=pl.BlockSpec((tm, D), lambda i: (i, 0)),
        compiler_params=_params(("parallel",), 2 * tm * D * 6),
    )(x, g)


def _resid_norm_fwd(xres, y, g_post, g_next, name):
    T, D = xres.shape
    tm = _blk(T, 256)

    def body(x_ref, y_ref, gp_ref, gn_ref, xn_ref, h_ref):
        yv = y_ref[...]
        r = lax.rsqrt(jnp.mean(yv * yv, axis=-1, keepdims=True) + RMS_EPS)
        xn = x_ref[...] + yv * r * gp_ref[...]
        xn_ref[...] = xn
        r2 = lax.rsqrt(jnp.mean(xn * xn, axis=-1, keepdims=True) + RMS_EPS)
        h_ref[...] = (xn * r2 * gn_ref[...]).astype(h_ref.dtype)

    row = pl.BlockSpec((tm, D), lambda i: (i, 0))
    (g_post, gp_spec), (g_next, gn_spec) = _layer_row(g_post), _layer_row(g_next)
    return _pallas(
        body, name=name,
        out_shape=(jax.ShapeDtypeStruct((T, D), _F32), jax.ShapeDtypeStruct((T, D), _MXU)),
        grid=(T // tm,), in_specs=[row, row, gp_spec, gn_spec], out_specs=(row, row),
        compiler_params=_params(("parallel",), 2 * tm * D * 14),
    )(xres, y, g_post, g_next)


def _resid_loss(xres, y, g_post, target, name):
    T, D = xres.shape
    tm = _blk(T, 256)

    def body(x_ref, y_ref, gp_ref, t_ref, dy_ref, loss_ref):
        yv = y_ref[...]
        r = lax.rsqrt(jnp.mean(yv * yv, axis=-1, keepdims=True) + RMS_EPS)
        err = x_ref[...] + yv * r * gp_ref[...] - t_ref[...]
        dy_ref[...] = err * (1.0 / D)

        @pl.when(pl.program_id(0) == 0)
        def _():
            loss_ref[...] = jnp.zeros_like(loss_ref)

        part = jnp.sum(jnp.sum(err * err, axis=-1, keepdims=True), axis=0, keepdims=True)
        loss_ref[...] += part

    row = pl.BlockSpec((tm, D), lambda i: (i, 0))
    g_post, gp_spec = _layer_row(g_post)
    return _pallas(
        body, name=name,
        out_shape=(jax.ShapeDtypeStruct((T, D), _F32), jax.ShapeDtypeStruct((1, 1), _F32)),
        grid=(T // tm,), in_specs=[row, row, gp_spec, row],
        out_specs=(row, pl.BlockSpec((1, 1), lambda i: (0, 0))),
        compiler_params=_params(("arbitrary",), 2 * tm * D * 16),
    )(xres, y, g_post, target)


def _norm_bwd(y, g, dout, dres, out_dtype, name, deps=()):
    T, D = y.shape
    tm = _blk(T, 256)
    nsteps = T // tm
    has_res = dres is not None

    def body(*refs):
        if has_res:
            y_ref, g_ref, do_ref, dr_ref, dy_ref, dg_ref, acc = refs
        else:
            y_ref, g_ref, do_ref, dy_ref, dg_ref, acc = refs
        i = pl.program_id(0)
        yv = y_ref[...]
        do = do_ref[...]
        r = lax.rsqrt(jnp.mean(yv * yv, axis=-1, keepdims=True) + RMS_EPS)
        gy = do * g_ref[...]
        dot = jnp.mean(yv * gy, axis=-1, keepdims=True)
        dy = r * gy - yv * (r * r * r * dot)
        if has_res:
            dy = dy + dr_ref[...]
        dy_ref[...] = dy.astype(dy_ref.dtype)

        @pl.when(i == 0)
        def _():
            acc[...] = jnp.zeros_like(acc)

        acc[...] += jnp.sum((do * yv * r).reshape(tm // SUBLANES, SUBLANES, D), axis=0)

        @pl.when(i == nsteps - 1)
        def _():
            dg_ref[...] = jnp.sum(acc[...], axis=0, keepdims=True)

    row = pl.BlockSpec((tm, D), lambda i: (i, 0))
    vec = pl.BlockSpec((1, D), lambda i: (0, 0))
    g, g_spec = _layer_row(g)
    ins = [y, g, dout] + ([dres] if has_res else [])
    in_specs = [row, g_spec, row] + ([row] if has_res else [])
    return _pallas(
        body, name=name, deps=deps,
        out_shape=(jax.ShapeDtypeStruct((T, D), out_dtype), jax.ShapeDtypeStruct((1, D), _F32)),
        grid=(nsteps,), in_specs=in_specs, out_specs=(row, vec),
        scratch_shapes=[pltpu.VMEM((SUBLANES, D), _F32)],
        compiler_params=_params(("arbitrary",), 2 * tm * D * 16),
    )(*ins)


def _matmul(a, b, *, ta=False, tb=False, out_dtype=_F32, epilogue=None, extra=None, out_cols=0,
            tm=1024, tk=2048, n_outer=False, name):
    M, K = (a.shape[1], a.shape[0]) if ta else a.shape
    N = b.shape[0] if tb else b.shape[1]
    tm = _blk(M, tm)
    width = N // out_cols if out_cols else 0
    per_step = 2 if (out_cols and width % MXU_COLS_V7X and out_cols % 2 == 0) else 1
    tn = per_step * width if out_cols else _blk(N, 1024)
    tk = _blk(K, tk)
    ij = (lambda g0, g1: (g1, g0)) if n_outer else (lambda g0, g1: (g0, g1))
    b_spec = (pl.BlockSpec((tn, tk), lambda g0, g1, k: (ij(g0, g1)[1], k)) if tb
              else pl.BlockSpec((tk, tn), lambda g0, g1, k: (k, ij(g0, g1)[1])))
    nk = K // tk
    a_spec = (pl.BlockSpec((tk, tm), lambda g0, g1, k: (k, ij(g0, g1)[0])) if ta
              else pl.BlockSpec((tm, tk), lambda g0, g1, k: (ij(g0, g1)[0], k)))
    if out_cols:
        assert epilogue is None
        o_spec = pl.BlockSpec((per_step, tm, width), lambda g0, g1, k: (ij(g0, g1)[1], ij(g0, g1)[0], 0))
        o_shape = (out_cols, M, width)
    else:
        o_spec = pl.BlockSpec((tm, tn), lambda g0, g1, k: ij(g0, g1))
        o_shape = (M, N)
    dims = (((0 if ta else 1,), (1 if tb else 0,)), ((), ()))
    n_extra = 1 if epilogue == "mul2" else 0
    n_out = 2 if epilogue == "relu2" else 1

    def finish(acc, extra_refs, out_refs):
        if out_cols:
            for cblk in range(per_step):
                out_refs[0][cblk] = acc[:, cblk * width:(cblk + 1) * width].astype(out_refs[0].dtype)
        elif epilogue is None:
            out_refs[0][...] = acc.astype(out_refs[0].dtype)
        elif epilogue == "relu2":
            rl = jnp.maximum(acc, 0.0)
            out_refs[0][...] = (rl * rl).astype(out_refs[0].dtype)
            out_refs[1][...] = rl.astype(out_refs[1].dtype)
        else:
            out_refs[0][...] = (acc * (2.0 * extra_refs[0][...].astype(_F32))).astype(out_refs[0].dtype)

    def body(a_ref, b_ref, *rest):
        extra_refs = rest[:n_extra]
        out_refs = rest[n_extra:n_extra + n_out]
        part = lax.dot_general(a_ref[...], b_ref[...], dims, preferred_element_type=_F32)
        if nk == 1:
            finish(part, extra_refs, out_refs)
            return
        acc = rest[-1]
        k = pl.program_id(2)

        @pl.when(k == 0)
        def _():
            acc[...] = part

        @pl.when(k > 0)
        def _():
            acc[...] += part

        @pl.when(k == nk - 1)
        def _():
            finish(acc[...], extra_refs, out_refs)

    if epilogue == "relu2":
        out_shape = (jax.ShapeDtypeStruct((M, N), _MXU), jax.ShapeDtypeStruct((M, N), _MXU))
        out_specs = (o_spec, o_spec)
        out_bytes = 2 * tm * tn * 2
    else:
        odt = _MXU if epilogue == "mul2" else out_dtype
        out_shape = jax.ShapeDtypeStruct(o_shape, odt)
        out_specs = o_spec
        out_bytes = tm * tn * jnp.dtype(odt).itemsize
    in_specs = [a_spec, b_spec] + ([o_spec] if n_extra else [])
    ins = [a, b] + ([extra] if n_extra else [])
    blocks = 2 * (tm * tk * 2 + tk * tn * 2 + out_bytes + n_extra * tm * tn * 2) + tm * tn * 4 * 2
    return _pallas(
        body, name=name, out_shape=out_shape,
        grid=(N // tn, M // tm, nk) if n_outer else (M // tm, N // tn, nk),
        in_specs=in_specs, out_specs=out_specs,
        scratch_shapes=[pltpu.VMEM((tm, tn), _F32)] if nk > 1 else [],
        compiler_params=_params(("parallel", "parallel", "arbitrary"), blocks),
    )(*ins)


CONV_HALO = 16
CONV_CHUNK = 256


def _tap_windows(win, n_taps_plus1, tc):
    n = win.shape[0]
    for s in range(SUBLANES):
        shifted = win if s == 0 else pltpu.roll(win, n - s, 0)
        for q in range((n_taps_plus1 + SUBLANES - 1) // SUBLANES):
            o = SUBLANES * q + s
            if 1 <= o < n_taps_plus1:
                yield o, shifted[SUBLANES * q:SUBLANES * q + tc, :]


def _conv_fwd(proj, wdw, bdw, l, cw, name, deps=()):
    T = proj.shape[0]
    ks = wdw.shape[1]
    cb = LANES
    tc = _blk(T, CONV_CHUNK)
    nblk = cw // cb

    def body(a_ref, g_ref, w_ref, b_ref, c_ref, upad):
        zeros = jnp.zeros((CONV_HALO, cb), _F32)
        upad[0:CONV_HALO, :] = zeros
        upad[T + CONV_HALO:T + 2 * CONV_HALO, :] = zeros
        upad[CONV_HALO:T + CONV_HALO, :] = a_ref[...] * _sigmoid(g_ref[...])

        def chunk(i, carry):
            t0 = pl.multiple_of(i * tc, tc)
            win = upad[pl.ds(t0, tc + 2 * CONV_HALO), :]
            acc = jnp.broadcast_to(b_ref[...], (tc, cb))
            for o, rows in _tap_windows(win, ks + 1, tc):
                j = o + ks // 2 - CONV_HALO
                acc = acc + rows * w_ref[j:j + 1, :]
            c_ref[pl.ds(t0, tc), :] = acc
            return carry

        lax.fori_loop(0, T // tc, chunk, 0)

    col = lambda off: pl.BlockSpec((T, cb), lambda i, off=off: (0, off + i))
    return _pallas(
        body, name=name, deps=deps, out_shape=jax.ShapeDtypeStruct((T, cw), _F32), grid=(nblk,),
        in_specs=[col(0), col(nblk), pl.BlockSpec((None, ks, cb), lambda i: (l, 0, i)),
                  pl.BlockSpec((None, 1, cb), lambda i: (l, 0, i))],
        out_specs=pl.BlockSpec((T, cb), lambda i: (0, i)),
        scratch_shapes=[pltpu.VMEM((T + 2 * CONV_HALO, cb), _F32)],
        compiler_params=_params(("parallel",), 2 * T * cb * 4 * 3 + T * cb * 4),
    )(proj, proj, wdw, bdw.reshape(bdw.shape[0], 1, cw))


def _ln_silu_fwd(c, lng, lnb, out_cols, name):
    T, cw = c.shape
    tm = _blk(T, 512)

    def body(c_ref, g_ref, b_ref, y_ref):
        cv = c_ref[...]
        mu = jnp.mean(cv, axis=-1, keepdims=True)
        xc = cv - mu
        var = jnp.mean(xc * xc, axis=-1, keepdims=True)
        z = xc * lax.rsqrt(var + LN_EPS) * g_ref[...] + b_ref[...]
        y_ref[...] = (z * _sigmoid(z)).astype(y_ref.dtype)

    row = pl.BlockSpec((tm, cw), lambda i: (i, 0))
    (lng, g_spec), (lnb, b_spec) = _layer_row(lng), _layer_row(lnb)
    return _pallas(
        body, name=name, out_shape=jax.ShapeDtypeStruct((T, out_cols), _MXU), grid=(T // tm,),
        in_specs=[row, g_spec, b_spec], out_specs=row,
        compiler_params=_params(("parallel",), 2 * tm * cw * 6),
    )(c, lng, lnb)


def _ln_silu_bwd(c, lng, lnb, dycat, name, deps=()):
    T, cw = c.shape
    tm = _blk(T, 512)
    nsteps = T // tm

    def body(c_ref, g_ref, b_ref, dy_ref, dc_ref, dg_ref, db_ref, accg, accb):
        i = pl.program_id(0)
        cv = c_ref[...]
        mu = jnp.mean(cv, axis=-1, keepdims=True)
        xc = cv - mu
        var = jnp.mean(xc * xc, axis=-1, keepdims=True)
        rstd = lax.rsqrt(var + LN_EPS)
        xhat = xc * rstd
        z = xhat * g_ref[...] + b_ref[...]
        sg = _sigmoid(z)
        dz = dy_ref[...] * (sg * (1.0 + z * (1.0 - sg)))
        dxh = dz * g_ref[...]
        m1 = jnp.mean(dxh, axis=-1, keepdims=True)
        m2 = jnp.mean(dxh * xhat, axis=-1, keepdims=True)
        dc_ref[...] = rstd * (dxh - m1 - xhat * m2)

        @pl.when(i == 0)
        def _():
            accg[...] = jnp.zeros_like(accg)
            accb[...] = jnp.zeros_like(accb)

        accg[...] += jnp.sum((dz * xhat).reshape(tm // SUBLANES, SUBLANES, cw), axis=0)
        accb[...] += jnp.sum(dz.reshape(tm // SUBLANES, SUBLANES, cw), axis=0)

        @pl.when(i == nsteps - 1)
        def _():
            dg_ref[...] = jnp.sum(accg[...], axis=0, keepdims=True)
            db_ref[...] = jnp.sum(accb[...], axis=0, keepdims=True)

    row = pl.BlockSpec((tm, cw), lambda i: (i, 0))
    vec = pl.BlockSpec((1, cw), lambda i: (0, 0))
    (lng, g_spec), (lnb, b_spec) = _layer_row(lng), _layer_row(lnb)
    return _pallas(
        body, name=name, deps=deps,
        out_shape=(jax.ShapeDtypeStruct((T, cw), _F32), jax.ShapeDtypeStruct((1, cw), _F32),
                   jax.ShapeDtypeStruct((1, cw), _F32)),
        grid=(nsteps,), in_specs=[row, g_spec, b_spec, row], out_specs=(row, vec, vec),
        scratch_shapes=[pltpu.VMEM((SUBLANES, cw), _F32), pltpu.VMEM((SUBLANES, cw), _F32)],
        compiler_params=_params(("arbitrary",), 2 * tm * cw * 12),
    )(c, lng, lnb, dycat)


def _conv_bwd(proj, dc, wdw, l, cw, name, deps=()):
    T = proj.shape[0]
    ks = wdw.shape[1]
    cb = LANES
    tc = _blk(T, CONV_CHUNK)
    nblk = cw // cb
    half = ks // 2

    def body(a_ref, g_ref, dc_ref, w_ref, da_ref, dg_ref, dwb_ref, upad, dpad, du, wacc):
        zeros = jnp.zeros((CONV_HALO, cb), _F32)
        for pad in (upad, dpad):
            pad[0:CONV_HALO, :] = zeros
            pad[T + CONV_HALO:T + 2 * CONV_HALO, :] = zeros
        sg = _sigmoid(g_ref[...])
        upad[CONV_HALO:T + CONV_HALO, :] = a_ref[...] * sg
        dpad[CONV_HALO:T + CONV_HALO, :] = dc_ref[...]
        wacc[...] = jnp.zeros_like(wacc)

        def chunk(i, carry):
            t0 = pl.multiple_of(i * tc, tc)
            dwin = dpad[pl.ds(t0, tc + 2 * CONV_HALO), :]
            uwin = upad[pl.ds(t0, tc + 2 * CONV_HALO), :]
            dcc = dwin[CONV_HALO:CONV_HALO + tc, :]
            acc = jnp.zeros((tc, cb), _F32)
            for o, rows in _tap_windows(dwin, CONV_HALO + half + 1, tc):
                j = CONV_HALO + half - o
                if 0 <= j < ks:
                    acc = acc + rows * w_ref[j:j + 1, :]
            du[pl.ds(t0, tc), :] = acc
            for o, rows in _tap_windows(uwin, CONV_HALO + half + 1, tc):
                j = o + half - CONV_HALO
                if 0 <= j < ks:
                    wacc[j] += jnp.sum((rows * dcc).reshape(tc // SUBLANES, SUBLANES, cb), axis=0)
            wacc[ks] += jnp.sum(dcc.reshape(tc // SUBLANES, SUBLANES, cb), axis=0)
            return carry

        lax.fori_loop(0, T // tc, chunk, 0)
        duv = du[...]
        av = a_ref[...]
        da_ref[...] = (duv * sg).astype(da_ref.dtype)
        dg_ref[...] = (duv * av * sg * (1.0 - sg)).astype(dg_ref.dtype)
        dwb_ref[...] = jnp.sum(wacc[...], axis=1)

    col = lambda off: pl.BlockSpec((T, cb), lambda i, off=off: (0, off + i))
    blk = pl.BlockSpec((T, cb), lambda i: (0, i))
    return _pallas(
        body, name=name, deps=deps,
        out_shape=(jax.ShapeDtypeStruct((T, cw), _MXU), jax.ShapeDtypeStruct((T, cw), _MXU),
                   jax.ShapeDtypeStruct((ks + 1, cw), _F32)),
        grid=(nblk,),
        in_specs=[col(0), col(nblk), blk, pl.BlockSpec((None, ks, cb), lambda i: (l, 0, i))],
        out_specs=(blk, blk, pl.BlockSpec((ks + 1, cb), lambda i: (0, i))),
        scratch_shapes=[pltpu.VMEM((T + 2 * CONV_HALO, cb), _F32), pltpu.VMEM((T + 2 * CONV_HALO, cb), _F32),
                        pltpu.VMEM((T, cb), _F32), pltpu.VMEM((ks + 1, SUBLANES, cb), _F32)],
        compiler_params=_params(("parallel",), 2 * T * cb * 4 * 4 + 3 * T * cb * 4),
    )(proj, proj, dc, wdw)


N_CLS = WIN_ROWS
N_DR = 2 * WIN_ROWS - 1
N_DC = 2 * WIN_COLS - 1
BAND = WIN_ROWS * GRID_W
QK_SCALE = HEAD_DIM ** -0.5
ROWS_PER_STEP_FWD = 4
ROWS_PER_STEP_BWD = 4
_NT = (((1,), (1,)), ((), ()))
_TN = (((0,), (0,)), ((), ()))


def _slab_iotas():
    wk = lax.broadcasted_iota(jnp.int32, (GRID_W, LANES), 0)
    lane = lax.broadcasted_iota(jnp.int32, (GRID_W, LANES), 1)
    wq = jnp.bitwise_and(lane, GRID_W - 1)
    head1 = lane >= GRID_W
    d = wk - wq + (WIN_COLS - 1)
    cs = jnp.clip(wq - WIN_COLS // 2, 0, GRID_W - WIN_COLS)
    window = (wk >= cs) & (wk < cs + WIN_COLS)
    return d, head1, window


def _pair_tiles(cls):
    return [(cls + 2 * p, p) for p in range(WIN_ROWS // 2)]


def _bias_table(rpb2, name):
    npair = rpb2.shape[0] // 2

    def body(rpb_ref, ot_ref, on_ref):
        p = pl.program_id(0)
        d, head1, window = _slab_iotas()
        slabs = []
        for dr in range(N_DR):
            val = jnp.zeros((GRID_W, LANES), _F32)
            for j in range(N_DC):
                s0 = rpb_ref[2 * p, dr * N_DC + j]
                s1 = rpb_ref[2 * p + 1, dr * N_DC + j]
                val = jnp.where(d == j, jnp.where(head1, s1, s0), val)
            slab = jnp.where(window, val, NEG_INF)
            slabs.append(slab)
            for cls in range(N_CLS):
                k = dr - cls
                if 0 <= k < WIN_ROWS:
                    ot_ref[cls, k * GRID_W:(k + 1) * GRID_W, :] = slab
        for e in range(N_DR - 1):
            tile = jnp.concatenate([slabs[e], slabs[e + 1]], axis=0).T
            for cls in range(N_CLS):
                for ee, pp in _pair_tiles(cls):
                    if ee == e:
                        on_ref[cls, :, pp * LANES:(pp + 1) * LANES] = tile

    return pl.pallas_call(
        body, name=name,
        out_shape=(jax.ShapeDtypeStruct((npair, N_CLS, BAND, LANES), _F32),
                   jax.ShapeDtypeStruct((npair, N_CLS, LANES, BAND), _F32)),
        grid=(npair,),
        in_specs=[pl.BlockSpec(memory_space=pltpu.SMEM)],
        out_specs=(pl.BlockSpec((None, N_CLS, BAND, LANES), lambda p: (p, 0, 0, 0)),
                   pl.BlockSpec((None, N_CLS, LANES, BAND), lambda p: (p, 0, 0, 0))),
        compiler_params=_params(("arbitrary",), 4 * N_CLS * BAND * LANES * 4),
    )(rpb2)


def _rpb_grad(gc, name):
    npair = gc.shape[0]

    def body(g_ref, o_ref):
        d, _, _ = _slab_iotas()
        rowi = lax.broadcasted_iota(jnp.int32, (4 * SUBLANES, LANES), 0)
        lanei = lax.broadcasted_iota(jnp.int32, (4 * SUBLANES, LANES), 1)
        head1 = lax.broadcasted_iota(jnp.int32, (1, LANES), 1) >= GRID_W
        tiles = [jnp.zeros((4 * SUBLANES, LANES), _F32) for _ in range(2)]
        yts = []
        for e in range(N_DR - 1):
            y = jnp.zeros((LANES, LANES), _F32)
            for cls in range(N_CLS):
                for ee, pp in _pair_tiles(cls):
                    if ee == e:
                        y = y + g_ref[cls, :, pp * LANES:(pp + 1) * LANES]
            yts.append(y.T)
        for dr in range(N_DR):
            ysum = jnp.zeros((GRID_W, LANES), _F32)
            if dr < N_DR - 1:
                ysum = ysum + yts[dr][:GRID_W]
            if dr >= 1:
                ysum = ysum + yts[dr - 1][GRID_W:]
            for j in range(N_DC):
                cs = jnp.sum(jnp.where(d == j, ysum, 0.0), axis=0, keepdims=True)
                s0 = jnp.sum(jnp.where(head1, 0.0, cs), axis=1, keepdims=True)
                s1 = jnp.sum(jnp.where(head1, cs, 0.0), axis=1, keepdims=True)
                here = (rowi == j) & (lanei == dr)
                tiles[0] = tiles[0] + jnp.where(here, s0, 0.0)
                tiles[1] = tiles[1] + jnp.where(here, s1, 0.0)
        o_ref[0] = tiles[0]
        o_ref[1] = tiles[1]

    return _pallas(
        body, name=name, out_shape=jax.ShapeDtypeStruct((npair, 2, 4 * SUBLANES, LANES), _F32), grid=(npair,),
        in_specs=[pl.BlockSpec((None, N_CLS, LANES, BAND), lambda p: (p, 0, 0, 0))],
        out_specs=pl.BlockSpec((None, 2, 4 * SUBLANES, LANES), lambda p: (p, 0, 0, 0)),
        compiler_params=_params(("parallel",), 2 * N_CLS * BAND * LANES * 4),
    )(gc)


def _block_diag(v, diag):
    return jnp.where(diag, jnp.concatenate([v, v], axis=0), 0.0).astype(_MXU)


def _diag_mask():
    r = lax.broadcasted_iota(jnp.int32, (LANES, LANES), 0) < GRID_W
    c = lax.broadcasted_iota(jnp.int32, (LANES, LANES), 1) < HEAD_DIM
    return r == c


def _row_geometry(r, rows):
    rs = jnp.clip(r - WIN_ROWS // 2, 0, rows - WIN_ROWS)
    cls = rs - r + (WIN_ROWS - 1)
    return pl.multiple_of(r * GRID_W, GRID_W), pl.multiple_of(rs * GRID_W, GRID_W), cls


def _probs_t(qsel, kband, bias):
    s = lax.dot_general(kband, qsel, _NT, preferred_element_type=_F32) + bias
    mx = jnp.max(s, axis=0, keepdims=True)
    e = jnp.exp(s - mx)
    return e * (1.0 / jnp.sum(e, axis=0, keepdims=True))


def _attn_fwd(proj, bias, ycat, cw, naw, name):
    T = proj.shape[0]
    rows = T // GRID_W
    npair = naw // LANES
    qoff, koff, voff = 2 * cw // LANES, (2 * cw + naw) // LANES, (2 * cw + 2 * naw) // LANES

    def body(q_ref, k_ref, v_ref, b_ref, ycat_ref, o_ref, kb, vb):
        kb[...] = k_ref[...].astype(_MXU)
        vb[...] = v_ref[...].astype(_MXU)
        diag = _diag_mask()
        m0 = lax.broadcasted_iota(jnp.int32, (GRID_W, LANES), 1) < HEAD_DIM

        def step(i, carry):
            us = range(ROWS_PER_STEP_FWD)
            geo = [_row_geometry(ROWS_PER_STEP_FWD * i + u, rows) for u in us]
            qsel = [_block_diag(q_ref[pl.ds(t0, GRID_W), :] * QK_SCALE, diag) for t0, _, _ in geo]
            kbands = [kb[pl.ds(b0, BAND), :] for _, b0, _ in geo]
            vbands = [vb[pl.ds(b0, BAND), :] for _, b0, _ in geo]
            biases = [b_ref[cls] for _, _, cls in geo]
            pts = [_probs_t(qsel[u], kbands[u], biases[u]) for u in us]
            ofs = [lax.dot_general(pts[u].astype(_MXU), vbands[u], _TN, preferred_element_type=_F32) for u in us]
            for u in us:
                o_ref[pl.ds(geo[u][0], GRID_W), :] = jnp.where(m0, ofs[u][:GRID_W], ofs[u][GRID_W:]).astype(o_ref.dtype)
            return carry

        lax.fori_loop(0, rows // ROWS_PER_STEP_FWD, step, 0)

    col = lambda off: pl.BlockSpec((T, LANES), lambda i, off=off: (0, off + i))
    return _pallas(
        body, name=name, out_shape=jax.ShapeDtypeStruct(ycat.shape, ycat.dtype), grid=(npair,),
        in_specs=[col(qoff), col(koff), col(voff),
                  pl.BlockSpec((None, N_CLS, BAND, LANES), lambda i: (i, 0, 0, 0)), _ANY],
        out_specs=pl.BlockSpec((T, LANES), lambda i: (0, cw // LANES + i)),
        input_output_aliases={4: 0},
        scratch_shapes=[pltpu.VMEM((T, LANES), _MXU), pltpu.VMEM((T, LANES), _MXU)],
        compiler_params=_params(("parallel",), 2 * (3 * T * LANES * 4 + N_CLS * BAND * LANES * 4 + T * LANES * 2)),
    )(proj, proj, proj, bias, ycat)


def _attn_bwd(proj, bias, dycat, cw, naw, name):
    T = proj.shape[0]
    rows = T // GRID_W
    npair = naw // LANES
    qoff, koff, voff = 2 * cw // LANES, (2 * cw + naw) // LANES, (2 * cw + 2 * naw) // LANES
    doff = cw // LANES

    def body(q_ref, k_ref, v_ref, b_ref, do_ref, dq_ref, dk_ref, dv_ref, g_ref, kb, vb, dka, dva):
        kb[...] = k_ref[...].astype(_MXU)
        vb[...] = v_ref[...].astype(_MXU)
        dka[...] = jnp.zeros_like(dka)
        dva[...] = jnp.zeros_like(dva)
        g_ref[...] = jnp.zeros_like(g_ref)
        diag = _diag_mask()
        m0 = lax.broadcasted_iota(jnp.int32, (GRID_W, LANES), 1) < HEAD_DIM

        def step(i, carry):
            us = range(ROWS_PER_STEP_BWD)
            geo = [_row_geometry(ROWS_PER_STEP_BWD * i + u, rows) for u in us]
            qsel = [_block_diag(q_ref[pl.ds(t0, GRID_W), :] * QK_SCALE, diag) for t0, _, _ in geo]
            dosel = [_block_diag(do_ref[pl.ds(t0, GRID_W), :], diag) for t0, _, _ in geo]
            kbands = [kb[pl.ds(b0, BAND), :] for _, b0, _ in geo]
            vbands = [vb[pl.ds(b0, BAND), :] for _, b0, _ in geo]
            biases = [b_ref[cls] for _, _, cls in geo]
            dsts, dqs, dks, dvs = [], [], [], []
            for u in us:
                sc = lax.dot_general(qsel[u], kbands[u], _NT, preferred_element_type=_F32) + biases[u]
                ex = jnp.exp(sc - jnp.max(sc, axis=1, keepdims=True))
                p = ex * (1.0 / jnp.sum(ex, axis=1, keepdims=True))
                dp = lax.dot_general(dosel[u], vbands[u], _NT, preferred_element_type=_F32)
                delta = jnp.sum(p * dp, axis=1, keepdims=True)
                dst = p * (dp - delta)
                dsb = dst.astype(_MXU)
                dqf = jnp.dot(dsb, kbands[u], preferred_element_type=_F32)
                dsts.append(dst)
                dqs.append((jnp.where(m0, dqf[:GRID_W], dqf[GRID_W:]) * QK_SCALE).astype(dq_ref.dtype))
                dks.append(lax.dot_general(qsel[u], dsb, _TN, preferred_element_type=_F32).T)
                dvs.append(lax.dot_general(dosel[u], p.astype(_MXU), _TN, preferred_element_type=_F32).T)
            for u in us:
                t0, b0, cls = geo[u]
                g_ref[cls] += dsts[u]
                dq_ref[pl.ds(t0, GRID_W), :] = dqs[u]
                dka[pl.ds(b0, BAND), :] += dks[u]
                dva[pl.ds(b0, BAND), :] += dvs[u]
            return carry

        lax.fori_loop(0, rows // ROWS_PER_STEP_BWD, step, 0)
        dk_ref[...] = dka[...].astype(dk_ref.dtype)
        dv_ref[...] = dva[...].astype(dv_ref.dtype)

    col = lambda off: pl.BlockSpec((T, LANES), lambda i, off=off: (0, off + i))
    blk = pl.BlockSpec((T, LANES), lambda i: (0, i))
    tbl = pl.BlockSpec((None, N_CLS, LANES, BAND), lambda i: (i, 0, 0, 0))
    o16 = jax.ShapeDtypeStruct((T, naw), _MXU)
    vm = 2 * (4 * T * LANES * 4 + 2 * N_CLS * BAND * LANES * 4 + 3 * T * LANES * 2) + 2 * T * LANES * 6
    return _pallas(
        body, name=name,
        out_shape=(o16, o16, o16, jax.ShapeDtypeStruct((npair, N_CLS, LANES, BAND), _F32)),
        grid=(npair,),
        in_specs=[col(qoff), col(koff), col(voff), tbl, col(doff)],
        out_specs=(blk, blk, blk, tbl),
        scratch_shapes=[pltpu.VMEM((T, LANES), _MXU), pltpu.VMEM((T, LANES), _MXU),
                        pltpu.VMEM((T, LANES), _F32), pltpu.VMEM((T, LANES), _F32)],
        compiler_params=_params(("parallel",), vm),
    )(proj, proj, proj, bias, dycat)


_ANY = pl.BlockSpec(memory_space=pl.ANY)
_HBM = pl.BlockSpec(memory_space=pltpu.HBM)
_SEM = pl.BlockSpec(memory_space=pltpu.SEMAPHORE)
_VMEM = pl.BlockSpec(memory_space=pltpu.VMEM)
_MESH_ID = pl.DeviceIdType.MESH
_EFFECT = pltpu.SideEffectType.DATAFLOW_SIDE_EFFECTING
_TOKEN = jax.ShapeDtypeStruct((SUBLANES, LANES), _F32)


def _mesh_pos():
    return tuple(lax.axis_index(a) for a in MESH_AXES)


def _in_hbm(a):
    return pltpu.with_memory_space_constraint(a, pltpu.HBM)


def _hbm_like(arrays):
    return [pltpu.HBM(a.shape, a.dtype) for a in arrays]


def _shard_ref(ref, axis, j, width):
    idx = [slice(None)] * len(ref.shape)
    idx[axis] = pl.ds(pl.multiple_of(j * width, math.gcd(width, LANES)), width)
    return ref.at[tuple(idx)]


def _all_gather(shards, axes, name):
    n = len(shards)
    widths = [s.shape[a] for s, a in zip(shards, axes)]
    out_shape = [jax.ShapeDtypeStruct(tuple(N_DEV * d if k == a else d for k, d in enumerate(s.shape)), s.dtype)
                 for s, a in zip(shards, axes)]

    def body(*refs):
        ins, outs = refs[:n], refs[n:2 * n]
        send_sems, recv_sems, local_sems = refs[2 * n:]
        x, y, c = _mesh_pos()
        me, sibling = (x, y, c), (x, y, 1 - c)
        chips = [(1 - x, y), (x, 1 - y), (1 - x, 1 - y)]

        def slot(i, px, py, pc):
            return _shard_ref(outs[i], axes[i], 4 * px + 2 * py + pc, widths[i])

        def copy(i, k, block, to, src=None):
            return pltpu.make_async_remote_copy(
                src_ref=slot(i, *block) if src is None else src, dst_ref=slot(i, *block),
                send_sem=send_sems.at[7 * i + k], recv_sem=recv_sems.at[7 * i + k],
                device_id=to, device_id_type=_MESH_ID)

        mine = [pltpu.make_async_copy(ins[i], slot(i, *me), local_sems.at[i]) for i in range(n)]
        for cp in mine:
            cp.start()
        first = []
        for i in range(n):
            first.append(copy(i, 0, me, sibling, src=ins[i]))
            first += [copy(i, 1 + j, me, (*chip, c), src=ins[i]) for j, chip in enumerate(chips)]
        for cp in first:
            cp.start()
        passed = []
        for j, chip in enumerate(chips):
            for i in range(n):
                copy(i, 1 + j, (*chip, c), me).wait_recv()
                fwd = copy(i, 4 + j, (*chip, c), sibling)
                fwd.start()
                passed.append(fwd)
        for i in range(n):
            copy(i, 0, sibling, me).wait_recv()
            for j, chip in enumerate(chips):
                copy(i, 4 + j, (*chip, 1 - c), me).wait_recv()
        for cp in first + passed:
            cp.wait_send()
        for cp in mine:
            cp.wait()

    return _pallas(
        body, name=name, out_shape=out_shape, in_specs=[_ANY] * n, out_specs=[_ANY] * n,
        scratch_shapes=[pltpu.SemaphoreType.DMA((7 * n,)), pltpu.SemaphoreType.DMA((7 * n,)),
                        pltpu.SemaphoreType.DMA((n,))],
    )(*shards)


def _block_of(ref, axis, blk):
    return _shard_ref(ref, axis, blk, ref.shape[axis] // N_DEV)


def _gather_start(lands, axes, after, name):
    n = len(lands)

    def body(*refs):
        land = refs[:n]
        send, recv_sib, recv_ici = refs[n + 1:n + 4]
        token = refs[-1]
        x, y, c = _mesh_pos()
        me = 4 * x + 2 * y + c
        for i in range(n):
            mine = _block_of(land[i], axes[i], me)
            pltpu.make_async_remote_copy(
                src_ref=mine, dst_ref=mine, send_sem=send.at[4 * i],
                recv_sem=recv_sib.at[i], device_id=(x, y, 1 - c), device_id_type=_MESH_ID).start()
            for j, chip in enumerate([(1 - x, y), (x, 1 - y), (1 - x, 1 - y)]):
                pltpu.make_async_remote_copy(
                    src_ref=mine, dst_ref=mine, send_sem=send.at[4 * i + 1 + j],
                    recv_sem=recv_ici.at[3 * i + j], device_id=(*chip, c), device_id_type=_MESH_ID).start()
        token[...] = jnp.zeros_like(token)

    dma = pltpu.SemaphoreType.DMA
    out = pl.pallas_call(
        body, name=name,
        out_shape=(dma((4 * n,)), dma((n,)), dma((3 * n,)), *_hbm_like(lands), _TOKEN),
        in_specs=[_HBM] * n + [_ANY], out_specs=(_SEM, _SEM, _SEM, *[_HBM] * n, _VMEM),
        input_output_aliases={i: 3 + i for i in range(n)},
        compiler_params=pltpu.CompilerParams(has_side_effects=_EFFECT),
    )(*[_in_hbm(a) for a in lands], after)
    return dict(send=out[0], recv_sib=out[1], recv_ici=out[2], lands=list(out[3:3 + n]), axes=axes, token=out[-1])


def _gather_forward(st, after, name):
    lands, axes = st["lands"], st["axes"]
    n = len(lands)

    def body(*refs):
        land = refs[:n]
        recv_ici = refs[n]
        send2, recv2 = refs[n + 2], refs[n + 3]
        token = refs[-1]
        x, y, c = _mesh_pos()
        for j, (px, py) in enumerate([(1 - x, y), (x, 1 - y), (1 - x, 1 - y)]):
            for i in range(n):
                blk = _block_of(land[i], axes[i], 4 * px + 2 * py + c)
                pltpu.make_async_remote_copy(
                    src_ref=blk, dst_ref=blk, send_sem=send2.at[3 * i + j],
                    recv_sem=recv_ici.at[3 * i + j], device_id=(px, py, c), device_id_type=_MESH_ID).wait_recv()
                pltpu.make_async_remote_copy(
                    src_ref=blk, dst_ref=blk, send_sem=send2.at[3 * i + j],
                    recv_sem=recv2.at[3 * i + j], device_id=(x, y, 1 - c), device_id_type=_MESH_ID).start()
        token[...] = jnp.zeros_like(token)

    dma = pltpu.SemaphoreType.DMA
    out = pl.pallas_call(
        body, name=name,
        out_shape=(dma((3 * n,)), dma((3 * n,)), *_hbm_like(lands), _TOKEN),
        in_specs=[_HBM] * n + [_SEM, _ANY], out_specs=(_SEM, _SEM, *[_HBM] * n, _VMEM),
        input_output_aliases={i: 2 + i for i in range(n)},
        compiler_params=pltpu.CompilerParams(has_side_effects=_EFFECT),
    )(*lands, st["recv_ici"], after)
    return dict(st, send2=out[0], recv2=out[1], lands=list(out[2:2 + n]), token=out[-1])


def _gather_finish(st, after, name):
    lands, axes = st["lands"], st["axes"]
    n = len(lands)

    def body(*refs):
        land = refs[:n]
        send, recv_sib, send2, recv2 = refs[n:n + 4]
        x, y, c = _mesh_pos()
        me = 4 * x + 2 * y + c
        sib = 4 * x + 2 * y + (1 - c)

        def desc(i, blk, s_sem, r_sem):
            ref = _block_of(land[i], axes[i], blk)
            return pltpu.make_async_remote_copy(
                src_ref=ref, dst_ref=ref, send_sem=s_sem, recv_sem=r_sem,
                device_id=(x, y, 1 - c), device_id_type=_MESH_ID)

        for i in range(n):
            desc(i, sib, send.at[4 * i], recv_sib.at[i]).wait_recv()
            for j, (px, py) in enumerate([(1 - x, y), (x, 1 - y), (1 - x, 1 - y)]):
                desc(i, 4 * px + 2 * py + (1 - c), send2.at[3 * i + j], recv2.at[3 * i + j]).wait_recv()
            for k in range(4):
                desc(i, me, send.at[4 * i + k], recv_sib.at[i]).wait_send()
            for j, (px, py) in enumerate([(1 - x, y), (x, 1 - y), (1 - x, 1 - y)]):
                desc(i, 4 * px + 2 * py + c, send2.at[3 * i + j], recv2.at[3 * i + j]).wait_send()

    out = pl.pallas_call(
        body, name=name, out_shape=tuple(_hbm_like(lands)),
        in_specs=[_HBM] * n + [_SEM] * 4 + [_ANY], out_specs=tuple([_HBM] * n),
        input_output_aliases={i: i for i in range(n)},
        compiler_params=pltpu.CompilerParams(has_side_effects=_EFFECT),
    )(*lands, st["send"], st["recv_sib"], st["send2"], st["recv2"], after)
    return list(out)


def _scatter_sibling_start(grads, name):
    n = len(grads)
    gots = [lax.empty((4,) + g.shape[1:], g.dtype) for g in grads]

    def body(*refs):
        grad, got = refs[:n], refs[n:2 * n]
        send, recv = refs[2 * n], refs[2 * n + 1]
        token = refs[-1]
        x, y, c = _mesh_pos()
        for i in range(n):
            for q in range(4):
                pltpu.make_async_remote_copy(
                    src_ref=grad[i].at[2 * q + (1 - c)], dst_ref=got[i].at[q], send_sem=send.at[4 * i + q],
                    recv_sem=recv.at[4 * i + q], device_id=(x, y, 1 - c), device_id_type=_MESH_ID).start()
        token[...] = jnp.zeros_like(token)

    dma = pltpu.SemaphoreType.DMA
    out = pl.pallas_call(
        body, name=name,
        out_shape=(dma((4 * n,)), dma((4 * n,)), *_hbm_like(grads), *_hbm_like(gots), _TOKEN),
        in_specs=[_HBM] * (2 * n), out_specs=(_SEM, _SEM, *[_HBM] * (2 * n), _VMEM),
        input_output_aliases={i: 2 + i for i in range(2 * n)},
        compiler_params=pltpu.CompilerParams(has_side_effects=_EFFECT),
    )(*[_in_hbm(a) for a in grads], *[_in_hbm(a) for a in gots])
    return dict(send=out[0], recv=out[1], grads=list(out[2:2 + n]), gots=list(out[2 + n:2 + 2 * n]), token=out[-1])


def _scatter_sibling_finish(st, after, name):
    grads, gots = st["grads"], st["gots"]
    n = len(grads)

    def body(*refs):
        grad, got = refs[:n], refs[n:2 * n]
        send, recv = refs[2 * n], refs[2 * n + 1]
        x, y, c = _mesh_pos()
        for i in range(n):
            for q in range(4):
                cp = pltpu.make_async_remote_copy(
                    src_ref=grad[i].at[2 * q + (1 - c)], dst_ref=got[i].at[q], send_sem=send.at[4 * i + q],
                    recv_sem=recv.at[4 * i + q], device_id=(x, y, 1 - c), device_id_type=_MESH_ID)
                cp.wait_recv()
                cp.wait_send()

    out = pl.pallas_call(
        body, name=name, out_shape=tuple(_hbm_like(grads) + _hbm_like(gots)),
        in_specs=[_HBM] * (2 * n) + [_SEM, _SEM, _ANY], out_specs=tuple([_HBM] * (2 * n)),
        input_output_aliases={i: i for i in range(2 * n)},
        compiler_params=pltpu.CompilerParams(has_side_effects=_EFFECT),
    )(*grads, *gots, st["send"], st["recv"], after)
    return list(out[:n]), list(out[n:])


def _scatter_add(grad, got, name):
    _, R, C = grad.shape
    tr = _blk(R, 512)
    my_c = lambda: lax.axis_index("c")
    my_chip = lambda: 2 * lax.axis_index("x") + lax.axis_index("y")

    def body(a_ref, b_ref, part_ref, fin_ref):
        s = (a_ref[...].astype(_F32) + b_ref[...].astype(_F32)).astype(part_ref.dtype)
        part_ref[...] = s

        @pl.when(pl.program_id(1) == my_chip())
        def _():
            fin_ref[...] = s

    shape = jax.ShapeDtypeStruct((4, R, C), grad.dtype)
    return _pallas(
        body, name=name, out_shape=(shape, shape), grid=(R // tr, 4),
        in_specs=[pl.BlockSpec((None, tr, C), lambda i, q: (2 * q + my_c(), i, 0)),
                  pl.BlockSpec((None, tr, C), lambda i, q: (q, i, 0))],
        out_specs=[pl.BlockSpec((None, tr, C), lambda i, q: (q, i, 0)),
                   pl.BlockSpec((None, tr, C), lambda i, q: (my_chip(), i, 0))],
        compiler_params=_params(("parallel", "arbitrary"), 2 * tr * C * 8),
    )(grad, got)


def _scatter_chips_start(parts, fins, name):
    n = len(parts)

    def body(*refs):
        part, fin = refs[:n], refs[n:2 * n]
        send, recv = refs[2 * n], refs[2 * n + 1]
        token = refs[-1]
        x, y, c = _mesh_pos()
        mine = 2 * x + y
        for i in range(n):
            for k, (tx, ty) in enumerate([(1 - x, y), (x, 1 - y), (1 - x, 1 - y)]):
                pltpu.make_async_remote_copy(
                    src_ref=part[i].at[2 * tx + ty], dst_ref=fin[i].at[mine], send_sem=send.at[3 * i + k],
                    recv_sem=recv.at[3 * i + k], device_id=(tx, ty, c), device_id_type=_MESH_ID).start()
        token[...] = jnp.zeros_like(token)

    dma = pltpu.SemaphoreType.DMA
    out = pl.pallas_call(
        body, name=name,
        out_shape=(dma((3 * n,)), dma((3 * n,)), *_hbm_like(parts), *_hbm_like(fins), _TOKEN),
        in_specs=[_HBM] * (2 * n), out_specs=(_SEM, _SEM, *[_HBM] * (2 * n), _VMEM),
        input_output_aliases={i: 2 + i for i in range(2 * n)},
        compiler_params=pltpu.CompilerParams(has_side_effects=_EFFECT),
    )(*[_in_hbm(a) for a in parts], *[_in_hbm(a) for a in fins])
    return dict(send=out[0], recv=out[1], parts=list(out[2:2 + n]), fins=list(out[2 + n:2 + 2 * n]), token=out[-1])


def _scatter_chips_finish(st, after, name):
    parts, fins = st["parts"], st["fins"]
    n = len(parts)

    def body(*refs):
        part, fin = refs[:n], refs[n:2 * n]
        send, recv = refs[2 * n], refs[2 * n + 1]
        x, y, c = _mesh_pos()
        for i in range(n):
            for k, (tx, ty) in enumerate([(1 - x, y), (x, 1 - y), (1 - x, 1 - y)]):
                cp = pltpu.make_async_remote_copy(
                    src_ref=part[i].at[2 * tx + ty], dst_ref=fin[i].at[2 * tx + ty], send_sem=send.at[3 * i + k],
                    recv_sem=recv.at[3 * i + k], device_id=(tx, ty, c), device_id_type=_MESH_ID)
                cp.wait_recv()
                cp.wait_send()

    out = pl.pallas_call(
        body, name=name, out_shape=tuple(_hbm_like(parts) + _hbm_like(fins)),
        in_specs=[_HBM] * (2 * n) + [_SEM, _SEM, _ANY], out_specs=tuple([_HBM] * (2 * n)),
        input_output_aliases={i: i for i in range(2 * n)},
        compiler_params=pltpu.CompilerParams(has_side_effects=_EFFECT),
    )(*parts, *fins, st["send"], st["recv"], after)
    return list(out[n:])


def _adamw(g, w, m, v):
    m = ADAM_B1 * m + (1.0 - ADAM_B1) * g
    v = ADAM_B2 * v + (1.0 - ADAM_B2) * (g * g)
    m_hat = m / (1.0 - ADAM_B1 ** ADAM_STEP)
    v_hat = v / (1.0 - ADAM_B2 ** ADAM_STEP)
    delta = -ADAM_LR * (m_hat / (jnp.sqrt(v_hat) + ADAM_EPS) + ADAM_WD * w)
    return delta, m, v


def _adam_layer(fin, w3, m3, v3, l, prev, name):
    L, R, C = w3.shape
    tr = _blk(R, max(SUBLANES, (1 << 18) // C))

    def body(f_ref, w_ref, m_ref, v_ref, *rest):
        g_ref, d_ref, nm_ref, nv_ref = rest[-4:]
        g = ((f_ref[0].astype(_F32) + f_ref[1].astype(_F32)) + f_ref[2].astype(_F32)) + f_ref[3].astype(_F32)
        d, nm, nv = _adamw(g, w_ref[...], m_ref[...], v_ref[...])
        g_ref[...] = g
        d_ref[...] = d
        nm_ref[...] = nm
        nv_ref[...] = nv

    lay = pl.BlockSpec((None, tr, C), lambda i: (l, i, 0))
    ins = [fin, w3, m3, v3]
    in_specs = [pl.BlockSpec((4, tr, C), lambda i: (0, i, 0)), lay, lay, lay]
    aliases = {}
    if prev is not None:
        ins += list(prev)
        in_specs += [_ANY] * 4
        aliases = {4 + k: k for k in range(4)}
    return _pallas(
        body, name=name, out_shape=[jax.ShapeDtypeStruct((L, R, C), _F32)] * 4, grid=(R // tr,),
        in_specs=in_specs, out_specs=[lay] * 4, input_output_aliases=aliases,
        compiler_params=_params(("parallel",), 2 * tr * C * (4 * 2 + 7 * 4)),
    )(*ins)


def _sum_parts(parts, name):
    _, R, C = parts.shape

    def body(p_ref, o_ref):
        acc = p_ref[0]
        for k in range(1, N_DEV):
            acc = acc + p_ref[k]
        o_ref[...] = acc

    tr = _blk(R, 512)
    return _pallas(
        body, name=name, out_shape=jax.ShapeDtypeStruct((R, C), _F32), grid=(R // tr,),
        in_specs=[pl.BlockSpec((N_DEV, tr, C), lambda i: (0, i, 0))],
        out_specs=pl.BlockSpec((tr, C), lambda i: (i, 0)),
        compiler_params=_params(("parallel",), 2 * tr * C * 4 * 9),
    )(parts)


def _adam_flat(g, w, m, v, name):
    R, C = g.shape
    tr = _blk(R, 512)

    def body(g_ref, w_ref, m_ref, v_ref, d_ref, nm_ref, nv_ref):
        d, nm, nv = _adamw(g_ref[...], w_ref[...], m_ref[...], v_ref[...])
        d_ref[...] = d
        nm_ref[...] = nm
        nv_ref[...] = nv

    spec = pl.BlockSpec((tr, C), lambda i: (i, 0))
    return _pallas(
        body, name=name, out_shape=[jax.ShapeDtypeStruct((R, C), _F32)] * 3, grid=(R // tr,),
        in_specs=[spec] * 4, out_specs=[spec] * 3,
        compiler_params=_params(("parallel",), 2 * tr * C * 4 * 7),
    )(g, w, m, v)


def _pack(arrays):
    flat = jnp.concatenate([a.reshape(-1) for a in arrays])
    tile = SUBLANES * LANES
    pad = (-flat.shape[0]) % tile
    return jnp.pad(flat, (0, pad)).reshape(-1, LANES)


def _unpack(packed, shapes):
    flat = packed.reshape(-1)
    out, off = [], 0
    for s in shapes:
        n = math.prod(s)
        out.append(flat[off:off + n].reshape(s))
        off += n
    return out


def kernel(x, w_in, w_dw, b_dw, conv_ln_g, conv_ln_b, rpb, w_out, w_up, w_down, pre_mix_g, post_mix_g, pre_mlp_g, post_mlp_g, loss_target, m_w_in, m_w_dw, m_b_dw, m_conv_ln_g, m_conv_ln_b, m_rpb, m_w_out, m_w_up, m_w_down, m_pre_mix_g, m_post_mix_g, m_pre_mlp_g, m_post_mlp_g, v_w_in, v_w_dw, v_b_dw, v_conv_ln_g, v_conv_ln_b, v_rpb, v_w_out, v_w_up, v_w_down, v_pre_mix_g, v_post_mix_g, v_pre_mlp_g, v_post_mlp_g):
    _, T, D = x.shape
    L = w_in.shape[0]
    cw = b_dw.shape[1]
    H = rpb.shape[1]
    naw = H * HEAD_DIM
    ks = w_dw.shape[1]
    assert T % GRID_W == 0 and T // GRID_W >= WIN_ROWS and H % 2 == 0 and cw % LANES == 0
    assert rpb.shape[2:] == (N_DR, N_DC) and w_dw.shape[2] * N_DEV == cw and ks // 2 < CONV_HALO
    assert naw == cw and w_out.shape[1] * N_DEV == cw + naw and (T // GRID_W) % ROWS_PER_STEP_FWD == 0

    xs = x.reshape(T, D)
    tgt = loss_target.reshape(T, D)
    row = lambda p, l: (p, l)
    mx, my, mc = (lax.axis_index(a) for a in MESH_AXES)
    dev = 4 * mx + 2 * my + mc

    ks_pad = ks + (-ks) % SUBLANES
    wdw_pad = jnp.pad(w_dw, ((0, 0), (0, ks_pad - ks), (0, 0))).reshape(L * ks_pad, w_dw.shape[2])
    wdw_full = _all_gather([wdw_pad], [1], "ag_wdw")[0].reshape(L, ks_pad, cw)[:, :ks]

    big = (w_in, w_out, w_up, w_down)
    names = ("in", "out", "up", "down")

    big_axes = (1, 0, 1, 0)

    def gather_start(l, which, after):
        lands = [_cast_slot(big[k], l, big_axes[k] == 1, f"cast_{names[k]}") for k in which]
        return _gather_start(lands, [big_axes[k] for k in which], after, "gather_start_%d" % len(which))

    def gather_pair(l, after):
        g_in = gather_start(l, [0], after)
        return g_in, gather_start(l, [1, 2, 3], g_in["token"])

    saved = []
    xin = xs
    h = _norm_fwd(xs, row(pre_mix_g, 0), "norm_first")
    g_in, g_rest = gather_pair(0, wdw_full)
    g_in = _gather_forward(g_in, g_rest["token"], "gather_forward_1")
    Win = _gather_finish(g_in, g_in["token"], "gather_finish_1")[0]
    dy = loss_sum = None
    for l in range(L):
        nxt = gather_pair(l + 1, Win) if l + 1 < L else None
        proj = _matmul(h, Win, name="mm_proj")
        behind = [nxt[1]["token"]] if nxt else []
        if l > 0:
            g_rest = _gather_forward(g_rest, proj, "gather_forward_3")
            behind.append(g_rest["token"])
        c = _conv_fwd(proj, wdw_full, b_dw, l, cw, "conv_fwd", deps=tuple(behind))
        yc = _ln_silu_fwd(c, row(conv_ln_g, l), row(conv_ln_b, l), cw + naw, "ln_silu_fwd")
        bias_t, bias = _bias_table(rpb[l].reshape(H, N_DR * N_DC), "bias_table")
        ycat = _attn_fwd(proj, bias_t, yc, cw, naw, "attn_fwd")
        if l == 0:
            g_rest = _gather_forward(g_rest, ycat, "gather_forward_3")
        Wout, Wup, Wdown = _gather_finish(g_rest, ycat, "gather_finish_3")
        Ws = (Win, Wout, Wup, Wdown)
        mix = _matmul(ycat, Wout, name="mm_mix")
        x1, h2 = _resid_norm_fwd(xin, mix, row(post_mix_g, l), row(pre_mlp_g, l), "resid_mix")
        act, rl = _matmul(h2, Wup, epilogue="relu2", name="mm_up")
        if nxt:
            g_in = _gather_forward(nxt[0], act, "gather_forward_1")
        f = _matmul(act, Wdown, tm=256, tk=act.shape[1], n_outer=True, name="mm_down")
        saved.append(dict(xin=xin, h=h, W=Ws, proj=proj, c=c, bias=bias, ycat=ycat,
                          mix=mix, x1=x1, h2=h2, act=act, rl=rl, f=f))
        if nxt:
            Win = _gather_finish(g_in, f, "gather_finish_1")[0]
            g_rest = nxt[1]
            xin, h = _resid_norm_fwd(x1, f, row(post_mlp_g, l), row(pre_mix_g, l + 1), "resid_mlp")
        else:
            dy, loss_sum = _resid_loss(x1, f, row(post_mlp_g, l), tgt, "resid_loss")

    loss = lax.psum(loss_sum[0, 0] * (0.5 / D), MESH_AXES)

    small_grads = [None] * L
    big_out = [None] * 4
    moments = ((m_w_in, v_w_in), (m_w_out, v_w_out), (m_w_up, v_w_up), (m_w_down, v_w_down))

    def scatter_begin(grads, which, l):
        tag = "_%d" % len(which)
        return dict(st=_scatter_sibling_start(grads, "scatter_sibling_start" + tag), which=which, l=l, tag=tag)

    def scatter_mid(sc, after):
        grads, gots = _scatter_sibling_finish(sc["st"], after, "scatter_sibling_finish" + sc["tag"])
        pf = [_scatter_add(g, o, f"scatter_add_{k}") for k, g, o in zip(sc["which"], grads, gots)]
        st = _scatter_chips_start([p for p, _ in pf], [q for _, q in pf], "scatter_chips_start" + sc["tag"])
        return dict(sc, st=st)

    def scatter_end(sc, after):
        fins = _scatter_chips_finish(sc["st"], after, "scatter_chips_finish" + sc["tag"])
        for k, fin in zip(sc["which"], fins):
            big_out[k] = _adam_layer(fin, big[k], moments[k][0], moments[k][1], sc["l"], big_out[k],
                                     f"adam_{k}_{sc['l']}")

    dxo = dy
    pending = None
    for l in reversed(range(L)):
        s = saved[l]
        Win, Wout, Wup, Wdown = s["W"]
        last = l == 0
        tok = lambda sc: (sc["st"]["token"],) if sc is not None else ()
        d_f, dg_post_mlp = _norm_bwd(s["f"], row(post_mlp_g, l), dxo, None, _MXU, "norm_bwd_mlp", deps=tok(pending))
        d_up = _matmul(d_f, Wdown, tb=True, epilogue="mul2", extra=s["rl"], name="mm_d_up")
        if pending is not None:
            pending = scatter_mid(pending, d_up)
        dWdown = _matmul(s["act"], d_f, ta=True, out_dtype=_WIRE, tk=T, name="mm_dw_down").reshape(N_DEV, -1, D)
        d_h2 = _matmul(d_up, Wup, tb=True, tm=256, tk=d_up.shape[1], n_outer=True, name="mm_d_h2")
        dWup = _matmul(s["h2"], d_up, ta=True, out_dtype=_WIRE, out_cols=N_DEV, tk=T, name="mm_dw_up")
        sc_mlp = scatter_begin([dWup, dWdown], [2, 3], l) if last else None
        dx1, dg_pre_mlp = _norm_bwd(s["x1"], row(pre_mlp_g, l), d_h2, dxo, _F32, "norm_bwd_premlp", deps=tok(pending))
        d_mix, dg_post_mix = _norm_bwd(s["mix"], row(post_mix_g, l), dx1, None, _MXU, "norm_bwd_mix", deps=tok(sc_mlp))
        d_ycat = _matmul(d_mix, Wout, tb=True, name="mm_d_ycat")
        if last:
            sc_mlp = scatter_mid(sc_mlp, d_ycat)
        dWout = _matmul(s["ycat"], d_mix, ta=True, out_dtype=_WIRE, tk=T, name="mm_dw_out").reshape(N_DEV, -1, D)
        sc_out = scatter_begin([dWout], [1], l) if last else None
        dc, dlng, dlnb = _ln_silu_bwd(s["c"], row(conv_ln_g, l), row(conv_ln_b, l), d_ycat, "ln_silu_bwd",
                                      deps=tok(sc_mlp))
        da, dgate, dwb = _conv_bwd(s["proj"], dc, wdw_full, l, cw, "conv_bwd", deps=tok(sc_out))
        dq, dk, dv, gcls = _attn_bwd(s["proj"], s["bias"], d_ycat, cw, naw, "attn_bwd")
        if last:
            sc_out = scatter_mid(sc_out, gcls)
        drpb = _rpb_grad(gcls, "rpb_grad").reshape(H, 4 * SUBLANES, LANES)[:, :N_DC, :N_DR].transpose(0, 2, 1)
        dproj = jnp.concatenate([da, dgate, dq, dk, dv], axis=1)
        dh = _matmul(dproj, Win, tb=True, tm=512, tk=dproj.shape[1], n_outer=True, name="mm_d_h")
        dWin = _matmul(s["h"], dproj, ta=True, out_dtype=_WIRE, out_cols=N_DEV, tm=512, tk=T, name="mm_dw_in")
        dxo, dg_pre_mix = _norm_bwd(s["xin"], row(pre_mix_g, l), dh, dx1, _F32, "norm_bwd_premix", deps=tok(sc_out))
        if pending is not None:
            scatter_end(pending, dxo)
        if last:
            sc_in = scatter_begin([dWin], [0], l)
            sc_in = scatter_mid(sc_in, sc_in["st"]["token"])
            scatter_end(sc_mlp, sc_in["st"]["token"])
            scatter_end(sc_out, sc_in["st"]["token"])
            scatter_end(sc_in, sc_in["st"]["token"])
        else:
            pending = scatter_begin([dWin, dWout, dWup, dWdown], [0, 1, 2, 3], l)
        small_grads[l] = [dwb[ks], dlng[0], dlnb[0], drpb, dg_pre_mix[0], dg_post_mix[0], dg_pre_mlp[0],
                          dg_post_mlp[0], dwb[:ks]]

    rep_shapes = [(L, cw), (L, cw), (L, cw), (L, H, N_DR, N_DC), (L, D), (L, D), (L, D), (L, D)]
    stacked = [jnp.stack([small_grads[l][k] for l in range(L)]) for k in range(9)]
    packed = _pack(stacked)
    parts = _all_gather([packed], [0], "ag_small")[0].reshape(N_DEV, *packed.shape)
    gsum = _sum_parts(parts, "sum_small")
    g_small = _unpack(gsum, rep_shapes + [(L, ks, cw)])
    g_rep, g_wdw_full = g_small[:8], g_small[8]
    wsh = w_dw.shape[2]
    g_wdw = lax.dynamic_slice_in_dim(g_wdw_full, dev * wsh, wsh, axis=2)

    rep_w = [b_dw, conv_ln_g, conv_ln_b, rpb, pre_mix_g, post_mix_g, pre_mlp_g, post_mlp_g]
    rep_m = [m_b_dw, m_conv_ln_g, m_conv_ln_b, m_rpb, m_pre_mix_g, m_post_mix_g, m_pre_mlp_g, m_post_mlp_g]
    rep_v = [v_b_dw, v_conv_ln_g, v_conv_ln_b, v_rpb, v_pre_mix_g, v_post_mix_g, v_pre_mlp_g, v_post_mlp_g]
    rep_out = _adam_flat(_pack(g_rep), _pack(rep_w), _pack(rep_m), _pack(rep_v), "adam_small")
    rep_delta, rep_nm, rep_nv = (_unpack(o, rep_shapes) for o in rep_out)
    dw_out = _adam_flat(_pack([g_wdw]), _pack([w_dw]), _pack([m_w_dw]), _pack([v_w_dw]), "adam_wdw")
    wdw_delta, wdw_nm, wdw_nv = (_unpack(o, [w_dw.shape])[0] for o in dw_out)

    def assemble(kind_big, rep_list, wdw_val):
        return [big_out[0][kind_big], wdw_val, rep_list[0], rep_list[1], rep_list[2], rep_list[3],
                big_out[1][kind_big], big_out[2][kind_big], big_out[3][kind_big],
                rep_list[4], rep_list[5], rep_list[6], rep_list[7]]

    grads_out = assemble(0, g_rep, g_wdw)
    deltas = assemble(1, rep_delta, wdw_delta)
    new_m = assemble(2, rep_nm, wdw_nm)
    new_v = assemble(3, rep_nv, wdw_nv)
    return (loss, dxo.reshape(1, T, D), *grads_out, *deltas, *new_m, *new_v)
```

```python
import math

import jax
import jax.numpy as jnp
from jax import lax
from jax.experimental import pallas as pl
from jax.experimental.pallas import tpu as pltpu

_MXU = jnp.bfloat16
_WIRE = jnp.bfloat16
_F32 = jnp.float32

N_DEV = 8
GRID_W = 64
WIN_ROWS = 8
WIN_COLS = 16
HEAD_DIM = 64
LANES = 128
MXU_COLS_V7X = 256
SUBLANES = 8
RMS_EPS = 1e-6
LN_EPS = 1e-5
NEG_INF = -1e30
ADAM_LR = 0.001
ADAM_B1 = 0.9
ADAM_B2 = 0.999
ADAM_EPS = 1e-08
ADAM_WD = 0.01
ADAM_STEP = 10
VMEM_BYTES_V7X = 64 << 20
VMEM_RESERVE = 12 << 20
MESH_AXES = ("x", "y", "c")


def _vmem_limit(block_bytes):
    return int(min(max(block_bytes + (8 << 20), 24 << 20), VMEM_BYTES_V7X - VMEM_RESERVE))


def _blk(n, pref):
    if n <= pref:
        return n
    for t in range(pref, 7, -1):
        if n % t == 0 and t % SUBLANES == 0:
            return t
    return n


def _sigmoid(v):
    return 1.0 / (1.0 + jnp.exp(-v))


def _params(sem, nbytes):
    return pltpu.CompilerParams(dimension_semantics=sem, vmem_limit_bytes=_vmem_limit(nbytes))


def _pallas(body, deps=(), **kw):
    n_in = len(kw["in_specs"])
    if deps:
        kw["in_specs"] = list(kw["in_specs"]) + [pl.BlockSpec(memory_space=pl.ANY)] * len(deps)
        inner = body

        def body(*refs):
            return inner(*refs[:n_in], *refs[n_in + len(deps):])

    call = pl.pallas_call(body, **kw)

    def run(*operands):
        return call(*[pltpu.with_memory_space_constraint(o, pltpu.HBM)
                      if jnp.issubdtype(o.dtype, jnp.floating) else o for o in (*operands, *deps)])

    return run


def _my_block():
    x, y, c = (lax.axis_index(a) for a in MESH_AXES)
    return 4 * x + 2 * y + c


def _cast_slot(w3, l, by_cols, name):
    _, R, C = w3.shape
    tr = _blk(R, 512)

    def body(w_ref, o_ref):
        o_ref[...] = w_ref[...].astype(o_ref.dtype)

    if by_cols:
        shape, o_spec = (R, N_DEV * C), pl.BlockSpec((tr, C), lambda i: (i, _my_block()))
    else:
        shape, o_spec = (N_DEV * R, C), pl.BlockSpec((tr, C), lambda i: (_my_block() * (R // tr) + i, 0))
    return _pallas(
        body, name=name, out_shape=jax.ShapeDtypeStruct(shape, _WIRE), grid=(R // tr,),
        in_specs=[pl.BlockSpec((None, tr, C), lambda i: (l, i, 0))], out_specs=o_spec,
        compiler_params=_params(("parallel",), 2 * tr * C * 6),
    )(w3)


def _layer_row(param):
    arr, l = param
    L, W = arr.shape
    return arr.reshape(L, 1, W), pl.BlockSpec((None, 1, W), lambda *_: (l, 0, 0))


def _norm_fwd(x, g, name):
    T, D = x.shape
    tm = _blk(T, 256)
    g, g_spec = _layer_row(g)

    def body(x_ref, g_ref, h_ref):
        xv = x_ref[...]
        r = lax.rsqrt(jnp.mean(xv * xv, axis=-1, keepdims=True) + RMS_EPS)
        h_ref[...] = (xv * r * g_ref[...]).astype(h_ref.dtype)

    return _pallas(
        body, name=name, out_shape=jax.ShapeDtypeStruct((T, D), _MXU), grid=(T // tm,),
        in_specs=[pl.BlockSpec((tm, D), lambda i: (i, 0)), g_spec],
        out_specs=pl.BlockSpec((tm, D), lambda i: (i, 0)),
        compiler_params=_params(("parallel",), 2 * tm * D * 6),
    )(x, g)


def _resid_norm_fwd(xres, y, g_post, g_next, name):
    T, D = xres.shape
    tm = _blk(T, 256)

    def body(x_ref, y_ref, gp_ref, gn_ref, xn_ref, h_ref):
        yv = y_ref[...]
        r = lax.rsqrt(jnp.mean(yv * yv, axis=-1, keepdims=True) + RMS_EPS)
        xn = x_ref[...] + yv * r * gp_ref[...]
        xn_ref[...] = xn
        r2 = lax.rsqrt(jnp.mean(xn * xn, axis=-1, keepdims=True) + RMS_EPS)
        h_ref[...] = (xn * r2 * gn_ref[...]).astype(h_ref.dtype)

    row = pl.BlockSpec((tm, D), lambda i: (i, 0))
    (g_post, gp_spec), (g_next, gn_spec) = _layer_row(g_post), _layer_row(g_next)
    return _pallas(
        body, name=name,
        out_shape=(jax.ShapeDtypeStruct((T, D), _F32), jax.ShapeDtypeStruct((T, D), _MXU)),
        grid=(T // tm,), in_specs=[row, row, gp_spec, gn_spec], out_specs=(row, row),
        compiler_params=_params(("parallel",), 2 * tm * D * 14),
    )(xres, y, g_post, g_next)


def _resid_loss(xres, y, g_post, target, name):
    T, D = xres.shape
    tm = _blk(T, 256)

    def body(x_ref, y_ref, gp_ref, t_ref, dy_ref, loss_ref):
        yv = y_ref[...]
        r = lax.rsqrt(jnp.mean(yv * yv, axis=-1, keepdims=True) + RMS_EPS)
        err = x_ref[...] + yv * r * gp_ref[...] - t_ref[...]
        dy_ref[...] = err * (1.0 / D)

        @pl.when(pl.program_id(0) == 0)
        def _():
            loss_ref[...] = jnp.zeros_like(loss_ref)

        part = jnp.sum(jnp.sum(err * err, axis=-1, keepdims=True), axis=0, keepdims=True)
        loss_ref[...] += part

    row = pl.BlockSpec((tm, D), lambda i: (i, 0))
    g_post, gp_spec = _layer_row(g_post)
    return _pallas(
        body, name=name,
        out_shape=(jax.ShapeDtypeStruct((T, D), _F32), jax.ShapeDtypeStruct((1, 1), _F32)),
        grid=(T // tm,), in_specs=[row, row, gp_spec, row],
        out_specs=(row, pl.BlockSpec((1, 1), lambda i: (0, 0))),
        compiler_params=_params(("arbitrary",), 2 * tm * D * 16),
    )(xres, y, g_post, target)


def _norm_bwd(y, g, dout, dres, out_dtype, name, deps=()):
    T, D = y.shape
    tm = _blk(T, 256)
    nsteps = T // tm
    has_res = dres is not None

    def body(*refs):
        if has_res:
            y_ref, g_ref, do_ref, dr_ref, dy_ref, dg_ref, acc = refs
        else:
            y_ref, g_ref, do_ref, dy_ref, dg_ref, acc = refs
        i = pl.program_id(0)
        yv = y_ref[...]
        do = do_ref[...]
        r = lax.rsqrt(jnp.mean(yv * yv, axis=-1, keepdims=True) + RMS_EPS)
        gy = do * g_ref[...]
        dot = jnp.mean(yv * gy, axis=-1, keepdims=True)
        dy = r * gy - yv * (r * r * r * dot)
        if has_res:
            dy = dy + dr_ref[...]
        dy_ref[...] = dy.astype(dy_ref.dtype)

        @pl.when(i == 0)
        def _():
            acc[...] = jnp.zeros_like(acc)

        acc[...] += jnp.sum((do * yv * r).reshape(tm // SUBLANES, SUBLANES, D), axis=0)

        @pl.when(i == nsteps - 1)
        def _():
            dg_ref[...] = jnp.sum(acc[...], axis=0, keepdims=True)

    row = pl.BlockSpec((tm, D), lambda i: (i, 0))
    vec = pl.BlockSpec((1, D), lambda i: (0, 0))
    g, g_spec = _layer_row(g)
    ins = [y, g, dout] + ([dres] if has_res else [])
    in_specs = [row, g_spec, row] + ([row] if has_res else [])
    return _pallas(
        body, name=name, deps=deps,
        out_shape=(jax.ShapeDtypeStruct((T, D), out_dtype), jax.ShapeDtypeStruct((1, D), _F32)),
        grid=(nsteps,), in_specs=in_specs, out_specs=(row, vec),
        scratch_shapes=[pltpu.VMEM((SUBLANES, D), _F32)],
        compiler_params=_params(("arbitrary",), 2 * tm * D * 16),
    )(*ins)


def _matmul(a, b, *, ta=False, tb=False, out_dtype=_F32, epilogue=None, extra=None, out_cols=0,
            tm=1024, tk=2048, n_outer=False, name):
    M, K = (a.shape[1], a.shape[0]) if ta else a.shape
    N = b.shape[0] if tb else b.shape[1]
    tm = _blk(M, tm)
    width = N // out_cols if out_cols else 0
    per_step = 2 if (out_cols and width % MXU_COLS_V7X and out_cols % 2 == 0) else 1
    tn = per_step * width if out_cols else _blk(N, 1024)
    tk = _blk(K, tk)
    ij = (lambda g0, g1: (g1, g0)) if n_outer else (lambda g0, g1: (g0, g1))
    b_spec = (pl.BlockSpec((tn, tk), lambda g0, g1, k: (ij(g0, g1)[1], k)) if tb
              else pl.BlockSpec((tk, tn), lambda g0, g1, k: (k, ij(g0, g1)[1])))
    nk = K // tk
    a_spec = (pl.BlockSpec((tk, tm), lambda g0, g1, k: (k, ij(g0, g1)[0])) if ta
              else pl.BlockSpec((tm, tk), lambda g0, g1, k: (ij(g0, g1)[0], k)))
    if out_cols:
        assert epilogue is None
        o_spec = pl.BlockSpec((per_step, tm, width), lambda g0, g1, k: (ij(g0, g1)[1], ij(g0, g1)[0], 0))
        o_shape = (out_cols, M, width)
    else:
        o_spec = pl.BlockSpec((tm, tn), lambda g0, g1, k: ij(g0, g1))
        o_shape = (M, N)
    dims = (((0 if ta else 1,), (1 if tb else 0,)), ((), ()))
    n_extra = 1 if epilogue == "mul2" else 0
    n_out = 2 if epilogue == "relu2" else 1

    def finish(acc, extra_refs, out_refs):
        if out_cols:
            for cblk in range(per_step):
                out_refs[0][cblk] = acc[:, cblk * width:(cblk + 1) * width].astype(out_refs[0].dtype)
        elif epilogue is None:
            out_refs[0][...] = acc.astype(out_refs[0].dtype)
        elif epilogue == "relu2":
            rl = jnp.maximum(acc, 0.0)
            out_refs[0][...] = (rl * rl).astype(out_refs[0].dtype)
            out_refs[1][...] = rl.astype(out_refs[1].dtype)
        else:
            out_refs[0][...] = (acc * (2.0 * extra_refs[0][...].astype(_F32))).astype(out_refs[0].dtype)

    def body(a_ref, b_ref, *rest):
        extra_refs = rest[:n_extra]
        out_refs = rest[n_extra:n_extra + n_out]
        part = lax.dot_general(a_ref[...], b_ref[...], dims, preferred_element_type=_F32)
        if nk == 1:
            finish(part, extra_refs, out_refs)
            return
        acc = rest[-1]
        k = pl.program_id(2)

        @pl.when(k == 0)
        def _():
            acc[...] = part

        @pl.when(k > 0)
        def _():
            acc[...] += part

        @pl.when(k == nk - 1)
        def _():
            finish(acc[...], extra_refs, out_refs)

    if epilogue == "relu2":
        out_shape = (jax.ShapeDtypeStruct((M, N), _MXU), jax.ShapeDtypeStruct((M, N), _MXU))
        out_specs = (o_spec, o_spec)
        out_bytes = 2 * tm * tn * 2
    else:
        odt = _MXU if epilogue == "mul2" else out_dtype
        out_shape = jax.ShapeDtypeStruct(o_shape, odt)
        out_specs = o_spec
        out_bytes = tm * tn * jnp.dtype(odt).itemsize
    in_specs = [a_spec, b_spec] + ([o_spec] if n_extra else [])
    ins = [a, b] + ([extra] if n_extra else [])
    blocks = 2 * (tm * tk * 2 + tk * tn * 2 + out_bytes + n_extra * tm * tn * 2) + tm * tn * 4 * 2
    return _pallas(
        body, name=name, out_shape=out_shape,
        grid=(N // tn, M // tm, nk) if n_outer else (M // tm, N // tn, nk),
        in_specs=in_specs, out_specs=out_specs,
        scratch_shapes=[pltpu.VMEM((tm, tn), _F32)] if nk > 1 else [],
        compiler_params=_params(("parallel", "parallel", "arbitrary"), blocks),
    )(*ins)


CONV_HALO = 16
CONV_CHUNK = 256


def _tap_windows(win, n_taps_plus1, tc):
    n = win.shape[0]
    for s in range(SUBLANES):
        shifted = win if s == 0 else pltpu.roll(win, n - s, 0)
        for q in range((n_taps_plus1 + SUBLANES - 1) // SUBLANES):
            o = SUBLANES * q + s
            if 1 <= o < n_taps_plus1:
                yield o, shifted[SUBLANES * q:SUBLANES * q + tc, :]


def _conv_fwd(proj, wdw, bdw, l, cw, name, deps=()):
    T = proj.shape[0]
    ks = wdw.shape[1]
    cb = LANES
    tc = _blk(T, CONV_CHUNK)
    nblk = cw // cb

    def body(a_ref, g_ref, w_ref, b_ref, c_ref, upad):
        zeros = jnp.zeros((CONV_HALO, cb), _F32)
        upad[0:CONV_HALO, :] = zeros
        upad[T + CONV_HALO:T + 2 * CONV_HALO, :] = zeros
        upad[CONV_HALO:T + CONV_HALO, :] = a_ref[...] * _sigmoid(g_ref[...])

        def chunk(i, carry):
            t0 = pl.multiple_of(i * tc, tc)
            win = upad[pl.ds(t0, tc + 2 * CONV_HALO), :]
            acc = jnp.broadcast_to(b_ref[...], (tc, cb))
            for o, rows in _tap_windows(win, ks + 1, tc):
                j = o + ks // 2 - CONV_HALO
                acc = acc + rows * w_ref[j:j + 1, :]
            c_ref[pl.ds(t0, tc), :] = acc
            return carry

        lax.fori_loop(0, T // tc, chunk, 0)

    col = lambda off: pl.BlockSpec((T, cb), lambda i, off=off: (0, off + i))
    return _pallas(
        body, name=name, deps=deps, out_shape=jax.ShapeDtypeStruct((T, cw), _F32), grid=(nblk,),
        in_specs=[col(0), col(nblk), pl.BlockSpec((None, ks, cb), lambda i: (l, 0, i)),
                  pl.BlockSpec((None, 1, cb), lambda i: (l, 0, i))],
        out_specs=pl.BlockSpec((T, cb), lambda i: (0, i)),
        scratch_shapes=[pltpu.VMEM((T + 2 * CONV_HALO, cb), _F32)],
        compiler_params=_params(("parallel",), 2 * T * cb * 4 * 3 + T * cb * 4),
    )(proj, proj, wdw, bdw.reshape(bdw.shape[0], 1, cw))


def _ln_silu_fwd(c, lng, lnb, out_cols, name):
    T, cw = c.shape
    tm = _blk(T, 512)

    def body(c_ref, g_ref, b_ref, y_ref):
        cv = c_ref[...]
        mu = jnp.mean(cv, axis=-1, keepdims=True)
        xc = cv - mu
        var = jnp.mean(xc * xc, axis=-1, keepdims=True)
        z = xc * lax.rsqrt(var + LN_EPS) * g_ref[...] + b_ref[...]
        y_ref[...] = (z * _sigmoid(z)).astype(y_ref.dtype)

    row = pl.BlockSpec((tm, cw), lambda i: (i, 0))
    (lng, g_spec), (lnb, b_spec) = _layer_row(lng), _layer_row(lnb)
    return _pallas(
        body, name=name, out_shape=jax.ShapeDtypeStruct((T, out_cols), _MXU), grid=(T // tm,),
        in_specs=[row, g_spec, b_spec], out_specs=row,
        compiler_params=_params(("parallel",), 2 * tm * cw * 6),
    )(c, lng, lnb)


def _ln_silu_bwd(c, lng, lnb, dycat, name, deps=()):
    T, cw = c.shape
    tm = _blk(T, 512)
    nsteps = T // tm

    def body(c_ref, g_ref, b_ref, dy_ref, dc_ref, dg_ref, db_ref, accg, accb):
        i = pl.program_id(0)
        cv = c_ref[...]
        mu = jnp.mean(cv, axis=-1, keepdims=True)
        xc = cv - mu
        var = jnp.mean(xc * xc, axis=-1, keepdims=True)
        rstd = lax.rsqrt(var + LN_EPS)
        xhat = xc * rstd
        z = xhat * g_ref[...] + b_ref[...]
        sg = _sigmoid(z)
        dz = dy_ref[...] * (sg * (1.0 + z * (1.0 - sg)))
        dxh = dz * g_ref[...]
        m1 = jnp.mean(dxh, axis=-1, keepdims=True)
        m2 = jnp.mean(dxh * xhat, axis=-1, keepdims=True)
        dc_ref[...] = rstd * (dxh - m1 - xhat * m2)

        @pl.when(i == 0)
        def _():
            accg[...] = jnp.zeros_like(accg)
            accb[...] = jnp.zeros_like(accb)

        accg[...] += jnp.sum((dz * xhat).reshape(tm // SUBLANES, SUBLANES, cw), axis=0)
        accb[...] += jnp.sum(dz.reshape(tm // SUBLANES, SUBLANES, cw), axis=0)

        @pl.when(i == nsteps - 1)
        def _():
            dg_ref[...] = jnp.sum(accg[...], axis=0, keepdims=True)
            db_ref[...] = jnp.sum(accb[...], axis=0, keepdims=True)

    row = pl.BlockSpec((tm, cw), lambda i: (i, 0))
    vec = pl.BlockSpec((1, cw), lambda i: (0, 0))
    (lng, g_spec), (lnb, b_spec) = _layer_row(lng), _layer_row(lnb)
    return _pallas(
        body, name=name, deps=deps,
        out_shape=(jax.ShapeDtypeStruct((T, cw), _F32), jax.ShapeDtypeStruct((1, cw), _F32),
                   jax.ShapeDtypeStruct((1, cw), _F32)),
        grid=(nsteps,), in_specs=[row, g_spec, b_spec, row], out_specs=(row, vec, vec),
        scratch_shapes=[pltpu.VMEM((SUBLANES, cw), _F32), pltpu.VMEM((SUBLANES, cw), _F32)],
        compiler_params=_params(("arbitrary",), 2 * tm * cw * 12),
    )(c, lng, lnb, dycat)


def _conv_bwd(proj, dc, wdw, l, cw, name, deps=()):
    T = proj.shape[0]
    ks = wdw.shape[1]
    cb = LANES
    tc = _blk(T, CONV_CHUNK)
    nblk = cw // cb
    half = ks // 2

    def body(a_ref, g_ref, dc_ref, w_ref, da_ref, dg_ref, dwb_ref, upad, dpad, du, wacc):
        zeros = jnp.zeros((CONV_HALO, cb), _F32)
        for pad in (upad, dpad):
            pad[0:CONV_HALO, :] = zeros
            pad[T + CONV_HALO:T + 2 * CONV_HALO, :] = zeros
        sg = _sigmoid(g_ref[...])
        upad[CONV_HALO:T + CONV_HALO, :] = a_ref[...] * sg
        dpad[CONV_HALO:T + CONV_HALO, :] = dc_ref[...]
        wacc[...] = jnp.zeros_like(wacc)

        def chunk(i, carry):
            t0 = pl.multiple_of(i * tc, tc)
            dwin = dpad[pl.ds(t0, tc + 2 * CONV_HALO), :]
            uwin = upad[pl.ds(t0, tc + 2 * CONV_HALO), :]
            dcc = dwin[CONV_HALO:CONV_HALO + tc, :]
            acc = jnp.zeros((tc, cb), _F32)
            for o, rows in _tap_windows(dwin, CONV_HALO + half + 1, tc):
                j = CONV_HALO + half - o
                if 0 <= j < ks:
                    acc = acc + rows * w_ref[j:j + 1, :]
            du[pl.ds(t0, tc), :] = acc
            for o, rows in _tap_windows(uwin, CONV_HALO + half + 1, tc):
                j = o + half - CONV_HALO
                if 0 <= j < ks:
                    wacc[j] += jnp.sum((rows * dcc).reshape(tc // SUBLANES, SUBLANES, cb), axis=0)
            wacc[ks] += jnp.sum(dcc.reshape(tc // SUBLANES, SUBLANES, cb), axis=0)
            return carry

        lax.fori_loop(0, T // tc, chunk, 0)
        duv = du[...]
        av = a_ref[...]
        da_ref[...] = (duv * sg).astype(da_ref.dtype)
        dg_ref[...] = (duv * av * sg * (1.0 - sg)).astype(dg_ref.dtype)
        dwb_ref[...] = jnp.sum(wacc[...], axis=1)

    col = lambda off: pl.BlockSpec((T, cb), lambda i, off=off: (0, off + i))
    blk = pl.BlockSpec((T, cb), lambda i: (0, i))
    return _pallas(
        body, name=name, deps=deps,
        out_shape=(jax.ShapeDtypeStruct((T, cw), _MXU), jax.ShapeDtypeStruct((T, cw), _MXU),
                   jax.ShapeDtypeStruct((ks + 1, cw), _F32)),
        grid=(nblk,),
        in_specs=[col(0), col(nblk), blk, pl.BlockSpec((None, ks, cb), lambda i: (l, 0, i))],
        out_specs=(blk, blk, pl.BlockSpec((ks + 1, cb), lambda i: (0, i))),
        scratch_shapes=[pltpu.VMEM((T + 2 * CONV_HALO, cb), _F32), pltpu.VMEM((T + 2 * CONV_HALO, cb), _F32),
                        pltpu.VMEM((T, cb), _F32), pltpu.VMEM((ks + 1, SUBLANES, cb), _F32)],
        compiler_params=_params(("parallel",), 2 * T * cb * 4 * 4 + 3 * T * cb * 4),
    )(proj, proj, dc, wdw)


N_CLS = WIN_ROWS
N_DR = 2 * WIN_ROWS - 1
N_DC = 2 * WIN_COLS - 1
BAND = WIN_ROWS * GRID_W
QK_SCALE = HEAD_DIM ** -0.5
ROWS_PER_STEP_FWD = 8
ROWS_PER_STEP_BWD = 4
_NT = (((1,), (1,)), ((), ()))
_TN = (((0,), (0,)), ((), ()))


def _slab_iotas():
    wk = lax.broadcasted_iota(jnp.int32, (GRID_W, LANES), 0)
    lane = lax.broadcasted_iota(jnp.int32, (GRID_W, LANES), 1)
    wq = jnp.bitwise_and(lane, GRID_W - 1)
    head1 = lane >= GRID_W
    d = wk - wq + (WIN_COLS - 1)
    cs = jnp.clip(wq - WIN_COLS // 2, 0, GRID_W - WIN_COLS)
    window = (wk >= cs) & (wk < cs + WIN_COLS)
    return d, head1, window


def _pair_tiles(cls):
    return [(cls + 2 * p, p) for p in range(WIN_ROWS // 2)]


def _bias_table(rpb2, name):
    npair = rpb2.shape[0] // 2

    def body(rpb_ref, ot_ref, on_ref):
        p = pl.program_id(0)
        d, head1, window = _slab_iotas()
        slabs = []
        for dr in range(N_DR):
            val = jnp.zeros((GRID_W, LANES), _F32)
            for j in range(N_DC):
                s0 = rpb_ref[2 * p, dr * N_DC + j]
                s1 = rpb_ref[2 * p + 1, dr * N_DC + j]
                val = jnp.where(d == j, jnp.where(head1, s1, s0), val)
            slab = jnp.where(window, val, NEG_INF)
            slabs.append(slab)
            for cls in range(N_CLS):
                k = dr - cls
                if 0 <= k < WIN_ROWS:
                    ot_ref[cls, k * GRID_W:(k + 1) * GRID_W, :] = slab
        for e in range(N_DR - 1):
            tile = jnp.concatenate([slabs[e], slabs[e + 1]], axis=0).T
            for cls in range(N_CLS):
                for ee, pp in _pair_tiles(cls):
                    if ee == e:
                        on_ref[cls, :, pp * LANES:(pp + 1) * LANES] = tile

    return pl.pallas_call(
        body, name=name,
        out_shape=(jax.ShapeDtypeStruct((npair, N_CLS, BAND, LANES), _F32),
                   jax.ShapeDtypeStruct((npair, N_CLS, LANES, BAND), _F32)),
        grid=(npair,),
        in_specs=[pl.BlockSpec(memory_space=pltpu.SMEM)],
        out_specs=(pl.BlockSpec((None, N_CLS, BAND, LANES), lambda p: (p, 0, 0, 0)),
                   pl.BlockSpec((None, N_CLS, LANES, BAND), lambda p: (p, 0, 0, 0))),
        compiler_params=_params(("arbitrary",), 4 * N_CLS * BAND * LANES * 4),
    )(rpb2)


def _rpb_grad(gc, name):
    npair = gc.shape[0]

    def body(g_ref, o_ref):
        d, _, _ = _slab_iotas()
        rowi = lax.broadcasted_iota(jnp.int32, (4 * SUBLANES, LANES), 0)
        lanei = lax.broadcasted_iota(jnp.int32, (4 * SUBLANES, LANES), 1)
        head1 = lax.broadcasted_iota(jnp.int32, (1, LANES), 1) >= GRID_W
        tiles = [jnp.zeros((4 * SUBLANES, LANES), _F32) for _ in range(2)]
        yts = []
        for e in range(N_DR - 1):
            y = jnp.zeros((LANES, LANES), _F32)
            for cls in range(N_CLS):
                for ee, pp in _pair_tiles(cls):
                    if ee == e:
                        y = y + g_ref[cls, :, pp * LANES:(pp + 1) * LANES]
            yts.append(y.T)
        for dr in range(N_DR):
            ysum = jnp.zeros((GRID_W, LANES), _F32)
            if dr < N_DR - 1:
                ysum = ysum + yts[dr][:GRID_W]
            if dr >= 1:
                ysum = ysum + yts[dr - 1][GRID_W:]
            for j in range(N_DC):
                cs = jnp.sum(jnp.where(d == j, ysum, 0.0), axis=0, keepdims=True)
                s0 = jnp.sum(jnp.where(head1, 0.0, cs), axis=1, keepdims=True)
                s1 = jnp.sum(jnp.where(head1, cs, 0.0), axis=1, keepdims=True)
                here = (rowi == j) & (lanei == dr)
                tiles[0] = tiles[0] + jnp.where(here, s0, 0.0)
                tiles[1] = tiles[1] + jnp.where(here, s1, 0.0)
        o_ref[0] = tiles[0]
        o_ref[1] = tiles[1]

    return _pallas(
        body, name=name, out_shape=jax.ShapeDtypeStruct((npair, 2, 4 * SUBLANES, LANES), _F32), grid=(npair,),
        in_specs=[pl.BlockSpec((None, N_CLS, LANES, BAND), lambda p: (p, 0, 0, 0))],
        out_specs=pl.BlockSpec((None, 2, 4 * SUBLANES, LANES), lambda p: (p, 0, 0, 0)),
        compiler_params=_params(("parallel",), 2 * N_CLS * BAND * LANES * 4),
    )(gc)


def _block_diag(v, diag):
    return jnp.where(diag, jnp.concatenate([v, v], axis=0), 0.0).astype(_MXU)


def _diag_mask():
    r = lax.broadcasted_iota(jnp.int32, (LANES, LANES), 0) < GRID_W
    c = lax.broadcasted_iota(jnp.int32, (LANES, LANES), 1) < HEAD_DIM
    return r == c


def _row_geometry(r, rows):
    rs = jnp.clip(r - WIN_ROWS // 2, 0, rows - WIN_ROWS)
    cls = rs - r + (WIN_ROWS - 1)
    return pl.multiple_of(r * GRID_W, GRID_W), pl.multiple_of(rs * GRID_W, GRID_W), cls


def _probs_t(qsel, kband, bias):
    s = lax.dot_general(kband, qsel, _NT, preferred_element_type=_F32) + bias
    mx = jnp.max(s, axis=0, keepdims=True)
    e = jnp.exp(s - mx)
    return e * (1.0 / jnp.sum(e, axis=0, keepdims=True))


def _attn_fwd(proj, bias, ycat, cw, naw, name):
    T = proj.shape[0]
    rows = T // GRID_W
    npair = naw // LANES
    qoff, koff, voff = 2 * cw // LANES, (2 * cw + naw) // LANES, (2 * cw + 2 * naw) // LANES

    def body(q_ref, k_ref, v_ref, b_ref, ycat_ref, o_ref, kb, vb):
        kb[...] = k_ref[...].astype(_MXU)
        vb[...] = v_ref[...].astype(_MXU)
        diag = _diag_mask()
        m0 = lax.broadcasted_iota(jnp.int32, (GRID_W, LANES), 1) < HEAD_DIM

        def step(i, carry):
            us = range(ROWS_PER_STEP_FWD)
            geo = [_row_geometry(ROWS_PER_STEP_FWD * i + u, rows) for u in us]
            qsel = [_block_diag(q_ref[pl.ds(t0, GRID_W), :] * QK_SCALE, diag) for t0, _, _ in geo]
            kbands = [kb[pl.ds(b0, BAND), :] for _, b0, _ in geo]
            vbands = [vb[pl.ds(b0, BAND), :] for _, b0, _ in geo]
            biases = [b_ref[cls] for _, _, cls in geo]
            pts = [_probs_t(qsel[u], kbands[u], biases[u]) for u in us]
            ofs = [lax.dot_general(pts[u].astype(_MXU), vbands[u], _TN, preferred_element_type=_F32) for u in us]
            for u in us:
                o_ref[pl.ds(geo[u][0], GRID_W), :] = jnp.where(m0, ofs[u][:GRID_W], ofs[u][GRID_W:]).astype(o_ref.dtype)
            return carry

        lax.fori_loop(0, rows // ROWS_PER_STEP_FWD, step, 0)

    col = lambda off: pl.BlockSpec((T, LANES), lambda i, off=off: (0, off + i))
    return _pallas(
        body, name=name, out_shape=jax.ShapeDtypeStruct(ycat.shape, ycat.dtype), grid=(npair,),
        in_specs=[col(qoff), col(koff), col(voff),
                  pl.BlockSpec((None, N_CLS, BAND, LANES), lambda i: (i, 0, 0, 0)), _ANY],
        out_specs=pl.BlockSpec((T, LANES), lambda i: (0, cw // LANES + i)),
        input_output_aliases={4: 0},
        scratch_shapes=[pltpu.VMEM((T, LANES), _MXU), pltpu.VMEM((T, LANES), _MXU)],
        compiler_params=_params(("parallel",), 2 * (3 * T * LANES * 4 + N_CLS * BAND * LANES * 4 + T * LANES * 2)),
    )(proj, proj, proj, bias, ycat)


def _attn_bwd(proj, bias, dycat, cw, naw, name):
    T = proj.shape[0]
    rows = T // GRID_W
    npair = naw // LANES
    qoff, koff, voff = 2 * cw // LANES, (2 * cw + naw) // LANES, (2 * cw + 2 * naw) // LANES
    doff = cw // LANES

    def body(q_ref, k_ref, v_ref, b_ref, do_ref, dq_ref, dk_ref, dv_ref, g_ref, kb, vb, dka, dva):
        kb[...] = k_ref[...].astype(_MXU)
        vb[...] = v_ref[...].astype(_MXU)
        dka[...] = jnp.zeros_like(dka)
        dva[...] = jnp.zeros_like(dva)
        g_ref[...] = jnp.zeros_like(g_ref)
        diag = _diag_mask()
        m0 = lax.broadcasted_iota(jnp.int32, (GRID_W, LANES), 1) < HEAD_DIM

        def step(i, carry):
            us = range(ROWS_PER_STEP_BWD)
            geo = [_row_geometry(ROWS_PER_STEP_BWD * i + u, rows) for u in us]
            qsel = [_block_diag(q_ref[pl.ds(t0, GRID_W), :] * QK_SCALE, diag) for t0, _, _ in geo]
            dosel = [_block_diag(do_ref[pl.ds(t0, GRID_W), :], diag) for t0, _, _ in geo]
            kbands = [kb[pl.ds(b0, BAND), :] for _, b0, _ in geo]
            vbands = [vb[pl.ds(b0, BAND), :] for _, b0, _ in geo]
            biases = [b_ref[cls] for _, _, cls in geo]
            dsts, dqs, dks, dvs = [], [], [], []
            for u in us:
                sc = lax.dot_general(qsel[u], kbands[u], _NT, preferred_element_type=_F32) + biases[u]
                ex = jnp.exp(sc - jnp.max(sc, axis=1, keepdims=True))
                p = ex * (1.0 / jnp.sum(ex, axis=1, keepdims=True))
                dp = lax.dot_general(dosel[u], vbands[u], _NT, preferred_element_type=_F32)
                delta = jnp.sum(p * dp, axis=1, keepdims=True)
                dst = p * (dp - delta)
                dsb = dst.astype(_MXU)
                dqf = jnp.dot(dsb, kbands[u], preferred_element_type=_F32)
                dsts.append(dst)
                dqs.append((jnp.where(m0, dqf[:GRID_W], dqf[GRID_W:]) * QK_SCALE).astype(dq_ref.dtype))
                dks.append(lax.dot_general(qsel[u], dsb, _TN, preferred_element_type=_F32).T)
                dvs.append(lax.dot_general(dosel[u], p.astype(_MXU), _TN, preferred_element_type=_F32).T)
            for u in us:
                t0, b0, cls = geo[u]
                g_ref[cls] += dsts[u]
                dq_ref[pl.ds(t0, GRID_W), :] = dqs[u]
                dka[pl.ds(b0, BAND), :] += dks[u]
                dva[pl.ds(b0, BAND), :] += dvs[u]
            return carry

        lax.fori_loop(0, rows // ROWS_PER_STEP_BWD, step, 0)
        dk_ref[...] = dka[...].astype(dk_ref.dtype)
        dv_ref[...] = dva[...].astype(dv_ref.dtype)

    col = lambda off: pl.BlockSpec((T, LANES), lambda i, off=off: (0, off + i))
    blk = pl.BlockSpec((T, LANES), lambda i: (0, i))
    tbl = pl.BlockSpec((None, N_CLS, LANES, BAND), lambda i: (i, 0, 0, 0))
    o16 = jax.ShapeDtypeStruct((T, naw), _MXU)
    vm = 2 * (4 * T * LANES * 4 + 2 * N_CLS * BAND * LANES * 4 + 3 * T * LANES * 2) + 2 * T * LANES * 6
    return _pallas(
        body, name=name,
        out_shape=(o16, o16, o16, jax.ShapeDtypeStruct((npair, N_CLS, LANES, BAND), _F32)),
        grid=(npair,),
        in_specs=[col(qoff), col(koff), col(voff), tbl, col(doff)],
        out_specs=(blk, blk, blk, tbl),
        scratch_shapes=[pltpu.VMEM((T, LANES), _MXU), pltpu.VMEM((T, LANES), _MXU),
                        pltpu.VMEM((T, LANES), _F32), pltpu.VMEM((T, LANES), _F32)],
        compiler_params=_params(("parallel",), vm),
    )(proj, proj, proj, bias, dycat)


_ANY = pl.BlockSpec(memory_space=pl.ANY)
_HBM = pl.BlockSpec(memory_space=pltpu.HBM)
_SEM = pl.BlockSpec(memory_space=pltpu.SEMAPHORE)
_VMEM = pl.BlockSpec(memory_space=pltpu.VMEM)
_MESH_ID = pl.DeviceIdType.MESH
_EFFECT = pltpu.SideEffectType.DATAFLOW_SIDE_EFFECTING
_TOKEN = jax.ShapeDtypeStruct((SUBLANES, LANES), _F32)


def _mesh_pos():
    return tuple(lax.axis_index(a) for a in MESH_AXES)


def _in_hbm(a):
    return pltpu.with_memory_space_constraint(a, pltpu.HBM)


def _hbm_like(arrays):
    return [pltpu.HBM(a.shape, a.dtype) for a in arrays]


def _shard_ref(ref, axis, j, width):
    idx = [slice(None)] * len(ref.shape)
    idx[axis] = pl.ds(pl.multiple_of(j * width, math.gcd(width, LANES)), width)
    return ref.at[tuple(idx)]


def _all_gather(shards, axes, name):
    n = len(shards)
    widths = [s.shape[a] for s, a in zip(shards, axes)]
    out_shape = [jax.ShapeDtypeStruct(tuple(N_DEV * d if k == a else d for k, d in enumerate(s.shape)), s.dtype)
                 for s, a in zip(shards, axes)]

    def body(*refs):
        ins, outs = refs[:n], refs[n:2 * n]
        send_sems, recv_sems, local_sems = refs[2 * n:]
        x, y, c = _mesh_pos()
        me, sibling = (x, y, c), (x, y, 1 - c)
        chips = [(1 - x, y), (x, 1 - y), (1 - x, 1 - y)]

        def slot(i, px, py, pc):
            return _shard_ref(outs[i], axes[i], 4 * px + 2 * py + pc, widths[i])

        def copy(i, k, block, to, src=None):
            return pltpu.make_async_remote_copy(
                src_ref=slot(i, *block) if src is None else src, dst_ref=slot(i, *block),
                send_sem=send_sems.at[7 * i + k], recv_sem=recv_sems.at[7 * i + k],
                device_id=to, device_id_type=_MESH_ID)

        mine = [pltpu.make_async_copy(ins[i], slot(i, *me), local_sems.at[i]) for i in range(n)]
        for cp in mine:
            cp.start()
        first = []
        for i in range(n):
            first.append(copy(i, 0, me, sibling, src=ins[i]))
            first += [copy(i, 1 + j, me, (*chip, c), src=ins[i]) for j, chip in enumerate(chips)]
        for cp in first:
            cp.start()
        passed = []
        for j, chip in enumerate(chips):
            for i in range(n):
                copy(i, 1 + j, (*chip, c), me).wait_recv()
                fwd = copy(i, 4 + j, (*chip, c), sibling)
                fwd.start()
                passed.append(fwd)
        for i in range(n):
            copy(i, 0, sibling, me).wait_recv()
            for j, chip in enumerate(chips):
                copy(i, 4 + j, (*chip, 1 - c), me).wait_recv()
        for cp in first + passed:
            cp.wait_send()
        for cp in mine:
            cp.wait()

    return _pallas(
        body, name=name, out_shape=out_shape, in_specs=[_ANY] * n, out_specs=[_ANY] * n,
        scratch_shapes=[pltpu.SemaphoreType.DMA((7 * n,)), pltpu.SemaphoreType.DMA((7 * n,)),
                        pltpu.SemaphoreType.DMA((n,))],
    )(*shards)


def _block_of(ref, axis, blk):
    return _shard_ref(ref, axis, blk, ref.shape[axis] // N_DEV)


def _gather_start(lands, axes, after, name):
    n = len(lands)

    def body(*refs):
        land = refs[:n]
        send, recv_sib, recv_ici = refs[n + 1:n + 4]
        token = refs[-1]
        x, y, c = _mesh_pos()
        me = 4 * x + 2 * y + c
        for i in range(n):
            mine = _block_of(land[i], axes[i], me)
            pltpu.make_async_remote_copy(
                src_ref=mine, dst_ref=mine, send_sem=send.at[4 * i],
                recv_sem=recv_sib.at[i], device_id=(x, y, 1 - c), device_id_type=_MESH_ID).start()
            for j, chip in enumerate([(1 - x, y), (x, 1 - y), (1 - x, 1 - y)]):
                pltpu.make_async_remote_copy(
                    src_ref=mine, dst_ref=mine, send_sem=send.at[4 * i + 1 + j],
                    recv_sem=recv_ici.at[3 * i + j], device_id=(*chip, c), device_id_type=_MESH_ID).start()
        token[...] = jnp.zeros_like(token)

    dma = pltpu.SemaphoreType.DMA
    out = pl.pallas_call(
        body, name=name,
        out_shape=(dma((4 * n,)), dma((n,)), dma((3 * n,)), *_hbm_like(lands), _TOKEN),
        in_specs=[_HBM] * n + [_ANY], out_specs=(_SEM, _SEM, _SEM, *[_HBM] * n, _VMEM),
        input_output_aliases={i: 3 + i for i in range(n)},
        compiler_params=pltpu.CompilerParams(has_side_effects=_EFFECT),
    )(*[_in_hbm(a) for a in lands], after)
    return dict(send=out[0], recv_sib=out[1], recv_ici=out[2], lands=list(out[3:3 + n]), axes=axes, token=out[-1])


def _gather_forward(st, after, name):
    lands, axes = st["lands"], st["axes"]
    n = len(lands)

    def body(*refs):
        land = refs[:n]
        recv_ici = refs[n]
        send2, recv2 = refs[n + 2], refs[n + 3]
        token = refs[-1]
        x, y, c = _mesh_pos()
        for j, (px, py) in enumerate([(1 - x, y), (x, 1 - y), (1 - x, 1 - y)]):
            for i in range(n):
                blk = _block_of(land[i], axes[i], 4 * px + 2 * py + c)
                pltpu.make_async_remote_copy(
                    src_ref=blk, dst_ref=blk, send_sem=send2.at[3 * i + j],
                    recv_sem=recv_ici.at[3 * i + j], device_id=(px, py, c), device_id_type=_MESH_ID).wait_recv()
                pltpu.make_async_remote_copy(
                    src_ref=blk, dst_ref=blk, send_sem=send2.at[3 * i + j],
                    recv_sem=recv2.at[3 * i + j], device_id=(x, y, 1 - c), device_id_type=_MESH_ID).start()
        token[...] = jnp.zeros_like(token)

    dma = pltpu.SemaphoreType.DMA
    out = pl.pallas_call(
        body, name=name,
        out_shape=(dma((3 * n,)), dma((3 * n,)), *_hbm_like(lands), _TOKEN),
        in_specs=[_HBM] * n + [_SEM, _ANY], out_specs=(_SEM, _SEM, *[_HBM] * n, _VMEM),
        input_output_aliases={i: 2 + i for i in range(n)},
        compiler_params=pltpu.CompilerParams(has_side_effects=_EFFECT),
    )(*lands, st["recv_ici"], after)
    return dict(st, send2=out[0], recv2=out[1], lands=list(out[2:2 + n]), token=out[-1])


def _gather_finish(st, after, name):
    lands, axes = st["lands"], st["axes"]
    n = len(lands)

    def body(*refs):
        land = refs[:n]
        send, recv_sib, send2, recv2 = refs[n:n + 4]
        x, y, c = _mesh_pos()
        me = 4 * x + 2 * y + c
        sib = 4 * x + 2 * y + (1 - c)

        def desc(i, blk, s_sem, r_sem):
            ref = _block_of(land[i], axes[i], blk)
            return pltpu.make_async_remote_copy(
                src_ref=ref, dst_ref=ref, send_sem=s_sem, recv_sem=r_sem,
                device_id=(x, y, 1 - c), device_id_type=_MESH_ID)

        for i in range(n):
            desc(i, sib, send.at[4 * i], recv_sib.at[i]).wait_recv()
            for j, (px, py) in enumerate([(1 - x, y), (x, 1 - y), (1 - x, 1 - y)]):
                desc(i, 4 * px + 2 * py + (1 - c), send2.at[3 * i + j], recv2.at[3 * i + j]).wait_recv()
            for k in range(4):
                desc(i, me, send.at[4 * i + k], recv_sib.at[i]).wait_send()
            for j, (px, py) in enumerate([(1 - x, y), (x, 1 - y), (1 - x, 1 - y)]):
                desc(i, 4 * px + 2 * py + c, send2.at[3 * i + j], recv2.at[3 * i + j]).wait_send()

    out = pl.pallas_call(
        body, name=name, out_shape=tuple(_hbm_like(lands)),
        in_specs=[_HBM] * n + [_SEM] * 4 + [_ANY], out_specs=tuple([_HBM] * n),
        input_output_aliases={i: i for i in range(n)},
        compiler_params=pltpu.CompilerParams(has_side_effects=_EFFECT),
    )(*lands, st["send"], st["recv_sib"], st["send2"], st["recv2"], after)
    return list(out)


def _scatter_sibling_start(grads, name):
    n = len(grads)
    gots = [lax.empty((4,) + g.shape[1:], g.dtype) for g in grads]

    def body(*refs):
        grad, got = refs[:n], refs[n:2 * n]
        send, recv = refs[2 * n], refs[2 * n + 1]
        token = refs[-1]
        x, y, c = _mesh_pos()
        for i in range(n):
            for q in range(4):
                pltpu.make_async_remote_copy(
                    src_ref=grad[i].at[2 * q + (1 - c)], dst_ref=got[i].at[q], send_sem=send.at[4 * i + q],
                    recv_sem=recv.at[4 * i + q], device_id=(x, y, 1 - c), device_id_type=_MESH_ID).start()
        token[...] = jnp.zeros_like(token)

    dma = pltpu.SemaphoreType.DMA
    out = pl.pallas_call(
        body, name=name,
        out_shape=(dma((4 * n,)), dma((4 * n,)), *_hbm_like(grads), *_hbm_like(gots), _TOKEN),
        in_specs=[_HBM] * (2 * n), out_specs=(_SEM, _SEM, *[_HBM] * (2 * n), _VMEM),
        input_output_aliases={i: 2 + i for i in range(2 * n)},
        compiler_params=pltpu.CompilerParams(has_side_effects=_EFFECT),
    )(*[_in_hbm(a) for a in grads], *[_in_hbm(a) for a in gots])
    return dict(send=out[0], recv=out[1], grads=list(out[2:2 + n]), gots=list(out[2 + n:2 + 2 * n]), token=out[-1])


def _scatter_sibling_finish(st, after, name):
    grads, gots = st["grads"], st["gots"]
    n = len(grads)

    def body(*refs):
        grad, got = refs[:n], refs[n:2 * n]
        send, recv = refs[2 * n], refs[2 * n + 1]
        x, y, c = _mesh_pos()
        for i in range(n):
            for q in range(4):
                cp = pltpu.make_async_remote_copy(
                    src_ref=grad[i].at[2 * q + (1 - c)], dst_ref=got[i].at[q], send_sem=send.at[4 * i + q],
                    recv_sem=recv.at[4 * i + q], device_id=(x, y, 1 - c), device_id_type=_MESH_ID)
                cp.wait_recv()
                cp.wait_send()

    out = pl.pallas_call(
        body, name=name, out_shape=tuple(_hbm_like(grads) + _hbm_like(gots)),
        in_specs=[_HBM] * (2 * n) + [_SEM, _SEM, _ANY], out_specs=tuple([_HBM] * (2 * n)),
        input_output_aliases={i: i for i in range(2 * n)},
        compiler_params=pltpu.CompilerParams(has_side_effects=_EFFECT),
    )(*grads, *gots, st["send"], st["recv"], after)
    return list(out[:n]), list(out[n:])


def _scatter_add(grad, got, name):
    _, R, C = grad.shape
    tr = _blk(R, 512)
    my_c = lambda: lax.axis_index("c")
    my_chip = lambda: 2 * lax.axis_index("x") + lax.axis_index("y")

    def body(a_ref, b_ref, part_ref, fin_ref):
        s = (a_ref[...].astype(_F32) + b_ref[...].astype(_F32)).astype(part_ref.dtype)
        part_ref[...] = s

        @pl.when(pl.program_id(1) == my_chip())
        def _():
            fin_ref[...] = s

    shape = jax.ShapeDtypeStruct((4, R, C), grad.dtype)
    return _pallas(
        body, name=name, out_shape=(shape, shape), grid=(R // tr, 4),
        in_specs=[pl.BlockSpec((None, tr, C), lambda i, q: (2 * q + my_c(), i, 0)),
                  pl.BlockSpec((None, tr, C), lambda i, q: (q, i, 0))],
        out_specs=[pl.BlockSpec((None, tr, C), lambda i, q: (q, i, 0)),
                   pl.BlockSpec((None, tr, C), lambda i, q: (my_chip(), i, 0))],
        compiler_params=_params(("parallel", "arbitrary"), 2 * tr * C * 8),
    )(grad, got)


def _scatter_chips_start(parts, fins, name):
    n = len(parts)

    def body(*refs):
        part, fin = refs[:n], refs[n:2 * n]
        send, recv = refs[2 * n], refs[2 * n + 1]
        token = refs[-1]
        x, y, c = _mesh_pos()
        mine = 2 * x + y
        for i in range(n):
            for k, (tx, ty) in enumerate([(1 - x, y), (x, 1 - y), (1 - x, 1 - y)]):
                pltpu.make_async_remote_copy(
                    src_ref=part[i].at[2 * tx + ty], dst_ref=fin[i].at[mine], send_sem=send.at[3 * i + k],
                    recv_sem=recv.at[3 * i + k], device_id=(tx, ty, c), device_id_type=_MESH_ID).start()
        token[...] = jnp.zeros_like(token)

    dma = pltpu.SemaphoreType.DMA
    out = pl.pallas_call(
        body, name=name,
        out_shape=(dma((3 * n,)), dma((3 * n,)), *_hbm_like(parts), *_hbm_like(fins), _TOKEN),
        in_specs=[_HBM] * (2 * n), out_specs=(_SEM, _SEM, *[_HBM] * (2 * n), _VMEM),
        input_output_aliases={i: 2 + i for i in range(2 * n)},
        compiler_params=pltpu.CompilerParams(has_side_effects=_EFFECT),
    )(*[_in_hbm(a) for a in parts], *[_in_hbm(a) for a in fins])
    return dict(send=out[0], recv=out[1], parts=list(out[2:2 + n]), fins=list(out[2 + n:2 + 2 * n]), token=out[-1])


def _scatter_chips_finish(st, after, name):
    parts, fins = st["parts"], st["fins"]
    n = len(parts)

    def body(*refs):
        part, fin = refs[:n], refs[n:2 * n]
        send, recv = refs[2 * n], refs[2 * n + 1]
        x, y, c = _mesh_pos()
        for i in range(n):
            for k, (tx, ty) in enumerate([(1 - x, y), (x, 1 - y), (1 - x, 1 - y)]):
                cp = pltpu.make_async_remote_copy(
                    src_ref=part[i].at[2 * tx + ty], dst_ref=fin[i].at[2 * tx + ty], send_sem=send.at[3 * i + k],
                    recv_sem=recv.at[3 * i + k], device_id=(tx, ty, c), device_id_type=_MESH_ID)
                cp.wait_recv()
                cp.wait_send()

    out = pl.pallas_call(
        body, name=name, out_shape=tuple(_hbm_like(parts) + _hbm_like(fins)),
        in_specs=[_HBM] * (2 * n) + [_SEM, _SEM, _ANY], out_specs=tuple([_HBM] * (2 * n)),
        input_output_aliases={i: i for i in range(2 * n)},
        compiler_params=pltpu.CompilerParams(has_side_effects=_EFFECT),
    )(*parts, *fins, st["send"], st["recv"], after)
    return list(out[n:])


def _adamw(g, w, m, v):
    m = ADAM_B1 * m + (1.0 - ADAM_B1) * g
    v = ADAM_B2 * v + (1.0 - ADAM_B2) * (g * g)
    m_hat = m / (1.0 - ADAM_B1 ** ADAM_STEP)
    v_hat = v / (1.0 - ADAM_B2 ** ADAM_STEP)
    delta = -ADAM_LR * (m_hat / (jnp.sqrt(v_hat) + ADAM_EPS) + ADAM_WD * w)
    return delta, m, v


def _adam_layer(fin, w3, m3, v3, l, prev, name):
    L, R, C = w3.shape
    tr = _blk(R, max(SUBLANES, (1 << 18) // C))

    def body(f_ref, w_ref, m_ref, v_ref, *rest):
        g_ref, d_ref, nm_ref, nv_ref = rest[-4:]
        g = ((f_ref[0].astype(_F32) + f_ref[1].astype(_F32)) + f_ref[2].astype(_F32)) + f_ref[3].astype(_F32)
        d, nm, nv = _adamw(g, w_ref[...], m_ref[...], v_ref[...])
        g_ref[...] = g
        d_ref[...] = d
        nm_ref[...] = nm
        nv_ref[...] = nv

    lay = pl.BlockSpec((None, tr, C), lambda i: (l, i, 0))
    ins = [fin, w3, m3, v3]
    in_specs = [pl.BlockSpec((4, tr, C), lambda i: (0, i, 0)), lay, lay, lay]
    aliases = {}
    if prev is not None:
        ins += list(prev)
        in_specs += [_ANY] * 4
        aliases = {4 + k: k for k in range(4)}
    return _pallas(
        body, name=name, out_shape=[jax.ShapeDtypeStruct((L, R, C), _F32)] * 4, grid=(R // tr,),
        in_specs=in_specs, out_specs=[lay] * 4, input_output_aliases=aliases,
        compiler_params=_params(("parallel",), 2 * tr * C * (4 * 2 + 7 * 4)),
    )(*ins)


def _sum_parts(parts, name):
    _, R, C = parts.shape

    def body(p_ref, o_ref):
        acc = p_ref[0]
        for k in range(1, N_DEV):
            acc = acc + p_ref[k]
        o_ref[...] = acc

    tr = _blk(R, 512)
    return _pallas(
        body, name=name, out_shape=jax.ShapeDtypeStruct((R, C), _F32), grid=(R // tr,),
        in_specs=[pl.BlockSpec((N_DEV, tr, C), lambda i: (0, i, 0))],
        out_specs=pl.BlockSpec((tr, C), lambda i: (i, 0)),
        compiler_params=_params(("parallel",), 2 * tr * C * 4 * 9),
    )(parts)


def _adam_flat(g, w, m, v, name):
    R, C = g.shape
    tr = _blk(R, 512)

    def body(g_ref, w_ref, m_ref, v_ref, d_ref, nm_ref, nv_ref):
        d, nm, nv = _adamw(g_ref[...], w_ref[...], m_ref[...], v_ref[...])
        d_ref[...] = d
        nm_ref[...] = nm
        nv_ref[...] = nv

    spec = pl.BlockSpec((tr, C), lambda i: (i, 0))
    return _pallas(
        body, name=name, out_shape=[jax.ShapeDtypeStruct((R, C), _F32)] * 3, grid=(R // tr,),
        in_specs=[spec] * 4, out_specs=[spec] * 3,
        compiler_params=_params(("parallel",), 2 * tr * C * 4 * 7),
    )(g, w, m, v)


def _pack(arrays):
    flat = jnp.concatenate([a.reshape(-1) for a in arrays])
    tile = SUBLANES * LANES
    pad = (-flat.shape[0]) % tile
    return jnp.pad(flat, (0, pad)).reshape(-1, LANES)


def _unpack(packed, shapes):
    flat = packed.reshape(-1)
    out, off = [], 0
    for s in shapes:
        n = math.prod(s)
        out.append(flat[off:off + n].reshape(s))
        off += n
    return out


def kernel(x, w_in, w_dw, b_dw, conv_ln_g, conv_ln_b, rpb, w_out, w_up, w_down, pre_mix_g, post_mix_g, pre_mlp_g, post_mlp_g, loss_target, m_w_in, m_w_dw, m_b_dw, m_conv_ln_g, m_conv_ln_b, m_rpb, m_w_out, m_w_up, m_w_down, m_pre_mix_g, m_post_mix_g, m_pre_mlp_g, m_post_mlp_g, v_w_in, v_w_dw, v_b_dw, v_conv_ln_g, v_conv_ln_b, v_rpb, v_w_out, v_w_up, v_w_down, v_pre_mix_g, v_post_mix_g, v_pre_mlp_g, v_post_mlp_g):
    _, T, D = x.shape
    L = w_in.shape[0]
    cw = b_dw.shape[1]
    H = rpb.shape[1]
    naw = H * HEAD_DIM
    ks = w_dw.shape[1]
    assert T % GRID_W == 0 and T // GRID_W >= WIN_ROWS and H % 2 == 0 and cw % LANES == 0
    assert rpb.shape[2:] == (N_DR, N_DC) and w_dw.shape[2] * N_DEV == cw and ks // 2 < CONV_HALO
    assert naw == cw and w_out.shape[1] * N_DEV == cw + naw and (T // GRID_W) % ROWS_PER_STEP_FWD == 0

    xs = x.reshape(T, D)
    tgt = loss_target.reshape(T, D)
    row = lambda p, l: (p, l)
    mx, my, mc = (lax.axis_index(a) for a in MESH_AXES)
    dev = 4 * mx + 2 * my + mc

    ks_pad = ks + (-ks) % SUBLANES
    wdw_pad = jnp.pad(w_dw, ((0, 0), (0, ks_pad - ks), (0, 0))).reshape(L * ks_pad, w_dw.shape[2])
    wdw_full = _all_gather([wdw_pad], [1], "ag_wdw")[0].reshape(L, ks_pad, cw)[:, :ks]

    big = (w_in, w_out, w_up, w_down)
    names = ("in", "out", "up", "down")

    big_axes = (1, 0, 1, 0)

    def gather_start(l, which, after):
        lands = [_cast_slot(big[k], l, big_axes[k] == 1, f"cast_{names[k]}") for k in which]
        return _gather_start(lands, [big_axes[k] for k in which], after, "gather_start_%d" % len(which))

    def gather_pair(l, after):
        g_in = gather_start(l, [0], after)
        return g_in, gather_start(l, [1, 2, 3], g_in["token"])

    saved = []
    xin = xs
    h = _norm_fwd(xs, row(pre_mix_g, 0), "norm_first")
    g_in, g_rest = gather_pair(0, wdw_full)
    g_in = _gather_forward(g_in, g_rest["token"], "gather_forward_1")
    Win = _gather_finish(g_in, g_in["token"], "gather_finish_1")[0]
    dy = loss_sum = None
    for l in range(L):
        nxt = gather_pair(l + 1, Win) if l + 1 < L else None
        proj = _matmul(h, Win, name="mm_proj")
        behind = [nxt[1]["token"]] if nxt else []
        if l > 0:
            g_rest = _gather_forward(g_rest, proj, "gather_forward_3")
            behind.append(g_rest["token"])
        c = _conv_fwd(proj, wdw_full, b_dw, l, cw, "conv_fwd", deps=tuple(behind))
        yc = _ln_silu_fwd(c, row(conv_ln_g, l), row(conv_ln_b, l), cw + naw, "ln_silu_fwd")
        bias_t, bias = _bias_table(rpb[l].reshape(H, N_DR * N_DC), "bias_table")
        ycat = _attn_fwd(proj, bias_t, yc, cw, naw, "attn_fwd")
        if l == 0:
            g_rest = _gather_forward(g_rest, ycat, "gather_forward_3")
        Wout, Wup, Wdown = _gather_finish(g_rest, ycat, "gather_finish_3")
        Ws = (Win, Wout, Wup, Wdown)
        mix = _matmul(ycat, Wout, name="mm_mix")
        x1, h2 = _resid_norm_fwd(xin, mix, row(post_mix_g, l), row(pre_mlp_g, l), "resid_mix")
        act, rl = _matmul(h2, Wup, epilogue="relu2", name="mm_up")
        if nxt:
            g_in = _gather_forward(nxt[0], act, "gather_forward_1")
        f = _matmul(act, Wdown, tm=256, tk=act.shape[1], n_outer=True, name="mm_down")
        saved.append(dict(xin=xin, h=h, W=Ws, proj=proj, c=c, bias=bias, ycat=ycat,
                          mix=mix, x1=x1, h2=h2, act=act, rl=rl, f=f))
        if nxt:
            Win = _gather_finish(g_in, f, "gather_finish_1")[0]
            g_rest = nxt[1]
            xin, h = _resid_norm_fwd(x1, f, row(post_mlp_g, l), row(pre_mix_g, l + 1), "resid_mlp")
        else:
            dy, loss_sum = _resid_loss(x1, f, row(post_mlp_g, l), tgt, "resid_loss")

    loss = lax.psum(loss_sum[0, 0] * (0.5 / D), MESH_AXES)

    small_grads = [None] * L
    big_out = [None] * 4
    moments = ((m_w_in, v_w_in), (m_w_out, v_w_out), (m_w_up, v_w_up), (m_w_down, v_w_down))

    def scatter_begin(grads, which, l):
        tag = "_%d" % len(which)
        return dict(st=_scatter_sibling_start(grads, "scatter_sibling_start" + tag), which=which, l=l, tag=tag)

    def scatter_mid(sc, after):
        grads, gots = _scatter_sibling_finish(sc["st"], after, "scatter_sibling_finish" + sc["tag"])
        pf = [_scatter_add(g, o, f"scatter_add_{k}") for k, g, o in zip(sc["which"], grads, gots)]
        st = _scatter_chips_start([p for p, _ in pf], [q for _, q in pf], "scatter_chips_start" + sc["tag"])
        return dict(sc, st=st)

    def scatter_end(sc, after):
        fins = _scatter_chips_finish(sc["st"], after, "scatter_chips_finish" + sc["tag"])
        for k, fin in zip(sc["which"], fins):
            big_out[k] = _adam_layer(fin, big[k], moments[k][0], moments[k][1], sc["l"], big_out[k],
                                     f"adam_{k}_{sc['l']}")

    dxo = dy
    pending = None
    for l in reversed(range(L)):
        s = saved[l]
        Win, Wout, Wup, Wdown = s["W"]
        last = l == 0
        tok = lambda sc: (sc["st"]["token"],) if sc is not None else ()
        d_f, dg_post_mlp = _norm_bwd(s["f"], row(post_mlp_g, l), dxo, None, _MXU, "norm_bwd_mlp", deps=tok(pending))
        d_up = _matmul(d_f, Wdown, tb=True, epilogue="mul2", extra=s["rl"], name="mm_d_up")
        if pending is not None:
            pending = scatter_mid(pending, d_up)
        dWdown = _matmul(s["act"], d_f, ta=True, out_dtype=_WIRE, tk=T, name="mm_dw_down").reshape(N_DEV, -1, D)
        d_h2 = _matmul(d_up, Wup, tb=True, tm=256, tk=d_up.shape[1], n_outer=True, name="mm_d_h2")
        dWup = _matmul(s["h2"], d_up, ta=True, out_dtype=_WIRE, out_cols=N_DEV, tk=T, name="mm_dw_up")
        sc_mlp = scatter_begin([dWup, dWdown], [2, 3], l) if last else None
        dx1, dg_pre_mlp = _norm_bwd(s["x1"], row(pre_mlp_g, l), d_h2, dxo, _F32, "norm_bwd_premlp", deps=tok(pending))
        d_mix, dg_post_mix = _norm_bwd(s["mix"], row(post_mix_g, l), dx1, None, _MXU, "norm_bwd_mix", deps=tok(sc_mlp))
        d_ycat = _matmul(d_mix, Wout, tb=True, name="mm_d_ycat")
        if last:
            sc_mlp = scatter_mid(sc_mlp, d_ycat)
        dWout = _matmul(s["ycat"], d_mix, ta=True, out_dtype=_WIRE, tk=T, name="mm_dw_out").reshape(N_DEV, -1, D)
        sc_out = scatter_begin([dWout], [1], l) if last else None
        dc, dlng, dlnb = _ln_silu_bwd(s["c"], row(conv_ln_g, l), row(conv_ln_b, l), d_ycat, "ln_silu_bwd",
                                      deps=tok(sc_mlp))
        da, dgate, dwb = _conv_bwd(s["proj"], dc, wdw_full, l, cw, "conv_bwd", deps=tok(sc_out))
        dq, dk, dv, gcls = _attn_bwd(s["proj"], s["bias"], d_ycat, cw, naw, "attn_bwd")
        if last:
            sc_out = scatter_mid(sc_out, gcls)
        drpb = _rpb_grad(gcls, "rpb_grad").reshape(H, 4 * SUBLANES, LANES)[:, :N_DC, :N_DR].transpose(0, 2, 1)
        dproj = jnp.concatenate([da, dgate, dq, dk, dv], axis=1)
        dh = _matmul(dproj, Win, tb=True, tm=512, tk=dproj.shape[1], n_outer=True, name="mm_d_h")
        dWin = _matmul(s["h"], dproj, ta=True, out_dtype=_WIRE, out_cols=N_DEV, tm=512, tk=T, name="mm_dw_in")
        dxo, dg_pre_mix = _norm_bwd(s["xin"], row(pre_mix_g, l), dh, dx1, _F32, "norm_bwd_premix", deps=tok(sc_out))
        if pending is not None:
            scatter_end(pending, dxo)
        if last:
            sc_in = scatter_begin([dWin], [0], l)
            sc_in = scatter_mid(sc_in, sc_in["st"]["token"])
            scatter_end(sc_mlp, sc_in["st"]["token"])
            scatter_end(sc_out, sc_in["st"]["token"])
            scatter_end(sc_in, sc_in["st"]["token"])
        else:
            pending = scatter_begin([dWin, dWout, dWup, dWdown], [0, 1, 2, 3], l)
        small_grads[l] = [dwb[ks], dlng[0], dlnb[0], drpb, dg_pre_mix[0], dg_post_mix[0], dg_pre_mlp[0],
                          dg_post_mlp[0], dwb[:ks]]

    rep_shapes = [(L, cw), (L, cw), (L, cw), (L, H, N_DR, N_DC), (L, D), (L, D), (L, D), (L, D)]
    stacked = [jnp.stack([small_grads[l][k] for l in range(L)]) for k in range(9)]
    packed = _pack(stacked)
    parts = _all_gather([packed], [0], "ag_small")[0].reshape(N_DEV, *packed.shape)
    gsum = _sum_parts(parts, "sum_small")
    g_small = _unpack(gsum, rep_shapes + [(L, ks, cw)])
    g_rep, g_wdw_full = g_small[:8], g_small[8]
    wsh = w_dw.shape[2]
    g_wdw = lax.dynamic_slice_in_dim(g_wdw_full, dev * wsh, wsh, axis=2)

    rep_w = [b_dw, conv_ln_g, conv_ln_b, rpb, pre_mix_g, post_mix_g, pre_mlp_g, post_mlp_g]
    rep_m = [m_b_dw, m_conv_ln_g, m_conv_ln_b, m_rpb, m_pre_mix_g, m_post_mix_g, m_pre_mlp_g, m_post_mlp_g]
    rep_v = [v_b_dw, v_conv_ln_g, v_conv_ln_b, v_rpb, v_pre_mix_g, v_post_mix_g, v_pre_mlp_g, v_post_mlp_g]
    rep_out = _adam_flat(_pack(g_rep), _pack(rep_w), _pack(rep_m), _pack(rep_v), "adam_small")
    rep_delta, rep_nm, rep_nv = (_unpack(o, rep_shapes) for o in rep_out)
    dw_out = _adam_flat(_pack([g_wdw]), _pack([w_dw]), _pack([m_w_dw]), _pack([v_w_dw]), "adam_wdw")
    wdw_delta, wdw_nm, wdw_nv = (_unpack(o, [w_dw.shape])[0] for o in dw_out)

    def assemble(kind_big, rep_list, wdw_val):
        return [big_out[0][kind_big], wdw_val, rep_list[0], rep_list[1], rep_list[2], rep_list[3],
                big_out[1][kind_big], big_out[2][kind_big], big_out[3][kind_big],
                rep_list[4], rep_list[5], rep_list[6], rep_list[7]]

    grads_out = assemble(0, g_rep, g_wdw)
    deltas = assemble(1, rep_delta, wdw_delta)
    new_m = assemble(2, rep_nm, wdw_nm)
    new_v = assemble(3, rep_nv, wdw_nv)
    return (loss, dxo.reshape(1, T, D), *grads_out, *deltas, *new_m, *new_v)
```

```python
import math

import jax
import jax.numpy as jnp
from jax import lax
from jax.experimental import pallas as pl
from jax.experimental.pallas import tpu as pltpu

_MXU = jnp.bfloat16
_WIRE = jnp.bfloat16
_F32 = jnp.float32

N_DEV = 8
GRID_W = 64
WIN_ROWS = 8
WIN_COLS = 16
HEAD_DIM = 64
LANES = 128
MXU_COLS_V7X = 256
SUBLANES = 8
RMS_EPS = 1e-6
LN_EPS = 1e-5
NEG_INF = -1e30
ADAM_LR = 0.001
ADAM_B1 = 0.9
ADAM_B2 = 0.999
ADAM_EPS = 1e-08
ADAM_WD = 0.01
ADAM_STEP = 10
VMEM_BYTES_V7X = 64 << 20
VMEM_RESERVE = 12 << 20
MESH_AXES = ("x", "y", "c")


def _vmem_limit(block_bytes):
    return int(min(max(block_bytes + (8 << 20), 24 << 20), VMEM_BYTES_V7X - VMEM_RESERVE))


def _blk(n, pref):
    if n <= pref:
        return n
    for t in range(pref, 7, -1):
        if n % t == 0 and t % SUBLANES == 0:
            return t
    return n


def _sigmoid(v):
    return 1.0 / (1.0 + jnp.exp(-v))


def _params(sem, nbytes):
    return pltpu.CompilerParams(dimension_semantics=sem, vmem_limit_bytes=_vmem_limit(nbytes))


def _pallas(body, deps=(), **kw):
    n_in = len(kw["in_specs"])
    if deps:
        kw["in_specs"] = list(kw["in_specs"]) + [pl.BlockSpec(memory_space=pl.ANY)] * len(deps)
        inner = body

        def body(*refs):
            return inner(*refs[:n_in], *refs[n_in + len(deps):])

    call = pl.pallas_call(body, **kw)

    def run(*operands):
        return call(*[pltpu.with_memory_space_constraint(o, pltpu.HBM)
                      if jnp.issubdtype(o.dtype, jnp.floating) else o for o in (*operands, *deps)])

    return run


def _my_block():
    x, y, c = (lax.axis_index(a) for a in MESH_AXES)
    return 4 * x + 2 * y + c


def _cast_slot(w3, l, by_cols, name):
    _, R, C = w3.shape
    tr = _blk(R, 512)

    def body(w_ref, o_ref):
        o_ref[...] = w_ref[...].astype(o_ref.dtype)

    if by_cols:
        shape, o_spec = (R, N_DEV * C), pl.BlockSpec((tr, C), lambda i: (i, _my_block()))
    else:
        shape, o_spec = (N_DEV * R, C), pl.BlockSpec((tr, C), lambda i: (_my_block() * (R // tr) + i, 0))
    return _pallas(
        body, name=name, out_shape=jax.ShapeDtypeStruct(shape, _WIRE), grid=(R // tr,),
        in_specs=[pl.BlockSpec((None, tr, C), lambda i: (l, i, 0))], out_specs=o_spec,
        compiler_params=_params(("parallel",), 2 * tr * C * 6),
    )(w3)


def _layer_row(param):
    arr, l = param
    L, W = arr.shape
    return arr.reshape(L, 1, W), pl.BlockSpec((None, 1, W), lambda *_: (l, 0, 0))


def _norm_fwd(x, g, name):
    T, D = x.shape
    tm = _blk(T, 256)
    g, g_spec = _layer_row(g)

    def body(x_ref, g_ref, h_ref):
        xv = x_ref[...]
        r = lax.rsqrt(jnp.mean(xv * xv, axis=-1, keepdims=True) + RMS_EPS)
        h_ref[...] = (xv * r * g_ref[...]).astype(h_ref.dtype)

    return _pallas(
        body, name=name, out_shape=jax.ShapeDtypeStruct((T, D), _MXU), grid=(T // tm,),
        in_specs=[pl.BlockSpec((tm, D), lambda i: (i, 0)), g_spec],
        out_specs=pl.BlockSpec((tm, D), lambda i: (i, 0)),
        compiler_params=_params(("parallel",), 2 * tm * D * 6),
    )(x, g)


def _resid_norm_fwd(xres, y, g_post, g_next, name):
    T, D = xres.shape
    tm = _blk(T, 256)

    def body(x_ref, y_ref, gp_ref, gn_ref, xn_ref, h_ref):
        yv = y_ref[...]
        r = lax.rsqrt(jnp.mean(yv * yv, axis=-1, keepdims=True) + RMS_EPS)
        xn = x_ref[...] + yv * r * gp_ref[...]
        xn_ref[...] = xn
        r2 = lax.rsqrt(jnp.mean(xn * xn, axis=-1, keepdims=True) + RMS_EPS)
        h_ref[...] = (xn * r2 * gn_ref[...]).astype(h_ref.dtype)

    row = pl.BlockSpec((tm, D), lambda i: (i, 0))
    (g_post, gp_spec), (g_next, gn_spec) = _layer_row(g_post), _layer_row(g_next)
    return _pallas(
        body, name=name,
        out_shape=(jax.ShapeDtypeStruct((T, D), _F32), jax.ShapeDtypeStruct((T, D), _MXU)),
        grid=(T // tm,), in_specs=[row, row, gp_spec, gn_spec], out_specs=(row, row),
        compiler_params=_params(("parallel",), 2 * tm * D * 14),
    )(xres, y, g_post, g_next)


def _resid_loss(xres, y, g_post, target, name):
    T, D = xres.shape
    tm = _blk(T, 256)

    def body(x_ref, y_ref, gp_ref, t_ref, dy_ref, loss_ref):
        yv = y_ref[...]
        r = lax.rsqrt(jnp.mean(yv * yv, axis=-1, keepdims=True) + RMS_EPS)
        err = x_ref[...] + yv * r * gp_ref[...] - t_ref[...]
        dy_ref[...] = err * (1.0 / D)

        @pl.when(pl.program_id(0) == 0)
        def _():
            loss_ref[...] = jnp.zeros_like(loss_ref)

        part = jnp.sum(jnp.sum(err * err, axis=-1, keepdims=True), axis=0, keepdims=True)
        loss_ref[...] += part

    row = pl.BlockSpec((tm, D), lambda i: (i, 0))
    g_post, gp_spec = _layer_row(g_post)
    return _pallas(
        body, name=name,
        out_shape=(jax.ShapeDtypeStruct((T, D), _F32), jax.ShapeDtypeStruct((1, 1), _F32)),
        grid=(T // tm,), in_specs=[row, row, gp_spec, row],
        out_specs=(row, pl.BlockSpec((1, 1), lambda i: (0, 0))),
        compiler_params=_params(("arbitrary",), 2 * tm * D * 16),
    )(xres, y, g_post, target)


def _norm_bwd(y, g, dout, dres, out_dtype, name, deps=()):
    T, D = y.shape
    tm = _blk(T, 256)
    nsteps = T // tm
    has_res = dres is not None

    def body(*refs):
        if has_res:
            y_ref, g_ref, do_ref, dr_ref, dy_ref, dg_ref, acc = refs
        else:
            y_ref, g_ref, do_ref, dy_ref, dg_ref, acc = refs
        i = pl.program_id(0)
        yv = y_ref[...]
        do = do_ref[...]
        r = lax.rsqrt(jnp.mean(yv * yv, axis=-1, keepdims=True) + RMS_EPS)
        gy = do * g_ref[...]
        dot = jnp.mean(yv * gy, axis=-1, keepdims=True)
        dy = r * gy - yv * (r * r * r * dot)
        if has_res:
            dy = dy + dr_ref[...]
        dy_ref[...] = dy.astype(dy_ref.dtype)

        @pl.when(i == 0)
        def _():
            acc[...] = jnp.zeros_like(acc)

        acc[...] += jnp.sum((do * yv * r).reshape(tm // SUBLANES, SUBLANES, D), axis=0)

        @pl.when(i == nsteps - 1)
        def _():
            dg_ref[...] = jnp.sum(acc[...], axis=0, keepdims=True)

    row = pl.BlockSpec((tm, D), lambda i: (i, 0))
    vec = pl.BlockSpec((1, D), lambda i: (0, 0))
    g, g_spec = _layer_row(g)
    ins = [y, g, dout] + ([dres] if has_res else [])
    in_specs = [row, g_spec, row] + ([row] if has_res else [])
    return _pallas(
        body, name=name, deps=deps,
        out_shape=(jax.ShapeDtypeStruct((T, D), out_dtype), jax.ShapeDtypeStruct((1, D), _F32)),
        grid=(nsteps,), in_specs=in_specs, out_specs=(row, vec),
        scratch_shapes=[pltpu.VMEM((SUBLANES, D), _F32)],
        compiler_params=_params(("arbitrary",), 2 * tm * D * 16),
    )(*ins)


def _matmul(a, b, *, ta=False, tb=False, out_dtype=_F32, epilogue=None, extra=None, out_cols=0,
            tm=1024, tk=2048, n_outer=False, name):
    M, K = (a.shape[1], a.shape[0]) if ta else a.shape
    N = b.shape[0] if tb else b.shape[1]
    tm = _blk(M, tm)
    width = N // out_cols if out_cols else 0
    per_step = 2 if (out_cols and width % MXU_COLS_V7X and out_cols % 2 == 0) else 1
    tn = per_step * width if out_cols else _blk(N, 1024)
    tk = _blk(K, tk)
    ij = (lambda g0, g1: (g1, g0)) if n_outer else (lambda g0, g1: (g0, g1))
    b_spec = (pl.BlockSpec((tn, tk), lambda g0, g1, k: (ij(g0, g1)[1], k)) if tb
              else pl.BlockSpec((tk, tn), lambda g0, g1, k: (k, ij(g0, g1)[1])))
    nk = K // tk
    a_spec = (pl.BlockSpec((tk, tm), lambda g0, g1, k: (k, ij(g0, g1)[0])) if ta
              else pl.BlockSpec((tm, tk), lambda g0, g1, k: (ij(g0, g1)[0], k)))
    if out_cols:
        assert epilogue is None
        o_spec = pl.BlockSpec((per_step, tm, width), lambda g0, g1, k: (ij(g0, g1)[1], ij(g0, g1)[0], 0))
        o_shape = (out_cols, M, width)
    else:
        o_spec = pl.BlockSpec((tm, tn), lambda g0, g1, k: ij(g0, g1))
        o_shape = (M, N)
    dims = (((0 if ta else 1,), (1 if tb else 0,)), ((), ()))
    n_extra = 1 if epilogue == "mul2" else 0
    n_out = 2 if epilogue == "relu2" else 1

    def finish(acc, extra_refs, out_refs):
        if out_cols:
            for cblk in range(per_step):
                out_refs[0][cblk] = acc[:, cblk * width:(cblk + 1) * width].astype(out_refs[0].dtype)
        elif epilogue is None:
            out_refs[0][...] = acc.astype(out_refs[0].dtype)
        elif epilogue == "relu2":
            rl = jnp.maximum(acc, 0.0)
            out_refs[0][...] = (rl * rl).astype(out_refs[0].dtype)
            out_refs[1][...] = rl.astype(out_refs[1].dtype)
        else:
            out_refs[0][...] = (acc * (2.0 * extra_refs[0][...].astype(_F32))).astype(out_refs[0].dtype)

    def body(a_ref, b_ref, *rest):
        extra_refs = rest[:n_extra]
        out_refs = rest[n_extra:n_extra + n_out]
        part = lax.dot_general(a_ref[...], b_ref[...], dims, preferred_element_type=_F32)
        if nk == 1:
            finish(part, extra_refs, out_refs)
            return
        acc = rest[-1]
        k = pl.program_id(2)

        @pl.when(k == 0)
        def _():
            acc[...] = part

        @pl.when(k > 0)
        def _():
            acc[...] += part

        @pl.when(k == nk - 1)
        def _():
            finish(acc[...], extra_refs, out_refs)

    if epilogue == "relu2":
        out_shape = (jax.ShapeDtypeStruct((M, N), _MXU), jax.ShapeDtypeStruct((M, N), _MXU))
        out_specs = (o_spec, o_spec)
        out_bytes = 2 * tm * tn * 2
    else:
        odt = _MXU if epilogue == "mul2" else out_dtype
        out_shape = jax.ShapeDtypeStruct(o_shape, odt)
        out_specs = o_spec
        out_bytes = tm * tn * jnp.dtype(odt).itemsize
    in_specs = [a_spec, b_spec] + ([o_spec] if n_extra else [])
    ins = [a, b] + ([extra] if n_extra else [])
    blocks = 2 * (tm * tk * 2 + tk * tn * 2 + out_bytes + n_extra * tm * tn * 2) + tm * tn * 4 * 2
    return _pallas(
        body, name=name, out_shape=out_shape,
        grid=(N // tn, M // tm, nk) if n_outer else (M // tm, N // tn, nk),
        in_specs=in_specs, out_specs=out_specs,
        scratch_shapes=[pltpu.VMEM((tm, tn), _F32)] if nk > 1 else [],
        compiler_params=_params(("parallel", "parallel", "arbitrary"), blocks),
    )(*ins)


CONV_HALO = 16
CONV_CHUNK = 256


def _tap_windows(win, n_taps_plus1, tc):
    n = win.shape[0]
    for s in range(SUBLANES):
        shifted = win if s == 0 else pltpu.roll(win, n - s, 0)
        for q in range((n_taps_plus1 + SUBLANES - 1) // SUBLANES):
            o = SUBLANES * q + s
            if 1 <= o < n_taps_plus1:
                yield o, shifted[SUBLANES * q:SUBLANES * q + tc, :]


def _conv_fwd(proj, wdw, bdw, l, cw, name, deps=()):
    T = proj.shape[0]
    ks = wdw.shape[1]
    cb = LANES
    tc = _blk(T, CONV_CHUNK)
    nblk = cw // cb

    def body(a_ref, g_ref, w_ref, b_ref, c_ref, upad):
        zeros = jnp.zeros((CONV_HALO, cb), _F32)
        upad[0:CONV_HALO, :] = zeros
        upad[T + CONV_HALO:T + 2 * CONV_HALO, :] = zeros
        upad[CONV_HALO:T + CONV_HALO, :] = a_ref[...] * _sigmoid(g_ref[...])

        def chunk(i, carry):
            t0 = pl.multiple_of(i * tc, tc)
            win = upad[pl.ds(t0, tc + 2 * CONV_HALO), :]
            acc = jnp.broadcast_to(b_ref[...], (tc, cb))
            for o, rows in _tap_windows(win, ks + 1, tc):
                j = o + ks // 2 - CONV_HALO
                acc = acc + rows * w_ref[j:j + 1, :]
            c_ref[pl.ds(t0, tc), :] = acc
            return carry

        lax.fori_loop(0, T // tc, chunk, 0)

    col = lambda off: pl.BlockSpec((T, cb), lambda i, off=off: (0, off + i))
    return _pallas(
        body, name=name, deps=deps, out_shape=jax.ShapeDtypeStruct((T, cw), _F32), grid=(nblk,),
        in_specs=[col(0), col(nblk), pl.BlockSpec((None, ks, cb), lambda i: (l, 0, i)),
                  pl.BlockSpec((None, 1, cb), lambda i: (l, 0, i))],
        out_specs=pl.BlockSpec((T, cb), lambda i: (0, i)),
        scratch_shapes=[pltpu.VMEM((T + 2 * CONV_HALO, cb), _F32)],
        compiler_params=_params(("parallel",), 2 * T * cb * 4 * 3 + T * cb * 4),
    )(proj, proj, wdw, bdw.reshape(bdw.shape[0], 1, cw))


def _ln_silu_fwd(c, lng, lnb, out_cols, name):
    T, cw = c.shape
    tm = _blk(T, 512)

    def body(c_ref, g_ref, b_ref, y_ref):
        cv = c_ref[...]
        mu = jnp.mean(cv, axis=-1, keepdims=True)
        xc = cv - mu
        var = jnp.mean(xc * xc, axis=-1, keepdims=True)
        z = xc * lax.rsqrt(var + LN_EPS) * g_ref[...] + b_ref[...]
        y_ref[...] = (z * _sigmoid(z)).astype(y_ref.dtype)

    row = pl.BlockSpec((tm, cw), lambda i: (i, 0))
    (lng, g_spec), (lnb, b_spec) = _layer_row(lng), _layer_row(lnb)
    return _pallas(
        body, name=name, out_shape=jax.ShapeDtypeStruct((T, out_cols), _MXU), grid=(T // tm,),
        in_specs=[row, g_spec, b_spec], out_specs=row,
        compiler_params=_params(("parallel",), 2 * tm * cw * 6),
    )(c, lng, lnb)


def _ln_silu_bwd(c, lng, lnb, dycat, name, deps=()):
    T, cw = c.shape
    tm = _blk(T, 512)
    nsteps = T // tm

    def body(c_ref, g_ref, b_ref, dy_ref, dc_ref, dg_ref, db_ref, accg, accb):
        i = pl.program_id(0)
        cv = c_ref[...]
        mu = jnp.mean(cv, axis=-1, keepdims=True)
        xc = cv - mu
        var = jnp.mean(xc * xc, axis=-1, keepdims=True)
        rstd = lax.rsqrt(var + LN_EPS)
        xhat = xc * rstd
        z = xhat * g_ref[...] + b_ref[...]
        sg = _sigmoid(z)
        dz = dy_ref[...] * (sg * (1.0 + z * (1.0 - sg)))
        dxh = dz * g_ref[...]
        m1 = jnp.mean(dxh, axis=-1, keepdims=True)
        m2 = jnp.mean(dxh * xhat, axis=-1, keepdims=True)
        dc_ref[...] = rstd * (dxh - m1 - xhat * m2)

        @pl.when(i == 0)
        def _():
            accg[...] = jnp.zeros_like(accg)
            accb[...] = jnp.zeros_like(accb)

        accg[...] += jnp.sum((dz * xhat).reshape(tm // SUBLANES, SUBLANES, cw), axis=0)
        accb[...] += jnp.sum(dz.reshape(tm // SUBLANES, SUBLANES, cw), axis=0)

        @pl.when(i == nsteps - 1)
        def _():
            dg_ref[...] = jnp.sum(accg[...], axis=0, keepdims=True)
            db_ref[...] = jnp.sum(accb[...], axis=0, keepdims=True)

    row = pl.BlockSpec((tm, cw), lambda i: (i, 0))
    vec = pl.BlockSpec((1, cw), lambda i: (0, 0))
    (lng, g_spec), (lnb, b_spec) = _layer_row(lng), _layer_row(lnb)
    return _pallas(
        body, name=name, deps=deps,
        out_shape=(jax.ShapeDtypeStruct((T, cw), _F32), jax.ShapeDtypeStruct((1, cw), _F32),
                   jax.ShapeDtypeStruct((1, cw), _F32)),
        grid=(nsteps,), in_specs=[row, g_spec, b_spec, row], out_specs=(row, vec, vec),
        scratch_shapes=[pltpu.VMEM((SUBLANES, cw), _F32), pltpu.VMEM((SUBLANES, cw), _F32)],
        compiler_params=_params(("arbitrary",), 2 * tm * cw * 12),
    )(c, lng, lnb, dycat)


def _conv_bwd(proj, dc, wdw, l, cw, name, deps=()):
    T = proj.shape[0]
    ks = wdw.shape[1]
    cb = LANES
    tc = _blk(T, CONV_CHUNK)
    nblk = cw // cb
    half = ks // 2

    def body(a_ref, g_ref, dc_ref, w_ref, da_ref, dg_ref, dwb_ref, upad, dpad, du, wacc):
        zeros = jnp.zeros((CONV_HALO, cb), _F32)
        for pad in (upad, dpad):
            pad[0:CONV_HALO, :] = zeros
            pad[T + CONV_HALO:T + 2 * CONV_HALO, :] = zeros
        sg = _sigmoid(g_ref[...])
        upad[CONV_HALO:T + CONV_HALO, :] = a_ref[...] * sg
        dpad[CONV_HALO:T + CONV_HALO, :] = dc_ref[...]
        wacc[...] = jnp.zeros_like(wacc)

        def chunk(i, carry):
            t0 = pl.multiple_of(i * tc, tc)
            dwin = dpad[pl.ds(t0, tc + 2 * CONV_HALO), :]
            uwin = upad[pl.ds(t0, tc + 2 * CONV_HALO), :]
            dcc = dwin[CONV_HALO:CONV_HALO + tc, :]
            acc = jnp.zeros((tc, cb), _F32)
            for o, rows in _tap_windows(dwin, CONV_HALO + half + 1, tc):
                j = CONV_HALO + half - o
                if 0 <= j < ks:
                    acc = acc + rows * w_ref[j:j + 1, :]
            du[pl.ds(t0, tc), :] = acc
            for o, rows in _tap_windows(uwin, CONV_HALO + half + 1, tc):
                j = o + half - CONV_HALO
                if 0 <= j < ks:
                    wacc[j] += jnp.sum((rows * dcc).reshape(tc // SUBLANES, SUBLANES, cb), axis=0)
            wacc[ks] += jnp.sum(dcc.reshape(tc // SUBLANES, SUBLANES, cb), axis=0)
            return carry

        lax.fori_loop(0, T // tc, chunk, 0)
        duv = du[...]
        av = a_ref[...]
        da_ref[...] = (duv * sg).astype(da_ref.dtype)
        dg_ref[...] = (duv * av * sg * (1.0 - sg)).astype(dg_ref.dtype)
        dwb_ref[...] = jnp.sum(wacc[...], axis=1)

    col = lambda off: pl.BlockSpec((T, cb), lambda i, off=off: (0, off + i))
    blk = pl.BlockSpec((T, cb), lambda i: (0, i))
    return _pallas(
        body, name=name, deps=deps,
        out_shape=(jax.ShapeDtypeStruct((T, cw), _MXU), jax.ShapeDtypeStruct((T, cw), _MXU),
                   jax.ShapeDtypeStruct((ks + 1, cw), _F32)),
        grid=(nblk,),
        in_specs=[col(0), col(nblk), blk, pl.BlockSpec((None, ks, cb), lambda i: (l, 0, i))],
        out_specs=(blk, blk, pl.BlockSpec((ks + 1, cb), lambda i: (0, i))),
        scratch_shapes=[pltpu.VMEM((T + 2 * CONV_HALO, cb), _F32), pltpu.VMEM((T + 2 * CONV_HALO, cb), _F32),
                        pltpu.VMEM((T, cb), _F32), pltpu.VMEM((ks + 1, SUBLANES, cb), _F32)],
        compiler_params=_params(("parallel",), 2 * T * cb * 4 * 4 + 3 * T * cb * 4),
    )(proj, proj, dc, wdw)


N_CLS = WIN_ROWS
N_DR = 2 * WIN_ROWS - 1
N_DC = 2 * WIN_COLS - 1
BAND = WIN_ROWS * GRID_W
QK_SCALE = HEAD_DIM ** -0.5
ROWS_PER_STEP_FWD = 8
ROWS_PER_STEP_BWD = 4
_NT = (((1,), (1,)), ((), ()))
_TN = (((0,), (0,)), ((), ()))


def _slab_iotas():
    wk = lax.broadcasted_iota(jnp.int32, (GRID_W, LANES), 0)
    lane = lax.broadcasted_iota(jnp.int32, (GRID_W, LANES), 1)
    wq = jnp.bitwise_and(lane, GRID_W - 1)
    head1 = lane >= GRID_W
    d = wk - wq + (WIN_COLS - 1)
    cs = jnp.clip(wq - WIN_COLS // 2, 0, GRID_W - WIN_COLS)
    window = (wk >= cs) & (wk < cs + WIN_COLS)
    return d, head1, window


def _pair_tiles(cls):
    return [(cls + 2 * p, p) for p in range(WIN_ROWS // 2)]


def _bias_table(rpb2, name):
    npair = rpb2.shape[0] // 2

    def body(rpb_ref, ot_ref, on_ref):
        p = pl.program_id(0)
        d, head1, window = _slab_iotas()
        slabs = []
        for dr in range(N_DR):
            val = jnp.zeros((GRID_W, LANES), _F32)
            for j in range(N_DC):
                s0 = rpb_ref[2 * p, dr * N_DC + j]
                s1 = rpb_ref[2 * p + 1, dr * N_DC + j]
                val = jnp.where(d == j, jnp.where(head1, s1, s0), val)
            slab = jnp.where(window, val, NEG_INF)
            slabs.append(slab)
            for cls in range(N_CLS):
                k = dr - cls
                if 0 <= k < WIN_ROWS:
                    ot_ref[cls, k * GRID_W:(k + 1) * GRID_W, :] = slab
        for e in range(N_DR - 1):
            tile = jnp.concatenate([slabs[e], slabs[e + 1]], axis=0).T
            for cls in range(N_CLS):
                for ee, pp in _pair_tiles(cls):
                    if ee == e:
                        on_ref[cls, :, pp * LANES:(pp + 1) * LANES] = tile

    return pl.pallas_call(
        body, name=name,
        out_shape=(jax.ShapeDtypeStruct((npair, N_CLS, BAND, LANES), _F32),
                   jax.ShapeDtypeStruct((npair, N_CLS, LANES, BAND), _F32)),
        grid=(npair,),
        in_specs=[pl.BlockSpec(memory_space=pltpu.SMEM)],
        out_specs=(pl.BlockSpec((None, N_CLS, BAND, LANES), lambda p: (p, 0, 0, 0)),
                   pl.BlockSpec((None, N_CLS, LANES, BAND), lambda p: (p, 0, 0, 0))),
        compiler_params=_params(("arbitrary",), 4 * N_CLS * BAND * LANES * 4),
    )(rpb2)


def _rpb_grad(gc, name):
    npair = gc.shape[0]

    def body(g_ref, o_ref):
        d, _, _ = _slab_iotas()
        rowi = lax.broadcasted_iota(jnp.int32, (4 * SUBLANES, LANES), 0)
        lanei = lax.broadcasted_iota(jnp.int32, (4 * SUBLANES, LANES), 1)
        head1 = lax.broadcasted_iota(jnp.int32, (1, LANES), 1) >= GRID_W
        tiles = [jnp.zeros((4 * SUBLANES, LANES), _F32) for _ in range(2)]
        yts = []
        for e in range(N_DR - 1):
            y = jnp.zeros((LANES, LANES), _F32)
            for cls in range(N_CLS):
                for ee, pp in _pair_tiles(cls):
                    if ee == e:
                        y = y + g_ref[cls, :, pp * LANES:(pp + 1) * LANES]
            yts.append(y.T)
        for dr in range(N_DR):
            ysum = jnp.zeros((GRID_W, LANES), _F32)
            if dr < N_DR - 1:
                ysum = ysum + yts[dr][:GRID_W]
            if dr >= 1:
                ysum = ysum + yts[dr - 1][GRID_W:]
            for j in range(N_DC):
                cs = jnp.sum(jnp.where(d == j, ysum, 0.0), axis=0, keepdims=True)
                s0 = jnp.sum(jnp.where(head1, 0.0, cs), axis=1, keepdims=True)
                s1 = jnp.sum(jnp.where(head1, cs, 0.0), axis=1, keepdims=True)
                here = (rowi == j) & (lanei == dr)
                tiles[0] = tiles[0] + jnp.where(here, s0, 0.0)
                tiles[1] = tiles[1] + jnp.where(here, s1, 0.0)
        o_ref[0] = tiles[0]
        o_ref[1] = tiles[1]

    return _pallas(
        body, name=name, out_shape=jax.ShapeDtypeStruct((npair, 2, 4 * SUBLANES, LANES), _F32), grid=(npair,),
        in_specs=[pl.BlockSpec((None, N_CLS, LANES, BAND), lambda p: (p, 0, 0, 0))],
        out_specs=pl.BlockSpec((None, 2, 4 * SUBLANES, LANES), lambda p: (p, 0, 0, 0)),
        compiler_params=_params(("parallel",), 2 * N_CLS * BAND * LANES * 4),
    )(gc)


def _block_diag(v, diag):
    return jnp.where(diag, jnp.concatenate([v, v], axis=0), 0.0).astype(_MXU)


def _diag_mask():
    r = lax.broadcasted_iota(jnp.int32, (LANES, LANES), 0) < GRID_W
    c = lax.broadcasted_iota(jnp.int32, (LANES, LANES), 1) < HEAD_DIM
    return r == c


def _row_geometry(r, rows):
    rs = jnp.clip(r - WIN_ROWS // 2, 0, rows - WIN_ROWS)
    cls = rs - r + (WIN_ROWS - 1)
    return pl.multiple_of(r * GRID_W, GRID_W), pl.multiple_of(rs * GRID_W, GRID_W), cls


def _probs_t(qsel, kband, bias):
    s = lax.dot_general(kband, qsel, _NT, preferred_element_type=_F32) + bias
    mx = jnp.max(s, axis=0, keepdims=True)
    e = jnp.exp(s - mx)
    return e * (1.0 / jnp.sum(e, axis=0, keepdims=True))


def _attn_fwd(proj, bias, ycat, cw, naw, name):
    T = proj.shape[0]
    rows = T // GRID_W
    npair = naw // LANES
    qoff, koff, voff = 2 * cw // LANES, (2 * cw + naw) // LANES, (2 * cw + 2 * naw) // LANES

    def body(q_ref, k_ref, v_ref, b_ref, ycat_ref, o_ref, kb, vb):
        kb[...] = k_ref[...].astype(_MXU)
        vb[...] = v_ref[...].astype(_MXU)
        diag = _diag_mask()
        m0 = lax.broadcasted_iota(jnp.int32, (GRID_W, LANES), 1) < HEAD_DIM

        def step(i, carry):
            us = range(ROWS_PER_STEP_FWD)
            geo = [_row_geometry(ROWS_PER_STEP_FWD * i + u, rows) for u in us]
            qsel = [_block_diag(q_ref[pl.ds(t0, GRID_W), :] * QK_SCALE, diag) for t0, _, _ in geo]
            kbands = [kb[pl.ds(b0, BAND), :] for _, b0, _ in geo]
            vbands = [vb[pl.ds(b0, BAND), :] for _, b0, _ in geo]
            biases = [b_ref[cls] for _, _, cls in geo]
            pts = [_probs_t(qsel[u], kbands[u], biases[u]) for u in us]
            ofs = [lax.dot_general(pts[u].astype(_MXU), vbands[u], _TN, preferred_element_type=_F32) for u in us]
            for u in us:
                o_ref[pl.ds(geo[u][0], GRID_W), :] = jnp.where(m0, ofs[u][:GRID_W], ofs[u][GRID_W:]).astype(o_ref.dtype)
            return carry

        lax.fori_loop(0, rows // ROWS_PER_STEP_FWD, step, 0)

    col = lambda off: pl.BlockSpec((T, LANES), lambda i, off=off: (0, off + i))
    return _pallas(
        body, name=name, out_shape=jax.ShapeDtypeStruct(ycat.shape, ycat.dtype), grid=(npair,),
        in_specs=[col(qoff), col(koff), col(voff),
                  pl.BlockSpec((None, N_CLS, BAND, LANES), lambda i: (i, 0, 0, 0)), _ANY],
        out_specs=pl.BlockSpec((T, LANES), lambda i: (0, cw // LANES + i)),
        input_output_aliases={4: 0},
        scratch_shapes=[pltpu.VMEM((T, LANES), _MXU), pltpu.VMEM((T, LANES), _MXU)],
        compiler_params=_params(("parallel",), 2 * (3 * T * LANES * 4 + N_CLS * BAND * LANES * 4 + T * LANES * 2)),
    )(proj, proj, proj, bias, ycat)


def _attn_bwd(proj, bias, dycat, cw, naw, name):
    T = proj.shape[0]
    rows = T // GRID_W
    npair = naw // LANES
    qoff, koff, voff = 2 * cw // LANES, (2 * cw + naw) // LANES, (2 * cw + 2 * naw) // LANES
    doff = cw // LANES

    def body(q_ref, k_ref, v_ref, b_ref, do_ref, dq_ref, dk_ref, dv_ref, g_ref, kb, vb, dka, dva):
        kb[...] = k_ref[...].astype(_MXU)
        vb[...] = v_ref[...].astype(_MXU)
        dka[...] = jnp.zeros_like(dka)
        dva[...] = jnp.zeros_like(dva)
        g_ref[...] = jnp.zeros_like(g_ref)
        diag = _diag_mask()
        m0 = lax.broadcasted_iota(jnp.int32, (GRID_W, LANES), 1) < HEAD_DIM

        def step(i, carry):
            us = range(ROWS_PER_STEP_BWD)
            geo = [_row_geometry(ROWS_PER_STEP_BWD * i + u, rows) for u in us]
            qsel = [_block_diag(q_ref[pl.ds(t0, GRID_W), :] * QK_SCALE, diag) for t0, _, _ in geo]
            dosel = [_block_diag(do_ref[pl.ds(t0, GRID_W), :], diag) for t0, _, _ in geo]
            kbands = [kb[pl.ds(b0, BAND), :] for _, b0, _ in geo]
            vbands = [vb[pl.ds(b0, BAND), :] for _, b0, _ in geo]
            biases = [b_ref[cls] for _, _, cls in geo]
            dsts, dqs, dks, dvs = [], [], [], []
            for u in us:
                sc = lax.dot_general(qsel[u], kbands[u], _NT, preferred_element_type=_F32) + biases[u]
                ex = jnp.exp(sc - jnp.max(sc, axis=1, keepdims=True))
                p = ex * (1.0 / jnp.sum(ex, axis=1, keepdims=True))
                dp = lax.dot_general(dosel[u], vbands[u], _NT, preferred_element_type=_F32)
                delta = jnp.sum(p * dp, axis=1, keepdims=True)
                dst = p * (dp - delta)
                dsb = dst.astype(_MXU)
                dqf = jnp.dot(dsb, kbands[u], preferred_element_type=_F32)
                dsts.append(dst)
                dqs.append((jnp.where(m0, dqf[:GRID_W], dqf[GRID_W:]) * QK_SCALE).astype(dq_ref.dtype))
                dks.append(lax.dot_general(qsel[u], dsb, _TN, preferred_element_type=_F32).T)
                dvs.append(lax.dot_general(dosel[u], p.astype(_MXU), _TN, preferred_element_type=_F32).T)
            for u in us:
                t0, b0, cls = geo[u]
                g_ref[cls] += dsts[u]
                dq_ref[pl.ds(t0, GRID_W), :] = dqs[u]
                dka[pl.ds(b0, BAND), :] += dks[u]
                dva[pl.ds(b0, BAND), :] += dvs[u]
            return carry

        lax.fori_loop(0, rows // ROWS_PER_STEP_BWD, step, 0)
        dk_ref[...] = dka[...].astype(dk_ref.dtype)
        dv_ref[...] = dva[...].astype(dv_ref.dtype)

    col = lambda off: pl.BlockSpec((T, LANES), lambda i, off=off: (0, off + i))
    blk = pl.BlockSpec((T, LANES), lambda i: (0, i))
    tbl = pl.BlockSpec((None, N_CLS, LANES, BAND), lambda i: (i, 0, 0, 0))
    o16 = jax.ShapeDtypeStruct((T, naw), _MXU)
    vm = 2 * (4 * T * LANES * 4 + 2 * N_CLS * BAND * LANES * 4 + 3 * T * LANES * 2) + 2 * T * LANES * 6
    return _pallas(
        body, name=name,
        out_shape=(o16, o16, o16, jax.ShapeDtypeStruct((npair, N_CLS, LANES, BAND), _F32)),
        grid=(npair,),
        in_specs=[col(qoff), col(koff), col(voff), tbl, col(doff)],
        out_specs=(blk, blk, blk, tbl),
        scratch_shapes=[pltpu.VMEM((T, LANES), _MXU), pltpu.VMEM((T, LANES), _MXU),
                        pltpu.VMEM((T, LANES), _F32), pltpu.VMEM((T, LANES), _F32)],
        compiler_params=_params(("parallel",), vm),
    )(proj, proj, proj, bias, dycat)


_ANY = pl.BlockSpec(memory_space=pl.ANY)
_HBM = pl.BlockSpec(memory_space=pltpu.HBM)
_SEM = pl.BlockSpec(memory_space=pltpu.SEMAPHORE)
_VMEM = pl.BlockSpec(memory_space=pltpu.VMEM)
_MESH_ID = pl.DeviceIdType.MESH
_EFFECT = pltpu.SideEffectType.DATAFLOW_SIDE_EFFECTING
_TOKEN = jax.ShapeDtypeStruct((SUBLANES, LANES), _F32)


def _mesh_pos():
    return tuple(lax.axis_index(a) for a in MESH_AXES)


def _in_hbm(a):
    return pltpu.with_memory_space_constraint(a, pltpu.HBM)


def _hbm_like(arrays):
    return [pltpu.HBM(a.shape, a.dtype) for a in arrays]


def _shard_ref(ref, axis, j, width):
    idx = [slice(None)] * len(ref.shape)
    idx[axis] = pl.ds(pl.multiple_of(j * width, math.gcd(width, LANES)), width)
    return ref.at[tuple(idx)]


def _all_gather(shards, axes, name):
    n = len(shards)
    widths = [s.shape[a] for s, a in zip(shards, axes)]
    out_shape = [jax.ShapeDtypeStruct(tuple(N_DEV * d if k == a else d for k, d in enumerate(s.shape)), s.dtype)
                 for s, a in zip(shards, axes)]

    def body(*refs):
        ins, outs = refs[:n], refs[n:2 * n]
        send_sems, recv_sems, local_sems = refs[2 * n:]
        x, y, c = _mesh_pos()
        me, sibling = (x, y, c), (x, y, 1 - c)
        chips = [(1 - x, y), (x, 1 - y), (1 - x, 1 - y)]

        def slot(i, px, py, pc):
            return _shard_ref(outs[i], axes[i], 4 * px + 2 * py + pc, widths[i])

        def copy(i, k, block, to, src=None):
            return pltpu.make_async_remote_copy(
                src_ref=slot(i, *block) if src is None else src, dst_ref=slot(i, *block),
                send_sem=send_sems.at[7 * i + k], recv_sem=recv_sems.at[7 * i + k],
                device_id=to, device_id_type=_MESH_ID)

        mine = [pltpu.make_async_copy(ins[i], slot(i, *me), local_sems.at[i]) for i in range(n)]
        for cp in mine:
            cp.start()
        first = []
        for i in range(n):
            first.append(copy(i, 0, me, sibling, src=ins[i]))
            first += [copy(i, 1 + j, me, (*chip, c), src=ins[i]) for j, chip in enumerate(chips)]
        for cp in first:
            cp.start()
        passed = []
        for j, chip in enumerate(chips):
            for i in range(n):
                copy(i, 1 + j, (*chip, c), me).wait_recv()
                fwd = copy(i, 4 + j, (*chip, c), sibling)
                fwd.start()
                passed.append(fwd)
        for i in range(n):
            copy(i, 0, sibling, me).wait_recv()
            for j, chip in enumerate(chips):
                copy(i, 4 + j, (*chip, 1 - c), me).wait_recv()
        for cp in first + passed:
            cp.wait_send()
        for cp in mine:
            cp.wait()

    return _pallas(
        body, name=name, out_shape=out_shape, in_specs=[_ANY] * n, out_specs=[_ANY] * n,
        scratch_shapes=[pltpu.SemaphoreType.DMA((7 * n,)), pltpu.SemaphoreType.DMA((7 * n,)),
                        pltpu.SemaphoreType.DMA((n,))],
    )(*shards)


def _block_of(ref, axis, blk):
    return _shard_ref(ref, axis, blk, ref.shape[axis] // N_DEV)


def _gather_start(lands, axes, after, name):
    n = len(lands)

    def body(*refs):
        land = refs[:n]
        send, recv_sib, recv_ici = refs[n + 1:n + 4]
        token = refs[-1]
        x, y, c = _mesh_pos()
        me = 4 * x + 2 * y + c
        for i in range(n):
            mine = _block_of(land[i], axes[i], me)
            pltpu.make_async_remote_copy(
                src_ref=mine, dst_ref=mine, send_sem=send.at[4 * i],
                recv_sem=recv_sib.at[i], device_id=(x, y, 1 - c), device_id_type=_MESH_ID).start()
            for j, chip in enumerate([(1 - x, y), (x, 1 - y), (1 - x, 1 - y)]):
                pltpu.make_async_remote_copy(
                    src_ref=mine, dst_ref=mine, send_sem=send.at[4 * i + 1 + j],
                    recv_sem=recv_ici.at[3 * i + j], device_id=(*chip, c), device_id_type=_MESH_ID).start()
        token[...] = jnp.zeros_like(token)

    dma = pltpu.SemaphoreType.DMA
    out = pl.pallas_call(
        body, name=name,
        out_shape=(dma((4 * n,)), dma((n,)), dma((3 * n,)), *_hbm_like(lands), _TOKEN),
        in_specs=[_HBM] * n + [_ANY], out_specs=(_SEM, _SEM, _SEM, *[_HBM] * n, _VMEM),
        input_output_aliases={i: 3 + i for i in range(n)},
        compiler_params=pltpu.CompilerParams(has_side_effects=_EFFECT),
    )(*[_in_hbm(a) for a in lands], after)
    return dict(send=out[0], recv_sib=out[1], recv_ici=out[2], lands=list(out[3:3 + n]), axes=axes, token=out[-1])


def _gather_forward(st, after, name):
    lands, axes = st["lands"], st["axes"]
    n = len(lands)

    def body(*refs):
        land = refs[:n]
        recv_ici = refs[n]
        send2, recv2 = refs[n + 2], refs[n + 3]
        token = refs[-1]
        x, y, c = _mesh_pos()
        for j, (px, py) in enumerate([(1 - x, y), (x, 1 - y), (1 - x, 1 - y)]):
            for i in range(n):
                blk = _block_of(land[i], axes[i], 4 * px + 2 * py + c)
                pltpu.make_async_remote_copy(
                    src_ref=blk, dst_ref=blk, send_sem=send2.at[3 * i + j],
                    recv_sem=recv_ici.at[3 * i + j], device_id=(px, py, c), device_id_type=_MESH_ID).wait_recv()
                pltpu.make_async_remote_copy(
                    src_ref=blk, dst_ref=blk, send_sem=send2.at[3 * i + j],
                    recv_sem=recv2.at[3 * i + j], device_id=(x, y, 1 - c), device_id_type=_MESH_ID).start()
        token[...] = jnp.zeros_like(token)

    dma = pltpu.SemaphoreType.DMA
    out = pl.pallas_call(
        body, name=name,
        out_shape=(dma((3 * n,)), dma((3 * n,)), *_hbm_like(lands), _TOKEN),
        in_specs=[_HBM] * n + [_SEM, _ANY], out_specs=(_SEM, _SEM, *[_HBM] * n, _VMEM),
        input_output_aliases={i: 2 + i for i in range(n)},
        compiler_params=pltpu.CompilerParams(has_side_effects=_EFFECT),
    )(*lands, st["recv_ici"], after)
    return dict(st, send2=out[0], recv2=out[1], lands=list(out[2:2 + n]), token=out[-1])


def _gather_finish(st, after, name):
    lands, axes = st["lands"], st["axes"]
    n = len(lands)

    def body(*refs):
        land = refs[:n]
        send, recv_sib, send2, recv2 = refs[n:n + 4]
        x, y, c = _mesh_pos()
        me = 4 * x + 2 * y + c
        sib = 4 * x + 2 * y + (1 - c)

        def desc(i, blk, s_sem, r_sem):
            ref = _block_of(land[i], axes[i], blk)
            return pltpu.make_async_remote_copy(
                src_ref=ref, dst_ref=ref, send_sem=s_sem, recv_sem=r_sem,
                device_id=(x, y, 1 - c), device_id_type=_MESH_ID)

        for i in range(n):
            desc(i, sib, send.at[4 * i], recv_sib.at[i]).wait_recv()
            for j, (px, py) in enumerate([(1 - x, y), (x, 1 - y), (1 - x, 1 - y)]):
                desc(i, 4 * px + 2 * py + (1 - c), send2.at[3 * i + j], recv2.at[3 * i + j]).wait_recv()
            for k in range(4):
                desc(i, me, send.at[4 * i + k], recv_sib.at[i]).wait_send()
            for j, (px, py) in enumerate([(1 - x, y), (x, 1 - y), (1 - x, 1 - y)]):
                desc(i, 4 * px + 2 * py + c, send2.at[3 * i + j], recv2.at[3 * i + j]).wait_send()

    out = pl.pallas_call(
        body, name=name, out_shape=tuple(_hbm_like(lands)),
        in_specs=[_HBM] * n + [_SEM] * 4 + [_ANY], out_specs=tuple([_HBM] * n),
        input_output_aliases={i: i for i in range(n)},
        compiler_params=pltpu.CompilerParams(has_side_effects=_EFFECT),
    )(*lands, st["send"], st["recv_sib"], st["send2"], st["recv2"], after)
    return list(out)


def _scatter_sibling_start(grads, name):
    n = len(grads)
    gots = [lax.empty((4,) + g.shape[1:], g.dtype) for g in grads]

    def body(*refs):
        grad, got = refs[:n], refs[n:2 * n]
        send, recv = refs[2 * n], refs[2 * n + 1]
        token = refs[-1]
        x, y, c = _mesh_pos()
        for i in range(n):
            for q in range(4):
                pltpu.make_async_remote_copy(
                    src_ref=grad[i].at[2 * q + (1 - c)], dst_ref=got[i].at[q], send_sem=send.at[4 * i + q],
                    recv_sem=recv.at[4 * i + q], device_id=(x, y, 1 - c), device_id_type=_MESH_ID).start()
        token[...] = jnp.zeros_like(token)

    dma = pltpu.SemaphoreType.DMA
    out = pl.pallas_call(
        body, name=name,
        out_shape=(dma((4 * n,)), dma((4 * n,)), *_hbm_like(grads), *_hbm_like(gots), _TOKEN),
        in_specs=[_HBM] * (2 * n), out_specs=(_SEM, _SEM, *[_HBM] * (2 * n), _VMEM),
        input_output_aliases={i: 2 + i for i in range(2 * n)},
        compiler_params=pltpu.CompilerParams(has_side_effects=_EFFECT),
    )(*[_in_hbm(a) for a in grads], *[_in_hbm(a) for a in gots])
    return dict(send=out[0], recv=out[1], grads=list(out[2:2 + n]), gots=list(out[2 + n:2 + 2 * n]), token=out[-1])


def _scatter_sibling_finish(st, after, name):
    grads, gots = st["grads"], st["gots"]
    n = len(grads)

    def body(*refs):
        grad, got = refs[:n], refs[n:2 * n]
        send, recv = refs[2 * n], refs[2 * n + 1]
        x, y, c = _mesh_pos()
        for i in range(n):
            for q in range(4):
                cp = pltpu.make_async_remote_copy(
                    src_ref=grad[i].at[2 * q + (1 - c)], dst_ref=got[i].at[q], send_sem=send.at[4 * i + q],
                    recv_sem=recv.at[4 * i + q], device_id=(x, y, 1 - c), device_id_type=_MESH_ID)
                cp.wait_recv()
                cp.wait_send()

    out = pl.pallas_call(
        body, name=name, out_shape=tuple(_hbm_like(grads) + _hbm_like(gots)),
        in_specs=[_HBM] * (2 * n) + [_SEM, _SEM, _ANY], out_specs=tuple([_HBM] * (2 * n)),
        input_output_aliases={i: i for i in range(2 * n)},
        compiler_params=pltpu.CompilerParams(has_side_effects=_EFFECT),
    )(*grads, *gots, st["send"], st["recv"], after)
    return list(out[:n]), list(out[n:])


def _scatter_add(grad, got, name):
    _, R, C = grad.shape
    tr = _blk(R, 512)
    my_c = lambda: lax.axis_index("c")
    my_chip = lambda: 2 * lax.axis_index("x") + lax.axis_index("y")

    def body(a_ref, b_ref, part_ref, fin_ref):
        s = (a_ref[...].astype(_F32) + b_ref[...].astype(_F32)).astype(part_ref.dtype)
        part_ref[...] = s

        @pl.when(pl.program_id(1) == my_chip())
        def _():
            fin_ref[...] = s

    shape = jax.ShapeDtypeStruct((4, R, C), grad.dtype)
    return _pallas(
        body, name=name, out_shape=(shape, shape), grid=(R // tr, 4),
        in_specs=[pl.BlockSpec((None, tr, C), lambda i, q: (2 * q + my_c(), i, 0)),
                  pl.BlockSpec((None, tr, C), lambda i, q: (q, i, 0))],
        out_specs=[pl.BlockSpec((None, tr, C), lambda i, q: (q, i, 0)),
                   pl.BlockSpec((None, tr, C), lambda i, q: (my_chip(), i, 0))],
        compiler_params=_params(("parallel", "arbitrary"), 2 * tr * C * 8),
    )(grad, got)


def _scatter_chips_start(parts, fins, name):
    n = len(parts)

    def body(*refs):
        part, fin = refs[:n], refs[n:2 * n]
        send, recv = refs[2 * n], refs[2 * n + 1]
        token = refs[-1]
        x, y, c = _mesh_pos()
        mine = 2 * x + y
        for i in range(n):
            for k, (tx, ty) in enumerate([(1 - x, y), (x, 1 - y), (1 - x, 1 - y)]):
                pltpu.make_async_remote_copy(
                    src_ref=part[i].at[2 * tx + ty], dst_ref=fin[i].at[mine], send_sem=send.at[3 * i + k],
                    recv_sem=recv.at[3 * i + k], device_id=(tx, ty, c), device_id_type=_MESH_ID).start()
        token[...] = jnp.zeros_like(token)

    dma = pltpu.SemaphoreType.DMA
    out = pl.pallas_call(
        body, name=name,
        out_shape=(dma((3 * n,)), dma((3 * n,)), *_hbm_like(parts), *_hbm_like(fins), _TOKEN),
        in_specs=[_HBM] * (2 * n), out_specs=(_SEM, _SEM, *[_HBM] * (2 * n), _VMEM),
        input_output_aliases={i: 2 + i for i in range(2 * n)},
        compiler_params=pltpu.CompilerParams(has_side_effects=_EFFECT),
    )(*[_in_hbm(a) for a in parts], *[_in_hbm(a) for a in fins])
    return dict(send=out[0], recv=out[1], parts=list(out[2:2 + n]), fins=list(out[2 + n:2 + 2 * n]), token=out[-1])


def _scatter_chips_finish(st, after, name):
    parts, fins = st["parts"], st["fins"]
    n = len(parts)

    def body(*refs):
        part, fin = refs[:n], refs[n:2 * n]
        send, recv = refs[2 * n], refs[2 * n + 1]
        x, y, c = _mesh_pos()
        for i in range(n):
            for k, (tx, ty) in enumerate([(1 - x, y), (x, 1 - y), (1 - x, 1 - y)]):
                cp = pltpu.make_async_remote_copy(
                    src_ref=part[i].at[2 * tx + ty], dst_ref=fin[i].at[2 * tx + ty], send_sem=send.at[3 * i + k],
                    recv_sem=recv.at[3 * i + k], device_id=(tx, ty, c), device_id_type=_MESH_ID)
                cp.wait_recv()
                cp.wait_send()

    out = pl.pallas_call(
        body, name=name, out_shape=tuple(_hbm_like(parts) + _hbm_like(fins)),
        in_specs=[_HBM] * (2 * n) + [_SEM, _SEM, _ANY], out_specs=tuple([_HBM] * (2 * n)),
        input_output_aliases={i: i for i in range(2 * n)},
        compiler_params=pltpu.CompilerParams(has_side_effects=_EFFECT),
    )(*parts, *fins, st["send"], st["recv"], after)
    return list(out[n:])


def _adamw(g, w, m, v):
    m = ADAM_B1 * m + (1.0 - ADAM_B1) * g
    v = ADAM_B2 * v + (1.0 - ADAM_B2) * (g * g)
    m_hat = m / (1.0 - ADAM_B1 ** ADAM_STEP)
    v_hat = v / (1.0 - ADAM_B2 ** ADAM_STEP)
    delta = -ADAM_LR * (m_hat / (jnp.sqrt(v_hat) + ADAM_EPS) + ADAM_WD * w)
    return delta, m, v


def _adam_layer(fin, w3, m3, v3, l, prev, name):
    L, R, C = w3.shape
    tr = _blk(R, max(SUBLANES, (1 << 18) // C))

    def body(f_ref, w_ref, m_ref, v_ref, *rest):
        g_ref, d_ref, nm_ref, nv_ref = rest[-4:]
        g = ((f_ref[0].astype(_F32) + f_ref[1].astype(_F32)) + f_ref[2].astype(_F32)) + f_ref[3].astype(_F32)
        d, nm, nv = _adamw(g, w_ref[...], m_ref[...], v_ref[...])
        g_ref[...] = g
        d_ref[...] = d
        nm_ref[...] = nm
        nv_ref[...] = nv

    lay = pl.BlockSpec((None, tr, C), lambda i: (l, i, 0))
    ins = [fin, w3, m3, v3]
    in_specs = [pl.BlockSpec((4, tr, C), lambda i: (0, i, 0)), lay, lay, lay]
    aliases = {}
    if prev is not None:
        ins += list(prev)
        in_specs += [_ANY] * 4
        aliases = {4 + k: k for k in range(4)}
    return _pallas(
        body, name=name, out_shape=[jax.ShapeDtypeStruct((L, R, C), _F32)] * 4, grid=(R // tr,),
        in_specs=in_specs, out_specs=[lay] * 4, input_output_aliases=aliases,
        compiler_params=_params(("parallel",), 2 * tr * C * (4 * 2 + 7 * 4)),
    )(*ins)


def _sum_parts(parts, name):
    _, R, C = parts.shape

    def body(p_ref, o_ref):
        acc = p_ref[0]
        for k in range(1, N_DEV):
            acc = acc + p_ref[k]
        o_ref[...] = acc

    tr = _blk(R, 512)
    return _pallas(
        body, name=name, out_shape=jax.ShapeDtypeStruct((R, C), _F32), grid=(R // tr,),
        in_specs=[pl.BlockSpec((N_DEV, tr, C), lambda i: (0, i, 0))],
        out_specs=pl.BlockSpec((tr, C), lambda i: (i, 0)),
        compiler_params=_params(("parallel",), 2 * tr * C * 4 * 9),
    )(parts)


def _adam_flat(g, w, m, v, name):
    R, C = g.shape
    tr = _blk(R, 512)

    def body(g_ref, w_ref, m_ref, v_ref, d_ref, nm_ref, nv_ref):
        d, nm, nv = _adamw(g_ref[...], w_ref[...], m_ref[...], v_ref[...])
        d_ref[...] = d
        nm_ref[...] = nm
        nv_ref[...] = nv

    spec = pl.BlockSpec((tr, C), lambda i: (i, 0))
    return _pallas(
        body, name=name, out_shape=[jax.ShapeDtypeStruct((R, C), _F32)] * 3, grid=(R // tr,),
        in_specs=[spec] * 4, out_specs=[spec] * 3,
        compiler_params=_params(("parallel",), 2 * tr * C * 4 * 7),
    )(g, w, m, v)


def _pack(arrays):
    flat = jnp.concatenate([a.reshape(-1) for a in arrays])
    tile = SUBLANES * LANES
    pad = (-flat.shape[0]) % tile
    return jnp.pad(flat, (0, pad)).reshape(-1, LANES)


def _unpack(packed, shapes):
    flat = packed.reshape(-1)
    out, off = [], 0
    for s in shapes:
        n = math.prod(s)
        out.append(flat[off:off + n].reshape(s))
        off += n
    return out


def kernel(x, w_in, w_dw, b_dw, conv_ln_g, conv_ln_b, rpb, w_out, w_up, w_down, pre_mix_g, post_mix_g, pre_mlp_g, post_mlp_g, loss_target, m_w_in, m_w_dw, m_b_dw, m_conv_ln_g, m_conv_ln_b, m_rpb, m_w_out, m_w_up, m_w_down, m_pre_mix_g, m_post_mix_g, m_pre_mlp_g, m_post_mlp_g, v_w_in, v_w_dw, v_b_dw, v_conv_ln_g, v_conv_ln_b, v_rpb, v_w_out, v_w_up, v_w_down, v_pre_mix_g, v_post_mix_g, v_pre_mlp_g, v_post_mlp_g):
    _, T, D = x.shape
    L = w_in.shape[0]
    cw = b_dw.shape[1]
    H = rpb.shape[1]
    naw = H * HEAD_DIM
    ks = w_dw.shape[1]
    assert T % GRID_W == 0 and T // GRID_W >= WIN_ROWS and H % 2 == 0 and cw % LANES == 0
    assert rpb.shape[2:] == (N_DR, N_DC) and w_dw.shape[2] * N_DEV == cw and ks // 2 < CONV_HALO
    assert naw == cw and w_out.shape[1] * N_DEV == cw + naw and (T // GRID_W) % ROWS_PER_STEP_FWD == 0

    xs = x.reshape(T, D)
    tgt = loss_target.reshape(T, D)
    row = lambda p, l: (p, l)
    mx, my, mc = (lax.axis_index(a) for a in MESH_AXES)
    dev = 4 * mx + 2 * my + mc

    ks_pad = ks + (-ks) % SUBLANES
    wdw_pad = jnp.pad(w_dw, ((0, 0), (0, ks_pad - ks), (0, 0))).reshape(L * ks_pad, w_dw.shape[2])
    wdw_full = _all_gather([wdw_pad], [1], "ag_wdw")[0].reshape(L, ks_pad, cw)[:, :ks]

    big = (w_in, w_out, w_up, w_down)
    names = ("in", "out", "up", "down")

    big_axes = (1, 0, 1, 0)

    def gather_start(l, which, after):
        lands = [_cast_slot(big[k], l, big_axes[k] == 1, f"cast_{names[k]}") for k in which]
        return _gather_start(lands, [big_axes[k] for k in which], after, "gather_start_%d" % len(which))

    def gather_pair(l, after):
        g_in = gather_start(l, [0], after)
        return g_in, gather_start(l, [1, 2, 3], g_in["token"])

    saved = []
    xin = xs
    h = _norm_fwd(xs, row(pre_mix_g, 0), "norm_first")
    g_in, g_rest = gather_pair(0, wdw_full)
    g_in = _gather_forward(g_in, g_rest["token"], "gather_forward_1")
    Win = _gather_finish(g_in, g_in["token"], "gather_finish_1")[0]
    dy = loss_sum = None
    for l in range(L):
        nxt = gather_pair(l + 1, Win) if l + 1 < L else None
        proj = _matmul(h, Win, name="mm_proj")
        behind = [nxt[1]["token"]] if nxt else []
        if l > 0:
            g_rest = _gather_forward(g_rest, proj, "gather_forward_3")
            behind.append(g_rest["token"])
        c = _conv_fwd(proj, wdw_full, b_dw, l, cw, "conv_fwd", deps=tuple(behind))
        yc = _ln_silu_fwd(c, row(conv_ln_g, l), row(conv_ln_b, l), cw + naw, "ln_silu_fwd")
        bias_t, bias = _bias_table(rpb[l].reshape(H, N_DR * N_DC), "bias_table")
        ycat = _attn_fwd(proj, bias_t, yc, cw, naw, "attn_fwd")
        if l == 0:
            g_rest = _gather_forward(g_rest, ycat, "gather_forward_3")
        Wout, Wup, Wdown = _gather_finish(g_rest, ycat, "gather_finish_3")
        Ws = (Win, Wout, Wup, Wdown)
        mix = _matmul(ycat, Wout, name="mm_mix")
        x1, h2 = _resid_norm_fwd(xin, mix, row(post_mix_g, l), row(pre_mlp_g, l), "resid_mix")
        act, rl = _matmul(h2, Wup, epilogue="relu2", name="mm_up")
        if nxt:
            g_in = _gather_forward(nxt[0], act, "gather_forward_1")
        f = _matmul(act, Wdown, tm=256, tk=act.shape[1], n_outer=True, name="mm_down")
        saved.append(dict(xin=xin, h=h, W=Ws, proj=proj, c=c, bias=bias, ycat=ycat,
                          mix=mix, x1=x1, h2=h2, act=act, rl=rl, f=f))
        if nxt:
            Win = _gather_finish(g_in, f, "gather_finish_1")[0]
            g_rest = nxt[1]
            xin, h = _resid_norm_fwd(x1, f, row(post_mlp_g, l), row(pre_mix_g, l + 1), "resid_mlp")
        else:
            dy, loss_sum = _resid_loss(x1, f, row(post_mlp_g, l), tgt, "resid_loss")

    loss = lax.psum(loss_sum[0, 0] * (0.5 / D), MESH_AXES)

    small_grads = [None] * L
    big_out = [None] * 4
    moments = ((m_w_in, v_w_in), (m_w_out, v_w_out), (m_w_up, v_w_up), (m_w_down, v_w_down))

    def scatter_begin(grads, which, l):
        tag = "_%d" % len(which)
        return dict(st=_scatter_sibling_start(grads, "scatter_sibling_start" + tag), which=which, l=l, tag=tag)

    def scatter_mid(sc, after):
        grads, gots = _scatter_sibling_finish(sc["st"], after, "scatter_sibling_finish" + sc["tag"])
        pf = [_scatter_add(g, o, f"scatter_add_{k}") for k, g, o in zip(sc["which"], grads, gots)]
        st = _scatter_chips_start([p for p, _ in pf], [q for _, q in pf], "scatter_chips_start" + sc["tag"])
        return dict(sc, st=st)

    def scatter_end(sc, after):
        fins = _scatter_chips_finish(sc["st"], after, "scatter_chips_finish" + sc["tag"])
        for k, fin in zip(sc["which"], fins):
            big_out[k] = _adam_layer(fin, big[k], moments[k][0], moments[k][1], sc["l"], big_out[k],
                                     f"adam_{k}_{sc['l']}")

    dxo = dy
    pending = None
    for l in reversed(range(L)):
        s = saved[l]
        Win, Wout, Wup, Wdown = s["W"]
        last = l == 0
        tok = lambda sc: (sc["st"]["token"],) if sc is not None else ()
        d_f, dg_post_mlp = _norm_bwd(s["f"], row(post_mlp_g, l), dxo, None, _MXU, "norm_bwd_mlp", deps=tok(pending))
        d_up = _matmul(d_f, Wdown, tb=True, epilogue="mul2", extra=s["rl"], name="mm_d_up")
        if pending is not None:
            pending = scatter_mid(pending, d_up)
        dWdown = _matmul(s["act"], d_f, ta=True, out_dtype=_WIRE, tk=T, name="mm_dw_down").reshape(N_DEV, -1, D)
        d_h2 = _matmul(d_up, Wup, tb=True, tm=256, tk=d_up.shape[1], n_outer=True, name="mm_d_h2")
        dWup = _matmul(s["h2"], d_up, ta=True, out_dtype=_WIRE, out_cols=N_DEV, tk=T, name="mm_dw_up")
        sc_mlp = scatter_begin([dWup, dWdown], [2, 3], l) if last else None
        dx1, dg_pre_mlp = _norm_bwd(s["x1"], row(pre_mlp_g, l), d_h2, dxo, _F32, "norm_bwd_premlp", deps=tok(pending))
        d_mix, dg_post_mix = _norm_bwd(s["mix"], row(post_mix_g, l), dx1, None, _MXU, "norm_bwd_mix", deps=tok(sc_mlp))
        d_ycat = _matmul(d_mix, Wout, tb=True, name="mm_d_ycat")
        if last:
            sc_mlp = scatter_mid(sc_mlp, d_ycat)
        dWout = _matmul(s["ycat"], d_mix, ta=True, out_dtype=_WIRE, tk=T, name="mm_dw_out").reshape(N_DEV, -1, D)
        sc_out = scatter_begin([dWout], [1], l) if last else None
        dc, dlng, dlnb = _ln_silu_bwd(s["c"], row(conv_ln_g, l), row(conv_ln_b, l), d_ycat, "ln_silu_bwd",
                                      deps=tok(sc_mlp))
        da, dgate, dwb = _conv_bwd(s["proj"], dc, wdw_full, l, cw, "conv_bwd", deps=tok(sc_out))
        dq, dk, dv, gcls = _attn_bwd(s["proj"], s["bias"], d_ycat, cw, naw, "attn_bwd")
        if last:
            sc_out = scatter_mid(sc_out, gcls)
        drpb = _rpb_grad(gcls, "rpb_grad").reshape(H, 4 * SUBLANES, LANES)[:, :N_DC, :N_DR].transpose(0, 2, 1)
        dproj = jnp.concatenate([da, dgate, dq, dk, dv], axis=1)
        dWin = _matmul(s["h"], dproj, ta=True, out_dtype=_WIRE, out_cols=N_DEV, tm=512, tk=T, name="mm_dw_in")
        sc_in = scatter_begin([dWin], [0], l) if last else None
        dh = _matmul(dproj, Win, tb=True, tm=512, tk=dproj.shape[1], n_outer=True, name="mm_d_h")
        if last:
            sc_in = scatter_mid(sc_in, dh)
        dxo, dg_pre_mix = _norm_bwd(s["xin"], row(pre_mix_g, l), dh, dx1, _F32, "norm_bwd_premix",
                                    deps=tok(sc_out) + tok(sc_in))
        if pending is not None:
            scatter_end(pending, dxo)
        if last:
            scatter_end(sc_mlp, dxo)
            scatter_end(sc_out, dxo)
            scatter_end(sc_in, dxo)
        else:
            pending = scatter_begin([dWin, dWout, dWup, dWdown], [0, 1, 2, 3], l)
        small_grads[l] = [dwb[ks], dlng[0], dlnb[0], drpb, dg_pre_mix[0], dg_post_mix[0], dg_pre_mlp[0],
                          dg_post_mlp[0], dwb[:ks]]

    rep_shapes = [(L, cw), (L, cw), (L, cw), (L, H, N_DR, N_DC), (L, D), (L, D), (L, D), (L, D)]
    stacked = [jnp.stack([small_grads[l][k] for l in range(L)]) for k in range(9)]
    packed = _pack(stacked)
    parts = _all_gather([packed], [0], "ag_small")[0].reshape(N_DEV, *packed.shape)
    gsum = _sum_parts(parts, "sum_small")
    g_small = _unpack(gsum, rep_shapes + [(L, ks, cw)])
    g_rep, g_wdw_full = g_small[:8], g_small[8]
    wsh = w_dw.shape[2]
    g_wdw = lax.dynamic_slice_in_dim(g_wdw_full, dev * wsh, wsh, axis=2)

    rep_w = [b_dw, conv_ln_g, conv_ln_b, rpb, pre_mix_g, post_mix_g, pre_mlp_g, post_mlp_g]
    rep_m = [m_b_dw, m_conv_ln_g, m_conv_ln_b, m_rpb, m_pre_mix_g, m_post_mix_g, m_pre_mlp_g, m_post_mlp_g]
    rep_v = [v_b_dw, v_conv_ln_g, v_conv_ln_b, v_rpb, v_pre_mix_g, v_post_mix_g, v_pre_mlp_g, v_post_mlp_g]
    rep_out = _adam_flat(_pack(g_rep), _pack(rep_w), _pack(rep_m), _pack(rep_v), "adam_small")
    rep_delta, rep_nm, rep_nv = (_unpack(o, rep_shapes) for o in rep_out)
    dw_out = _adam_flat(_pack([g_wdw]), _pack([w_dw]), _pack([m_w_dw]), _pack([v_w_dw]), "adam_wdw")
    wdw_delta, wdw_nm, wdw_nv = (_unpack(o, [w_dw.shape])[0] for o in dw_out)

    def assemble(kind_big, rep_list, wdw_val):
        return [big_out[0][kind_big], wdw_val, rep_list[0], rep_list[1], rep_list[2], rep_list[3],
                big_out[1][kind_big], big_out[2][kind_big], big_out[3][kind_big],
                rep_list[4], rep_list[5], rep_list[6], rep_list[7]]

    grads_out = assemble(0, g_rep, g_wdw)
    deltas = assemble(1, rep_delta, wdw_delta)
    new_m = assemble(2, rep_nm, wdw_nm)
    new_v = assemble(3, rep_nv, wdw_nv)
    return (loss, dxo.reshape(1, T, D), *grads_out, *deltas, *new_m, *new_v)
```
